```python
import math
import jax, jax.numpy as jnp
from jax import lax
import numpy as np

D_MODEL = 1024
BATCH = 16
SEQ = 256
DEPTH = 2
DEC_BATCH = 8
DEC_SEQ = 1024
PAST_LEN = 512

GRID_W = 64
N_DIR = 2
EPS = 1e-6
ATT_HD = 64
ATT_WIDTH = D_MODEL // 2
ATT_HEADS = ATT_WIDTH // ATT_HD
ATT_KV_HEADS = ATT_HEADS // 4
ATT_KV_WIDTH = ATT_KV_HEADS * ATT_HD
Q_BLOCK = 128
ROPE_THETA = 10000.0
S5_CH = 16
S5_STATE = 64
S5_WIDTH = D_MODEL // 4
S5_GROUPS = S5_WIDTH // S5_CH
M_HD = 64
M_WIDTH = D_MODEL // 4
M_HEADS = M_WIDTH // M_HD
M_CHUNK = 64
M_GATE_COLS = 2 * N_DIR * M_HEADS
MIX_WIDTH = ATT_WIDTH + S5_WIDTH + M_WIDTH
SPLIT_SIZES = (ATT_WIDTH, ATT_KV_WIDTH, ATT_KV_WIDTH, S5_WIDTH, M_WIDTH, M_WIDTH, M_WIDTH, M_WIDTH, M_GATE_COLS)
N_IN = sum(SPLIT_SIZES)
D_FF = ((8 * D_MODEL // 3 + 127) // 128) * 128
CONV_W = 3

kernel_name = "hybrid_dit_s5_gqa_mlstm_prefix_step"


def rms_norm(x, g):
    xf = x.astype(jnp.float32)
    xf = xf * lax.rsqrt(jnp.mean(xf * xf, axis=-1, keepdims=True) + EPS)
    return (xf * g.astype(jnp.float32)).astype(x.dtype)


def adaln(cvec, w_mod, b_mod):
    mod = jax.nn.silu(cvec) @ w_mod + b_mod
    return jnp.split(mod[:, None, :], 6, axis=-1)


def axial_rope(x):
    L = x.shape[1]
    n_rows = L // GRID_W
    row = jnp.repeat(jnp.arange(n_rows), GRID_W)
    col = jnp.tile(jnp.arange(GRID_W), n_rows)
    half = ATT_HD // 2
    inv_freq = 1.0 / (ROPE_THETA ** (jnp.arange(0, half, 2, dtype=jnp.float32) / half))

    def rot(xs, pos):
        ang = pos.astype(jnp.float32)[:, None] * inv_freq[None, :]
        cos = jnp.cos(ang)[None, :, None, :]
        sin = jnp.sin(ang)[None, :, None, :]
        x1, x2 = jnp.split(xs.astype(jnp.float32), 2, axis=-1)
        return jnp.concatenate([x1 * cos - x2 * sin, x2 * cos + x1 * sin], axis=-1)

    out = jnp.concatenate([rot(x[..., :half], row), rot(x[..., half:], col)], axis=-1)
    return out.astype(x.dtype)


def block_attention(q, k, v):
    B, Lq = q.shape[0], q.shape[1]
    G = ATT_HEADS // ATT_KV_HEADS
    qb = q.reshape(B, Lq // Q_BLOCK, Q_BLOCK, ATT_KV_HEADS, G, ATT_HD).transpose(1, 0, 2, 3, 4, 5)
    scale = ATT_HD ** -0.5

    def one_block(qblk):
        s = jnp.einsum('bqhgd,bshd->bhgqs', qblk, k).astype(jnp.float32) * scale
        p = jax.nn.softmax(s, axis=-1).astype(v.dtype)
        return jnp.einsum('bhgqs,bshd->bqhgd', p, v)

    out = lax.map(one_block, qb)
    return out.transpose(1, 0, 2, 3, 4, 5).reshape(B, Lq, ATT_WIDTH)


def s5_scan(u, a_re, a_im, log_dt, b_re, b_im, init_re, init_im):
    dt = jnp.exp(log_dt)[:, None]
    mag = jnp.exp(dt * a_re)
    abar_re = mag * jnp.cos(dt * a_im)
    abar_im = mag * jnp.sin(dt * a_im)
    den = a_re * a_re + a_im * a_im
    nr = abar_re - 1.0
    ni = abar_im
    f_re = (nr * a_re + ni * a_im) / den
    f_im = (ni * a_re - nr * a_im) / den
    bb_re = f_re[..., None] * b_re - f_im[..., None] * b_im
    bb_im = f_re[..., None] * b_im + f_im[..., None] * b_re
    x_re = jnp.einsum('blgc,gpc->blgp', u, bb_re)
    x_im = jnp.einsum('blgc,gpc->blgp', u, bb_im)
    x_re = x_re.at[:, 0].add(abar_re * init_re - abar_im * init_im)
    x_im = x_im.at[:, 0].add(abar_re * init_im + abar_im * init_re)
    A_re = jnp.broadcast_to(abar_re, x_re.shape)
    A_im = jnp.broadcast_to(abar_im, x_im.shape)

    def combine(e1, e2):
        a1r, a1i, b1r, b1i = e1
        a2r, a2i, b2r, b2i = e2
        return (a1r * a2r - a1i * a2i, a1r * a2i + a1i * a2r,
                a2r * b1r - a2i * b1i + b2r, a2r * b1i + a2i * b1r + b2i)

    _, _, s_re, s_im = lax.associative_scan(combine, (A_re, A_im, x_re, x_im), axis=1)
    return s_re, s_im


def s5_mixer(u, a_re, a_im, log_dt, b_re, b_im, c_re, c_im, d_skip, glu_w, glu_b, init_re, init_im):
    B, L = u.shape[0], u.shape[1]
    f32 = jnp.float32
    uf = u.astype(f32).reshape(B, L, S5_GROUPS, S5_CH)
    y = uf * d_skip.astype(f32).reshape(S5_GROUPS, S5_CH)
    fin_re, fin_im = [], []
    for d in range(N_DIR):
        ud = uf if d == 0 else uf[:, ::-1]
        s_re, s_im = s5_scan(ud, a_re[d].astype(f32), a_im[d].astype(f32), log_dt[d].astype(f32),
                             b_re[d].astype(f32), b_im[d].astype(f32),
                             init_re[:, d].astype(f32), init_im[:, d].astype(f32))
        yd = (jnp.einsum('blgp,gcp->blgc', s_re, c_re[d].astype(f32))
              - jnp.einsum('blgp,gcp->blgc', s_im, c_im[d].astype(f32)))
        y = y + (yd if d == 0 else yd[:, ::-1])
        fin_re.append(s_re[:, -1])
        fin_im.append(s_im[:, -1])
    y = y.reshape(B, L, S5_WIDTH)
    z = jax.nn.gelu(y)
    out = z * jax.nn.sigmoid(z @ glu_w.astype(f32) + glu_b.astype(f32))
    return out.astype(u.dtype), jnp.stack(fin_re, axis=1), jnp.stack(fin_im, axis=1)


def mlstm_scan(q, k, v, li, lf, C0, n0, m0):
    B, H, L, HD = q.shape
    nc = L // M_CHUNK

    def chunks(t):
        return jnp.moveaxis(t.reshape((B, H, nc, M_CHUNK) + t.shape[3:]), 2, 0)

    mask = jnp.tril(jnp.ones((M_CHUNK, M_CHUNK), dtype=bool))

    def step(carry, xs):
        C, n, m = carry
        qc, kc, vc, lic, lfc = xs
        b = jnp.cumsum(lfc, axis=-1)
        dmat = b[..., :, None] - b[..., None, :] + lic[..., None, :]
        dmat = jnp.where(mask, dmat, -jnp.inf)
        inter = b + m[..., None]
        m_t = jnp.maximum(inter, jnp.max(dmat, axis=-1))
        w = jnp.exp(dmat - m_t[..., None])
        s_inter = jnp.exp(inter - m_t)
        sc = jnp.einsum('bhtd,bhsd->bhts', qc, kc) * w
        num = (jnp.einsum('bhts,bhsd->bhtd', sc, vc)
               + s_inter[..., None] * jnp.einsum('bhde,bhte->bhtd', C, qc))
        den = jnp.sum(sc, axis=-1) + s_inter * jnp.einsum('bhe,bhte->bht', n, qc)
        h = num / jnp.maximum(jnp.abs(den), jnp.exp(-m_t))[..., None]
        b_last = b[..., -1]
        g = b_last[..., None] - b + lic
        m_new = jnp.maximum(b_last + m, jnp.max(g, axis=-1))
        wk = jnp.exp(g - m_new[..., None])
        decay = jnp.exp(b_last + m - m_new)
        C_new = decay[..., None, None] * C + jnp.einsum('bhs,bhsd,bhse->bhde', wk, vc, kc)
        n_new = decay[..., None] * n + jnp.einsum('bhs,bhse->bhe', wk, kc)
        return (C_new, n_new, m_new), h

    (Cf, nf, mf), h = lax.scan(step, (C0, n0, m0), (chunks(q), chunks(k), chunks(v), chunks(li), chunks(lf)))
    h = jnp.moveaxis(h, 0, 2).reshape(B, H, L, HD)
    return h, Cf, nf, mf


def mlstm_mixer(mq, mk, mv, mo, mg, gate_b, norm_g, C0, n0, m0):
    B, L = mq.shape[0], mq.shape[1]
    f32 = jnp.float32

    def heads(t):
        return t.astype(f32).reshape(B, L, M_HEADS, M_HD).transpose(0, 2, 1, 3)

    q = heads(mq)
    k = heads(mk) * (M_HD ** -0.5)
    v = heads(mv)
    g = mg.astype(f32).reshape(B, L, 2, N_DIR, M_HEADS) + gate_b.astype(f32)
    g = g.transpose(2, 3, 0, 4, 1)
    li_all = g[0]
    lf_all = jax.nn.log_sigmoid(g[1])
    h = jnp.zeros_like(q)
    fin_C, fin_n, fin_m = [], [], []
    for d in range(N_DIR):
        def flip(t):
            return t if d == 0 else jnp.flip(t, axis=2)
        hd, Cf, nf, mf = mlstm_scan(flip(q), flip(k), flip(v), flip(li_all[d]), flip(lf_all[d]),
                                    C0[:, d].astype(f32), n0[:, d].astype(f32), m0[:, d].astype(f32))
        h = h + flip(hd)
        fin_C.append(Cf)
        fin_n.append(nf)
        fin_m.append(mf)
    h = h * lax.rsqrt(jnp.mean(h * h, axis=-1, keepdims=True) + EPS) * norm_g.astype(f32)
    h = h.transpose(0, 2, 1, 3).reshape(B, L, M_WIDTH) * jax.nn.sigmoid(mo.astype(f32))
    return (h.astype(mq.dtype), jnp.stack(fin_C, axis=1), jnp.stack(fin_n, axis=1), jnp.stack(fin_m, axis=1))


def conv_ffn(h, w_gate, w_up, conv_w, conv_b, w_down):
    a = h @ w_gate
    ap = jnp.pad(a, ((0, 0), (1, 1), (0, 0)))
    a = ap[:, :-2] * conv_w[0] + ap[:, 1:-1] * conv_w[1] + ap[:, 2:] * conv_w[2] + conv_b
    return (jax.nn.silu(a) * (h @ w_up)) @ w_down


def trunk_layer(x, cvec, lp, ctx):
    B, L = x.shape[0], x.shape[1]
    sh1, sc1, g1, sh2, sc2, g2 = adaln(cvec, lp['ada_w'], lp['ada_b'])
    h = rms_norm(x, lp['norm1_g']) * (1.0 + sc1) + sh1
    proj = h @ lp['w_in']
    points = [int(p) for p in np.cumsum(SPLIT_SIZES)[:-1]]
    aq, ak, av, su, mq, mk, mv, mo, mg = jnp.split(proj, points, axis=-1)
    aq = rms_norm(aq.reshape(B, L, ATT_HEADS, ATT_HD), lp['q_norm_g'])
    ak = rms_norm(ak.reshape(B, L, ATT_KV_HEADS, ATT_HD), lp['k_norm_g'])
    av = av.reshape(B, L, ATT_KV_HEADS, ATT_HD)
    if ctx is None:
        keys, vals = ak, av
        s5_init_re = jnp.zeros((B, N_DIR, S5_GROUPS, S5_STATE), jnp.float32)
        s5_init_im = jnp.zeros((B, N_DIR, S5_GROUPS, S5_STATE), jnp.float32)
        m_C0 = jnp.zeros((B, N_DIR, M_HEADS, M_HD, M_HD), jnp.float32)
        m_n0 = jnp.zeros((B, N_DIR, M_HEADS, M_HD), jnp.float32)
        m_m0 = jnp.zeros((B, N_DIR, M_HEADS), jnp.float32)
    else:
        ctx_k, ctx_v, s5_init_re, s5_init_im, m_C0, m_n0, m_m0 = ctx
        aq = axial_rope(aq)
        keys = jnp.concatenate([ctx_k.astype(ak.dtype), axial_rope(ak)], axis=1)
        vals = jnp.concatenate([ctx_v.astype(av.dtype), av], axis=1)
    att = block_attention(aq, keys, vals)
    s5o, s5_re, s5_im = s5_mixer(su, lp['s5_a_re'], lp['s5_a_im'], lp['s5_log_dt'], lp['s5_b_re'], lp['s5_b_im'],
                                 lp['s5_c_re'], lp['s5_c_im'], lp['s5_d'], lp['s5_glu_w'], lp['s5_glu_b'],
                                 s5_init_re, s5_init_im)
    mlo, mC, mn, mm = mlstm_mixer(mq, mk, mv, mo, mg, lp['m_gate_b'], lp['m_norm_g'], m_C0, m_n0, m_m0)
    mix = jnp.concatenate([att, s5o.astype(att.dtype), mlo.astype(att.dtype)], axis=-1) @ lp['w_out']
    x = x + g1 * mix
    h2 = rms_norm(x, lp['norm2_g']) * (1.0 + sc2) + sh2
    x = x + g2 * conv_ffn(h2, lp['ffn_w_gate'], lp['ffn_w_up'], lp['ffn_conv_w'], lp['ffn_conv_b'], lp['ffn_w_down'])
    return x, (ak, av, s5_re, s5_im, mC, mn, mm)


def setup_inputs(seed: int = 0) -> dict:
    key = jax.random.key(seed)
    keys = iter(jax.random.split(key, 48))
    f32 = jnp.float32

    def nrm(shape, s):
        return jax.random.normal(next(keys), shape, f32) * s

    D = D_MODEL
    inp = {}
    inp['x_prompt'] = nrm((BATCH, SEQ, D), 1.0)
    inp['x_sample'] = nrm((DEC_BATCH, DEC_SEQ, D), 1.0)
    inp['c'] = nrm((DEC_BATCH, D), 1.0)
    inp['cache_attn_k'] = nrm((DEC_BATCH, DEPTH, PAST_LEN, ATT_KV_HEADS, ATT_HD), 1.0)
    inp['cache_attn_v'] = nrm((DEC_BATCH, DEPTH, PAST_LEN, ATT_KV_HEADS, ATT_HD), 1.0)
    inp['state_s5_re'] = nrm((DEC_BATCH, DEPTH, N_DIR, S5_GROUPS, S5_STATE), 0.1)
    inp['state_s5_im'] = nrm((DEC_BATCH, DEPTH, N_DIR, S5_GROUPS, S5_STATE), 0.1)
    inp['state_mlstm_C'] = nrm((DEC_BATCH, DEPTH, N_DIR, M_HEADS, M_HD, M_HD), 0.1)
    inp['state_mlstm_n'] = nrm((DEC_BATCH, DEPTH, N_DIR, M_HEADS, M_HD), 0.1)
    inp['state_mlstm_m'] = nrm((DEC_BATCH, DEPTH, N_DIR, M_HEADS), 0.5)
    inp['c_ctx'] = nrm((D,), 1.0)
    inp['ada_w'] = nrm((DEPTH, D, 6 * D), D ** -0.5)
    inp['ada_b'] = nrm((DEPTH, 6 * D), 0.02)
    inp['norm1_g'] = 1.0 + nrm((DEPTH, D), 0.02)
    inp['norm2_g'] = 1.0 + nrm((DEPTH, D), 0.02)
    inp['w_in'] = nrm((DEPTH, D, N_IN), D ** -0.5)
    inp['q_norm_g'] = 1.0 + nrm((DEPTH, ATT_HD), 0.02)
    inp['k_norm_g'] = 1.0 + nrm((DEPTH, ATT_HD), 0.02)
    inp['s5_a_re'] = -0.5 + nrm((DEPTH, N_DIR, S5_GROUPS, S5_STATE), 0.01)
    inp['s5_a_im'] = jnp.pi * jnp.arange(S5_STATE, dtype=f32) + nrm((DEPTH, N_DIR, S5_GROUPS, S5_STATE), 0.01)
    inp['s5_log_dt'] = jax.random.uniform(next(keys), (DEPTH, N_DIR, S5_GROUPS), f32, math.log(1e-3), math.log(1e-1))
    inp['s5_b_re'] = nrm((DEPTH, N_DIR, S5_GROUPS, S5_STATE, S5_CH), (2 * S5_CH) ** -0.5)
    inp['s5_b_im'] = nrm((DEPTH, N_DIR, S5_GROUPS, S5_STATE, S5_CH), (2 * S5_CH) ** -0.5)
    inp['s5_c_re'] = nrm((DEPTH, N_DIR, S5_GROUPS, S5_CH, S5_STATE), S5_STATE ** -0.5)
    inp['s5_c_im'] = nrm((DEPTH, N_DIR, S5_GROUPS, S5_CH, S5_STATE), S5_STATE ** -0.5)
    inp['s5_d'] = nrm((DEPTH, S5_WIDTH), 1.0)
    inp['s5_glu_w'] = nrm((DEPTH, S5_WIDTH, S5_WIDTH), S5_WIDTH ** -0.5)
    inp['s5_glu_b'] = nrm((DEPTH, S5_WIDTH), 0.02)
    ig = nrm((DEPTH, N_DIR, M_HEADS), 0.1)
    fg = jnp.linspace(3.0, 6.0, M_HEADS, dtype=f32) + nrm((DEPTH, N_DIR, M_HEADS), 0.1)
    inp['m_gate_b'] = jnp.stack([ig, fg], axis=1)
    inp['m_norm_g'] = 1.0 + nrm((DEPTH, M_HD), 0.02)
    inp['w_out'] = nrm((DEPTH, MIX_WIDTH, D), MIX_WIDTH ** -0.5)
    inp['ffn_w_gate'] = nrm((DEPTH, D, D_FF), D ** -0.5)
    inp['ffn_w_up'] = nrm((DEPTH, D, D_FF), D ** -0.5)
    inp['ffn_conv_w'] = nrm((DEPTH, CONV_W, D_FF), CONV_W ** -0.5)
    inp['ffn_conv_b'] = nrm((DEPTH, D_FF), 0.02)
    inp['ffn_w_down'] = nrm((DEPTH, D_FF, D), D_FF ** -0.5)
    return inp


def reference(x_prompt, x_sample, c, cache_attn_k, cache_attn_v, state_s5_re, state_s5_im,
              state_mlstm_C, state_mlstm_n, state_mlstm_m, c_ctx, ada_w, ada_b, norm1_g, norm2_g,
              w_in, q_norm_g, k_norm_g, s5_a_re, s5_a_im, s5_log_dt, s5_b_re, s5_b_im, s5_c_re, s5_c_im,
              s5_d, s5_glu_w, s5_glu_b, m_gate_b, m_norm_g, w_out, ffn_w_gate, ffn_w_up, ffn_conv_w,
              ffn_conv_b, ffn_w_down):
    xp = x_prompt
    xs = x_sample
    ctx_out = []
    for l in range(DEPTH):
        lp = dict(ada_w=ada_w[l], ada_b=ada_b[l], norm1_g=norm1_g[l], norm2_g=norm2_g[l], w_in=w_in[l],
                  q_norm_g=q_norm_g[l], k_norm_g=k_norm_g[l], s5_a_re=s5_a_re[l], s5_a_im=s5_a_im[l],
                  s5_log_dt=s5_log_dt[l], s5_b_re=s5_b_re[l], s5_b_im=s5_b_im[l], s5_c_re=s5_c_re[l],
                  s5_c_im=s5_c_im[l], s5_d=s5_d[l], s5_glu_w=s5_glu_w[l], s5_glu_b=s5_glu_b[l],
                  m_gate_b=m_gate_b[l], m_norm_g=m_norm_g[l], w_out=w_out[l], ffn_w_gate=ffn_w_gate[l],
                  ffn_w_up=ffn_w_up[l], ffn_conv_w=ffn_conv_w[l], ffn_conv_b=ffn_conv_b[l], ffn_w_down=ffn_w_down[l])
        xp, st = trunk_layer(xp, c_ctx[None, :], lp, None)
        ctx_out.append(st)
        ctx = (cache_attn_k[:, l], cache_attn_v[:, l], state_s5_re[:, l], state_s5_im[:, l],
               state_mlstm_C[:, l], state_mlstm_n[:, l], state_mlstm_m[:, l])
        xs, _ = trunk_layer(xs, c, lp, ctx)
    new_attn_k = jnp.stack([s[0] for s in ctx_out], axis=1)
    new_attn_v = jnp.stack([s[1] for s in ctx_out], axis=1)
    new_s5_re = jnp.stack([s[2] for s in ctx_out], axis=1)
    new_s5_im = jnp.stack([s[3] for s in ctx_out], axis=1)
    new_mlstm_C = jnp.stack([s[4] for s in ctx_out], axis=1)
    new_mlstm_n = jnp.stack([s[5] for s in ctx_out], axis=1)
    new_mlstm_m = jnp.stack([s[6] for s in ctx_out], axis=1)
    return (xp, xs, new_attn_k, new_attn_v, new_s5_re, new_s5_im, new_mlstm_C, new_mlstm_n, new_mlstm_m)
```

```python
import functools
import math

import numpy as np
import jax
import jax.numpy as jnp
from jax import lax
from jax.experimental import pallas as pl
from jax.experimental.pallas import tpu as pltpu

F32 = jnp.float32
BF16 = jnp.bfloat16

D_MODEL = 1024
DEPTH = 2
GRID_W = 64
N_DIR = 2
EPS = 1e-6
ATT_HD = 64
ATT_WIDTH = 512
ATT_HEADS = 8
ATT_KV_HEADS = 2
ATT_GROUP = ATT_HEADS // ATT_KV_HEADS
ATT_KV_WIDTH = ATT_KV_HEADS * ATT_HD
ROPE_THETA = 10000.0
S5_CH = 16
S5_STATE = 64
S5_WIDTH = 256
S5_GROUPS = 16
S5_FLAT = S5_GROUPS * S5_STATE
M_HD = 64
M_WIDTH = 256
M_HEADS = 4
M_CHUNK = 64
M_GATE_COLS = 2 * N_DIR * M_HEADS
M_CHAINS = N_DIR * M_HEADS
D_FF = 2816
PROJ_MAIN = 2048
COL_K = ATT_WIDTH
COL_V = COL_K + ATT_KV_WIDTH
COL_S5 = COL_V + ATT_KV_WIDTH
COL_MQ = COL_S5 + S5_WIDTH
LANES = 128
SUBLANES = 8
VMEM_LIMIT = 56 * 1024 * 1024

PRE_TM = 256
ATT_TQ = 256
S5_TC = 64
POST_TM = 1024
FF_CHUNK = 256


def _bf(x):
    return x.astype(BF16)


def _dot(a, b):
    return jnp.dot(a, b, preferred_element_type=F32)


def _split(x):
    hi = _bf(x)
    lo = _bf(x - hi.astype(F32))
    return hi, lo


def _seg_mean_sq(x, blk):
    hi, lo = _split(x * x)
    return _dot(hi, blk) + _dot(lo, blk)


def _sigmoid(x):
    return 1.0 / (1.0 + jnp.exp(-x))


def _log_sigmoid(x):
    return -(jnp.maximum(-x, 0.0) + jnp.log1p(jnp.exp(-jnp.abs(x))))


def _params(*sem):
    return pltpu.CompilerParams(dimension_semantics=sem, vmem_limit_bytes=VMEM_LIMIT)


def _ada_kernel(c_ref, w_ref, b_ref, o_ref):
    c = c_ref[...]
    s = c * _sigmoid(c)
    o_ref[...] = _dot(_bf(s), _bf(w_ref[...])) + b_ref[...]


def _ada_call(cvec, ada_w, ada_b):
    rows = cvec.shape[0]
    tn = 1024
    n = ada_w.shape[-1]
    return pl.pallas_call(
        _ada_kernel,
        grid=(DEPTH, n // tn),
        in_specs=[pl.BlockSpec((rows, D_MODEL), lambda l, j: (0, 0)),
                  pl.BlockSpec((None, D_MODEL, tn), lambda l, j: (l, 0, j)),
                  pl.BlockSpec((None, 1, tn), lambda l, j: (l, 0, j))],
        out_specs=pl.BlockSpec((None, rows, tn), lambda l, j: (l, 0, j)),
        out_shape=jax.ShapeDtypeStruct((DEPTH, rows, n), F32),
        compiler_params=_params("arbitrary", "arbitrary"),
        name="adaln",
    )(cvec, ada_w, ada_b.reshape(DEPTH, 1, n))


def _s5_disc_kernel(are_ref, aim_ref, ldt_ref, arex_ref, aimx_ref, bre_ref, bim_ref,
                    abr_ref, abi_ref, bbr_ref, bbi_ref):
    dt = jnp.exp(ldt_ref[...])

    def disc(a_re, a_im):
        mag = jnp.exp(dt * a_re)
        ab_re = mag * jnp.cos(dt * a_im)
        ab_im = mag * jnp.sin(dt * a_im)
        den = a_re * a_re + a_im * a_im
        nr = ab_re - 1.0
        ni = ab_im
        f_re = (nr * a_re + ni * a_im) / den
        f_im = (ni * a_re - nr * a_im) / den
        return ab_re, ab_im, f_re, f_im

    ab_re, ab_im, _, _ = disc(are_ref[...], aim_ref[...])
    abr_ref[...] = ab_re
    abi_ref[...] = ab_im
    _, _, f_re, f_im = disc(arex_ref[...], aimx_ref[...])
    b_re = bre_ref[...]
    b_im = bim_ref[...]
    bbr_ref[...] = f_re * b_re - f_im * b_im
    bbi_ref[...] = f_re * b_im + f_im * b_re


def _s5_disc_call(a_re, a_im, log_dt, b_re, b_im):
    r = DEPTH * N_DIR * S5_GROUPS
    a_re2 = a_re.reshape(r, S5_STATE)
    a_im2 = a_im.reshape(r, S5_STATE)
    wide = S5_STATE * S5_CH
    out_shape = (jax.ShapeDtypeStruct((r, S5_STATE), F32), jax.ShapeDtypeStruct((r, S5_STATE), F32),
                 jax.ShapeDtypeStruct((r, wide), F32), jax.ShapeDtypeStruct((r, wide), F32))
    return pl.pallas_call(_s5_disc_kernel, out_shape=out_shape, name="s5_disc")(
        a_re2, a_im2, log_dt.reshape(r, 1),
        jnp.repeat(a_re2, S5_CH, axis=1), jnp.repeat(a_im2, S5_CH, axis=1),
        b_re.reshape(r, wide), b_im.reshape(r, wide))


def _block_diag(blocks):
    g, r, c = blocks.shape
    eye = jnp.eye(g, dtype=blocks.dtype)
    return jnp.einsum('grc,gh->grhc', blocks, eye).reshape(g * r, g * c)


def _rope(x, cos, sin_signed, second):
    width = x.shape[-1]
    quarter = ATT_HD // 4
    partner = jnp.where(second, pltpu.roll(x, quarter, axis=1), pltpu.roll(x, width - quarter, axis=1))
    return x * cos + partner * sin_signed


def _pre_kernel(*refs, rope):
    if rope:
        (x_ref, mod_ref, g1_ref, w_ref, wg_ref, blkq_ref, blkk_ref, qg_ref, kg_ref,
         cosq_ref, sinq_ref, cosk_ref, sink_ref, proj_ref, mg_ref, qn_ref, kn_ref) = refs
    else:
        (x_ref, mod_ref, g1_ref, w_ref, wg_ref, blkq_ref, blkk_ref, qg_ref, kg_ref,
         proj_ref, mg_ref, qn_ref, kn_ref) = refs
    x = x_ref[...]
    ms = jnp.mean(x * x, axis=-1, keepdims=True)
    xn = x * lax.rsqrt(ms + EPS) * g1_ref[...]
    h = _bf(xn * (1.0 + mod_ref[1:2, :]) + mod_ref[0:1, :])
    proj = _dot(h, w_ref[...])
    proj_ref[...] = proj
    mg_ref[...] = _dot(h, wg_ref[...])
    q = proj[:, :COL_K]
    k = proj[:, COL_K:COL_V]
    qn = q * lax.rsqrt(_seg_mean_sq(q, blkq_ref[...]) + EPS) * qg_ref[...]
    kn = k * lax.rsqrt(_seg_mean_sq(k, blkk_ref[...]) + EPS) * kg_ref[...]
    if rope:
        lane_q = lax.broadcasted_iota(jnp.int32, qn.shape, 1)
        lane_k = lax.broadcasted_iota(jnp.int32, kn.shape, 1)
        qn = _rope(qn, cosq_ref[...], sinq_ref[...], (lane_q & (ATT_HD // 4)) != 0)
        kn = _rope(kn, cosk_ref[...], sink_ref[...], (lane_k & (ATT_HD // 4)) != 0)
    qn_ref[...] = qn
    kn_ref[...] = kn


def _rope_tables(seq_len):
    n_rows = seq_len // GRID_W
    row = jnp.repeat(jnp.arange(n_rows), GRID_W)
    col = jnp.tile(jnp.arange(GRID_W), n_rows)
    half = ATT_HD // 2
    inv_freq = 1.0 / (ROPE_THETA ** (jnp.arange(0, half, 2, dtype=F32) / half))

    def tables(pos):
        ang = pos.astype(F32)[:, None] * inv_freq[None, :]
        cos = jnp.cos(ang)
        sin = jnp.sin(ang)
        return jnp.concatenate([cos, cos], axis=-1), jnp.concatenate([-sin, sin], axis=-1)

    cr, sr = tables(row)
    cc, sc = tables(col)
    cos = jnp.concatenate([cr, cc], axis=-1)
    sin = jnp.concatenate([sr, sc], axis=-1)
    return (jnp.tile(cos, (1, ATT_HEADS)), jnp.tile(sin, (1, ATT_HEADS)),
            jnp.tile(cos, (1, ATT_KV_HEADS)), jnp.tile(sin, (1, ATT_KV_HEADS)))


def _pre_call(x2, mod, g1, w_main, w_gate, blkq, blkk, qg, kg, rope_tabs, seq_len, per_batch_mod):
    t = x2.shape[0]
    tm = PRE_TM
    tiles_per_seq = seq_len // tm
    rope = rope_tabs is not None
    if per_batch_mod:
        mod_map = lambda i: (i // tiles_per_seq, 0, 0)
    else:
        mod_map = lambda i: (0, 0, 0)
    const = lambda i: (0, 0)
    in_specs = [pl.BlockSpec((tm, D_MODEL), lambda i: (i, 0)),
                pl.BlockSpec((None, 6, D_MODEL), mod_map),
                pl.BlockSpec((1, D_MODEL), const),
                pl.BlockSpec((D_MODEL, PROJ_MAIN), const),
                pl.BlockSpec((D_MODEL, LANES), const),
                pl.BlockSpec((ATT_WIDTH, ATT_WIDTH), const),
                pl.BlockSpec((ATT_KV_WIDTH, ATT_KV_WIDTH), const),
                pl.BlockSpec((1, ATT_WIDTH), const),
                pl.BlockSpec((1, ATT_KV_WIDTH), const)]
    args = [x2, mod, g1, w_main, w_gate, blkq, blkk, qg, kg]
    if rope:
        pos_map = lambda i: (i % tiles_per_seq, 0)
        in_specs += [pl.BlockSpec((tm, ATT_WIDTH), pos_map), pl.BlockSpec((tm, ATT_WIDTH), pos_map),
                     pl.BlockSpec((tm, ATT_KV_WIDTH), pos_map), pl.BlockSpec((tm, ATT_KV_WIDTH), pos_map)]
        args += list(rope_tabs)
    out_specs = [pl.BlockSpec((tm, PROJ_MAIN), lambda i: (i, 0)),
                 pl.BlockSpec((tm, LANES), lambda i: (i, 0)),
                 pl.BlockSpec((tm, ATT_WIDTH), lambda i: (i, 0)),
                 pl.BlockSpec((tm, ATT_KV_WIDTH), lambda i: (i, 0))]
    out_shape = [jax.ShapeDtypeStruct((t, PROJ_MAIN), F32), jax.ShapeDtypeStruct((t, LANES), F32),
                 jax.ShapeDtypeStruct((t, ATT_WIDTH), F32), jax.ShapeDtypeStruct((t, ATT_KV_WIDTH), F32)]
    return pl.pallas_call(
        functools.partial(_pre_kernel, rope=rope),
        grid=(t // tm,), in_specs=in_specs, out_specs=out_specs, out_shape=out_shape,
        compiler_params=_params("arbitrary"), name="pre_rope" if rope else "pre",
    )(*args)


def _attn_kernel(q_ref, k_ref, v_ref, o_ref):
    scale = ATT_HD ** -0.5
    outs = []
    for kvh in range(ATT_KV_HEADS):
        lo = kvh * ATT_HD
        kb = _bf(k_ref[:, lo:lo + ATT_HD])
        vb = _bf(v_ref[:, lo:lo + ATT_HD])
        for g in range(ATT_GROUP):
            c0 = (kvh * ATT_GROUP + g) * ATT_HD
            qb = _bf(q_ref[:, c0:c0 + ATT_HD] * scale)
            s = lax.dot_general(qb, kb, (((1,), (1,)), ((), ())), preferred_element_type=F32)
            m = jnp.max(s, axis=-1, keepdims=True)
            e = jnp.exp(s - m)
            den = jnp.sum(e, axis=-1, keepdims=True)
            outs.append(_dot(_bf(e), vb) / den)
    o_ref[...] = jnp.concatenate(outs, axis=-1)


def _attn_call(qn, keys, vals, batch, lq):
    lk = keys.shape[1]
    tq = ATT_TQ
    nq = lq // tq
    return pl.pallas_call(
        _attn_kernel,
        grid=(batch, nq),
        in_specs=[pl.BlockSpec((tq, ATT_WIDTH), lambda b, i: (b * nq + i, 0)),
                  pl.BlockSpec((None, lk, ATT_KV_WIDTH), lambda b, i: (b, 0, 0)),
                  pl.BlockSpec((None, lk, ATT_KV_WIDTH), lambda b, i: (b, 0, 0))],
        out_specs=pl.BlockSpec((tq, ATT_WIDTH), lambda b, i: (b * nq + i, 0)),
        out_shape=jax.ShapeDtypeStruct((batch * lq, ATT_WIDTH), F32),
        compiler_params=_params("arbitrary", "arbitrary"), name="attn",
    )(qn, keys, vals)


def _s5_kernel(u_ref, bmat_ref, cre_ref, cim_ref, ar_ref, ai_ref, ir_ref, ii_ref, d_ref, gw_ref, gb_ref,
               out_ref, fr_ref, fi_ref, xs_ref, *, batch, seq_len):
    tc = S5_TC
    rows = tc * batch
    n_chunks = seq_len // tc
    n_sub = batch // SUBLANES

    out_ref[...] = u_ref[...] * d_ref[...]

    for d in range(N_DIR):
        a_re = ar_ref[d]
        a_im = ai_ref[d]

        def chunk_body(ci, carry, d=d, a_re=a_re, a_im=a_im):
            c = ci if d == 0 else n_chunks - 1 - ci
            r0 = pl.multiple_of(c * rows, rows)
            xs_ref[...] = _dot(_bf(u_ref[pl.ds(r0, rows), :]), bmat_ref[d])
            new_carry = []
            for sub in range(n_sub):
                def step(t, st, sub=sub):
                    s_re, s_im = st
                    tt = t if d == 0 else tc - 1 - t
                    r = pl.multiple_of(tt * batch + sub * SUBLANES, SUBLANES)
                    x_re = xs_ref[pl.ds(r, SUBLANES), :S5_FLAT]
                    x_im = xs_ref[pl.ds(r, SUBLANES), S5_FLAT:]
                    n_re = a_re * s_re - a_im * s_im + x_re
                    n_im = a_re * s_im + a_im * s_re + x_im
                    xs_ref[pl.ds(r, SUBLANES), :S5_FLAT] = n_re
                    xs_ref[pl.ds(r, SUBLANES), S5_FLAT:] = n_im
                    return n_re, n_im

                new_carry.append(lax.fori_loop(0, tc, step, carry[sub], unroll=2))
            y = (_dot(_bf(xs_ref[:, :S5_FLAT]), cre_ref[d]) - _dot(_bf(xs_ref[:, S5_FLAT:]), cim_ref[d]))
            out_ref[pl.ds(r0, rows), :] += y
            return tuple(new_carry)

        init = tuple((ir_ref[d, sub * SUBLANES:(sub + 1) * SUBLANES, :],
                      ii_ref[d, sub * SUBLANES:(sub + 1) * SUBLANES, :]) for sub in range(n_sub))
        fin = lax.fori_loop(0, n_chunks, chunk_body, init)
        for sub in range(n_sub):
            fr_ref[d, sub * SUBLANES:(sub + 1) * SUBLANES, :] = fin[sub][0]
            fi_ref[d, sub * SUBLANES:(sub + 1) * SUBLANES, :] = fin[sub][1]

    def glu_body(ci, _):
        r0 = pl.multiple_of(ci * rows, rows)
        z = jax.nn.gelu(out_ref[pl.ds(r0, rows), :])
        gate = _sigmoid(_dot(_bf(z), gw_ref[...]) + gb_ref[...])
        out_ref[pl.ds(r0, rows), :] = z * gate
        return 0

    lax.fori_loop(0, n_chunks, glu_body, 0)


def _s5_call(u_tb, bmat, cre, cim, ab_re, ab_im, init_re, init_im, d_skip, glu_w, glu_b, batch, seq_len):
    rows = seq_len * batch
    out_shape = (jax.ShapeDtypeStruct((rows, S5_WIDTH), F32),
                 jax.ShapeDtypeStruct((N_DIR, batch, S5_FLAT), F32),
                 jax.ShapeDtypeStruct((N_DIR, batch, S5_FLAT), F32))
    return pl.pallas_call(
        functools.partial(_s5_kernel, batch=batch, seq_len=seq_len),
        out_shape=out_shape,
        scratch_shapes=[pltpu.VMEM((S5_TC * batch, 2 * S5_FLAT), F32)],
        compiler_params=pltpu.CompilerParams(vmem_limit_bytes=VMEM_LIMIT), name="s5",
    )(u_tb, bmat, cre, cim, ab_re, ab_im, init_re, init_im, d_skip, glu_w, glu_b)


def _mlstm_kernel(q_ref, k_ref, kt_ref, v_ref, o_ref, mg_ref, mgt_ref, gbr_ref, gbc_ref, ct0_ref, n0_ref, m0_ref,
                  ng_ref, tril_ref, triu_ref, blk_ref, out_ref, ct_ref, n_ref, m_ref, h_ref, *, seq_len):
    n_chunks = seq_len // M_CHUNK
    k_scale = M_HD ** -0.5
    ct_ref[...] = ct0_ref[...]
    n_ref[...] = n0_ref[...]
    m_ref[...] = m0_ref[...]
    h_ref[...] = jnp.zeros_like(h_ref)
    t_ids = lax.broadcasted_iota(jnp.int32, (M_CHUNK, M_CHUNK), 0)
    s_ids = lax.broadcasted_iota(jnp.int32, (M_CHUNK, M_CHUNK), 1)
    neg_inf = jnp.float32(-jnp.inf)

    def direction(d, c):
        r0 = pl.multiple_of(c * M_CHUNK, M_CHUNK)
        col_mat = tril_ref[...] if d == 0 else triu_ref[...]
        row_mat = triu_ref[...] if d == 0 else tril_ref[...]
        valid = (s_ids <= t_ids) if d == 0 else (s_ids >= t_ids)
        last = M_CHUNK - 1 if d == 0 else 0
        gc = mg_ref[pl.ds(r0, M_CHUNK), :] + gbr_ref[...]
        lf_hi, lf_lo = _split(_log_sigmoid(gc))
        b_cols = _dot(col_mat, lf_hi) + _dot(col_mat, lf_lo)
        gt = mgt_ref[c] + gbc_ref[...]
        li_rows = gt[:M_CHAINS]
        lft_hi, lft_lo = _split(_log_sigmoid(gt[M_CHAINS:]))
        b_rows = _dot(lft_hi, row_mat) + _dot(lft_lo, row_mat)
        for h in range(M_HEADS):
            j = d * M_HEADS + h
            lanes = slice(h * M_HD, (h + 1) * M_HD)
            b_c = b_cols[:, M_CHAINS + j:M_CHAINS + j + 1]
            li_c = gc[:, j:j + 1]
            b_r = b_rows[j:j + 1, :]
            li_r = li_rows[j:j + 1, :]
            total = b_c[last:last + 1, :]
            m_prev = m_ref[:, j:j + 1]
            n_prev = n_ref[j:j + 1, :]
            ct_prev = ct_ref[j]
            q_c = q_ref[pl.ds(r0, M_CHUNK), lanes]
            k_c = k_ref[pl.ds(r0, M_CHUNK), lanes] * k_scale
            v_c = v_ref[pl.ds(r0, M_CHUNK), lanes]
            kt_c = kt_ref[c, lanes, :] * k_scale
            q_b = _bf(q_c)

            dmat = jnp.where(valid, b_c - b_r + li_r, neg_inf)
            inter = b_c + m_prev
            m_t = jnp.maximum(inter, jnp.max(dmat, axis=-1, keepdims=True))
            w = jnp.exp(dmat - m_t)
            s_inter = jnp.exp(inter - m_t)
            sc = _dot(q_b, _bf(kt_c)) * w
            num = _dot(_bf(sc), _bf(v_c)) + s_inter * _dot(q_b, _bf(ct_prev))
            den = jnp.sum(sc, axis=-1, keepdims=True) + s_inter * jnp.sum(q_c * n_prev, axis=-1, keepdims=True)
            hh = num / jnp.maximum(jnp.abs(den), jnp.exp(-m_t))
            h_ref[pl.ds(r0, M_CHUNK), lanes] += hh

            g_r = total - b_r + li_r
            g_c = total - b_c + li_c
            m_new = jnp.maximum(total + m_prev, jnp.max(g_r, axis=-1, keepdims=True))
            wk_c = jnp.exp(g_c - m_new)
            decay = jnp.exp(total + m_prev - m_new)
            ct_ref[j] = decay * ct_prev + _dot(_bf(kt_c), _bf(v_c * wk_c))
            n_ref[j:j + 1, :] = decay * n_prev + jnp.sum(k_c * wk_c, axis=0, keepdims=True)
            m_ref[:, j:j + 1] = m_new

    def body(i, _):
        direction(0, i)
        direction(1, n_chunks - 1 - i)
        return 0

    lax.fori_loop(0, n_chunks, body, 0)

    h = h_ref[...]
    hn = h * lax.rsqrt(_seg_mean_sq(h, blk_ref[...]) + EPS) * ng_ref[...]
    out_ref[...] = hn * _sigmoid(o_ref[...])


def _mlstm_call(proj, k_t, mg, mg_t, gb_row, gb_col, ct0, n0, m0, ng, tril, triu, blk, batch, seq_len):
    n_chunks = seq_len // M_CHUNK
    col = COL_MQ // M_WIDTH
    const2 = lambda b: (0, 0)
    in_specs = [pl.BlockSpec((seq_len, M_WIDTH), lambda b: (b, col)),
                pl.BlockSpec((seq_len, M_WIDTH), lambda b: (b, col + 1)),
                pl.BlockSpec((None, n_chunks, M_WIDTH, M_CHUNK), lambda b: (b, 0, 0, 0)),
                pl.BlockSpec((seq_len, M_WIDTH), lambda b: (b, col + 2)),
                pl.BlockSpec((seq_len, M_WIDTH), lambda b: (b, col + 3)),
                pl.BlockSpec((seq_len, LANES), lambda b: (b, 0)),
                pl.BlockSpec((None, n_chunks, M_GATE_COLS, M_CHUNK), lambda b: (b, 0, 0, 0)),
                pl.BlockSpec((1, LANES), const2),
                pl.BlockSpec((M_GATE_COLS, 1), const2),
                pl.BlockSpec((None, M_CHAINS, M_HD, M_HD), lambda b: (b, 0, 0, 0)),
                pl.BlockSpec((None, M_CHAINS, M_HD), lambda b: (b, 0, 0)),
                pl.BlockSpec((None, 1, M_CHAINS), lambda b: (b, 0, 0)),
                pl.BlockSpec((1, M_WIDTH), const2),
                pl.BlockSpec((M_CHUNK, M_CHUNK), const2),
                pl.BlockSpec((M_CHUNK, M_CHUNK), const2),
                pl.BlockSpec((M_WIDTH, M_WIDTH), const2)]
    out_specs = [pl.BlockSpec((seq_len, M_WIDTH), lambda b: (b, 0)),
                 pl.BlockSpec((None, M_CHAINS, M_HD, M_HD), lambda b: (b, 0, 0, 0)),
                 pl.BlockSpec((None, M_CHAINS, M_HD), lambda b: (b, 0, 0)),
                 pl.BlockSpec((None, 1, M_CHAINS), lambda b: (b, 0, 0))]
    out_shape = [jax.ShapeDtypeStruct((batch * seq_len, M_WIDTH), F32),
                 jax.ShapeDtypeStruct((batch, M_CHAINS, M_HD, M_HD), F32),
                 jax.ShapeDtypeStruct((batch, M_CHAINS, M_HD), F32),
                 jax.ShapeDtypeStruct((batch, 1, M_CHAINS), F32)]
    return pl.pallas_call(
        functools.partial(_mlstm_kernel, seq_len=seq_len),
        grid=(batch,), in_specs=in_specs, out_specs=out_specs, out_shape=out_shape,
        scratch_shapes=[pltpu.VMEM((seq_len, M_WIDTH), F32)],
        compiler_params=_params("arbitrary"), name="mlstm",
    )(proj, proj, k_t, proj, proj, mg, mg_t, gb_row, gb_col, ct0, n0, m0, ng, tril, triu, blk)


def _post_kernel(x_ref, att_ref, s5_ref, ml_ref, mod_ref, g2_ref, wo_ref, wg_ref, wu_ref, cw_ref, cb_ref, wd_ref,
                 o_ref, x1_ref, h2_ref, acc_ref, *, seq_len):
    j = pl.program_id(1)

    @pl.when(j == 0)
    def _():
        mix = (_dot(_bf(att_ref[...]), wo_ref[:ATT_WIDTH, :])
               + _dot(_bf(s5_ref[...]), wo_ref[ATT_WIDTH:ATT_WIDTH + S5_WIDTH, :])
               + _dot(_bf(ml_ref[...]), wo_ref[ATT_WIDTH + S5_WIDTH:, :]))
        x1 = x_ref[...] + mod_ref[2:3, :] * mix
        x1_ref[...] = x1
        ms = jnp.mean(x1 * x1, axis=-1, keepdims=True)
        xn = x1 * lax.rsqrt(ms + EPS) * g2_ref[...]
        h2_ref[...] = _bf(xn * (1.0 + mod_ref[4:5, :]) + mod_ref[3:4, :])
        acc_ref[...] = jnp.zeros_like(acc_ref)

    h2 = h2_ref[...]
    a = _dot(h2, wg_ref[...])
    tm = a.shape[0]
    pos = lax.broadcasted_iota(jnp.int32, a.shape, 0) % seq_len
    a_prev = jnp.where(pos == 0, 0.0, pltpu.roll(a, 1, axis=0))
    a_next = jnp.where(pos == seq_len - 1, 0.0, pltpu.roll(a, tm - 1, axis=0))
    ac = a_prev * cw_ref[0:1, :] + a * cw_ref[1:2, :] + a_next * cw_ref[2:3, :] + cb_ref[...]
    up = _dot(h2, wu_ref[...])
    gated = _bf(ac * _sigmoid(ac) * up)
    acc_ref[...] += _dot(gated, wd_ref[...])

    @pl.when(j == pl.num_programs(1) - 1)
    def _():
        o_ref[...] = x1_ref[...] + mod_ref[5:6, :] * acc_ref[...]


def _post_call(x2, att, s5o, mlo, mod, g2, w_out, w_gate, w_up, conv_w, conv_b, w_down, seq_len, per_batch_mod):
    t = x2.shape[0]
    tm = POST_TM
    n_ff = D_FF // FF_CHUNK
    seqs_per_tile = tm // seq_len
    if per_batch_mod:
        mod_map = lambda i, j: (i * seqs_per_tile, 0, 0)
    else:
        mod_map = lambda i, j: (0, 0, 0)
    row = lambda i, j: (i, 0)
    const = lambda i, j: (0, 0)
    in_specs = [pl.BlockSpec((tm, D_MODEL), row),
                pl.BlockSpec((tm, ATT_WIDTH), row),
                pl.BlockSpec((tm, S5_WIDTH), row),
                pl.BlockSpec((tm, M_WIDTH), row),
                pl.BlockSpec((None, 6, D_MODEL), mod_map),
                pl.BlockSpec((1, D_MODEL), const),
                pl.BlockSpec((D_MODEL, D_MODEL), const),
                pl.BlockSpec((D_MODEL, FF_CHUNK), lambda i, j: (0, j)),
                pl.BlockSpec((D_MODEL, FF_CHUNK), lambda i, j: (0, j)),
                pl.BlockSpec((3, FF_CHUNK), lambda i, j: (0, j)),
                pl.BlockSpec((1, FF_CHUNK), lambda i, j: (0, j)),
                pl.BlockSpec((FF_CHUNK, D_MODEL), lambda i, j: (j, 0))]
    return pl.pallas_call(
        functools.partial(_post_kernel, seq_len=seq_len),
        grid=(t // tm, n_ff), in_specs=in_specs,
        out_specs=pl.BlockSpec((tm, D_MODEL), row),
        out_shape=jax.ShapeDtypeStruct((t, D_MODEL), F32),
        scratch_shapes=[pltpu.VMEM((tm, D_MODEL), F32), pltpu.VMEM((tm, D_MODEL), BF16),
                        pltpu.VMEM((tm, D_MODEL), F32)],
        compiler_params=_params("arbitrary", "arbitrary"), name="post",
    )(x2, att, s5o, mlo, mod, g2, w_out, w_gate, w_up, conv_w, conv_b, w_down)


def _trunk_layer(x2, batch, seq_len, mod, lp, ctx, rope_tabs):
    latent = ctx is not None
    proj, mg, qn, kn = _pre_call(x2, mod, lp['g1'], lp['w_main'], lp['w_gatecols'], lp['blkq'], lp['blkk'],
                                 lp['qg'], lp['kg'], rope_tabs if latent else None, seq_len, latent)
    v_new = proj[:, COL_V:COL_S5].reshape(batch, seq_len, ATT_KV_WIDTH)
    k_new = kn.reshape(batch, seq_len, ATT_KV_WIDTH)
    if latent:
        ctx_k, ctx_v, s5_re0, s5_im0, c0, n0, m0 = ctx
        past = ctx_k.shape[1]
        keys = jnp.concatenate([ctx_k.reshape(batch, past, ATT_KV_WIDTH), k_new], axis=1)
        vals = jnp.concatenate([ctx_v.reshape(batch, past, ATT_KV_WIDTH), v_new], axis=1)
        init_re = s5_re0.reshape(batch, N_DIR, S5_FLAT).transpose(1, 0, 2)
        init_im = s5_im0.reshape(batch, N_DIR, S5_FLAT).transpose(1, 0, 2)
        ct0 = jnp.swapaxes(c0, -1, -2).reshape(batch, M_CHAINS, M_HD, M_HD)
        n0 = n0.reshape(batch, M_CHAINS, M_HD)
        m0 = m0.reshape(batch, 1, M_CHAINS)
    else:
        keys, vals = k_new, v_new
        init_re = jnp.zeros((N_DIR, batch, S5_FLAT), F32)
        init_im = jnp.zeros((N_DIR, batch, S5_FLAT), F32)
        ct0 = jnp.zeros((batch, M_CHAINS, M_HD, M_HD), F32)
        n0 = jnp.zeros((batch, M_CHAINS, M_HD), F32)
        m0 = jnp.zeros((batch, 1, M_CHAINS), F32)

    att = _attn_call(qn, keys, vals, batch, seq_len)

    u_tb = (proj[:, COL_S5:COL_MQ].reshape(batch, seq_len, S5_WIDTH).transpose(1, 0, 2)
            .reshape(seq_len * batch, S5_WIDTH))
    s5_tb, fin_re, fin_im = _s5_call(u_tb, lp['s5_bmat'], lp['s5_cre'], lp['s5_cim'], lp['s5_ab_re'], lp['s5_ab_im'],
                                     init_re, init_im, lp['s5_d'], lp['s5_glu_w'], lp['s5_glu_b'], batch, seq_len)
    s5o = s5_tb.reshape(seq_len, batch, S5_WIDTH).transpose(1, 0, 2).reshape(batch * seq_len, S5_WIDTH)

    n_chunks = seq_len // M_CHUNK
    k_t = (proj[:, COL_MQ + M_WIDTH:COL_MQ + 2 * M_WIDTH].reshape(batch, n_chunks, M_CHUNK, M_WIDTH)
           .transpose(0, 1, 3, 2))
    mg_t = mg[:, :M_GATE_COLS].reshape(batch, n_chunks, M_CHUNK, M_GATE_COLS).transpose(0, 1, 3, 2)
    mlo, ct_f, n_f, m_f = _mlstm_call(proj, k_t, mg, mg_t, lp['m_gb_row'], lp['m_gb_col'], ct0, n0, m0, lp['m_ng'],
                                      lp['tril'], lp['triu'], lp['blkm'], batch, seq_len)

    x_out = _post_call(x2, att, s5o, mlo, mod, lp['g2'], lp['w_out'], lp['ffn_w_gate'], lp['ffn_w_up'],
                       lp['ffn_conv_w'], lp['ffn_conv_b'], lp['ffn_w_down'], seq_len, latent)
    states = (k_new.reshape(batch, seq_len, ATT_KV_HEADS, ATT_HD),
              v_new.reshape(batch, seq_len, ATT_KV_HEADS, ATT_HD),
              fin_re.transpose(1, 0, 2).reshape(batch, N_DIR, S5_GROUPS, S5_STATE),
              fin_im.transpose(1, 0, 2).reshape(batch, N_DIR, S5_GROUPS, S5_STATE),
              jnp.swapaxes(ct_f, -1, -2).reshape(batch, N_DIR, M_HEADS, M_HD, M_HD),
              n_f.reshape(batch, N_DIR, M_HEADS, M_HD),
              m_f.reshape(batch, N_DIR, M_HEADS))
    return x_out, states


def _head_block(width, head_dim):
    ids = np.arange(width) // head_dim
    return jnp.asarray((ids[:, None] == ids[None, :]).astype(np.float32) / head_dim, dtype=BF16)


def kernel(x_prompt, x_sample, c, cache_attn_k, cache_attn_v, state_s5_re, state_s5_im, state_mlstm_C, state_mlstm_n, state_mlstm_m, c_ctx, ada_w, ada_b, norm1_g, norm2_g, w_in, q_norm_g, k_norm_g, s5_a_re, s5_a_im, s5_log_dt, s5_b_re, s5_b_im, s5_c_re, s5_c_im, s5_d, s5_glu_w, s5_glu_b, m_gate_b, m_norm_g, w_out, ffn_w_gate, ffn_w_up, ffn_conv_w, ffn_conv_b, ffn_w_down):
    batch, seq = x_prompt.shape[0], x_prompt.shape[1]
    dec_batch, dec_seq = x_sample.shape[0], x_sample.shape[1]

    n_mod_rows = 2 * SUBLANES
    cvec = jnp.zeros((n_mod_rows, D_MODEL), F32).at[0].set(c_ctx).at[1:1 + dec_batch].set(c)
    mod_all = _ada_call(cvec, ada_w, ada_b).reshape(DEPTH, n_mod_rows, 6, D_MODEL)

    ab_re, ab_im, bb_re, bb_im = _s5_disc_call(s5_a_re, s5_a_im, s5_log_dt, s5_b_re, s5_b_im)
    ab_re = ab_re.reshape(DEPTH, N_DIR, 1, S5_FLAT)
    ab_im = ab_im.reshape(DEPTH, N_DIR, 1, S5_FLAT)
    bb_re = bb_re.reshape(DEPTH, N_DIR, S5_GROUPS, S5_STATE, S5_CH)
    bb_im = bb_im.reshape(DEPTH, N_DIR, S5_GROUPS, S5_STATE, S5_CH)

    rope_tabs = _rope_tables(dec_seq)
    tri = np.tril(np.ones((M_CHUNK, M_CHUNK), np.float32))
    tril = jnp.asarray(tri, dtype=BF16)
    triu = jnp.asarray(tri.T, dtype=BF16)
    blkq = _head_block(ATT_WIDTH, ATT_HD)
    blkk = _head_block(ATT_KV_WIDTH, ATT_HD)
    blkm = _head_block(M_WIDTH, M_HD)

    xp = x_prompt.reshape(batch * seq, D_MODEL)
    xs = x_sample.reshape(dec_batch * dec_seq, D_MODEL)
    ctx_out = []
    for l in range(DEPTH):
        w_in_b = _bf(w_in[l])
        gate_cols = jnp.zeros((D_MODEL, LANES), BF16).at[:, :M_GATE_COLS].set(w_in_b[:, PROJ_MAIN:])
        bmat = jnp.stack([jnp.concatenate(
            [_block_diag(jnp.swapaxes(bb_re[l, d], -1, -2)), _block_diag(jnp.swapaxes(bb_im[l, d], -1, -2))],
            axis=1) for d in range(N_DIR)])
        cre = jnp.stack([_block_diag(jnp.swapaxes(s5_c_re[l, d], -1, -2)) for d in range(N_DIR)])
        cim = jnp.stack([_block_diag(jnp.swapaxes(s5_c_im[l, d], -1, -2)) for d in range(N_DIR)])
        gb = m_gate_b[l].reshape(M_GATE_COLS)
        lp = dict(
            g1=norm1_g[l].reshape(1, D_MODEL), g2=norm2_g[l].reshape(1, D_MODEL),
            w_main=w_in_b[:, :PROJ_MAIN], w_gatecols=gate_cols, blkq=blkq, blkk=blkk, blkm=blkm,
            qg=jnp.tile(q_norm_g[l], ATT_HEADS).reshape(1, ATT_WIDTH),
            kg=jnp.tile(k_norm_g[l], ATT_KV_HEADS).reshape(1, ATT_KV_WIDTH),
            s5_bmat=_bf(bmat), s5_cre=_bf(cre), s5_cim=_bf(cim),
            s5_ab_re=jnp.broadcast_to(ab_re[l], (N_DIR, SUBLANES, S5_FLAT)),
            s5_ab_im=jnp.broadcast_to(ab_im[l], (N_DIR, SUBLANES, S5_FLAT)),
            s5_d=s5_d[l].reshape(1, S5_WIDTH), s5_glu_w=_bf(s5_glu_w[l]), s5_glu_b=s5_glu_b[l].reshape(1, S5_WIDTH),
            m_gb_row=jnp.zeros((1, LANES), F32).at[0, :M_GATE_COLS].set(gb), m_gb_col=gb.reshape(M_GATE_COLS, 1),
            m_ng=jnp.tile(m_norm_g[l], M_HEADS).reshape(1, M_WIDTH), tril=tril, triu=triu,
            w_out=_bf(w_out[l]), ffn_w_gate=_bf(ffn_w_gate[l]), ffn_w_up=_bf(ffn_w_up[l]),
            ffn_conv_w=ffn_conv_w[l], ffn_conv_b=ffn_conv_b[l].reshape(1, D_FF), ffn_w_down=_bf(ffn_w_down[l]))
        xp, st = _trunk_layer(xp, batch, seq, mod_all[l, 0:1], lp, None, None)
        ctx_out.append(st)
        ctx = (cache_attn_k[:, l], cache_attn_v[:, l], state_s5_re[:, l], state_s5_im[:, l],
               state_mlstm_C[:, l], state_mlstm_n[:, l], state_mlstm_m[:, l])
        xs, _ = _trunk_layer(xs, dec_batch, dec_seq, mod_all[l, 1:1 + dec_batch], lp, ctx, rope_tabs)
    outs = [jnp.stack([s[i] for s in ctx_out], axis=1) for i in range(7)]
    return (xp.reshape(batch, seq, D_MODEL), xs.reshape(dec_batch, dec_seq, D_MODEL), *outs)
```

```python
import functools

import numpy as np
import jax
import jax.numpy as jnp
from jax import lax
from jax.experimental import pallas as pl
from jax.experimental.pallas import tpu as pltpu

F32 = jnp.float32
BF16 = jnp.bfloat16

D_MODEL = 1024
DEPTH = 2
GRID_W = 64
N_DIR = 2
EPS = 1e-6
ATT_HD = 64
ATT_WIDTH = 512
ATT_HEADS = 8
ATT_KV_HEADS = 2
ATT_GROUP = ATT_HEADS // ATT_KV_HEADS
ATT_KV_WIDTH = ATT_KV_HEADS * ATT_HD
ROPE_THETA = 10000.0
S5_CH = 16
S5_STATE = 64
S5_WIDTH = 256
S5_GROUPS = 16
S5_FLAT = S5_GROUPS * S5_STATE
M_HD = 64
M_WIDTH = 256
M_HEADS = 4
M_GATE_COLS = 2 * N_DIR * M_HEADS
M_CHAINS = N_DIR * M_HEADS
D_FF = 2816
PROJ_MAIN = 2048
COL_K = ATT_WIDTH
COL_V = COL_K + ATT_KV_WIDTH
COL_S5 = COL_V + ATT_KV_WIDTH
COL_MQ = COL_S5 + S5_WIDTH
COL_MK = COL_MQ + M_WIDTH
COL_MV = COL_MK + M_WIDTH
COL_MO = COL_MV + M_WIDTH
LANES = 128
SUBLANES = 8
BF16_ROWS = 16
VMEM_LIMIT = 56 * 1024 * 1024

PRE_TM = 256
ATT_TQ = 256
S5_TC = 64
M_TILE = 128
M_BG = 4
CN_ROWS = M_HD + BF16_ROWS
POST_TM = 1024
FF_CHUNK = 256


def _bf(x):
    return x.astype(BF16)


def _dot(a, b):
    return jnp.dot(a, b, preferred_element_type=F32)


def _split(x):
    hi = _bf(x)
    lo = _bf(x - hi.astype(F32))
    return hi, lo


def _seg_mean_sq(x, blk):
    hi, lo = _split(x * x)
    return _dot(hi, blk) + _dot(lo, blk)


def _sigmoid(x):
    return 1.0 / (1.0 + jnp.exp(-x))


def _log_sigmoid(x):
    return -(jnp.maximum(-x, 0.0) + jnp.log1p(jnp.exp(-jnp.abs(x))))


def _params(*sem):
    return pltpu.CompilerParams(dimension_semantics=sem, vmem_limit_bytes=VMEM_LIMIT)


def _ada_kernel(c_ref, w_ref, b_ref, o_ref):
    c = c_ref[...]
    s = c * _sigmoid(c)
    o_ref[...] = _dot(_bf(s), _bf(w_ref[...])) + b_ref[...]


def _ada_call(cvec, ada_w, ada_b):
    rows = cvec.shape[0]
    tn = 1024
    n = ada_w.shape[-1]
    return pl.pallas_call(
        _ada_kernel,
        grid=(DEPTH, n // tn),
        in_specs=[pl.BlockSpec((rows, D_MODEL), lambda l, j: (0, 0)),
                  pl.BlockSpec((None, D_MODEL, tn), lambda l, j: (l, 0, j)),
                  pl.BlockSpec((None, 1, tn), lambda l, j: (l, 0, j))],
        out_specs=pl.BlockSpec((None, rows, tn), lambda l, j: (l, 0, j)),
        out_shape=jax.ShapeDtypeStruct((DEPTH, rows, n), F32),
        compiler_params=_params("arbitrary", "arbitrary"),
        name="adaln",
    )(cvec, ada_w, ada_b.reshape(DEPTH, 1, n))


def _s5_disc_kernel(are_ref, aim_ref, ldt_ref, arex_ref, aimx_ref, bre_ref, bim_ref,
                    abr_ref, abi_ref, bbr_ref, bbi_ref):
    dt = jnp.exp(ldt_ref[...])

    def disc(a_re, a_im):
        mag = jnp.exp(dt * a_re)
        ab_re = mag * jnp.cos(dt * a_im)
        ab_im = mag * jnp.sin(dt * a_im)
        den = a_re * a_re + a_im * a_im
        nr = ab_re - 1.0
        ni = ab_im
        f_re = (nr * a_re + ni * a_im) / den
        f_im = (ni * a_re - nr * a_im) / den
        return ab_re, ab_im, f_re, f_im

    ab_re, ab_im, _, _ = disc(are_ref[...], aim_ref[...])
    abr_ref[...] = ab_re
    abi_ref[...] = ab_im
    _, _, f_re, f_im = disc(arex_ref[...], aimx_ref[...])
    b_re = bre_ref[...]
    b_im = bim_ref[...]
    bbr_ref[...] = f_re * b_re - f_im * b_im
    bbi_ref[...] = f_re * b_im + f_im * b_re


def _s5_disc_call(a_re, a_im, log_dt, b_re, b_im):
    r = DEPTH * N_DIR * S5_GROUPS
    a_re2 = a_re.reshape(r, S5_STATE)
    a_im2 = a_im.reshape(r, S5_STATE)
    wide = S5_STATE * S5_CH
    out_shape = (jax.ShapeDtypeStruct((r, S5_STATE), F32), jax.ShapeDtypeStruct((r, S5_STATE), F32),
                 jax.ShapeDtypeStruct((r, wide), F32), jax.ShapeDtypeStruct((r, wide), F32))
    return pl.pallas_call(_s5_disc_kernel, out_shape=out_shape, name="s5_disc")(
        a_re2, a_im2, log_dt.reshape(r, 1),
        jnp.repeat(a_re2, S5_CH, axis=1), jnp.repeat(a_im2, S5_CH, axis=1),
        b_re.reshape(r, wide), b_im.reshape(r, wide))


def _block_diag(blocks):
    g, r, c = blocks.shape
    eye = jnp.eye(g, dtype=blocks.dtype)
    return jnp.einsum('grc,gh->grhc', blocks, eye).reshape(g * r, g * c)


def _rope(x, cos, sin_signed, second):
    width = x.shape[-1]
    quarter = ATT_HD // 4
    partner = jnp.where(second, pltpu.roll(x, quarter, axis=1), pltpu.roll(x, width - quarter, axis=1))
    return x * cos + partner * sin_signed


def _pre_kernel(*refs, rope):
    if rope:
        (x_ref, mod_ref, g1_ref, w_ref, wg_ref, blkq_ref, blkk_ref, qg_ref, kg_ref,
         cosq_ref, sinq_ref, cosk_ref, sink_ref, proj_ref, mg_ref, qn_ref, kn_ref, mqk_ref) = refs
    else:
        (x_ref, mod_ref, g1_ref, w_ref, wg_ref, blkq_ref, blkk_ref, qg_ref, kg_ref,
         proj_ref, mg_ref, qn_ref, kn_ref, mqk_ref) = refs
    x = x_ref[...]
    ms = jnp.mean(x * x, axis=-1, keepdims=True)
    xn = x * lax.rsqrt(ms + EPS) * g1_ref[...]
    h = _bf(xn * (1.0 + mod_ref[1:2, :]) + mod_ref[0:1, :])
    proj = _dot(h, w_ref[...])
    proj_ref[...] = proj
    mg_ref[...] = _dot(h, wg_ref[...])
    mqk_ref[:, :M_WIDTH] = _bf(proj[:, COL_MQ:COL_MK])
    mqk_ref[:, M_WIDTH:] = _bf(proj[:, COL_MK:COL_MV] * (M_HD ** -0.5))
    q = proj[:, :COL_K]
    k = proj[:, COL_K:COL_V]
    qn = q * lax.rsqrt(_seg_mean_sq(q, blkq_ref[...]) + EPS) * qg_ref[...]
    kn = k * lax.rsqrt(_seg_mean_sq(k, blkk_ref[...]) + EPS) * kg_ref[...]
    if rope:
        lane_q = lax.broadcasted_iota(jnp.int32, qn.shape, 1)
        lane_k = lax.broadcasted_iota(jnp.int32, kn.shape, 1)
        qn = _rope(qn, cosq_ref[...], sinq_ref[...], (lane_q & (ATT_HD // 4)) != 0)
        kn = _rope(kn, cosk_ref[...], sink_ref[...], (lane_k & (ATT_HD // 4)) != 0)
    qn_ref[...] = qn
    kn_ref[...] = kn


def _rope_tables(seq_len):
    n_rows = seq_len // GRID_W
    row = jnp.repeat(jnp.arange(n_rows), GRID_W)
    col = jnp.tile(jnp.arange(GRID_W), n_rows)
    half = ATT_HD // 2
    inv_freq = 1.0 / (ROPE_THETA ** (jnp.arange(0, half, 2, dtype=F32) / half))

    def tables(pos):
        ang = pos.astype(F32)[:, None] * inv_freq[None, :]
        cos = jnp.cos(ang)
        sin = jnp.sin(ang)
        return jnp.concatenate([cos, cos], axis=-1), jnp.concatenate([-sin, sin], axis=-1)

    cr, sr = tables(row)
    cc, sc = tables(col)
    cos = jnp.concatenate([cr, cc], axis=-1)
    sin = jnp.concatenate([sr, sc], axis=-1)
    return (jnp.tile(cos, (1, ATT_HEADS)), jnp.tile(sin, (1, ATT_HEADS)),
            jnp.tile(cos, (1, ATT_KV_HEADS)), jnp.tile(sin, (1, ATT_KV_HEADS)))


def _pre_call(x2, mod, g1, w_main, w_gate, blkq, blkk, qg, kg, rope_tabs, seq_len, per_batch_mod):
    t = x2.shape[0]
    tm = PRE_TM
    tiles_per_seq = seq_len // tm
    rope = rope_tabs is not None
    if per_batch_mod:
        mod_map = lambda i: (i // tiles_per_seq, 0, 0)
    else:
        mod_map = lambda i: (0, 0, 0)
    const = lambda i: (0, 0)
    row = lambda i: (i, 0)
    in_specs = [pl.BlockSpec((tm, D_MODEL), row),
                pl.BlockSpec((None, 6, D_MODEL), mod_map),
                pl.BlockSpec((1, D_MODEL), const),
                pl.BlockSpec((D_MODEL, PROJ_MAIN), const),
                pl.BlockSpec((D_MODEL, LANES), const),
                pl.BlockSpec((ATT_WIDTH, ATT_WIDTH), const),
                pl.BlockSpec((ATT_KV_WIDTH, ATT_KV_WIDTH), const),
                pl.BlockSpec((1, ATT_WIDTH), const),
                pl.BlockSpec((1, ATT_KV_WIDTH), const)]
    args = [x2, mod, g1, w_main, w_gate, blkq, blkk, qg, kg]
    if rope:
        pos_map = lambda i: (i % tiles_per_seq, 0)
        in_specs += [pl.BlockSpec((tm, ATT_WIDTH), pos_map), pl.BlockSpec((tm, ATT_WIDTH), pos_map),
                     pl.BlockSpec((tm, ATT_KV_WIDTH), pos_map), pl.BlockSpec((tm, ATT_KV_WIDTH), pos_map)]
        args += list(rope_tabs)
    out_specs = [pl.BlockSpec((tm, PROJ_MAIN), row),
                 pl.BlockSpec((tm, LANES), row),
                 pl.BlockSpec((tm, ATT_WIDTH), row),
                 pl.BlockSpec((tm, ATT_KV_WIDTH), row),
                 pl.BlockSpec((tm, 2 * M_WIDTH), row)]
    out_shape = [jax.ShapeDtypeStruct((t, PROJ_MAIN), F32), jax.ShapeDtypeStruct((t, LANES), F32),
                 jax.ShapeDtypeStruct((t, ATT_WIDTH), F32), jax.ShapeDtypeStruct((t, ATT_KV_WIDTH), F32),
                 jax.ShapeDtypeStruct((t, 2 * M_WIDTH), BF16)]
    return pl.pallas_call(
        functools.partial(_pre_kernel, rope=rope),
        grid=(t // tm,), in_specs=in_specs, out_specs=out_specs, out_shape=out_shape,
        compiler_params=_params("arbitrary"), name="pre_rope" if rope else "pre",
    )(*args)


def _attn_kernel(q_ref, k_ref, v_ref, o_ref):
    scale = ATT_HD ** -0.5
    outs = []
    for kvh in range(ATT_KV_HEADS):
        lo = kvh * ATT_HD
        kb = _bf(k_ref[:, lo:lo + ATT_HD])
        vb = _bf(v_ref[:, lo:lo + ATT_HD])
        for g in range(ATT_GROUP):
            c0 = (kvh * ATT_GROUP + g) * ATT_HD
            qb = _bf(q_ref[:, c0:c0 + ATT_HD] * scale)
            s = lax.dot_general(qb, kb, (((1,), (1,)), ((), ())), preferred_element_type=F32)
            m = jnp.max(s, axis=-1, keepdims=True)
            e = jnp.exp(s - m)
            den = jnp.sum(e, axis=-1, keepdims=True)
            outs.append(_dot(_bf(e), vb) / den)
    o_ref[...] = jnp.concatenate(outs, axis=-1)


def _attn_call(qn, keys, vals, batch, lq):
    lk = keys.shape[1]
    tq = ATT_TQ
    nq = lq // tq
    return pl.pallas_call(
        _attn_kernel,
        grid=(batch, nq),
        in_specs=[pl.BlockSpec((tq, ATT_WIDTH), lambda b, i: (b * nq + i, 0)),
                  pl.BlockSpec((None, lk, ATT_KV_WIDTH), lambda b, i: (b, 0, 0)),
                  pl.BlockSpec((None, lk, ATT_KV_WIDTH), lambda b, i: (b, 0, 0))],
        out_specs=pl.BlockSpec((tq, ATT_WIDTH), lambda b, i: (b * nq + i, 0)),
        out_shape=jax.ShapeDtypeStruct((batch * lq, ATT_WIDTH), F32),
        compiler_params=_params("arbitrary", "arbitrary"), name="attn",
    )(qn, keys, vals)


def _s5_kernel(u_ref, bmat_ref, cre_ref, cim_ref, ar_ref, ai_ref, ir_ref, ii_ref, d_ref, gw_ref, gb_ref,
               out_ref, fr_ref, fi_ref, xs_ref, *, batch, seq_len):
    tc = S5_TC
    rows = tc * batch
    n_chunks = seq_len // tc
    n_sub = batch // SUBLANES

    out_ref[...] = u_ref[...] * d_ref[...]

    for d in range(N_DIR):
        a_re = ar_ref[d]
        a_im = ai_ref[d]

        def chunk_body(ci, carry, d=d, a_re=a_re, a_im=a_im):
            c = ci if d == 0 else n_chunks - 1 - ci
            r0 = pl.multiple_of(c * rows, rows)
            xs_ref[...] = _dot(_bf(u_ref[pl.ds(r0, rows), :]), bmat_ref[d])
            new_carry = []
            for sub in range(n_sub):
                def step(t, st, sub=sub):
                    s_re, s_im = st
                    tt = t if d == 0 else tc - 1 - t
                    r = pl.multiple_of(tt * batch + sub * SUBLANES, SUBLANES)
                    x_re = xs_ref[pl.ds(r, SUBLANES), :S5_FLAT]
                    x_im = xs_ref[pl.ds(r, SUBLANES), S5_FLAT:]
                    n_re = a_re * s_re - a_im * s_im + x_re
                    n_im = a_re * s_im + a_im * s_re + x_im
                    xs_ref[pl.ds(r, SUBLANES), :S5_FLAT] = n_re
                    xs_ref[pl.ds(r, SUBLANES), S5_FLAT:] = n_im
                    return n_re, n_im

                new_carry.append(lax.fori_loop(0, tc, step, carry[sub], unroll=2))
            y = (_dot(_bf(xs_ref[:, :S5_FLAT]), cre_ref[d]) - _dot(_bf(xs_ref[:, S5_FLAT:]), cim_ref[d]))
            out_ref[pl.ds(r0, rows), :] += y
            return tuple(new_carry)

        init = tuple((ir_ref[d, sub * SUBLANES:(sub + 1) * SUBLANES, :],
                      ii_ref[d, sub * SUBLANES:(sub + 1) * SUBLANES, :]) for sub in range(n_sub))
        fin = lax.fori_loop(0, n_chunks, chunk_body, init)
        for sub in range(n_sub):
            fr_ref[d, sub * SUBLANES:(sub + 1) * SUBLANES, :] = fin[sub][0]
            fi_ref[d, sub * SUBLANES:(sub + 1) * SUBLANES, :] = fin[sub][1]

    def glu_body(ci, _):
        r0 = pl.multiple_of(ci * rows, rows)
        z = jax.nn.gelu(out_ref[pl.ds(r0, rows), :])
        gate = _sigmoid(_dot(_bf(z), gw_ref[...]) + gb_ref[...])
        out_ref[pl.ds(r0, rows), :] = z * gate
        return 0

    lax.fori_loop(0, n_chunks, glu_body, 0)


def _s5_call(u_tb, bmat, cre, cim, ab_re, ab_im, init_re, init_im, d_skip, glu_w, glu_b, batch, seq_len):
    rows = seq_len * batch
    out_shape = (jax.ShapeDtypeStruct((rows, S5_WIDTH), F32),
                 jax.ShapeDtypeStruct((N_DIR, batch, S5_FLAT), F32),
                 jax.ShapeDtypeStruct((N_DIR, batch, S5_FLAT), F32))
    return pl.pallas_call(
        functools.partial(_s5_kernel, batch=batch, seq_len=seq_len),
        out_shape=out_shape,
        scratch_shapes=[pltpu.VMEM((S5_TC * batch, 2 * S5_FLAT), F32)],
        compiler_params=pltpu.CompilerParams(vmem_limit_bytes=VMEM_LIMIT), name="s5",
    )(u_tb, bmat, cre, cim, ab_re, ab_im, init_re, init_im, d_skip, glu_w, glu_b)


def _mlstm_kernel(q_ref, k_ref, vt_ref, mg_ref, mgt_ref, gbr_ref, gbc_ref, c0_ref, n0_ref, m0_ref,
                  tril_ref, triu_ref, ones_ref, ht_ref, cn_ref, m_ref, *, seq_len):
    tile = M_TILE
    n_chunks = seq_len // tile
    bg = q_ref.shape[0]
    cn_ref[:, :, :M_HD, :] = c0_ref[...]
    cn_ref[:, :, M_HD:, :] = n0_ref[...]
    m_ref[...] = m0_ref[...]
    ht_ref[...] = jnp.zeros_like(ht_ref)
    s_ids = lax.broadcasted_iota(jnp.int32, (tile, tile), 0)
    t_ids = lax.broadcasted_iota(jnp.int32, (tile, tile), 1)
    first_row = lax.broadcasted_iota(jnp.int32, (CN_ROWS - M_HD, tile), 0) == 0
    neg_inf = jnp.float32(-jnp.inf)
    ones = ones_ref[...]

    def direction(d, c):
        t0 = pl.multiple_of(c * tile, tile)
        col_mat = tril_ref[...] if d == 0 else triu_ref[...]
        row_mat = triu_ref[...] if d == 0 else tril_ref[...]
        valid = (s_ids <= t_ids) if d == 0 else (s_ids >= t_ids)
        gt = mgt_ref[:, :, pl.ds(t0, tile)] + gbc_ref[...]
        lft_hi, lft_lo = _split(_log_sigmoid(gt).reshape(bg * M_GATE_COLS, tile))
        b_rows = (_dot(lft_hi, row_mat) + _dot(lft_lo, row_mat)).reshape(bg, M_GATE_COLS, tile)
        totals = (_dot(lft_hi, ones) + _dot(lft_lo, ones)).reshape(bg, M_GATE_COLS, tile)
        gc = mg_ref[:, pl.ds(t0, tile), :] + gbr_ref[...]
        lfc_hi, lfc_lo = _split(_log_sigmoid(gc))
        col_b = jnp.broadcast_to(col_mat, (bg, tile, tile))
        b_cols = (jnp.einsum('bts,bsl->btl', col_b, lfc_hi, preferred_element_type=F32)
                  + jnp.einsum('bts,bsl->btl', col_b, lfc_lo, preferred_element_type=F32))
        for h in range(M_HEADS):
            j = d * M_HEADS + h
            fj = M_CHAINS + j
            rows = slice(h * M_HD, (h + 1) * M_HD)
            q_c = q_ref[:, pl.ds(t0, tile), rows]
            k_c = k_ref[:, pl.ds(t0, tile), rows]
            vt_c = vt_ref[:, rows, pl.ds(t0, tile)]
            b_t = b_rows[:, fj:fj + 1, :]
            li_t = gt[:, j:j + 1, :]
            tot = totals[:, fj:fj + 1, :]
            m_prev = m_ref[:, j]
            cn_prev = cn_ref[:, j]

            c_col = gc[:, :, j:j + 1] - b_cols[:, :, fj:fj + 1]
            cm = jnp.where(valid, c_col, neg_inf)
            inter = b_t + m_prev
            m_t = jnp.maximum(inter, b_t + jnp.max(cm, axis=1, keepdims=True))
            w_t = jnp.exp(cm + (b_t - m_t))
            s_inter = jnp.exp(inter - m_t)
            sc_t = jnp.einsum('bse,bte->bst', k_c, q_c, preferred_element_type=F32) * w_t
            num = jnp.einsum('bds,bst->bdt', _bf(vt_c), _bf(sc_t), preferred_element_type=F32)
            ext = jnp.einsum('bre,bte->brt', _bf(cn_prev), q_c, preferred_element_type=F32)
            num = num + s_inter * ext[:, :M_HD, :]
            den = jnp.sum(sc_t, axis=1, keepdims=True) + s_inter * ext[:, M_HD:M_HD + 1, :]
            ht_ref[:, rows, pl.ds(t0, tile)] += num / jnp.maximum(jnp.abs(den), jnp.exp(-m_t))

            g_row = tot - b_t + li_t
            m_new = jnp.maximum(tot + m_prev, jnp.max(g_row, axis=-1, keepdims=True))
            wk = jnp.exp(g_row - m_new)
            decay = jnp.exp(tot + m_prev - m_new)
            vw = jnp.concatenate([vt_c * wk, jnp.where(first_row, wk, 0.0)], axis=1)
            cn_ref[:, j] = (decay[:, :, :M_HD] * cn_prev
                            + jnp.einsum('brs,bse->bre', _bf(vw), k_c, preferred_element_type=F32))
            m_ref[:, j] = m_new

    def body(i, _):
        direction(0, i)
        direction(1, n_chunks - 1 - i)
        return 0

    lax.fori_loop(0, n_chunks, body, 0)


def _mlstm_call(mqk3, vt, mg3, mgt, gb_row, gb_col, c0, n0, m0, tril, triu, ones, batch, seq_len):
    bg = M_BG
    const2 = lambda g: (0, 0)
    lead3 = lambda g: (g, 0, 0)
    lead4 = lambda g: (g, 0, 0, 0)
    in_specs = [pl.BlockSpec((bg, seq_len, M_WIDTH), lead3),
                pl.BlockSpec((bg, seq_len, M_WIDTH), lambda g: (g, 0, 1)),
                pl.BlockSpec((bg, M_WIDTH, seq_len), lead3),
                pl.BlockSpec((bg, seq_len, LANES), lead3),
                pl.BlockSpec((bg, M_GATE_COLS, seq_len), lead3),
                pl.BlockSpec((1, LANES), const2),
                pl.BlockSpec((M_GATE_COLS, 1), const2),
                pl.BlockSpec((bg, M_CHAINS, M_HD, M_HD), lead4),
                pl.BlockSpec((bg, M_CHAINS, CN_ROWS - M_HD, M_HD), lead4),
                pl.BlockSpec((bg, M_CHAINS, 1, LANES), lead4),
                pl.BlockSpec((M_TILE, M_TILE), const2),
                pl.BlockSpec((M_TILE, M_TILE), const2),
                pl.BlockSpec((M_TILE, M_TILE), const2)]
    out_specs = [pl.BlockSpec((bg, M_WIDTH, seq_len), lead3),
                 pl.BlockSpec((bg, M_CHAINS, CN_ROWS, M_HD), lead4),
                 pl.BlockSpec((bg, M_CHAINS, 1, LANES), lead4)]
    out_shape = [jax.ShapeDtypeStruct((batch, M_WIDTH, seq_len), F32),
                 jax.ShapeDtypeStruct((batch, M_CHAINS, CN_ROWS, M_HD), F32),
                 jax.ShapeDtypeStruct((batch, M_CHAINS, 1, LANES), F32)]
    return pl.pallas_call(
        functools.partial(_mlstm_kernel, seq_len=seq_len),
        grid=(batch // bg,), in_specs=in_specs, out_specs=out_specs, out_shape=out_shape,
        compiler_params=_params("arbitrary"), name="mlstm",
    )(mqk3, mqk3, vt, mg3, mgt, gb_row, gb_col, c0, n0, m0, tril, triu, ones)


def _post_kernel(x_ref, att_ref, s5_ref, mh_ref, mo_ref, mod_ref, g2_ref, ng_ref, blkm_ref, wo_ref, wg_ref, wu_ref,
                 cw_ref, cb_ref, wd_ref, o_ref, x1_ref, h2_ref, acc_ref, *, seq_len):
    j = pl.program_id(1)

    @pl.when(j == 0)
    def _():
        mh = mh_ref[...]
        ml = mh * lax.rsqrt(_seg_mean_sq(mh, blkm_ref[...]) + EPS) * ng_ref[...] * _sigmoid(mo_ref[...])
        mix = (_dot(_bf(att_ref[...]), wo_ref[:ATT_WIDTH, :])
               + _dot(_bf(s5_ref[...]), wo_ref[ATT_WIDTH:ATT_WIDTH + S5_WIDTH, :])
               + _dot(_bf(ml), wo_ref[ATT_WIDTH + S5_WIDTH:, :]))
        x1 = x_ref[...] + mod_ref[2:3, :] * mix
        x1_ref[...] = x1
        ms = jnp.mean(x1 * x1, axis=-1, keepdims=True)
        xn = x1 * lax.rsqrt(ms + EPS) * g2_ref[...]
        h2_ref[...] = _bf(xn * (1.0 + mod_ref[4:5, :]) + mod_ref[3:4, :])
        acc_ref[...] = jnp.zeros_like(acc_ref)

    h2 = h2_ref[...]
    a = _dot(h2, wg_ref[...])
    tm = a.shape[0]
    pos = lax.broadcasted_iota(jnp.int32, a.shape, 0) % seq_len
    a_prev = jnp.where(pos == 0, 0.0, pltpu.roll(a, 1, axis=0))
    a_next = jnp.where(pos == seq_len - 1, 0.0, pltpu.roll(a, tm - 1, axis=0))
    ac = a_prev * cw_ref[0:1, :] + a * cw_ref[1:2, :] + a_next * cw_ref[2:3, :] + cb_ref[...]
    up = _dot(h2, wu_ref[...])
    gated = _bf(ac * _sigmoid(ac) * up)
    acc_ref[...] += _dot(gated, wd_ref[...])

    @pl.when(j == pl.num_programs(1) - 1)
    def _():
        o_ref[...] = x1_ref[...] + mod_ref[5:6, :] * acc_ref[...]


def _post_call(x2, att, s5o, mh, proj, mod, g2, ng, blkm, w_out, w_gate, w_up, conv_w, conv_b, w_down,
               seq_len, per_batch_mod):
    t = x2.shape[0]
    tm = POST_TM
    n_ff = D_FF // FF_CHUNK
    seqs_per_tile = tm // seq_len
    if per_batch_mod:
        mod_map = lambda i, j: (i * seqs_per_tile, 0, 0)
    else:
        mod_map = lambda i, j: (0, 0, 0)
    row = lambda i, j: (i, 0)
    const = lambda i, j: (0, 0)
    ff_col = lambda i, j: (0, j)
    in_specs = [pl.BlockSpec((tm, D_MODEL), row),
                pl.BlockSpec((tm, ATT_WIDTH), row),
                pl.BlockSpec((tm, S5_WIDTH), row),
                pl.BlockSpec((tm, M_WIDTH), row),
                pl.BlockSpec((tm, M_WIDTH), lambda i, j: (i, COL_MO // M_WIDTH)),
                pl.BlockSpec((None, 6, D_MODEL), mod_map),
                pl.BlockSpec((1, D_MODEL), const),
                pl.BlockSpec((1, M_WIDTH), const),
                pl.BlockSpec((M_WIDTH, M_WIDTH), const),
                pl.BlockSpec((D_MODEL, D_MODEL), const),
                pl.BlockSpec((D_MODEL, FF_CHUNK), ff_col),
                pl.BlockSpec((D_MODEL, FF_CHUNK), ff_col),
                pl.BlockSpec((3, FF_CHUNK), ff_col),
                pl.BlockSpec((1, FF_CHUNK), ff_col),
                pl.BlockSpec((FF_CHUNK, D_MODEL), lambda i, j: (j, 0))]
    return pl.pallas_call(
        functools.partial(_post_kernel, seq_len=seq_len),
        grid=(t // tm, n_ff), in_specs=in_specs,
        out_specs=pl.BlockSpec((tm, D_MODEL), row),
        out_shape=jax.ShapeDtypeStruct((t, D_MODEL), F32),
        scratch_shapes=[pltpu.VMEM((tm, D_MODEL), F32), pltpu.VMEM((tm, D_MODEL), BF16),
                        pltpu.VMEM((tm, D_MODEL), F32)],
        compiler_params=_params("arbitrary", "arbitrary"), name="post",
    )(x2, att, s5o, mh, proj, mod, g2, ng, blkm, w_out, w_gate, w_up, conv_w, conv_b, w_down)


def _trunk_layer(x2, batch, seq_len, mod, lp, ctx, rope_tabs):
    latent = ctx is not None
    proj, mg, qn, kn, mqk = _pre_call(x2, mod, lp['g1'], lp['w_main'], lp['w_gatecols'], lp['blkq'], lp['blkk'],
                                      lp['qg'], lp['kg'], rope_tabs if latent else None, seq_len, latent)
    v_new = proj[:, COL_V:COL_S5].reshape(batch, seq_len, ATT_KV_WIDTH)
    k_new = kn.reshape(batch, seq_len, ATT_KV_WIDTH)
    n_pad = CN_ROWS - M_HD
    if latent:
        ctx_k, ctx_v, s5_re0, s5_im0, c0, n0, m0 = ctx
        past = ctx_k.shape[1]
        keys = jnp.concatenate([ctx_k.reshape(batch, past, ATT_KV_WIDTH), k_new], axis=1)
        vals = jnp.concatenate([ctx_v.reshape(batch, past, ATT_KV_WIDTH), v_new], axis=1)
        init_re = s5_re0.reshape(batch, N_DIR, S5_FLAT).transpose(1, 0, 2)
        init_im = s5_im0.reshape(batch, N_DIR, S5_FLAT).transpose(1, 0, 2)
        c0 = c0.reshape(batch, M_CHAINS, M_HD, M_HD)
        n0 = jnp.zeros((batch, M_CHAINS, n_pad, M_HD), F32).at[:, :, 0, :].set(n0.reshape(batch, M_CHAINS, M_HD))
        m0 = jnp.broadcast_to(m0.reshape(batch, M_CHAINS, 1, 1), (batch, M_CHAINS, 1, LANES))
    else:
        keys, vals = k_new, v_new
        init_re = jnp.zeros((N_DIR, batch, S5_FLAT), F32)
        init_im = jnp.zeros((N_DIR, batch, S5_FLAT), F32)
        c0 = jnp.zeros((batch, M_CHAINS, M_HD, M_HD), F32)
        n0 = jnp.zeros((batch, M_CHAINS, n_pad, M_HD), F32)
        m0 = jnp.zeros((batch, M_CHAINS, 1, LANES), F32)

    att = _attn_call(qn, keys, vals, batch, seq_len)

    u_tb = (proj[:, COL_S5:COL_MQ].reshape(batch, seq_len, S5_WIDTH).transpose(1, 0, 2)
            .reshape(seq_len * batch, S5_WIDTH))
    s5_tb, fin_re, fin_im = _s5_call(u_tb, lp['s5_bmat'], lp['s5_cre'], lp['s5_cim'], lp['s5_ab_re'], lp['s5_ab_im'],
                                     init_re, init_im, lp['s5_d'], lp['s5_glu_w'], lp['s5_glu_b'], batch, seq_len)
    s5o = s5_tb.reshape(seq_len, batch, S5_WIDTH).transpose(1, 0, 2).reshape(batch * seq_len, S5_WIDTH)

    vt = proj[:, COL_MV:COL_MO].reshape(batch, seq_len, M_WIDTH).transpose(0, 2, 1)
    mgt = mg[:, :M_GATE_COLS].reshape(batch, seq_len, M_GATE_COLS).transpose(0, 2, 1)
    ht, cn_f, m_f = _mlstm_call(mqk.reshape(batch, seq_len, 2 * M_WIDTH), vt, mg.reshape(batch, seq_len, LANES), mgt,
                                lp['m_gb_row'], lp['m_gb_col'], c0, n0, m0, lp['tril'], lp['triu'], lp['ones'],
                                batch, seq_len)
    mh = ht.transpose(0, 2, 1).reshape(batch * seq_len, M_WIDTH)

    x_out = _post_call(x2, att, s5o, mh, proj, mod, lp['g2'], lp['m_ng'], lp['blkm'], lp['w_out'], lp['ffn_w_gate'],
                       lp['ffn_w_up'], lp['ffn_conv_w'], lp['ffn_conv_b'], lp['ffn_w_down'], seq_len, latent)
    states = (k_new.reshape(batch, seq_len, ATT_KV_HEADS, ATT_HD),
              v_new.reshape(batch, seq_len, ATT_KV_HEADS, ATT_HD),
              fin_re.transpose(1, 0, 2).reshape(batch, N_DIR, S5_GROUPS, S5_STATE),
              fin_im.transpose(1, 0, 2).reshape(batch, N_DIR, S5_GROUPS, S5_STATE),
              cn_f[:, :, :M_HD, :].reshape(batch, N_DIR, M_HEADS, M_HD, M_HD),
              cn_f[:, :, M_HD, :].reshape(batch, N_DIR, M_HEADS, M_HD),
              m_f[:, :, 0, 0].reshape(batch, N_DIR, M_HEADS))
    return x_out, states


def _head_block(width, head_dim):
    ids = np.arange(width) // head_dim
    return jnp.asarray((ids[:, None] == ids[None, :]).astype(np.float32) / head_dim, dtype=BF16)


def kernel(x_prompt, x_sample, c, cache_attn_k, cache_attn_v, state_s5_re, state_s5_im, state_mlstm_C, state_mlstm_n, state_mlstm_m, c_ctx, ada_w, ada_b, norm1_g, norm2_g, w_in, q_norm_g, k_norm_g, s5_a_re, s5_a_im, s5_log_dt, s5_b_re, s5_b_im, s5_c_re, s5_c_im, s5_d, s5_glu_w, s5_glu_b, m_gate_b, m_norm_g, w_out, ffn_w_gate, ffn_w_up, ffn_conv_w, ffn_conv_b, ffn_w_down):
    batch, seq = x_prompt.shape[0], x_prompt.shape[1]
    dec_batch, dec_seq = x_sample.shape[0], x_sample.shape[1]

    n_mod_rows = 2 * SUBLANES
    cvec = jnp.zeros((n_mod_rows, D_MODEL), F32).at[0].set(c_ctx).at[1:1 + dec_batch].set(c)
    mod_all = _ada_call(cvec, ada_w, ada_b).reshape(DEPTH, n_mod_rows, 6, D_MODEL)

    ab_re, ab_im, bb_re, bb_im = _s5_disc_call(s5_a_re, s5_a_im, s5_log_dt, s5_b_re, s5_b_im)
    ab_re = ab_re.reshape(DEPTH, N_DIR, 1, S5_FLAT)
    ab_im = ab_im.reshape(DEPTH, N_DIR, 1, S5_FLAT)
    bb_re = bb_re.reshape(DEPTH, N_DIR, S5_GROUPS, S5_STATE, S5_CH)
    bb_im = bb_im.reshape(DEPTH, N_DIR, S5_GROUPS, S5_STATE, S5_CH)

    rope_tabs = _rope_tables(dec_seq)
    tri = np.tril(np.ones((M_TILE, M_TILE), np.float32))
    tril = jnp.asarray(tri, dtype=BF16)
    triu = jnp.asarray(tri.T, dtype=BF16)
    ones = jnp.ones((M_TILE, M_TILE), BF16)
    blkq = _head_block(ATT_WIDTH, ATT_HD)
    blkk = _head_block(ATT_KV_WIDTH, ATT_HD)
    blkm = _head_block(M_WIDTH, M_HD)

    xp = x_prompt.reshape(batch * seq, D_MODEL)
    xs = x_sample.reshape(dec_batch * dec_seq, D_MODEL)
    ctx_out = []
    for l in range(DEPTH):
        w_in_b = _bf(w_in[l])
        gate_cols = jnp.zeros((D_MODEL, LANES), BF16).at[:, :M_GATE_COLS].set(w_in_b[:, PROJ_MAIN:])
        bmat = jnp.stack([jnp.concatenate(
            [_block_diag(jnp.swapaxes(bb_re[l, d], -1, -2)), _block_diag(jnp.swapaxes(bb_im[l, d], -1, -2))],
            axis=1) for d in range(N_DIR)])
        cre = jnp.stack([_block_diag(jnp.swapaxes(s5_c_re[l, d], -1, -2)) for d in range(N_DIR)])
        cim = jnp.stack([_block_diag(jnp.swapaxes(s5_c_im[l, d], -1, -2)) for d in range(N_DIR)])
        gb = m_gate_b[l].reshape(M_GATE_COLS)
        lp = dict(
            g1=norm1_g[l].reshape(1, D_MODEL), g2=norm2_g[l].reshape(1, D_MODEL),
            w_main=w_in_b[:, :PROJ_MAIN], w_gatecols=gate_cols, blkq=blkq, blkk=blkk, blkm=blkm,
            qg=jnp.tile(q_norm_g[l], ATT_HEADS).reshape(1, ATT_WIDTH),
            kg=jnp.tile(k_norm_g[l], ATT_KV_HEADS).reshape(1, ATT_KV_WIDTH),
            s5_bmat=_bf(bmat), s5_cre=_bf(cre), s5_cim=_bf(cim),
            s5_ab_re=jnp.broadcast_to(ab_re[l], (N_DIR, SUBLANES, S5_FLAT)),
            s5_ab_im=jnp.broadcast_to(ab_im[l], (N_DIR, SUBLANES, S5_FLAT)),
            s5_d=s5_d[l].reshape(1, S5_WIDTH), s5_glu_w=_bf(s5_glu_w[l]), s5_glu_b=s5_glu_b[l].reshape(1, S5_WIDTH),
            m_gb_row=jnp.zeros((1, LANES), F32).at[0, :M_GATE_COLS].set(gb), m_gb_col=gb.reshape(M_GATE_COLS, 1),
            m_ng=jnp.tile(m_norm_g[l], M_HEADS).reshape(1, M_WIDTH), tril=tril, triu=triu, ones=ones,
            w_out=_bf(w_out[l]), ffn_w_gate=_bf(ffn_w_gate[l]), ffn_w_up=_bf(ffn_w_up[l]),
            ffn_conv_w=ffn_conv_w[l], ffn_conv_b=ffn_conv_b[l].reshape(1, D_FF), ffn_w_down=_bf(ffn_w_down[l]))
        xp, st = _trunk_layer(xp, batch, seq, mod_all[l, 0:1], lp, None, None)
        ctx_out.append(st)
        ctx = (cache_attn_k[:, l], cache_attn_v[:, l], state_s5_re[:, l], state_s5_im[:, l],
               state_mlstm_C[:, l], state_mlstm_n[:, l], state_mlstm_m[:, l])
        xs, _ = _trunk_layer(xs, dec_batch, dec_seq, mod_all[l, 1:1 + dec_batch], lp, ctx, rope_tabs)
    outs = [jnp.stack([s[i] for s in ctx_out], axis=1) for i in range(7)]
    return (xp.reshape(batch, seq, D_MODEL), xs.reshape(dec_batch, dec_seq, D_MODEL), *outs)
```

```python
import functools

import numpy as np
import jax
import jax.numpy as jnp
from jax import lax
from jax.experimental import pallas as pl
from jax.experimental.pallas import tpu as pltpu

F32 = jnp.float32
BF16 = jnp.bfloat16

D_MODEL = 1024
DEPTH = 2
GRID_W = 64
N_DIR = 2
EPS = 1e-6
ATT_HD = 64
ATT_WIDTH = 512
ATT_HEADS = 8
ATT_KV_HEADS = 2
ATT_GROUP = ATT_HEADS // ATT_KV_HEADS
ATT_KV_WIDTH = ATT_KV_HEADS * ATT_HD
ROPE_THETA = 10000.0
S5_CH = 16
S5_STATE = 64
S5_WIDTH = 256
S5_GROUPS = 16
S5_FLAT = S5_GROUPS * S5_STATE
M_HD = 64
M_WIDTH = 256
M_HEADS = 4
M_GATE_COLS = 2 * N_DIR * M_HEADS
M_CHAINS = N_DIR * M_HEADS
D_FF = 2816
PROJ_MAIN = 2048
COL_K = ATT_WIDTH
COL_V = COL_K + ATT_KV_WIDTH
COL_S5 = COL_V + ATT_KV_WIDTH
COL_MQ = COL_S5 + S5_WIDTH
COL_MK = COL_MQ + M_WIDTH
COL_MV = COL_MK + M_WIDTH
COL_MO = COL_MV + M_WIDTH
LANES = 128
SUBLANES = 8
BF16_ROWS = 16
VMEM_LIMIT = 56 * 1024 * 1024

PRE_TM = 256
ATT_TQ = 256
S5_TC = 64
M_TILE = 128
M_BG = 4
CN_ROWS = M_HD + BF16_ROWS
POST_TM = 1024
FF_CHUNK = 256


def _bf(x):
    return x.astype(BF16)


def _dot(a, b):
    return jnp.dot(a, b, preferred_element_type=F32)


def _split(x):
    hi = _bf(x)
    lo = _bf(x - hi.astype(F32))
    return hi, lo


def _seg_mean_sq(x, blk):
    hi, lo = _split(x * x)
    return _dot(hi, blk) + _dot(lo, blk)


def _sigmoid(x):
    return 1.0 / (1.0 + jnp.exp(-x))


def _log_sigmoid(x):
    return -(jnp.maximum(-x, 0.0) + jnp.log1p(jnp.exp(-jnp.abs(x))))


def _params(*sem):
    return pltpu.CompilerParams(dimension_semantics=sem, vmem_limit_bytes=VMEM_LIMIT)


def _ada_kernel(c_ref, w_ref, b_ref, o_ref):
    c = c_ref[...]
    s = c * _sigmoid(c)
    o_ref[...] = _dot(_bf(s), _bf(w_ref[...])) + b_ref[...]


def _ada_call(cvec, ada_w, ada_b):
    rows = cvec.shape[0]
    tn = 1024
    n = ada_w.shape[-1]
    return pl.pallas_call(
        _ada_kernel,
        grid=(DEPTH, n // tn),
        in_specs=[pl.BlockSpec((rows, D_MODEL), lambda l, j: (0, 0)),
                  pl.BlockSpec((None, D_MODEL, tn), lambda l, j: (l, 0, j)),
                  pl.BlockSpec((None, 1, tn), lambda l, j: (l, 0, j))],
        out_specs=pl.BlockSpec((None, rows, tn), lambda l, j: (l, 0, j)),
        out_shape=jax.ShapeDtypeStruct((DEPTH, rows, n), F32),
        compiler_params=_params("arbitrary", "arbitrary"),
        name="adaln",
    )(cvec, ada_w, ada_b.reshape(DEPTH, 1, n))


def _s5_disc_kernel(are_ref, aim_ref, ldt_ref, arex_ref, aimx_ref, bre_ref, bim_ref,
                    abr_ref, abi_ref, bbr_ref, bbi_ref):
    dt = jnp.exp(ldt_ref[...])

    def disc(a_re, a_im):
        mag = jnp.exp(dt * a_re)
        ab_re = mag * jnp.cos(dt * a_im)
        ab_im = mag * jnp.sin(dt * a_im)
        den = a_re * a_re + a_im * a_im
        nr = ab_re - 1.0
        ni = ab_im
        f_re = (nr * a_re + ni * a_im) / den
        f_im = (ni * a_re - nr * a_im) / den
        return ab_re, ab_im, f_re, f_im

    ab_re, ab_im, _, _ = disc(are_ref[...], aim_ref[...])
    abr_ref[...] = ab_re
    abi_ref[...] = ab_im
    _, _, f_re, f_im = disc(arex_ref[...], aimx_ref[...])
    b_re = bre_ref[...]
    b_im = bim_ref[...]
    bbr_ref[...] = f_re * b_re - f_im * b_im
    bbi_ref[...] = f_re * b_im + f_im * b_re


def _s5_disc_call(a_re, a_im, log_dt, b_re, b_im):
    r = DEPTH * N_DIR * S5_GROUPS
    a_re2 = a_re.reshape(r, S5_STATE)
    a_im2 = a_im.reshape(r, S5_STATE)
    wide = S5_STATE * S5_CH
    out_shape = (jax.ShapeDtypeStruct((r, S5_STATE), F32), jax.ShapeDtypeStruct((r, S5_STATE), F32),
                 jax.ShapeDtypeStruct((r, wide), F32), jax.ShapeDtypeStruct((r, wide), F32))
    return pl.pallas_call(_s5_disc_kernel, out_shape=out_shape, name="s5_disc")(
        a_re2, a_im2, log_dt.reshape(r, 1),
        jnp.repeat(a_re2, S5_CH, axis=1), jnp.repeat(a_im2, S5_CH, axis=1),
        b_re.reshape(r, wide), b_im.reshape(r, wide))


def _block_diag(blocks):
    g, r, c = blocks.shape
    eye = jnp.eye(g, dtype=blocks.dtype)
    return jnp.einsum('grc,gh->grhc', blocks, eye).reshape(g * r, g * c)


def _rope(x, cos, sin_signed, second):
    width = x.shape[-1]
    quarter = ATT_HD // 4
    partner = jnp.where(second, pltpu.roll(x, quarter, axis=1), pltpu.roll(x, width - quarter, axis=1))
    return x * cos + partner * sin_signed


def _pre_kernel(*refs, rope):
    if rope:
        (x_ref, mod_ref, g1_ref, w_ref, wg_ref, blkq_ref, blkk_ref, qg_ref, kg_ref,
         cosq_ref, sinq_ref, cosk_ref, sink_ref, proj_ref, mg_ref, qn_ref, kn_ref, mqk_ref, vt_ref, mgt_ref) = refs
    else:
        (x_ref, mod_ref, g1_ref, w_ref, wg_ref, blkq_ref, blkk_ref, qg_ref, kg_ref,
         proj_ref, mg_ref, qn_ref, kn_ref, mqk_ref, vt_ref, mgt_ref) = refs
    x = x_ref[...]
    ms = jnp.mean(x * x, axis=-1, keepdims=True)
    xn = x * lax.rsqrt(ms + EPS) * g1_ref[...]
    h = _bf(xn * (1.0 + mod_ref[1:2, :]) + mod_ref[0:1, :])
    proj = _dot(h, w_ref[...])
    proj_ref[...] = proj
    mg = _dot(h, wg_ref[...])
    mg_ref[...] = mg
    mqk_ref[:, :M_WIDTH] = _bf(proj[:, COL_MQ:COL_MK])
    mqk_ref[:, M_WIDTH:] = _bf(proj[:, COL_MK:COL_MV] * (M_HD ** -0.5))
    vt_ref[...] = proj[:, COL_MV:COL_MO].T
    mgt_ref[...] = mg.T[:M_GATE_COLS, :]
    q = proj[:, :COL_K]
    k = proj[:, COL_K:COL_V]
    qn = q * lax.rsqrt(_seg_mean_sq(q, blkq_ref[...]) + EPS) * qg_ref[...]
    kn = k * lax.rsqrt(_seg_mean_sq(k, blkk_ref[...]) + EPS) * kg_ref[...]
    if rope:
        lane_q = lax.broadcasted_iota(jnp.int32, qn.shape, 1)
        lane_k = lax.broadcasted_iota(jnp.int32, kn.shape, 1)
        qn = _rope(qn, cosq_ref[...], sinq_ref[...], (lane_q & (ATT_HD // 4)) != 0)
        kn = _rope(kn, cosk_ref[...], sink_ref[...], (lane_k & (ATT_HD // 4)) != 0)
    qn_ref[...] = qn
    kn_ref[...] = kn


def _rope_tables(seq_len):
    n_rows = seq_len // GRID_W
    row = jnp.repeat(jnp.arange(n_rows), GRID_W)
    col = jnp.tile(jnp.arange(GRID_W), n_rows)
    half = ATT_HD // 2
    inv_freq = 1.0 / (ROPE_THETA ** (jnp.arange(0, half, 2, dtype=F32) / half))

    def tables(pos):
        ang = pos.astype(F32)[:, None] * inv_freq[None, :]
        cos = jnp.cos(ang)
        sin = jnp.sin(ang)
        return jnp.concatenate([cos, cos], axis=-1), jnp.concatenate([-sin, sin], axis=-1)

    cr, sr = tables(row)
    cc, sc = tables(col)
    cos = jnp.concatenate([cr, cc], axis=-1)
    sin = jnp.concatenate([sr, sc], axis=-1)
    return (jnp.tile(cos, (1, ATT_HEADS)), jnp.tile(sin, (1, ATT_HEADS)),
            jnp.tile(cos, (1, ATT_KV_HEADS)), jnp.tile(sin, (1, ATT_KV_HEADS)))


def _pre_call(x2, mod, g1, w_main, w_gate, blkq, blkk, qg, kg, rope_tabs, seq_len, per_batch_mod):
    t = x2.shape[0]
    tm = PRE_TM
    tiles_per_seq = seq_len // tm
    rope = rope_tabs is not None
    if per_batch_mod:
        mod_map = lambda i: (i // tiles_per_seq, 0, 0)
    else:
        mod_map = lambda i: (0, 0, 0)
    const = lambda i: (0, 0)
    row = lambda i: (i, 0)
    in_specs = [pl.BlockSpec((tm, D_MODEL), row),
                pl.BlockSpec((None, 6, D_MODEL), mod_map),
                pl.BlockSpec((1, D_MODEL), const),
                pl.BlockSpec((D_MODEL, PROJ_MAIN), const),
                pl.BlockSpec((D_MODEL, LANES), const),
                pl.BlockSpec((ATT_WIDTH, ATT_WIDTH), const),
                pl.BlockSpec((ATT_KV_WIDTH, ATT_KV_WIDTH), const),
                pl.BlockSpec((1, ATT_WIDTH), const),
                pl.BlockSpec((1, ATT_KV_WIDTH), const)]
    args = [x2, mod, g1, w_main, w_gate, blkq, blkk, qg, kg]
    if rope:
        pos_map = lambda i: (i % tiles_per_seq, 0)
        in_specs += [pl.BlockSpec((tm, ATT_WIDTH), pos_map), pl.BlockSpec((tm, ATT_WIDTH), pos_map),
                     pl.BlockSpec((tm, ATT_KV_WIDTH), pos_map), pl.BlockSpec((tm, ATT_KV_WIDTH), pos_map)]
        args += list(rope_tabs)
    time_on_lanes = lambda i: (i // tiles_per_seq, 0, i % tiles_per_seq)
    batch = t // seq_len
    out_specs = [pl.BlockSpec((tm, PROJ_MAIN), row),
                 pl.BlockSpec((tm, LANES), row),
                 pl.BlockSpec((tm, ATT_WIDTH), row),
                 pl.BlockSpec((tm, ATT_KV_WIDTH), row),
                 pl.BlockSpec((tm, 2 * M_WIDTH), row),
                 pl.BlockSpec((None, M_WIDTH, tm), time_on_lanes),
                 pl.BlockSpec((None, M_GATE_COLS, tm), time_on_lanes)]
    out_shape = [jax.ShapeDtypeStruct((t, PROJ_MAIN), F32), jax.ShapeDtypeStruct((t, LANES), F32),
                 jax.ShapeDtypeStruct((t, ATT_WIDTH), F32), jax.ShapeDtypeStruct((t, ATT_KV_WIDTH), F32),
                 jax.ShapeDtypeStruct((t, 2 * M_WIDTH), BF16),
                 jax.ShapeDtypeStruct((batch, M_WIDTH, seq_len), F32),
                 jax.ShapeDtypeStruct((batch, M_GATE_COLS, seq_len), F32)]
    return pl.pallas_call(
        functools.partial(_pre_kernel, rope=rope),
        grid=(t // tm,), in_specs=in_specs, out_specs=out_specs, out_shape=out_shape,
        compiler_params=_params("arbitrary"), name="pre_rope" if rope else "pre",
    )(*args)


def _attn_kernel(*refs, cached):
    if cached:
        q_ref, k_ref, v_ref, ck_ref, cv_ref, o_ref = refs
        k_all = _bf(jnp.concatenate([ck_ref[...], k_ref[...]], axis=0))
        v_all = _bf(jnp.concatenate([cv_ref[...], v_ref[...]], axis=0))
    else:
        q_ref, k_ref, v_ref, o_ref = refs
        k_all = _bf(k_ref[...])
        v_all = _bf(v_ref[...])
    scale = ATT_HD ** -0.5
    outs = []
    for kvh in range(ATT_KV_HEADS):
        lo = kvh * ATT_HD
        kb = k_all[:, lo:lo + ATT_HD]
        vb = v_all[:, lo:lo + ATT_HD]
        for g in range(ATT_GROUP):
            c0 = (kvh * ATT_GROUP + g) * ATT_HD
            qb = _bf(q_ref[:, c0:c0 + ATT_HD] * scale)
            s = lax.dot_general(qb, kb, (((1,), (1,)), ((), ())), preferred_element_type=F32)
            m = jnp.max(s, axis=-1, keepdims=True)
            e = jnp.exp(s - m)
            den = jnp.sum(e, axis=-1, keepdims=True)
            outs.append(_dot(_bf(e), vb) / den)
    o_ref[...] = jnp.concatenate(outs, axis=-1)


def _attn_call(qn, kn, proj, cache_k, cache_v, batch, lq):
    tq = ATT_TQ
    nq = lq // tq
    cached = cache_k is not None
    in_specs = [pl.BlockSpec((tq, ATT_WIDTH), lambda b, i: (b * nq + i, 0)),
                pl.BlockSpec((lq, ATT_KV_WIDTH), lambda b, i: (b, 0)),
                pl.BlockSpec((lq, ATT_KV_WIDTH), lambda b, i: (b, COL_V // ATT_KV_WIDTH))]
    args = [qn, kn, proj]
    if cached:
        past = cache_k.shape[1]
        in_specs += [pl.BlockSpec((None, past, ATT_KV_WIDTH), lambda b, i: (b, 0, 0)),
                     pl.BlockSpec((None, past, ATT_KV_WIDTH), lambda b, i: (b, 0, 0))]
        args += [cache_k, cache_v]
    return pl.pallas_call(
        functools.partial(_attn_kernel, cached=cached),
        grid=(batch, nq), in_specs=in_specs,
        out_specs=pl.BlockSpec((tq, ATT_WIDTH), lambda b, i: (b * nq + i, 0)),
        out_shape=jax.ShapeDtypeStruct((batch * lq, ATT_WIDTH), F32),
        compiler_params=_params("arbitrary", "arbitrary"), name="attn_cached" if cached else "attn",
    )(*args)


def _s5_kernel(u_ref, bmat_ref, cre_ref, cim_ref, ar_ref, ai_ref, ir_ref, ii_ref, d_ref, gw_ref, gb_ref,
               out_ref, fr_ref, fi_ref, utb_ref, ytb_ref, xs_ref, *, batch, seq_len):
    tc = S5_TC
    rows = tc * batch
    n_chunks = seq_len // tc
    n_sub = batch // SUBLANES
    halves = S5_WIDTH // LANES

    for b in range(batch):
        for hf in range(halves):
            utb_ref[hf, pl.ds(b, seq_len, stride=batch), :] = (
                u_ref[b * seq_len:(b + 1) * seq_len, hf * LANES:(hf + 1) * LANES])
    for hf in range(halves):
        ytb_ref[hf] = utb_ref[hf] * d_ref[:, hf * LANES:(hf + 1) * LANES]

    for d in range(N_DIR):
        a_re = ar_ref[d]
        a_im = ai_ref[d]

        def chunk_body(ci, carry, d=d, a_re=a_re, a_im=a_im):
            c = ci if d == 0 else n_chunks - 1 - ci
            r0 = pl.multiple_of(c * rows, rows)
            u_c = jnp.concatenate([utb_ref[hf, pl.ds(r0, rows), :] for hf in range(halves)], axis=-1)
            xs_ref[...] = _dot(_bf(u_c), bmat_ref[d])
            new_carry = []
            for sub in range(n_sub):
                def step(t, st, sub=sub):
                    s_re, s_im = st
                    tt = t if d == 0 else tc - 1 - t
                    r = pl.multiple_of(tt * batch + sub * SUBLANES, SUBLANES)
                    x_re = xs_ref[pl.ds(r, SUBLANES), :S5_FLAT]
                    x_im = xs_ref[pl.ds(r, SUBLANES), S5_FLAT:]
                    n_re = a_re * s_re - a_im * s_im + x_re
                    n_im = a_re * s_im + a_im * s_re + x_im
                    xs_ref[pl.ds(r, SUBLANES), :S5_FLAT] = n_re
                    xs_ref[pl.ds(r, SUBLANES), S5_FLAT:] = n_im
                    return n_re, n_im

                new_carry.append(lax.fori_loop(0, tc, step, carry[sub], unroll=2))
            y = (_dot(_bf(xs_ref[:, :S5_FLAT]), cre_ref[d]) - _dot(_bf(xs_ref[:, S5_FLAT:]), cim_ref[d]))
            for hf in range(halves):
                ytb_ref[hf, pl.ds(r0, rows), :] += y[:, hf * LANES:(hf + 1) * LANES]
            return tuple(new_carry)

        init = tuple((ir_ref[d, sub * SUBLANES:(sub + 1) * SUBLANES, :],
                      ii_ref[d, sub * SUBLANES:(sub + 1) * SUBLANES, :]) for sub in range(n_sub))
        fin = lax.fori_loop(0, n_chunks, chunk_body, init)
        for sub in range(n_sub):
            fr_ref[d, sub * SUBLANES:(sub + 1) * SUBLANES, :] = fin[sub][0]
            fi_ref[d, sub * SUBLANES:(sub + 1) * SUBLANES, :] = fin[sub][1]

    for b in range(batch):
        z = jax.nn.gelu(jnp.concatenate(
            [ytb_ref[hf, pl.ds(b, seq_len, stride=batch), :] for hf in range(halves)], axis=-1))
        gate = _sigmoid(_dot(_bf(z), gw_ref[...]) + gb_ref[...])
        out_ref[b * seq_len:(b + 1) * seq_len, :] = z * gate


def _s5_call(proj, bmat, cre, cim, ab_re, ab_im, init_re, init_im, d_skip, glu_w, glu_b, batch, seq_len):
    rows = seq_len * batch
    single = pl.Buffered(1)
    whole = lambda shape: pl.BlockSpec(shape, lambda i: (0,) * len(shape))
    in_specs = [pl.BlockSpec((rows, S5_WIDTH), lambda i: (0, COL_S5 // S5_WIDTH), pipeline_mode=single),
                whole(bmat.shape), whole(cre.shape), whole(cim.shape), whole(ab_re.shape), whole(ab_im.shape),
                whole(init_re.shape), whole(init_im.shape), whole(d_skip.shape), whole(glu_w.shape),
                whole(glu_b.shape)]
    out_specs = [pl.BlockSpec((rows, S5_WIDTH), lambda i: (0, 0), pipeline_mode=single),
                 whole((N_DIR, batch, S5_FLAT)), whole((N_DIR, batch, S5_FLAT))]
    out_shape = (jax.ShapeDtypeStruct((rows, S5_WIDTH), F32),
                 jax.ShapeDtypeStruct((N_DIR, batch, S5_FLAT), F32),
                 jax.ShapeDtypeStruct((N_DIR, batch, S5_FLAT), F32))
    return pl.pallas_call(
        functools.partial(_s5_kernel, batch=batch, seq_len=seq_len),
        grid=(1,), in_specs=in_specs, out_specs=out_specs, out_shape=out_shape,
        scratch_shapes=[pltpu.VMEM((S5_WIDTH // LANES, rows, LANES), F32),
                        pltpu.VMEM((S5_WIDTH // LANES, rows, LANES), F32),
                        pltpu.VMEM((S5_TC * batch, 2 * S5_FLAT), F32)],
        compiler_params=_params("arbitrary"), name="s5",
    )(proj, bmat, cre, cim, ab_re, ab_im, init_re, init_im, d_skip, glu_w, glu_b)


def _mlstm_kernel(q_ref, k_ref, vt_ref, mg_ref, mgt_ref, gbr_ref, gbc_ref, c0_ref, n0_ref, m0_ref,
                  tril_ref, triu_ref, ones_ref, ht_ref, cn_ref, m_ref, *, seq_len):
    tile = M_TILE
    n_chunks = seq_len // tile
    bg = q_ref.shape[0]
    cn_ref[:, :, :M_HD, :] = c0_ref[...]
    cn_ref[:, :, M_HD:, :] = n0_ref[...]
    m_ref[...] = m0_ref[...]
    ht_ref[...] = jnp.zeros_like(ht_ref)
    s_ids = lax.broadcasted_iota(jnp.int32, (tile, tile), 0)
    t_ids = lax.broadcasted_iota(jnp.int32, (tile, tile), 1)
    first_row = lax.broadcasted_iota(jnp.int32, (CN_ROWS - M_HD, tile), 0) == 0
    neg_inf = jnp.float32(-jnp.inf)
    ones = ones_ref[...]

    def direction(d, c):
        t0 = pl.multiple_of(c * tile, tile)
        col_mat = tril_ref[...] if d == 0 else triu_ref[...]
        row_mat = triu_ref[...] if d == 0 else tril_ref[...]
        valid = (s_ids <= t_ids) if d == 0 else (s_ids >= t_ids)
        gt = mgt_ref[:, :, pl.ds(t0, tile)] + gbc_ref[...]
        lft_hi, lft_lo = _split(_log_sigmoid(gt).reshape(bg * M_GATE_COLS, tile))
        b_rows = (_dot(lft_hi, row_mat) + _dot(lft_lo, row_mat)).reshape(bg, M_GATE_COLS, tile)
        totals = (_dot(lft_hi, ones) + _dot(lft_lo, ones)).reshape(bg, M_GATE_COLS, tile)
        gc = mg_ref[:, pl.ds(t0, tile), :] + gbr_ref[...]
        lfc_hi, lfc_lo = _split(_log_sigmoid(gc))
        col_b = jnp.broadcast_to(col_mat, (bg, tile, tile))
        b_cols = (jnp.einsum('bts,bsl->btl', col_b, lfc_hi, preferred_element_type=F32)
                  + jnp.einsum('bts,bsl->btl', col_b, lfc_lo, preferred_element_type=F32))
        for h in range(M_HEADS):
            j = d * M_HEADS + h
            fj = M_CHAINS + j
            rows = slice(h * M_HD, (h + 1) * M_HD)
            q_c = q_ref[:, pl.ds(t0, tile), rows]
            k_c = k_ref[:, pl.ds(t0, tile), rows]
            vt_c = vt_ref[:, rows, pl.ds(t0, tile)]
            b_t = b_rows[:, fj:fj + 1, :]
            li_t = gt[:, j:j + 1, :]
            tot = totals[:, fj:fj + 1, :]
            m_prev = m_ref[:, j]
            cn_prev = cn_ref[:, j]

            c_col = gc[:, :, j:j + 1] - b_cols[:, :, fj:fj + 1]
            cm = jnp.where(valid, c_col, neg_inf)
            inter = b_t + m_prev
            m_t = jnp.maximum(inter, b_t + jnp.max(cm, axis=1, keepdims=True))
            w_t = jnp.exp(cm + (b_t - m_t))
            s_inter = jnp.exp(inter - m_t)
            sc_t = jnp.einsum('bse,bte->bst', k_c, q_c, preferred_element_type=F32) * w_t
            num = jnp.einsum('bds,bst->bdt', _bf(vt_c), _bf(sc_t), preferred_element_type=F32)
            ext = jnp.einsum('bre,bte->brt', _bf(cn_prev), q_c, preferred_element_type=F32)
            num = num + s_inter * ext[:, :M_HD, :]
            den = jnp.sum(sc_t, axis=1, keepdims=True) + s_inter * ext[:, M_HD:M_HD + 1, :]
            ht_ref[:, rows, pl.ds(t0, tile)] += num / jnp.maximum(jnp.abs(den), jnp.exp(-m_t))

            g_row = tot - b_t + li_t
            m_new = jnp.maximum(tot + m_prev, jnp.max(g_row, axis=-1, keepdims=True))
            wk = jnp.exp(g_row - m_new)
            decay = jnp.exp(tot + m_prev - m_new)
            vw = jnp.concatenate([vt_c * wk, jnp.where(first_row, wk, 0.0)], axis=1)
            cn_ref[:, j] = (decay[:, :, :M_HD] * cn_prev
                            + jnp.einsum('brs,bse->bre', _bf(vw), k_c, preferred_element_type=F32))
            m_ref[:, j] = m_new

    def body(i, _):
        direction(0, i)
        direction(1, n_chunks - 1 - i)
        return 0

    lax.fori_loop(0, n_chunks, body, 0)


def _mlstm_call(mqk3, vt, mg3, mgt, gb_row, gb_col, c0, n0, m0, tril, triu, ones, batch, seq_len):
    bg = M_BG
    const2 = lambda g: (0, 0)
    lead3 = lambda g: (g, 0, 0)
    lead4 = lambda g: (g, 0, 0, 0)
    in_specs = [pl.BlockSpec((bg, seq_len, M_WIDTH), lead3),
                pl.BlockSpec((bg, seq_len, M_WIDTH), lambda g: (g, 0, 1)),
                pl.BlockSpec((bg, M_WIDTH, seq_len), lead3),
                pl.BlockSpec((bg, seq_len, LANES), lead3),
                pl.BlockSpec((bg, M_GATE_COLS, seq_len), lead3),
                pl.BlockSpec((1, LANES), const2),
                pl.BlockSpec((M_GATE_COLS, 1), const2),
                pl.BlockSpec((bg, M_CHAINS, M_HD, M_HD), lead4),
                pl.BlockSpec((bg, M_CHAINS, CN_ROWS - M_HD, M_HD), lead4),
                pl.BlockSpec((bg, M_CHAINS, 1, LANES), lead4),
                pl.BlockSpec((M_TILE, M_TILE), const2),
                pl.BlockSpec((M_TILE, M_TILE), const2),
                pl.BlockSpec((M_TILE, M_TILE), const2)]
    out_specs = [pl.BlockSpec((bg, M_WIDTH, seq_len), lead3),
                 pl.BlockSpec((bg, M_CHAINS, CN_ROWS, M_HD), lead4),
                 pl.BlockSpec((bg, M_CHAINS, 1, LANES), lead4)]
    out_shape = [jax.ShapeDtypeStruct((batch, M_WIDTH, seq_len), F32),
                 jax.ShapeDtypeStruct((batch, M_CHAINS, CN_ROWS, M_HD), F32),
                 jax.ShapeDtypeStruct((batch, M_CHAINS, 1, LANES), F32)]
    return pl.pallas_call(
        functools.partial(_mlstm_kernel, seq_len=seq_len),
        grid=(batch // bg,), in_specs=in_specs, out_specs=out_specs, out_shape=out_shape,
        compiler_params=_params("arbitrary"), name="mlstm",
    )(mqk3, mqk3, vt, mg3, mgt, gb_row, gb_col, c0, n0, m0, tril, triu, ones)


def _post_kernel(x_ref, att_ref, s5_ref, ht_ref, mo_ref, mod_ref, g2_ref, ng_ref, blkm_ref, wo_ref, wg_ref, wu_ref,
                 cw_ref, cb_ref, wd_ref, o_ref, x1_ref, h2_ref, acc_ref, *, seq_len):
    j = pl.program_id(1)

    @pl.when(j == 0)
    def _():
        mh = jnp.concatenate([ht_ref[s].T for s in range(ht_ref.shape[0])], axis=0)
        ml = mh * lax.rsqrt(_seg_mean_sq(mh, blkm_ref[...]) + EPS) * ng_ref[...] * _sigmoid(mo_ref[...])
        mix = (_dot(_bf(att_ref[...]), wo_ref[:ATT_WIDTH, :])
               + _dot(_bf(s5_ref[...]), wo_ref[ATT_WIDTH:ATT_WIDTH + S5_WIDTH, :])
               + _dot(_bf(ml), wo_ref[ATT_WIDTH + S5_WIDTH:, :]))
        x1 = x_ref[...] + mod_ref[2:3, :] * mix
        x1_ref[...] = x1
        ms = jnp.mean(x1 * x1, axis=-1, keepdims=True)
        xn = x1 * lax.rsqrt(ms + EPS) * g2_ref[...]
        h2_ref[...] = _bf(xn * (1.0 + mod_ref[4:5, :]) + mod_ref[3:4, :])
        acc_ref[...] = jnp.zeros_like(acc_ref)

    h2 = h2_ref[...]
    a = _dot(h2, wg_ref[...])
    tm = a.shape[0]
    pos = lax.broadcasted_iota(jnp.int32, a.shape, 0) % seq_len
    a_prev = jnp.where(pos == 0, 0.0, pltpu.roll(a, 1, axis=0))
    a_next = jnp.where(pos == seq_len - 1, 0.0, pltpu.roll(a, tm - 1, axis=0))
    ac = a_prev * cw_ref[0:1, :] + a * cw_ref[1:2, :] + a_next * cw_ref[2:3, :] + cb_ref[...]
    up = _dot(h2, wu_ref[...])
    gated = _bf(ac * _sigmoid(ac) * up)
    acc_ref[...] += _dot(gated, wd_ref[...])

    @pl.when(j == pl.num_programs(1) - 1)
    def _():
        o_ref[...] = x1_ref[...] + mod_ref[5:6, :] * acc_ref[...]


def _post_call(x2, att, s5o, ht, proj, mod, g2, ng, blkm, w_out, w_gate, w_up, conv_w, conv_b, w_down,
               seq_len, per_batch_mod):
    t = x2.shape[0]
    tm = POST_TM
    n_ff = D_FF // FF_CHUNK
    seqs_per_tile = tm // seq_len
    if per_batch_mod:
        mod_map = lambda i, j: (i * seqs_per_tile, 0, 0)
    else:
        mod_map = lambda i, j: (0, 0, 0)
    row = lambda i, j: (i, 0)
    const = lambda i, j: (0, 0)
    ff_col = lambda i, j: (0, j)
    in_specs = [pl.BlockSpec((tm, D_MODEL), row),
                pl.BlockSpec((tm, ATT_WIDTH), row),
                pl.BlockSpec((tm, S5_WIDTH), row),
                pl.BlockSpec((seqs_per_tile, M_WIDTH, seq_len), lambda i, j: (i, 0, 0)),
                pl.BlockSpec((tm, M_WIDTH), lambda i, j: (i, COL_MO // M_WIDTH)),
                pl.BlockSpec((None, 6, D_MODEL), mod_map),
                pl.BlockSpec((1, D_MODEL), const),
                pl.BlockSpec((1, M_WIDTH), const),
                pl.BlockSpec((M_WIDTH, M_WIDTH), const),
                pl.BlockSpec((D_MODEL, D_MODEL), const),
                pl.BlockSpec((D_MODEL, FF_CHUNK), ff_col),
                pl.BlockSpec((D_MODEL, FF_CHUNK), ff_col),
                pl.BlockSpec((3, FF_CHUNK), ff_col),
                pl.BlockSpec((1, FF_CHUNK), ff_col),
                pl.BlockSpec((FF_CHUNK, D_MODEL), lambda i, j: (j, 0))]
    return pl.pallas_call(
        functools.partial(_post_kernel, seq_len=seq_len),
        grid=(t // tm, n_ff), in_specs=in_specs,
        out_specs=pl.BlockSpec((tm, D_MODEL), row),
        out_shape=jax.ShapeDtypeStruct((t, D_MODEL), F32),
        scratch_shapes=[pltpu.VMEM((tm, D_MODEL), F32), pltpu.VMEM((tm, D_MODEL), BF16),
                        pltpu.VMEM((tm, D_MODEL), F32)],
        compiler_params=_params("arbitrary", "arbitrary"), name="post",
    )(x2, att, s5o, ht, proj, mod, g2, ng, blkm, w_out, w_gate, w_up, conv_w, conv_b, w_down)


def _trunk_layer(x2, batch, seq_len, mod, lp, ctx, rope_tabs):
    latent = ctx is not None
    proj, mg, qn, kn, mqk, vt, mgt = _pre_call(x2, mod, lp['g1'], lp['w_main'], lp['w_gatecols'], lp['blkq'],
                                               lp['blkk'], lp['qg'], lp['kg'], rope_tabs if latent else None,
                                               seq_len, latent)
    n_pad = CN_ROWS - M_HD
    if latent:
        ctx_k, ctx_v, s5_re0, s5_im0, c0, n0, m0 = ctx
        past = ctx_k.shape[1]
        cache_k = ctx_k.reshape(batch, past, ATT_KV_WIDTH)
        cache_v = ctx_v.reshape(batch, past, ATT_KV_WIDTH)
        init_re = s5_re0.reshape(batch, N_DIR, S5_FLAT).transpose(1, 0, 2)
        init_im = s5_im0.reshape(batch, N_DIR, S5_FLAT).transpose(1, 0, 2)
        c0 = c0.reshape(batch, M_CHAINS, M_HD, M_HD)
        n0 = jnp.zeros((batch, M_CHAINS, n_pad, M_HD), F32).at[:, :, 0, :].set(n0.reshape(batch, M_CHAINS, M_HD))
        m0 = jnp.broadcast_to(m0.reshape(batch, M_CHAINS, 1, 1), (batch, M_CHAINS, 1, LANES))
    else:
        cache_k = cache_v = None
        init_re = jnp.zeros((N_DIR, batch, S5_FLAT), F32)
        init_im = jnp.zeros((N_DIR, batch, S5_FLAT), F32)
        c0 = jnp.zeros((batch, M_CHAINS, M_HD, M_HD), F32)
        n0 = jnp.zeros((batch, M_CHAINS, n_pad, M_HD), F32)
        m0 = jnp.zeros((batch, M_CHAINS, 1, LANES), F32)

    att = _attn_call(qn, kn, proj, cache_k, cache_v, batch, seq_len)

    s5o, fin_re, fin_im = _s5_call(proj, lp['s5_bmat'], lp['s5_cre'], lp['s5_cim'], lp['s5_ab_re'], lp['s5_ab_im'],
                                   init_re, init_im, lp['s5_d'], lp['s5_glu_w'], lp['s5_glu_b'], batch, seq_len)

    ht, cn_f, m_f = _mlstm_call(mqk.reshape(batch, seq_len, 2 * M_WIDTH), vt, mg.reshape(batch, seq_len, LANES), mgt,
                                lp['m_gb_row'], lp['m_gb_col'], c0, n0, m0, lp['tril'], lp['triu'], lp['ones'],
                                batch, seq_len)

    x_out = _post_call(x2, att, s5o, ht, proj, mod, lp['g2'], lp['m_ng'], lp['blkm'], lp['w_out'], lp['ffn_w_gate'],
                       lp['ffn_w_up'], lp['ffn_conv_w'], lp['ffn_conv_b'], lp['ffn_w_down'], seq_len, latent)
    if latent:
        return x_out, None
    states = (kn.reshape(batch, seq_len, ATT_KV_HEADS, ATT_HD),
              proj[:, COL_V:COL_S5].reshape(batch, seq_len, ATT_KV_HEADS, ATT_HD),
              fin_re.transpose(1, 0, 2).reshape(batch, N_DIR, S5_GROUPS, S5_STATE),
              fin_im.transpose(1, 0, 2).reshape(batch, N_DIR, S5_GROUPS, S5_STATE),
              cn_f[:, :, :M_HD, :].reshape(batch, N_DIR, M_HEADS, M_HD, M_HD),
              cn_f[:, :, M_HD, :].reshape(batch, N_DIR, M_HEADS, M_HD),
              m_f[:, :, 0, 0].reshape(batch, N_DIR, M_HEADS))
    return x_out, states


def _head_block(width, head_dim):
    ids = np.arange(width) // head_dim
    return jnp.asarray((ids[:, None] == ids[None, :]).astype(np.float32) / head_dim, dtype=BF16)


def kernel(x_prompt, x_sample, c, cache_attn_k, cache_attn_v, state_s5_re, state_s5_im, state_mlstm_C, state_mlstm_n, state_mlstm_m, c_ctx, ada_w, ada_b, norm1_g, norm2_g, w_in, q_norm_g, k_norm_g, s5_a_re, s5_a_im, s5_log_dt, s5_b_re, s5_b_im, s5_c_re, s5_c_im, s5_d, s5_glu_w, s5_glu_b, m_gate_b, m_norm_g, w_out, ffn_w_gate, ffn_w_up, ffn_conv_w, ffn_conv_b, ffn_w_down):
    batch, seq = x_prompt.shape[0], x_prompt.shape[1]
    dec_batch, dec_seq = x_sample.shape[0], x_sample.shape[1]

    n_mod_rows = 2 * SUBLANES
    cvec = jnp.zeros((n_mod_rows, D_MODEL), F32).at[0].set(c_ctx).at[1:1 + dec_batch].set(c)
    mod_all = _ada_call(cvec, ada_w, ada_b).reshape(DEPTH, n_mod_rows, 6, D_MODEL)

    ab_re, ab_im, bb_re, bb_im = _s5_disc_call(s5_a_re, s5_a_im, s5_log_dt, s5_b_re, s5_b_im)
    ab_re = ab_re.reshape(DEPTH, N_DIR, 1, S5_FLAT)
    ab_im = ab_im.reshape(DEPTH, N_DIR, 1, S5_FLAT)
    bb_re = bb_re.reshape(DEPTH, N_DIR, S5_GROUPS, S5_STATE, S5_CH)
    bb_im = bb_im.reshape(DEPTH, N_DIR, S5_GROUPS, S5_STATE, S5_CH)

    rope_tabs = _rope_tables(dec_seq)
    tri = np.tril(np.ones((M_TILE, M_TILE), np.float32))
    tril = jnp.asarray(tri, dtype=BF16)
    triu = jnp.asarray(tri.T, dtype=BF16)
    ones = jnp.ones((M_TILE, M_TILE), BF16)
    blkq = _head_block(ATT_WIDTH, ATT_HD)
    blkk = _head_block(ATT_KV_WIDTH, ATT_HD)
    blkm = _head_block(M_WIDTH, M_HD)

    xp = x_prompt.reshape(batch * seq, D_MODEL)
    xs = x_sample.reshape(dec_batch * dec_seq, D_MODEL)
    ctx_out = []
    for l in range(DEPTH):
        w_in_b = _bf(w_in[l])
        gate_cols = jnp.zeros((D_MODEL, LANES), BF16).at[:, :M_GATE_COLS].set(w_in_b[:, PROJ_MAIN:])
        bmat = jnp.stack([jnp.concatenate(
            [_block_diag(jnp.swapaxes(bb_re[l, d], -1, -2)), _block_diag(jnp.swapaxes(bb_im[l, d], -1, -2))],
            axis=1) for d in range(N_DIR)])
        cre = jnp.stack([_block_diag(jnp.swapaxes(s5_c_re[l, d], -1, -2)) for d in range(N_DIR)])
        cim = jnp.stack([_block_diag(jnp.swapaxes(s5_c_im[l, d], -1, -2)) for d in range(N_DIR)])
        gb = m_gate_b[l].reshape(M_GATE_COLS)
        lp = dict(
            g1=norm1_g[l].reshape(1, D_MODEL), g2=norm2_g[l].reshape(1, D_MODEL),
            w_main=w_in_b[:, :PROJ_MAIN], w_gatecols=gate_cols, blkq=blkq, blkk=blkk, blkm=blkm,
            qg=jnp.tile(q_norm_g[l], ATT_HEADS).reshape(1, ATT_WIDTH),
            kg=jnp.tile(k_norm_g[l], ATT_KV_HEADS).reshape(1, ATT_KV_WIDTH),
            s5_bmat=_bf(bmat), s5_cre=_bf(cre), s5_cim=_bf(cim),
            s5_ab_re=jnp.broadcast_to(ab_re[l], (N_DIR, SUBLANES, S5_FLAT)),
            s5_ab_im=jnp.broadcast_to(ab_im[l], (N_DIR, SUBLANES, S5_FLAT)),
            s5_d=s5_d[l].reshape(1, S5_WIDTH), s5_glu_w=_bf(s5_glu_w[l]), s5_glu_b=s5_glu_b[l].reshape(1, S5_WIDTH),
            m_gb_row=jnp.zeros((1, LANES), F32).at[0, :M_GATE_COLS].set(gb), m_gb_col=gb.reshape(M_GATE_COLS, 1),
            m_ng=jnp.tile(m_norm_g[l], M_HEADS).reshape(1, M_WIDTH), tril=tril, triu=triu, ones=ones,
            w_out=_bf(w_out[l]), ffn_w_gate=_bf(ffn_w_gate[l]), ffn_w_up=_bf(ffn_w_up[l]),
            ffn_conv_w=ffn_conv_w[l], ffn_conv_b=ffn_conv_b[l].reshape(1, D_FF), ffn_w_down=_bf(ffn_w_down[l]))
        xp, st = _trunk_layer(xp, batch, seq, mod_all[l, 0:1], lp, None, None)
        ctx_out.append(st)
        ctx = (cache_attn_k[:, l], cache_attn_v[:, l], state_s5_re[:, l], state_s5_im[:, l],
               state_mlstm_C[:, l], state_mlstm_n[:, l], state_mlstm_m[:, l])
        xs, _ = _trunk_layer(xs, dec_batch, dec_seq, mod_all[l, 1:1 + dec_batch], lp, ctx, rope_tabs)
    outs = [jnp.stack([s[i] for s in ctx_out], axis=1) for i in range(7)]
    return (xp.reshape(batch, seq, D_MODEL), xs.reshape(dec_batch, dec_seq, D_MODEL), *outs)
```

```python
import functools

import numpy as np
import jax
import jax.numpy as jnp
from jax import lax
from jax.experimental import pallas as pl
from jax.experimental.pallas import tpu as pltpu

F32 = jnp.float32
BF16 = jnp.bfloat16

D_MODEL = 1024
DEPTH = 2
GRID_W = 64
N_DIR = 2
EPS = 1e-6
ATT_HD = 64
ATT_WIDTH = 512
ATT_HEADS = 8
ATT_KV_HEADS = 2
ATT_GROUP = ATT_HEADS // ATT_KV_HEADS
ATT_KV_WIDTH = ATT_KV_HEADS * ATT_HD
ROPE_THETA = 10000.0
S5_CH = 16
S5_STATE = 64
S5_WIDTH = 256
S5_GROUPS = 16
S5_FLAT = S5_GROUPS * S5_STATE
M_HD = 64
M_WIDTH = 256
M_HEADS = 4
M_GATE_COLS = 2 * N_DIR * M_HEADS
M_CHAINS = N_DIR * M_HEADS
D_FF = 2816
PROJ_MAIN = 2048
COL_K = ATT_WIDTH
COL_V = COL_K + ATT_KV_WIDTH
COL_S5 = COL_V + ATT_KV_WIDTH
COL_MQ = COL_S5 + S5_WIDTH
COL_MK = COL_MQ + M_WIDTH
COL_MV = COL_MK + M_WIDTH
COL_MO = COL_MV + M_WIDTH
LANES = 128
SUBLANES = 8
BF16_ROWS = 16
VMEM_LIMIT = 56 * 1024 * 1024

PRE_TM = 256
ATT_TQ = 256
S5_TC = 64
M_TILE = 128
M_BG = 4
CN_ROWS = M_HD + BF16_ROWS
POST_TM = 1024
FFN_TM = 512
FF_CHUNK = 256


def _bf(x):
    return x.astype(BF16)


def _dot(a, b):
    return jnp.dot(a, b, preferred_element_type=F32)


def _split(x):
    hi = _bf(x)
    lo = _bf(x - hi.astype(F32))
    return hi, lo


def _seg_mean_sq(x, blk):
    hi, lo = _split(x * x)
    return _dot(hi, blk) + _dot(lo, blk)


def _sigmoid(x):
    return 1.0 / (1.0 + jnp.exp(-x))


def _log_sigmoid(x):
    return -(jnp.maximum(-x, 0.0) + jnp.log1p(jnp.exp(-jnp.abs(x))))


def _params(*sem):
    return pltpu.CompilerParams(dimension_semantics=sem, vmem_limit_bytes=VMEM_LIMIT)


def _ada_kernel(c_ref, w_ref, b_ref, o_ref):
    c = c_ref[...]
    s = c * _sigmoid(c)
    o_ref[...] = _dot(_bf(s), _bf(w_ref[...])) + b_ref[...]


def _ada_call(cvec, ada_w, ada_b):
    rows = cvec.shape[0]
    tn = 1024
    n = ada_w.shape[-1]
    return pl.pallas_call(
        _ada_kernel,
        grid=(DEPTH, n // tn),
        in_specs=[pl.BlockSpec((rows, D_MODEL), lambda l, j: (0, 0)),
                  pl.BlockSpec((None, D_MODEL, tn), lambda l, j: (l, 0, j)),
                  pl.BlockSpec((None, 1, tn), lambda l, j: (l, 0, j))],
        out_specs=pl.BlockSpec((None, rows, tn), lambda l, j: (l, 0, j)),
        out_shape=jax.ShapeDtypeStruct((DEPTH, rows, n), F32),
        compiler_params=_params("arbitrary", "arbitrary"),
        name="adaln",
    )(cvec, ada_w, ada_b.reshape(DEPTH, 1, n))


def _s5_disc_kernel(are_ref, aim_ref, ldt_ref, arex_ref, aimx_ref, bre_ref, bim_ref,
                    abr_ref, abi_ref, bbr_ref, bbi_ref):
    dt = jnp.exp(ldt_ref[...])

    def disc(a_re, a_im):
        mag = jnp.exp(dt * a_re)
        ab_re = mag * jnp.cos(dt * a_im)
        ab_im = mag * jnp.sin(dt * a_im)
        den = a_re * a_re + a_im * a_im
        nr = ab_re - 1.0
        ni = ab_im
        f_re = (nr * a_re + ni * a_im) / den
        f_im = (ni * a_re - nr * a_im) / den
        return ab_re, ab_im, f_re, f_im

    ab_re, ab_im, _, _ = disc(are_ref[...], aim_ref[...])
    abr_ref[...] = ab_re
    abi_ref[...] = ab_im
    _, _, f_re, f_im = disc(arex_ref[...], aimx_ref[...])
    b_re = bre_ref[...]
    b_im = bim_ref[...]
    bbr_ref[...] = f_re * b_re - f_im * b_im
    bbi_ref[...] = f_re * b_im + f_im * b_re


def _s5_disc_call(a_re, a_im, log_dt, b_re, b_im):
    r = DEPTH * N_DIR * S5_GROUPS
    a_re2 = a_re.reshape(r, S5_STATE)
    a_im2 = a_im.reshape(r, S5_STATE)
    wide = S5_STATE * S5_CH
    out_shape = (jax.ShapeDtypeStruct((r, S5_STATE), F32), jax.ShapeDtypeStruct((r, S5_STATE), F32),
                 jax.ShapeDtypeStruct((r, wide), F32), jax.ShapeDtypeStruct((r, wide), F32))
    return pl.pallas_call(_s5_disc_kernel, out_shape=out_shape, name="s5_disc")(
        a_re2, a_im2, log_dt.reshape(r, 1),
        jnp.repeat(a_re2, S5_CH, axis=1), jnp.repeat(a_im2, S5_CH, axis=1),
        b_re.reshape(r, wide), b_im.reshape(r, wide))


def _block_diag(blocks):
    g, r, c = blocks.shape
    eye = jnp.eye(g, dtype=blocks.dtype)
    return jnp.einsum('grc,gh->grhc', blocks, eye).reshape(g * r, g * c)


def _rope(x, cos, sin_signed, second):
    width = x.shape[-1]
    quarter = ATT_HD // 4
    partner = jnp.where(second, pltpu.roll(x, quarter, axis=1), pltpu.roll(x, width - quarter, axis=1))
    return x * cos + partner * sin_signed


def _pre_kernel(*refs, rope):
    if rope:
        (x_ref, mod_ref, g1_ref, w_ref, wg_ref, blkq_ref, blkk_ref, qg_ref, kg_ref,
         cosq_ref, sinq_ref, cosk_ref, sink_ref, proj_ref, mg_ref, qn_ref, kn_ref, mqk_ref, vt_ref, mgt_ref) = refs
    else:
        (x_ref, mod_ref, g1_ref, w_ref, wg_ref, blkq_ref, blkk_ref, qg_ref, kg_ref,
         proj_ref, mg_ref, qn_ref, kn_ref, mqk_ref, vt_ref, mgt_ref) = refs
    x = x_ref[...]
    ms = jnp.mean(x * x, axis=-1, keepdims=True)
    xn = x * lax.rsqrt(ms + EPS) * g1_ref[...]
    h = _bf(xn * (1.0 + mod_ref[1:2, :]) + mod_ref[0:1, :])
    proj = _dot(h, w_ref[...])
    proj_ref[...] = proj
    mg = _dot(h, wg_ref[...])
    mg_ref[...] = mg
    mqk_ref[:, :M_WIDTH] = _bf(proj[:, COL_MQ:COL_MK])
    mqk_ref[:, M_WIDTH:] = _bf(proj[:, COL_MK:COL_MV] * (M_HD ** -0.5))
    vt_ref[...] = proj[:, COL_MV:COL_MO].T
    mgt_ref[...] = mg.T[:M_GATE_COLS, :]
    q = proj[:, :COL_K]
    k = proj[:, COL_K:COL_V]
    qn = q * lax.rsqrt(_seg_mean_sq(q, blkq_ref[...]) + EPS) * qg_ref[...]
    kn = k * lax.rsqrt(_seg_mean_sq(k, blkk_ref[...]) + EPS) * kg_ref[...]
    if rope:
        lane_q = lax.broadcasted_iota(jnp.int32, qn.shape, 1)
        lane_k = lax.broadcasted_iota(jnp.int32, kn.shape, 1)
        qn = _rope(qn, cosq_ref[...], sinq_ref[...], (lane_q & (ATT_HD // 4)) != 0)
        kn = _rope(kn, cosk_ref[...], sink_ref[...], (lane_k & (ATT_HD // 4)) != 0)
    qn_ref[...] = qn
    kn_ref[...] = kn


def _rope_tables(seq_len):
    n_rows = seq_len // GRID_W
    row = jnp.repeat(jnp.arange(n_rows), GRID_W)
    col = jnp.tile(jnp.arange(GRID_W), n_rows)
    half = ATT_HD // 2
    inv_freq = 1.0 / (ROPE_THETA ** (jnp.arange(0, half, 2, dtype=F32) / half))

    def tables(pos):
        ang = pos.astype(F32)[:, None] * inv_freq[None, :]
        cos = jnp.cos(ang)
        sin = jnp.sin(ang)
        return jnp.concatenate([cos, cos], axis=-1), jnp.concatenate([-sin, sin], axis=-1)

    cr, sr = tables(row)
    cc, sc = tables(col)
    cos = jnp.concatenate([cr, cc], axis=-1)
    sin = jnp.concatenate([sr, sc], axis=-1)
    return (jnp.tile(cos, (1, ATT_HEADS)), jnp.tile(sin, (1, ATT_HEADS)),
            jnp.tile(cos, (1, ATT_KV_HEADS)), jnp.tile(sin, (1, ATT_KV_HEADS)))


def _pre_call(x2, mod, g1, w_main, w_gate, blkq, blkk, qg, kg, rope_tabs, seq_len, per_batch_mod):
    t = x2.shape[0]
    tm = PRE_TM
    tiles_per_seq = seq_len // tm
    rope = rope_tabs is not None
    if per_batch_mod:
        mod_map = lambda i: (i // tiles_per_seq, 0, 0)
    else:
        mod_map = lambda i: (0, 0, 0)
    const = lambda i: (0, 0)
    row = lambda i: (i, 0)
    in_specs = [pl.BlockSpec((tm, D_MODEL), row),
                pl.BlockSpec((None, 6, D_MODEL), mod_map),
                pl.BlockSpec((1, D_MODEL), const),
                pl.BlockSpec((D_MODEL, PROJ_MAIN), const),
                pl.BlockSpec((D_MODEL, LANES), const),
                pl.BlockSpec((ATT_WIDTH, ATT_WIDTH), const),
                pl.BlockSpec((ATT_KV_WIDTH, ATT_KV_WIDTH), const),
                pl.BlockSpec((1, ATT_WIDTH), const),
                pl.BlockSpec((1, ATT_KV_WIDTH), const)]
    args = [x2, mod, g1, w_main, w_gate, blkq, blkk, qg, kg]
    if rope:
        pos_map = lambda i: (i % tiles_per_seq, 0)
        in_specs += [pl.BlockSpec((tm, ATT_WIDTH), pos_map), pl.BlockSpec((tm, ATT_WIDTH), pos_map),
                     pl.BlockSpec((tm, ATT_KV_WIDTH), pos_map), pl.BlockSpec((tm, ATT_KV_WIDTH), pos_map)]
        args += list(rope_tabs)
    time_on_lanes = lambda i: (i // tiles_per_seq, 0, i % tiles_per_seq)
    batch = t // seq_len
    out_specs = [pl.BlockSpec((tm, PROJ_MAIN), row),
                 pl.BlockSpec((tm, LANES), row),
                 pl.BlockSpec((tm, ATT_WIDTH), row),
                 pl.BlockSpec((tm, ATT_KV_WIDTH), row),
                 pl.BlockSpec((tm, 2 * M_WIDTH), row),
                 pl.BlockSpec((None, M_WIDTH, tm), time_on_lanes),
                 pl.BlockSpec((None, M_GATE_COLS, tm), time_on_lanes)]
    out_shape = [jax.ShapeDtypeStruct((t, PROJ_MAIN), F32), jax.ShapeDtypeStruct((t, LANES), F32),
                 jax.ShapeDtypeStruct((t, ATT_WIDTH), F32), jax.ShapeDtypeStruct((t, ATT_KV_WIDTH), F32),
                 jax.ShapeDtypeStruct((t, 2 * M_WIDTH), BF16),
                 jax.ShapeDtypeStruct((batch, M_WIDTH, seq_len), F32),
                 jax.ShapeDtypeStruct((batch, M_GATE_COLS, seq_len), F32)]
    return pl.pallas_call(
        functools.partial(_pre_kernel, rope=rope),
        grid=(t // tm,), in_specs=in_specs, out_specs=out_specs, out_shape=out_shape,
        compiler_params=_params("arbitrary"), name="pre_rope" if rope else "pre",
    )(*args)


def _attn_kernel(*refs, cached):
    if cached:
        q_ref, k_ref, v_ref, ck_ref, cv_ref, o_ref = refs
        k_all = _bf(jnp.concatenate([ck_ref[...], k_ref[...]], axis=0))
        v_all = _bf(jnp.concatenate([cv_ref[...], v_ref[...]], axis=0))
    else:
        q_ref, k_ref, v_ref, o_ref = refs
        k_all = _bf(k_ref[...])
        v_all = _bf(v_ref[...])
    scale = ATT_HD ** -0.5
    outs = []
    for kvh in range(ATT_KV_HEADS):
        lo = kvh * ATT_HD
        kb = k_all[:, lo:lo + ATT_HD]
        vb = v_all[:, lo:lo + ATT_HD]
        for g in range(ATT_GROUP):
            c0 = (kvh * ATT_GROUP + g) * ATT_HD
            qb = _bf(q_ref[:, c0:c0 + ATT_HD] * scale)
            s = lax.dot_general(qb, kb, (((1,), (1,)), ((), ())), preferred_element_type=F32)
            m = jnp.max(s, axis=-1, keepdims=True)
            e = jnp.exp(s - m)
            den = jnp.sum(e, axis=-1, keepdims=True)
            outs.append(_dot(_bf(e), vb) / den)
    o_ref[...] = jnp.concatenate(outs, axis=-1)


def _attn_call(qn, kn, proj, cache_k, cache_v, batch, lq):
    tq = ATT_TQ
    nq = lq // tq
    cached = cache_k is not None
    in_specs = [pl.BlockSpec((tq, ATT_WIDTH), lambda b, i: (b * nq + i, 0)),
                pl.BlockSpec((lq, ATT_KV_WIDTH), lambda b, i: (b, 0)),
                pl.BlockSpec((lq, ATT_KV_WIDTH), lambda b, i: (b, COL_V // ATT_KV_WIDTH))]
    args = [qn, kn, proj]
    if cached:
        past = cache_k.shape[1]
        in_specs += [pl.BlockSpec((None, past, ATT_KV_WIDTH), lambda b, i: (b, 0, 0)),
                     pl.BlockSpec((None, past, ATT_KV_WIDTH), lambda b, i: (b, 0, 0))]
        args += [cache_k, cache_v]
    return pl.pallas_call(
        functools.partial(_attn_kernel, cached=cached),
        grid=(batch, nq), in_specs=in_specs,
        out_specs=pl.BlockSpec((tq, ATT_WIDTH), lambda b, i: (b * nq + i, 0)),
        out_shape=jax.ShapeDtypeStruct((batch * lq, ATT_WIDTH), F32),
        compiler_params=_params("arbitrary", "arbitrary"), name="attn_cached" if cached else "attn",
    )(*args)


def _s5_kernel(u_ref, bmat_ref, cre_ref, cim_ref, ar_ref, ai_ref, ir_ref, ii_ref, d_ref, gw_ref, gb_ref,
               out_ref, fr_ref, fi_ref, utb_ref, ytb_ref, xs_ref, *, batch, seq_len):
    tc = S5_TC
    rows = tc * batch
    n_chunks = seq_len // tc
    n_sub = batch // SUBLANES
    halves = S5_WIDTH // LANES

    for b in range(batch):
        for hf in range(halves):
            utb_ref[hf, pl.ds(b, seq_len, stride=batch), :] = (
                u_ref[b * seq_len:(b + 1) * seq_len, hf * LANES:(hf + 1) * LANES])
    for hf in range(halves):
        ytb_ref[hf] = utb_ref[hf] * d_ref[:, hf * LANES:(hf + 1) * LANES]

    for d in range(N_DIR):
        a_re = ar_ref[d]
        a_im = ai_ref[d]

        def chunk_body(ci, carry, d=d, a_re=a_re, a_im=a_im):
            c = ci if d == 0 else n_chunks - 1 - ci
            r0 = pl.multiple_of(c * rows, rows)
            u_c = jnp.concatenate([utb_ref[hf, pl.ds(r0, rows), :] for hf in range(halves)], axis=-1)
            xs_ref[...] = _dot(_bf(u_c), bmat_ref[d])
            new_carry = []
            for sub in range(n_sub):
                def step(t, st, sub=sub):
                    s_re, s_im = st
                    tt = t if d == 0 else tc - 1 - t
                    r = pl.multiple_of(tt * batch + sub * SUBLANES, SUBLANES)
                    x_re = xs_ref[pl.ds(r, SUBLANES), :S5_FLAT]
                    x_im = xs_ref[pl.ds(r, SUBLANES), S5_FLAT:]
                    n_re = a_re * s_re - a_im * s_im + x_re
                    n_im = a_re * s_im + a_im * s_re + x_im
                    xs_ref[pl.ds(r, SUBLANES), :S5_FLAT] = n_re
                    xs_ref[pl.ds(r, SUBLANES), S5_FLAT:] = n_im
                    return n_re, n_im

                new_carry.append(lax.fori_loop(0, tc, step, carry[sub], unroll=2))
            y = (_dot(_bf(xs_ref[:, :S5_FLAT]), cre_ref[d]) - _dot(_bf(xs_ref[:, S5_FLAT:]), cim_ref[d]))
            for hf in range(halves):
                ytb_ref[hf, pl.ds(r0, rows), :] += y[:, hf * LANES:(hf + 1) * LANES]
            return tuple(new_carry)

        init = tuple((ir_ref[d, sub * SUBLANES:(sub + 1) * SUBLANES, :],
                      ii_ref[d, sub * SUBLANES:(sub + 1) * SUBLANES, :]) for sub in range(n_sub))
        fin = lax.fori_loop(0, n_chunks, chunk_body, init)
        for sub in range(n_sub):
            fr_ref[d, sub * SUBLANES:(sub + 1) * SUBLANES, :] = fin[sub][0]
            fi_ref[d, sub * SUBLANES:(sub + 1) * SUBLANES, :] = fin[sub][1]

    for b in range(batch):
        z = jax.nn.gelu(jnp.concatenate(
            [ytb_ref[hf, pl.ds(b, seq_len, stride=batch), :] for hf in range(halves)], axis=-1))
        gate = _sigmoid(_dot(_bf(z), gw_ref[...]) + gb_ref[...])
        out_ref[b * seq_len:(b + 1) * seq_len, :] = z * gate


def _s5_call(proj, bmat, cre, cim, ab_re, ab_im, init_re, init_im, d_skip, glu_w, glu_b, batch, seq_len):
    rows = seq_len * batch
    single = pl.Buffered(1)
    whole = lambda shape: pl.BlockSpec(shape, lambda i: (0,) * len(shape))
    in_specs = [pl.BlockSpec((rows, S5_WIDTH), lambda i: (0, COL_S5 // S5_WIDTH), pipeline_mode=single),
                whole(bmat.shape), whole(cre.shape), whole(cim.shape), whole(ab_re.shape), whole(ab_im.shape),
                whole(init_re.shape), whole(init_im.shape), whole(d_skip.shape), whole(glu_w.shape),
                whole(glu_b.shape)]
    out_specs = [pl.BlockSpec((rows, S5_WIDTH), lambda i: (0, 0), pipeline_mode=single),
                 whole((N_DIR, batch, S5_FLAT)), whole((N_DIR, batch, S5_FLAT))]
    out_shape = (jax.ShapeDtypeStruct((rows, S5_WIDTH), F32),
                 jax.ShapeDtypeStruct((N_DIR, batch, S5_FLAT), F32),
                 jax.ShapeDtypeStruct((N_DIR, batch, S5_FLAT), F32))
    return pl.pallas_call(
        functools.partial(_s5_kernel, batch=batch, seq_len=seq_len),
        grid=(1,), in_specs=in_specs, out_specs=out_specs, out_shape=out_shape,
        scratch_shapes=[pltpu.VMEM((S5_WIDTH // LANES, rows, LANES), F32),
                        pltpu.VMEM((S5_WIDTH // LANES, rows, LANES), F32),
                        pltpu.VMEM((S5_TC * batch, 2 * S5_FLAT), F32)],
        compiler_params=_params("arbitrary"), name="s5",
    )(proj, bmat, cre, cim, ab_re, ab_im, init_re, init_im, d_skip, glu_w, glu_b)


def _mlstm_kernel(q_ref, k_ref, vt_ref, mg_ref, mgt_ref, gbr_ref, gbc_ref, c0_ref, n0_ref, m0_ref,
                  tril_ref, triu_ref, ones_ref, ht_ref, cn_ref, m_ref, *, seq_len):
    tile = M_TILE
    n_chunks = seq_len // tile
    bg = q_ref.shape[0]
    cn_ref[:, :, :M_HD, :] = c0_ref[...]
    cn_ref[:, :, M_HD:, :] = n0_ref[...]
    m_ref[...] = m0_ref[...]
    ht_ref[...] = jnp.zeros_like(ht_ref)
    s_ids = lax.broadcasted_iota(jnp.int32, (tile, tile), 0)
    t_ids = lax.broadcasted_iota(jnp.int32, (tile, tile), 1)
    first_row = lax.broadcasted_iota(jnp.int32, (CN_ROWS - M_HD, tile), 0) == 0
    neg_inf = jnp.float32(-jnp.inf)
    ones = ones_ref[...]

    def direction(d, c):
        t0 = pl.multiple_of(c * tile, tile)
        col_mat = tril_ref[...] if d == 0 else triu_ref[...]
        row_mat = triu_ref[...] if d == 0 else tril_ref[...]
        valid = (s_ids <= t_ids) if d == 0 else (s_ids >= t_ids)
        gt = mgt_ref[:, :, pl.ds(t0, tile)] + gbc_ref[...]
        lft_hi, lft_lo = _split(_log_sigmoid(gt).reshape(bg * M_GATE_COLS, tile))
        b_rows = (_dot(lft_hi, row_mat) + _dot(lft_lo, row_mat)).reshape(bg, M_GATE_COLS, tile)
        totals = (_dot(lft_hi, ones) + _dot(lft_lo, ones)).reshape(bg, M_GATE_COLS, tile)
        gc = mg_ref[:, pl.ds(t0, tile), :] + gbr_ref[...]
        lfc_hi, lfc_lo = _split(_log_sigmoid(gc))
        col_b = jnp.broadcast_to(col_mat, (bg, tile, tile))
        b_cols = (jnp.einsum('bts,bsl->btl', col_b, lfc_hi, preferred_element_type=F32)
                  + jnp.einsum('bts,bsl->btl', col_b, lfc_lo, preferred_element_type=F32))
        for h in range(M_HEADS):
            j = d * M_HEADS + h
            fj = M_CHAINS + j
            rows = slice(h * M_HD, (h + 1) * M_HD)
            q_c = q_ref[:, pl.ds(t0, tile), rows]
            k_c = k_ref[:, pl.ds(t0, tile), rows]
            vt_c = vt_ref[:, rows, pl.ds(t0, tile)]
            b_t = b_rows[:, fj:fj + 1, :]
            li_t = gt[:, j:j + 1, :]
            tot = totals[:, fj:fj + 1, :]
            m_prev = m_ref[:, j]
            cn_prev = cn_ref[:, j]

            c_col = gc[:, :, j:j + 1] - b_cols[:, :, fj:fj + 1]
            cm = jnp.where(valid, c_col, neg_inf)
            inter = b_t + m_prev
            m_t = jnp.maximum(inter, b_t + jnp.max(cm, axis=1, keepdims=True))
            w_t = jnp.exp(cm + (b_t - m_t))
            s_inter = jnp.exp(inter - m_t)
            sc_t = jnp.einsum('bse,bte->bst', k_c, q_c, preferred_element_type=F32) * w_t
            num = jnp.einsum('bds,bst->bdt', _bf(vt_c), _bf(sc_t), preferred_element_type=F32)
            ext = jnp.einsum('bre,bte->brt', _bf(cn_prev), q_c, preferred_element_type=F32)
            num = num + s_inter * ext[:, :M_HD, :]
            den = jnp.sum(sc_t, axis=1, keepdims=True) + s_inter * ext[:, M_HD:M_HD + 1, :]
            ht_ref[:, rows, pl.ds(t0, tile)] += num / jnp.maximum(jnp.abs(den), jnp.exp(-m_t))

            g_row = tot - b_t + li_t
            m_new = jnp.maximum(tot + m_prev, jnp.max(g_row, axis=-1, keepdims=True))
            wk = jnp.exp(g_row - m_new)
            decay = jnp.exp(tot + m_prev - m_new)
            vw = jnp.concatenate([vt_c * wk, jnp.where(first_row, wk, 0.0)], axis=1)
            cn_ref[:, j] = (decay[:, :, :M_HD] * cn_prev
                            + jnp.einsum('brs,bse->bre', _bf(vw), k_c, preferred_element_type=F32))
            m_ref[:, j] = m_new

    def body(i, _):
        direction(0, i)
        direction(1, n_chunks - 1 - i)
        return 0

    lax.fori_loop(0, n_chunks, body, 0)


def _mlstm_call(mqk3, vt, mg3, mgt, gb_row, gb_col, c0, n0, m0, tril, triu, ones, batch, seq_len):
    bg = M_BG
    const2 = lambda g: (0, 0)
    lead3 = lambda g: (g, 0, 0)
    lead4 = lambda g: (g, 0, 0, 0)
    in_specs = [pl.BlockSpec((bg, seq_len, M_WIDTH), lead3),
                pl.BlockSpec((bg, seq_len, M_WIDTH), lambda g: (g, 0, 1)),
                pl.BlockSpec((bg, M_WIDTH, seq_len), lead3),
                pl.BlockSpec((bg, seq_len, LANES), lead3),
                pl.BlockSpec((bg, M_GATE_COLS, seq_len), lead3),
                pl.BlockSpec((1, LANES), const2),
                pl.BlockSpec((M_GATE_COLS, 1), const2),
                pl.BlockSpec((bg, M_CHAINS, M_HD, M_HD), lead4),
                pl.BlockSpec((bg, M_CHAINS, CN_ROWS - M_HD, M_HD), lead4),
                pl.BlockSpec((bg, M_CHAINS, 1, LANES), lead4),
                pl.BlockSpec((M_TILE, M_TILE), const2),
                pl.BlockSpec((M_TILE, M_TILE), const2),
                pl.BlockSpec((M_TILE, M_TILE), const2)]
    out_specs = [pl.BlockSpec((bg, M_WIDTH, seq_len), lead3),
                 pl.BlockSpec((bg, M_CHAINS, CN_ROWS, M_HD), lead4),
                 pl.BlockSpec((bg, M_CHAINS, 1, LANES), lead4)]
    out_shape = [jax.ShapeDtypeStruct((batch, M_WIDTH, seq_len), F32),
                 jax.ShapeDtypeStruct((batch, M_CHAINS, CN_ROWS, M_HD), F32),
                 jax.ShapeDtypeStruct((batch, M_CHAINS, 1, LANES), F32)]
    return pl.pallas_call(
        functools.partial(_mlstm_kernel, seq_len=seq_len),
        grid=(batch // bg,), in_specs=in_specs, out_specs=out_specs, out_shape=out_shape,
        compiler_params=_params("arbitrary"), name="mlstm",
    )(mqk3, mqk3, vt, mg3, mgt, gb_row, gb_col, c0, n0, m0, tril, triu, ones)


def _post_kernel(x_ref, att_ref, s5_ref, ht_ref, mo_ref, mod_ref, g2_ref, ng_ref, blkm_ref, wo_ref,
                 x1_ref, h2_ref):
    mh = jnp.concatenate([ht_ref[s].T for s in range(ht_ref.shape[0])], axis=0)
    ml = mh * lax.rsqrt(_seg_mean_sq(mh, blkm_ref[...]) + EPS) * ng_ref[...] * _sigmoid(mo_ref[...])
    mix = (_dot(_bf(att_ref[...]), wo_ref[:ATT_WIDTH, :])
           + _dot(_bf(s5_ref[...]), wo_ref[ATT_WIDTH:ATT_WIDTH + S5_WIDTH, :])
           + _dot(_bf(ml), wo_ref[ATT_WIDTH + S5_WIDTH:, :]))
    x1 = x_ref[...] + mod_ref[2:3, :] * mix
    x1_ref[...] = x1
    ms = jnp.mean(x1 * x1, axis=-1, keepdims=True)
    xn = x1 * lax.rsqrt(ms + EPS) * g2_ref[...]
    h2_ref[...] = _bf(xn * (1.0 + mod_ref[4:5, :]) + mod_ref[3:4, :])


def _post_call(x2, att, s5o, ht, proj, mod, g2, ng, blkm, w_out, seq_len, per_batch_mod):
    t = x2.shape[0]
    tm = POST_TM
    seqs_per_tile = tm // seq_len
    if per_batch_mod:
        mod_map = lambda i: (i * seqs_per_tile, 0, 0)
    else:
        mod_map = lambda i: (0, 0, 0)
    row = lambda i: (i, 0)
    const = lambda i: (0, 0)
    in_specs = [pl.BlockSpec((tm, D_MODEL), row),
                pl.BlockSpec((tm, ATT_WIDTH), row),
                pl.BlockSpec((tm, S5_WIDTH), row),
                pl.BlockSpec((seqs_per_tile, M_WIDTH, seq_len), lambda i: (i, 0, 0)),
                pl.BlockSpec((tm, M_WIDTH), lambda i: (i, COL_MO // M_WIDTH)),
                pl.BlockSpec((None, 6, D_MODEL), mod_map),
                pl.BlockSpec((1, D_MODEL), const),
                pl.BlockSpec((1, M_WIDTH), const),
                pl.BlockSpec((M_WIDTH, M_WIDTH), const),
                pl.BlockSpec((D_MODEL, D_MODEL), const)]
    return pl.pallas_call(
        _post_kernel,
        grid=(t // tm,), in_specs=in_specs,
        out_specs=[pl.BlockSpec((tm, D_MODEL), row), pl.BlockSpec((tm, D_MODEL), row)],
        out_shape=[jax.ShapeDtypeStruct((t, D_MODEL), F32), jax.ShapeDtypeStruct((t, D_MODEL), BF16)],
        compiler_params=_params("arbitrary"), name="post",
    )(x2, att, s5o, ht, proj, mod, g2, ng, blkm, w_out)


def _ffn_kernel(x1_ref, h2_ref, hprev_ref, hnext_ref, mod_ref, wg_ref, wu_ref, cw_ref, cb_ref, wd_ref, o_ref,
                gated_ref, *, seq_len):
    tm = h2_ref.shape[0]
    h2 = h2_ref[...]
    h2_ext = jnp.concatenate([h2, hprev_ref[...], hnext_ref[...]], axis=0)
    row = lax.broadcasted_iota(jnp.int32, (tm, FF_CHUNK), 0)
    pos = (pl.program_id(0) * tm + row) % seq_len
    seq_start = pos == 0
    seq_end = pos == seq_len - 1
    tile_start = row == 0
    tile_end = row == tm - 1
    for j in range(D_FF // FF_CHUNK):
        cols = slice(j * FF_CHUNK, (j + 1) * FF_CHUNK)
        a_ext = _dot(h2_ext, wg_ref[:, cols])
        a = a_ext[:tm]
        before = a_ext[tm + BF16_ROWS - 1:tm + BF16_ROWS, :]
        after = a_ext[tm + BF16_ROWS:tm + BF16_ROWS + 1, :]
        a_prev = jnp.where(tile_start, before, pltpu.roll(a, 1, axis=0))
        a_next = jnp.where(tile_end, after, pltpu.roll(a, tm - 1, axis=0))
        a_prev = jnp.where(seq_start, 0.0, a_prev)
        a_next = jnp.where(seq_end, 0.0, a_next)
        ac = a_prev * cw_ref[0:1, cols] + a * cw_ref[1:2, cols] + a_next * cw_ref[2:3, cols] + cb_ref[:, cols]
        up = _dot(h2, wu_ref[:, cols])
        gated_ref[:, cols] = _bf(ac * _sigmoid(ac) * up)
    o_ref[...] = x1_ref[...] + mod_ref[5:6, :] * _dot(gated_ref[...], wd_ref[...])


def _ffn_call(x1, h2, mod, w_gate, w_up, conv_w, conv_b, w_down, seq_len, per_batch_mod):
    t = x1.shape[0]
    tm = FFN_TM
    halo_blocks_per_tile = tm // BF16_ROWS
    last_halo_block = t // BF16_ROWS - 1
    if per_batch_mod:
        mod_map = lambda i: ((i * tm) // seq_len, 0, 0)
    else:
        mod_map = lambda i: (0, 0, 0)
    row = lambda i: (i, 0)
    single = pl.Buffered(1)
    whole = lambda shape: pl.BlockSpec(shape, lambda i: (0, 0), pipeline_mode=single)
    in_specs = [pl.BlockSpec((tm, D_MODEL), row),
                pl.BlockSpec((tm, D_MODEL), row),
                pl.BlockSpec((BF16_ROWS, D_MODEL), lambda i: (jnp.maximum(i * halo_blocks_per_tile - 1, 0), 0)),
                pl.BlockSpec((BF16_ROWS, D_MODEL),
                             lambda i: (jnp.minimum((i + 1) * halo_blocks_per_tile, last_halo_block), 0)),
                pl.BlockSpec((None, 6, D_MODEL), mod_map),
                whole((D_MODEL, D_FF)), whole((D_MODEL, D_FF)), whole((3, D_FF)), whole((1, D_FF)),
                whole((D_FF, D_MODEL))]
    return pl.pallas_call(
        functools.partial(_ffn_kernel, seq_len=seq_len),
        grid=(t // tm,), in_specs=in_specs,
        out_specs=pl.BlockSpec((tm, D_MODEL), row),
        out_shape=jax.ShapeDtypeStruct((t, D_MODEL), F32),
        scratch_shapes=[pltpu.VMEM((tm, D_FF), BF16)],
        compiler_params=_params("arbitrary"), name="ffn",
    )(x1, h2, h2, h2, mod, w_gate, w_up, conv_w, conv_b, w_down)


def _trunk_layer(x2, batch, seq_len, mod, lp, ctx, rope_tabs):
    latent = ctx is not None
    proj, mg, qn, kn, mqk, vt, mgt = _pre_call(x2, mod, lp['g1'], lp['w_main'], lp['w_gatecols'], lp['blkq'],
                                               lp['blkk'], lp['qg'], lp['kg'], rope_tabs if latent else None,
                                               seq_len, latent)
    n_pad = CN_ROWS - M_HD
    if latent:
        ctx_k, ctx_v, s5_re0, s5_im0, c0, n0, m0 = ctx
        past = ctx_k.shape[1]
        cache_k = ctx_k.reshape(batch, past, ATT_KV_WIDTH)
        cache_v = ctx_v.reshape(batch, past, ATT_KV_WIDTH)
        init_re = s5_re0.reshape(batch, N_DIR, S5_FLAT).transpose(1, 0, 2)
        init_im = s5_im0.reshape(batch, N_DIR, S5_FLAT).transpose(1, 0, 2)
        c0 = c0.reshape(batch, M_CHAINS, M_HD, M_HD)
        n0 = jnp.zeros((batch, M_CHAINS, n_pad, M_HD), F32).at[:, :, 0, :].set(n0.reshape(batch, M_CHAINS, M_HD))
        m0 = jnp.broadcast_to(m0.reshape(batch, M_CHAINS, 1, 1), (batch, M_CHAINS, 1, LANES))
    else:
        cache_k = cache_v = None
        init_re = jnp.zeros((N_DIR, batch, S5_FLAT), F32)
        init_im = jnp.zeros((N_DIR, batch, S5_FLAT), F32)
        c0 = jnp.zeros((batch, M_CHAINS, M_HD, M_HD), F32)
        n0 = jnp.zeros((batch, M_CHAINS, n_pad, M_HD), F32)
        m0 = jnp.zeros((batch, M_CHAINS, 1, LANES), F32)

    att = _attn_call(qn, kn, proj, cache_k, cache_v, batch, seq_len)

    s5o, fin_re, fin_im = _s5_call(proj, lp['s5_bmat'], lp['s5_cre'], lp['s5_cim'], lp['s5_ab_re'], lp['s5_ab_im'],
                                   init_re, init_im, lp['s5_d'], lp['s5_glu_w'], lp['s5_glu_b'], batch, seq_len)

    ht, cn_f, m_f = _mlstm_call(mqk.reshape(batch, seq_len, 2 * M_WIDTH), vt, mg.reshape(batch, seq_len, LANES), mgt,
                                lp['m_gb_row'], lp['m_gb_col'], c0, n0, m0, lp['tril'], lp['triu'], lp['ones'],
                                batch, seq_len)

    x1, h2 = _post_call(x2, att, s5o, ht, proj, mod, lp['g2'], lp['m_ng'], lp['blkm'], lp['w_out'], seq_len, latent)
    x_out = _ffn_call(x1, h2, mod, lp['ffn_w_gate'], lp['ffn_w_up'], lp['ffn_conv_w'], lp['ffn_conv_b'],
                      lp['ffn_w_down'], seq_len, latent)
    if latent:
        return x_out, None
    states = (kn.reshape(batch, seq_len, ATT_KV_HEADS, ATT_HD),
              proj[:, COL_V:COL_S5].reshape(batch, seq_len, ATT_KV_HEADS, ATT_HD),
              fin_re.transpose(1, 0, 2).reshape(batch, N_DIR, S5_GROUPS, S5_STATE),
              fin_im.transpose(1, 0, 2).reshape(batch, N_DIR, S5_GROUPS, S5_STATE),
              cn_f[:, :, :M_HD, :].reshape(batch, N_DIR, M_HEADS, M_HD, M_HD),
              cn_f[:, :, M_HD, :].reshape(batch, N_DIR, M_HEADS, M_HD),
              m_f[:, :, 0, 0].reshape(batch, N_DIR, M_HEADS))
    return x_out, states


def _head_block(width, head_dim):
    ids = np.arange(width) // head_dim
    return jnp.asarray((ids[:, None] == ids[None, :]).astype(np.float32) / head_dim, dtype=BF16)


def kernel(x_prompt, x_sample, c, cache_attn_k, cache_attn_v, state_s5_re, state_s5_im, state_mlstm_C, state_mlstm_n, state_mlstm_m, c_ctx, ada_w, ada_b, norm1_g, norm2_g, w_in, q_norm_g, k_norm_g, s5_a_re, s5_a_im, s5_log_dt, s5_b_re, s5_b_im, s5_c_re, s5_c_im, s5_d, s5_glu_w, s5_glu_b, m_gate_b, m_norm_g, w_out, ffn_w_gate, ffn_w_up, ffn_conv_w, ffn_conv_b, ffn_w_down):
    batch, seq = x_prompt.shape[0], x_prompt.shape[1]
    dec_batch, dec_seq = x_sample.shape[0], x_sample.shape[1]

    n_mod_rows = 2 * SUBLANES
    cvec = jnp.zeros((n_mod_rows, D_MODEL), F32).at[0].set(c_ctx).at[1:1 + dec_batch].set(c)
    mod_all = _ada_call(cvec, ada_w, ada_b).reshape(DEPTH, n_mod_rows, 6, D_MODEL)

    ab_re, ab_im, bb_re, bb_im = _s5_disc_call(s5_a_re, s5_a_im, s5_log_dt, s5_b_re, s5_b_im)
    ab_re = ab_re.reshape(DEPTH, N_DIR, 1, S5_FLAT)
    ab_im = ab_im.reshape(DEPTH, N_DIR, 1, S5_FLAT)
    bb_re = bb_re.reshape(DEPTH, N_DIR, S5_GROUPS, S5_STATE, S5_CH)
    bb_im = bb_im.reshape(DEPTH, N_DIR, S5_GROUPS, S5_STATE, S5_CH)

    rope_tabs = _rope_tables(dec_seq)
    tri = np.tril(np.ones((M_TILE, M_TILE), np.float32))
    tril = jnp.asarray(tri, dtype=BF16)
    triu = jnp.asarray(tri.T, dtype=BF16)
    ones = jnp.ones((M_TILE, M_TILE), BF16)
    blkq = _head_block(ATT_WIDTH, ATT_HD)
    blkk = _head_block(ATT_KV_WIDTH, ATT_HD)
    blkm = _head_block(M_WIDTH, M_HD)

    xp = x_prompt.reshape(batch * seq, D_MODEL)
    xs = x_sample.reshape(dec_batch * dec_seq, D_MODEL)
    ctx_out = []
    for l in range(DEPTH):
        w_in_b = _bf(w_in[l])
        gate_cols = jnp.zeros((D_MODEL, LANES), BF16).at[:, :M_GATE_COLS].set(w_in_b[:, PROJ_MAIN:])
        bmat = jnp.stack([jnp.concatenate(
            [_block_diag(jnp.swapaxes(bb_re[l, d], -1, -2)), _block_diag(jnp.swapaxes(bb_im[l, d], -1, -2))],
            axis=1) for d in range(N_DIR)])
        cre = jnp.stack([_block_diag(jnp.swapaxes(s5_c_re[l, d], -1, -2)) for d in range(N_DIR)])
        cim = jnp.stack([_block_diag(jnp.swapaxes(s5_c_im[l, d], -1, -2)) for d in range(N_DIR)])
        gb = m_gate_b[l].reshape(M_GATE_COLS)
        lp = dict(
            g1=norm1_g[l].reshape(1, D_MODEL), g2=norm2_g[l].reshape(1, D_MODEL),
            w_main=w_in_b[:, :PROJ_MAIN], w_gatecols=gate_cols, blkq=blkq, blkk=blkk, blkm=blkm,
            qg=jnp.tile(q_norm_g[l], ATT_HEADS).reshape(1, ATT_WIDTH),
            kg=jnp.tile(k_norm_g[l], ATT_KV_HEADS).reshape(1, ATT_KV_WIDTH),
            s5_bmat=_bf(bmat), s5_cre=_bf(cre), s5_cim=_bf(cim),
            s5_ab_re=jnp.broadcast_to(ab_re[l], (N_DIR, SUBLANES, S5_FLAT)),
            s5_ab_im=jnp.broadcast_to(ab_im[l], (N_DIR, SUBLANES, S5_FLAT)),
            s5_d=s5_d[l].reshape(1, S5_WIDTH), s5_glu_w=_bf(s5_glu_w[l]), s5_glu_b=s5_glu_b[l].reshape(1, S5_WIDTH),
            m_gb_row=jnp.zeros((1, LANES), F32).at[0, :M_GATE_COLS].set(gb), m_gb_col=gb.reshape(M_GATE_COLS, 1),
            m_ng=jnp.tile(m_norm_g[l], M_HEADS).reshape(1, M_WIDTH), tril=tril, triu=triu, ones=ones,
            w_out=_bf(w_out[l]), ffn_w_gate=_bf(ffn_w_gate[l]), ffn_w_up=_bf(ffn_w_up[l]),
            ffn_conv_w=ffn_conv_w[l], ffn_conv_b=ffn_conv_b[l].reshape(1, D_FF), ffn_w_down=_bf(ffn_w_down[l]))
        xp, st = _trunk_layer(xp, batch, seq, mod_all[l, 0:1], lp, None, None)
        ctx_out.append(st)
        ctx = (cache_attn_k[:, l], cache_attn_v[:, l], state_s5_re[:, l], state_s5_im[:, l],
               state_mlstm_C[:, l], state_mlstm_n[:, l], state_mlstm_m[:, l])
        xs, _ = _trunk_layer(xs, dec_batch, dec_seq, mod_all[l, 1:1 + dec_batch], lp, ctx, rope_tabs)
    outs = [jnp.stack([s[i] for s in ctx_out], axis=1) for i in range(7)]
    return (xp.reshape(batch, seq, D_MODEL), xs.reshape(dec_batch, dec_seq, D_MODEL), *outs)
```

```python
import functools

import numpy as np
import jax
import jax.numpy as jnp
from jax import lax
from jax.experimental import pallas as pl
from jax.experimental.pallas import tpu as pltpu

F32 = jnp.float32
BF16 = jnp.bfloat16

D_MODEL = 1024
DEPTH = 2
GRID_W = 64
N_DIR = 2
EPS = 1e-6
ATT_HD = 64
ATT_WIDTH = 512
ATT_HEADS = 8
ATT_KV_HEADS = 2
ATT_GROUP = ATT_HEADS // ATT_KV_HEADS
ATT_KV_WIDTH = ATT_KV_HEADS * ATT_HD
ROPE_THETA = 10000.0
LOG2_E = 1.4426950408889634
S5_CH = 16
S5_STATE = 64
S5_WIDTH = 256
S5_GROUPS = 16
S5_FLAT = S5_GROUPS * S5_STATE
M_HD = 64
M_WIDTH = 256
M_HEADS = 4
M_GATE_COLS = 2 * N_DIR * M_HEADS
M_CHAINS = N_DIR * M_HEADS
D_FF = 2816
PROJ_MAIN = 2048
COL_K = ATT_WIDTH
COL_V = COL_K + ATT_KV_WIDTH
COL_S5 = COL_V + ATT_KV_WIDTH
COL_MQ = COL_S5 + S5_WIDTH
COL_MK = COL_MQ + M_WIDTH
COL_MV = COL_MK + M_WIDTH
COL_MO = COL_MV + M_WIDTH
LANES = 128
SUBLANES = 8
BF16_ROWS = 16
VMEM_LIMIT = 56 * 1024 * 1024

PRE_TM = 512
ATT_TQ = 512
S5_ROWS = 256
S5_SLOTS = 3
M_TILE = 128
M_BG = 4
CN_ROWS = M_HD + BF16_ROWS
POST_TM = 1024
FFN_TM = 512
FF_CHUNK = 256


def _bf(x):
    return x.astype(BF16)


def _dot(a, b):
    return jnp.dot(a, b, preferred_element_type=F32)


def _split(x):
    hi = _bf(x)
    lo = _bf(x - hi.astype(F32))
    return hi, lo


def _seg_mean_sq(x, blk):
    hi, lo = _split(x * x)
    return _dot(hi, blk) + _dot(lo, blk)


def _sigmoid(x):
    return 1.0 / (1.0 + jnp.exp(-x))


def _log_sigmoid(x):
    return -(jnp.maximum(-x, 0.0) + jnp.log1p(jnp.exp(-jnp.abs(x))))


def _params(*sem):
    return pltpu.CompilerParams(dimension_semantics=sem, vmem_limit_bytes=VMEM_LIMIT)


def _ada_kernel(c_ref, w_ref, b_ref, o_ref):
    c = c_ref[...]
    s = c * _sigmoid(c)
    o_ref[...] = _dot(_bf(s), _bf(w_ref[...])) + b_ref[...]


def _ada_call(cvec, ada_w, ada_b):
    rows = cvec.shape[0]
    tn = 1024
    n = ada_w.shape[-1]
    return pl.pallas_call(
        _ada_kernel,
        grid=(DEPTH, n // tn),
        in_specs=[pl.BlockSpec((rows, D_MODEL), lambda l, j: (0, 0)),
                  pl.BlockSpec((None, D_MODEL, tn), lambda l, j: (l, 0, j)),
                  pl.BlockSpec((None, 1, tn), lambda l, j: (l, 0, j))],
        out_specs=pl.BlockSpec((None, rows, tn), lambda l, j: (l, 0, j)),
        out_shape=jax.ShapeDtypeStruct((DEPTH, rows, n), F32),
        compiler_params=_params("arbitrary", "arbitrary"),
        name="adaln",
    )(cvec, ada_w, ada_b.reshape(DEPTH, 1, n))


def _s5_disc_kernel(are_ref, aim_ref, ldt_ref, arex_ref, aimx_ref, bre_ref, bim_ref,
                    abr_ref, abi_ref, bbr_ref, bbi_ref):
    dt = jnp.exp(ldt_ref[...])

    def disc(a_re, a_im):
        mag = jnp.exp(dt * a_re)
        ab_re = mag * jnp.cos(dt * a_im)
        ab_im = mag * jnp.sin(dt * a_im)
        den = a_re * a_re + a_im * a_im
        nr = ab_re - 1.0
        ni = ab_im
        f_re = (nr * a_re + ni * a_im) / den
        f_im = (ni * a_re - nr * a_im) / den
        return ab_re, ab_im, f_re, f_im

    ab_re, ab_im, _, _ = disc(are_ref[...], aim_ref[...])
    abr_ref[...] = ab_re
    abi_ref[...] = ab_im
    _, _, f_re, f_im = disc(arex_ref[...], aimx_ref[...])
    b_re = bre_ref[...]
    b_im = bim_ref[...]
    bbr_ref[...] = f_re * b_re - f_im * b_im
    bbi_ref[...] = f_re * b_im + f_im * b_re


def _s5_disc_call(a_re, a_im, log_dt, b_re, b_im):
    r = DEPTH * N_DIR * S5_GROUPS
    a_re2 = a_re.reshape(r, S5_STATE)
    a_im2 = a_im.reshape(r, S5_STATE)
    wide = S5_STATE * S5_CH
    out_shape = (jax.ShapeDtypeStruct((r, S5_STATE), F32), jax.ShapeDtypeStruct((r, S5_STATE), F32),
                 jax.ShapeDtypeStruct((r, wide), F32), jax.ShapeDtypeStruct((r, wide), F32))
    return pl.pallas_call(_s5_disc_kernel, out_shape=out_shape, name="s5_disc")(
        a_re2, a_im2, log_dt.reshape(r, 1),
        jnp.repeat(a_re2, S5_CH, axis=1), jnp.repeat(a_im2, S5_CH, axis=1),
        b_re.reshape(r, wide), b_im.reshape(r, wide))


def _block_diag(blocks):
    g, r, c = blocks.shape
    eye = jnp.eye(g, dtype=blocks.dtype)
    return jnp.einsum('grc,gh->grhc', blocks, eye).reshape(g * r, g * c)


def _rope(x, cos, sin_signed, second):
    width = x.shape[-1]
    quarter = ATT_HD // 4
    partner = jnp.where(second, pltpu.roll(x, quarter, axis=1), pltpu.roll(x, width - quarter, axis=1))
    return x * cos + partner * sin_signed


def _pre_kernel(*refs, rope):
    if rope:
        (x_ref, mod_ref, g1_ref, w_ref, wg_ref, blkq_ref, blkk_ref, qg_ref, kg_ref,
         cosq_ref, sinq_ref, cosk_ref, sink_ref, proj_ref, mg_ref, qn_ref, kn_ref, mqk_ref, vt_ref, mgt_ref) = refs
    else:
        (x_ref, mod_ref, g1_ref, w_ref, wg_ref, blkq_ref, blkk_ref, qg_ref, kg_ref,
         proj_ref, mg_ref, qn_ref, kn_ref, mqk_ref, vt_ref, mgt_ref) = refs
    x = x_ref[...]
    ms = jnp.mean(x * x, axis=-1, keepdims=True)
    xn = x * lax.rsqrt(ms + EPS) * g1_ref[...]
    h = _bf(xn * (1.0 + mod_ref[1:2, :]) + mod_ref[0:1, :])
    proj = _dot(h, w_ref[...])
    proj_ref[...] = proj
    mg = _dot(h, wg_ref[...])
    mg_ref[...] = mg
    mqk_ref[:, :M_WIDTH] = _bf(proj[:, COL_MQ:COL_MK])
    mqk_ref[:, M_WIDTH:] = _bf(proj[:, COL_MK:COL_MV] * (M_HD ** -0.5))
    vt_ref[...] = proj[:, COL_MV:COL_MO].T
    mgt_ref[...] = mg.T[:M_GATE_COLS, :]
    q = proj[:, :COL_K]
    k = proj[:, COL_K:COL_V]
    qn = q * lax.rsqrt(_seg_mean_sq(q, blkq_ref[...]) + EPS) * qg_ref[...]
    kn = k * lax.rsqrt(_seg_mean_sq(k, blkk_ref[...]) + EPS) * kg_ref[...]
    if rope:
        lane_q = lax.broadcasted_iota(jnp.int32, qn.shape, 1)
        lane_k = lax.broadcasted_iota(jnp.int32, kn.shape, 1)
        qn = _rope(qn, cosq_ref[...], sinq_ref[...], (lane_q & (ATT_HD // 4)) != 0)
        kn = _rope(kn, cosk_ref[...], sink_ref[...], (lane_k & (ATT_HD // 4)) != 0)
    qn_ref[...] = qn
    kn_ref[...] = kn


def _rope_tables(seq_len):
    n_rows = seq_len // GRID_W
    row = jnp.repeat(jnp.arange(n_rows), GRID_W)
    col = jnp.tile(jnp.arange(GRID_W), n_rows)
    half = ATT_HD // 2
    inv_freq = 1.0 / (ROPE_THETA ** (jnp.arange(0, half, 2, dtype=F32) / half))

    def tables(pos):
        ang = pos.astype(F32)[:, None] * inv_freq[None, :]
        cos = jnp.cos(ang)
        sin = jnp.sin(ang)
        return jnp.concatenate([cos, cos], axis=-1), jnp.concatenate([-sin, sin], axis=-1)

    cr, sr = tables(row)
    cc, sc = tables(col)
    cos = jnp.concatenate([cr, cc], axis=-1)
    sin = jnp.concatenate([sr, sc], axis=-1)
    return (jnp.tile(cos, (1, ATT_HEADS)), jnp.tile(sin, (1, ATT_HEADS)),
            jnp.tile(cos, (1, ATT_KV_HEADS)), jnp.tile(sin, (1, ATT_KV_HEADS)))


def _pre_call(x2, mod, g1, w_main, w_gate, blkq, blkk, qg, kg, rope_tabs, seq_len, per_batch_mod):
    t = x2.shape[0]
    tm = min(PRE_TM, seq_len)
    tiles_per_seq = seq_len // tm
    rope = rope_tabs is not None
    if per_batch_mod:
        mod_map = lambda i: (i // tiles_per_seq, 0, 0)
    else:
        mod_map = lambda i: (0, 0, 0)
    const = lambda i: (0, 0)
    row = lambda i: (i, 0)
    in_specs = [pl.BlockSpec((tm, D_MODEL), row),
                pl.BlockSpec((None, 6, D_MODEL), mod_map),
                pl.BlockSpec((1, D_MODEL), const),
                pl.BlockSpec((D_MODEL, PROJ_MAIN), const),
                pl.BlockSpec((D_MODEL, LANES), const),
                pl.BlockSpec((ATT_WIDTH, ATT_WIDTH), const),
                pl.BlockSpec((ATT_KV_WIDTH, ATT_KV_WIDTH), const),
                pl.BlockSpec((1, ATT_WIDTH), const),
                pl.BlockSpec((1, ATT_KV_WIDTH), const)]
    args = [x2, mod, g1, w_main, w_gate, blkq, blkk, qg, kg]
    if rope:
        pos_map = lambda i: (i % tiles_per_seq, 0)
        in_specs += [pl.BlockSpec((tm, ATT_WIDTH), pos_map), pl.BlockSpec((tm, ATT_WIDTH), pos_map),
                     pl.BlockSpec((tm, ATT_KV_WIDTH), pos_map), pl.BlockSpec((tm, ATT_KV_WIDTH), pos_map)]
        args += list(rope_tabs)
    time_on_lanes = lambda i: (i // tiles_per_seq, 0, i % tiles_per_seq)
    batch = t // seq_len
    out_specs = [pl.BlockSpec((tm, PROJ_MAIN), row),
                 pl.BlockSpec((tm, LANES), row),
                 pl.BlockSpec((tm, ATT_WIDTH), row),
                 pl.BlockSpec((tm, ATT_KV_WIDTH), row),
                 pl.BlockSpec((tm, 2 * M_WIDTH), row),
                 pl.BlockSpec((None, M_WIDTH, tm), time_on_lanes),
                 pl.BlockSpec((None, M_GATE_COLS, tm), time_on_lanes)]
    out_shape = [jax.ShapeDtypeStruct((t, PROJ_MAIN), F32), jax.ShapeDtypeStruct((t, LANES), F32),
                 jax.ShapeDtypeStruct((t, ATT_WIDTH), F32), jax.ShapeDtypeStruct((t, ATT_KV_WIDTH), F32),
                 jax.ShapeDtypeStruct((t, 2 * M_WIDTH), BF16),
                 jax.ShapeDtypeStruct((batch, M_WIDTH, seq_len), F32),
                 jax.ShapeDtypeStruct((batch, M_GATE_COLS, seq_len), F32)]
    return pl.pallas_call(
        functools.partial(_pre_kernel, rope=rope),
        grid=(t // tm,), in_specs=in_specs, out_specs=out_specs, out_shape=out_shape,
        compiler_params=_params("arbitrary"), name="pre_rope" if rope else "pre",
    )(*args)


def _attn_kernel(*refs, cached):
    if cached:
        q_ref, k_ref, v_ref, ck_ref, cv_ref, o_ref = refs
        k_all = _bf(jnp.concatenate([ck_ref[...], k_ref[...]], axis=0))
        v_all = _bf(jnp.concatenate([cv_ref[...], v_ref[...]], axis=0))
    else:
        q_ref, k_ref, v_ref, o_ref = refs
        k_all = _bf(k_ref[...])
        v_all = _bf(v_ref[...])
    scale = ATT_HD ** -0.5 * LOG2_E
    outs = []
    for kvh in range(ATT_KV_HEADS):
        lo = kvh * ATT_HD
        kb = k_all[:, lo:lo + ATT_HD]
        vb = v_all[:, lo:lo + ATT_HD]
        for g in range(ATT_GROUP):
            c0 = (kvh * ATT_GROUP + g) * ATT_HD
            qb = _bf(q_ref[:, c0:c0 + ATT_HD] * scale)
            s = lax.dot_general(qb, kb, (((1,), (1,)), ((), ())), preferred_element_type=F32)
            m = jnp.max(s, axis=-1, keepdims=True)
            e = jnp.exp2(s - m)
            den = jnp.sum(e, axis=-1, keepdims=True)
            outs.append(_dot(_bf(e), vb) / den)
    o_ref[...] = jnp.concatenate(outs, axis=-1)


def _attn_call(qn, kn, proj, cache_k, cache_v, batch, lq):
    tq = min(ATT_TQ, lq)
    nq = lq // tq
    cached = cache_k is not None
    in_specs = [pl.BlockSpec((tq, ATT_WIDTH), lambda b, i: (b * nq + i, 0)),
                pl.BlockSpec((lq, ATT_KV_WIDTH), lambda b, i: (b, 0)),
                pl.BlockSpec((lq, ATT_KV_WIDTH), lambda b, i: (b, COL_V // ATT_KV_WIDTH))]
    args = [qn, kn, proj]
    if cached:
        past = cache_k.shape[1]
        in_specs += [pl.BlockSpec((None, past, ATT_KV_WIDTH), lambda b, i: (b, 0, 0)),
                     pl.BlockSpec((None, past, ATT_KV_WIDTH), lambda b, i: (b, 0, 0))]
        args += [cache_k, cache_v]
    return pl.pallas_call(
        functools.partial(_attn_kernel, cached=cached),
        grid=(batch, nq), in_specs=in_specs,
        out_specs=pl.BlockSpec((tq, ATT_WIDTH), lambda b, i: (b * nq + i, 0)),
        out_shape=jax.ShapeDtypeStruct((batch * lq, ATT_WIDTH), F32),
        compiler_params=_params("arbitrary", "arbitrary"), name="attn_cached" if cached else "attn",
    )(*args)


def _s5_kernel(u_ref, bmat_ref, cre_ref, cim_ref, ar_ref, ai_ref, ir_ref, ii_ref, d_ref, gw_ref, gb_ref,
               out_ref, fr_ref, fi_ref, utb_ref, ytb_ref, xs_ref, *, batch, seq_len):
    rows = S5_ROWS
    tc = rows // batch
    n_chunks = seq_len // tc
    n_sub = batch // SUBLANES
    halves = S5_WIDTH // LANES

    def aligned(x, m):
        return x if isinstance(x, int) else pl.multiple_of(x, m)

    for b in range(batch):
        for hf in range(halves):
            utb_ref[hf, pl.ds(b, seq_len, stride=batch), :] = (
                u_ref[b * seq_len:(b + 1) * seq_len, hf * LANES:(hf + 1) * LANES])
    for hf in range(halves):
        ytb_ref[hf] = utb_ref[hf] * d_ref[:, hf * LANES:(hf + 1) * LANES]

    def chunk_of(d, i):
        return i if d == 0 else n_chunks - 1 - i

    def stage_in(i, slot):
        for d in range(N_DIR):
            r0 = aligned(chunk_of(d, i) * rows, rows)
            u_c = jnp.concatenate([utb_ref[hf, pl.ds(r0, rows), :] for hf in range(halves)], axis=-1)
            xs_ref[slot, d] = _dot(_bf(u_c), bmat_ref[d])

    def stage_scan(slot, carry):
        new_carry = []
        for d in range(N_DIR):
            a_re = ar_ref[d]
            a_im = ai_ref[d]
            per_sub = []
            for sub in range(n_sub):
                s_re, s_im = carry[d][sub]
                for t in range(tc):
                    r = (t if d == 0 else tc - 1 - t) * batch + sub * SUBLANES
                    x_re = xs_ref[slot, d, r:r + SUBLANES, :S5_FLAT]
                    x_im = xs_ref[slot, d, r:r + SUBLANES, S5_FLAT:]
                    s_re, s_im = (a_re * s_re - a_im * s_im + x_re, a_re * s_im + a_im * s_re + x_im)
                    xs_ref[slot, d, r:r + SUBLANES, :S5_FLAT] = s_re
                    xs_ref[slot, d, r:r + SUBLANES, S5_FLAT:] = s_im
                per_sub.append((s_re, s_im))
            new_carry.append(tuple(per_sub))
        return tuple(new_carry)

    def stage_out(i, slot):
        for d in range(N_DIR):
            r0 = aligned(chunk_of(d, i) * rows, rows)
            y = (_dot(_bf(xs_ref[slot, d, :, :S5_FLAT]), cre_ref[d])
                 - _dot(_bf(xs_ref[slot, d, :, S5_FLAT:]), cim_ref[d]))
            for hf in range(halves):
                ytb_ref[hf, pl.ds(r0, rows), :] += y[:, hf * LANES:(hf + 1) * LANES]

    carry = tuple(tuple((ir_ref[d, sub * SUBLANES:(sub + 1) * SUBLANES, :],
                         ii_ref[d, sub * SUBLANES:(sub + 1) * SUBLANES, :]) for sub in range(n_sub))
                  for d in range(N_DIR))
    def step(i, phase, carry):
        stage_in(i + 1, (phase + 1) % S5_SLOTS)
        carry = stage_scan(phase, carry)
        stage_out(i - 1, (phase - 1) % S5_SLOTS)
        return carry

    stage_in(0, 0)
    stage_in(1, 1)
    carry = stage_scan(0, carry)
    n_steady = n_chunks - 2
    n_peeled = n_steady % S5_SLOTS
    for i in range(1, 1 + n_peeled):
        carry = step(i, i % S5_SLOTS, carry)
    first = 1 + n_peeled

    def body(g, carry):
        for k in range(S5_SLOTS):
            carry = step(first + g * S5_SLOTS + k, (first + k) % S5_SLOTS, carry)
        return carry

    carry = lax.fori_loop(0, n_steady // S5_SLOTS, body, carry)
    last = n_chunks - 1
    carry = stage_scan(last % S5_SLOTS, carry)
    stage_out(last - 1, (last - 1) % S5_SLOTS)
    stage_out(last, last % S5_SLOTS)
    for d in range(N_DIR):
        for sub in range(n_sub):
            fr_ref[d, sub * SUBLANES:(sub + 1) * SUBLANES, :] = carry[d][sub][0]
            fi_ref[d, sub * SUBLANES:(sub + 1) * SUBLANES, :] = carry[d][sub][1]

    for b in range(batch):
        z = jax.nn.gelu(jnp.concatenate(
            [ytb_ref[hf, pl.ds(b, seq_len, stride=batch), :] for hf in range(halves)], axis=-1))
        gate = _sigmoid(_dot(_bf(z), gw_ref[...]) + gb_ref[...])
        out_ref[b * seq_len:(b + 1) * seq_len, :] = z * gate


def _s5_call(proj, bmat, cre, cim, ab_re, ab_im, init_re, init_im, d_skip, glu_w, glu_b, batch, seq_len):
    rows = seq_len * batch
    single = pl.Buffered(1)
    whole = lambda shape: pl.BlockSpec(shape, lambda i: (0,) * len(shape), pipeline_mode=single)
    in_specs = [pl.BlockSpec((rows, S5_WIDTH), lambda i: (0, COL_S5 // S5_WIDTH), pipeline_mode=single),
                whole(bmat.shape), whole(cre.shape), whole(cim.shape), whole(ab_re.shape), whole(ab_im.shape),
                whole(init_re.shape), whole(init_im.shape), whole(d_skip.shape), whole(glu_w.shape),
                whole(glu_b.shape)]
    out_specs = [pl.BlockSpec((rows, S5_WIDTH), lambda i: (0, 0), pipeline_mode=single),
                 whole((N_DIR, batch, S5_FLAT)), whole((N_DIR, batch, S5_FLAT))]
    out_shape = (jax.ShapeDtypeStruct((rows, S5_WIDTH), F32),
                 jax.ShapeDtypeStruct((N_DIR, batch, S5_FLAT), F32),
                 jax.ShapeDtypeStruct((N_DIR, batch, S5_FLAT), F32))
    return pl.pallas_call(
        functools.partial(_s5_kernel, batch=batch, seq_len=seq_len),
        grid=(1,), in_specs=in_specs, out_specs=out_specs, out_shape=out_shape,
        scratch_shapes=[pltpu.VMEM((S5_WIDTH // LANES, rows, LANES), F32),
                        pltpu.VMEM((S5_WIDTH // LANES, rows, LANES), F32),
                        pltpu.VMEM((S5_SLOTS, N_DIR, S5_ROWS, 2 * S5_FLAT), F32)],
        compiler_params=_params("arbitrary"), name="s5",
    )(proj, bmat, cre, cim, ab_re, ab_im, init_re, init_im, d_skip, glu_w, glu_b)


def _mlstm_kernel(q_ref, k_ref, vt_ref, mg_ref, mgt_ref, gbr_ref, gbc_ref, c0_ref, n0_ref, m0_ref,
                  tril_ref, triu_ref, ones_ref, ht_ref, cn_ref, m_ref, *, seq_len):
    tile = M_TILE
    n_chunks = seq_len // tile
    bg = q_ref.shape[0]
    cn_ref[:, :, :M_HD, :] = c0_ref[...]
    cn_ref[:, :, M_HD:, :] = n0_ref[...]
    m_ref[...] = m0_ref[...]
    ht_ref[...] = jnp.zeros_like(ht_ref)
    s_ids = lax.broadcasted_iota(jnp.int32, (tile, tile), 0)
    t_ids = lax.broadcasted_iota(jnp.int32, (tile, tile), 1)
    first_row = lax.broadcasted_iota(jnp.int32, (CN_ROWS - M_HD, tile), 0) == 0
    neg_inf = jnp.float32(-jnp.inf)
    ones = ones_ref[...]

    def direction(d, c):
        t0 = pl.multiple_of(c * tile, tile)
        col_mat = tril_ref[...] if d == 0 else triu_ref[...]
        row_mat = triu_ref[...] if d == 0 else tril_ref[...]
        valid = (s_ids <= t_ids) if d == 0 else (s_ids >= t_ids)
        gt = mgt_ref[:, :, pl.ds(t0, tile)] + gbc_ref[...]
        lft_hi, lft_lo = _split(_log_sigmoid(gt).reshape(bg * M_GATE_COLS, tile))
        b_rows = (_dot(lft_hi, row_mat) + _dot(lft_lo, row_mat)).reshape(bg, M_GATE_COLS, tile)
        totals = (_dot(lft_hi, ones) + _dot(lft_lo, ones)).reshape(bg, M_GATE_COLS, tile)
        gc = mg_ref[:, pl.ds(t0, tile), :] + gbr_ref[...]
        lfc_hi, lfc_lo = _split(_log_sigmoid(gc))
        col_b = jnp.broadcast_to(col_mat, (bg, tile, tile))
        b_cols = (jnp.einsum('bts,bsl->btl', col_b, lfc_hi, preferred_element_type=F32)
                  + jnp.einsum('bts,bsl->btl', col_b, lfc_lo, preferred_element_type=F32))
        for h in range(M_HEADS):
            j = d * M_HEADS + h
            fj = M_CHAINS + j
            rows = slice(h * M_HD, (h + 1) * M_HD)
            q_c = q_ref[:, pl.ds(t0, tile), rows]
            k_c = k_ref[:, pl.ds(t0, tile), rows]
            vt_c = vt_ref[:, rows, pl.ds(t0, tile)]
            b_t = b_rows[:, fj:fj + 1, :]
            li_t = gt[:, j:j + 1, :]
            tot = totals[:, fj:fj + 1, :]
            m_prev = m_ref[:, j]
            cn_prev = cn_ref[:, j]

            c_col = gc[:, :, j:j + 1] - b_cols[:, :, fj:fj + 1]
            cm = jnp.where(valid, c_col, neg_inf)
            inter = b_t + m_prev
            m_t = jnp.maximum(inter, b_t + jnp.max(cm, axis=1, keepdims=True))
            w_t = jnp.exp(cm + (b_t - m_t))
            s_inter = jnp.exp(inter - m_t)
            sc_t = jnp.einsum('bse,bte->bst', k_c, q_c, preferred_element_type=F32) * w_t
            num = jnp.einsum('bds,bst->bdt', _bf(vt_c), _bf(sc_t), preferred_element_type=F32)
            ext = jnp.einsum('bre,bte->brt', _bf(cn_prev), q_c, preferred_element_type=F32)
            num = num + s_inter * ext[:, :M_HD, :]
            den = jnp.sum(sc_t, axis=1, keepdims=True) + s_inter * ext[:, M_HD:M_HD + 1, :]
            ht_ref[:, rows, pl.ds(t0, tile)] += num / jnp.maximum(jnp.abs(den), jnp.exp(-m_t))

            g_row = tot - b_t + li_t
            m_new = jnp.maximum(tot + m_prev, jnp.max(g_row, axis=-1, keepdims=True))
            wk = jnp.exp(g_row - m_new)
            decay = jnp.exp(tot + m_prev - m_new)
            vw = jnp.concatenate([vt_c * wk, jnp.where(first_row, wk, 0.0)], axis=1)
            cn_ref[:, j] = (decay[:, :, :M_HD] * cn_prev
                            + jnp.einsum('brs,bse->bre', _bf(vw), k_c, preferred_element_type=F32))
            m_ref[:, j] = m_new

    def body(i, _):
        direction(0, i)
        direction(1, n_chunks - 1 - i)
        return 0

    lax.fori_loop(0, n_chunks, body, 0)


def _mlstm_call(mqk3, vt, mg3, mgt, gb_row, gb_col, c0, n0, m0, tril, triu, ones, batch, seq_len):
    bg = M_BG
    const2 = lambda g: (0, 0)
    lead3 = lambda g: (g, 0, 0)
    lead4 = lambda g: (g, 0, 0, 0)
    in_specs = [pl.BlockSpec((bg, seq_len, M_WIDTH), lead3),
                pl.BlockSpec((bg, seq_len, M_WIDTH), lambda g: (g, 0, 1)),
                pl.BlockSpec((bg, M_WIDTH, seq_len), lead3),
                pl.BlockSpec((bg, seq_len, LANES), lead3),
                pl.BlockSpec((bg, M_GATE_COLS, seq_len), lead3),
                pl.BlockSpec((1, LANES), const2),
                pl.BlockSpec((M_GATE_COLS, 1), const2),
                pl.BlockSpec((bg, M_CHAINS, M_HD, M_HD), lead4),
                pl.BlockSpec((bg, M_CHAINS, CN_ROWS - M_HD, M_HD), lead4),
                pl.BlockSpec((bg, M_CHAINS, 1, LANES), lead4),
                pl.BlockSpec((M_TILE, M_TILE), const2),
                pl.BlockSpec((M_TILE, M_TILE), const2),
                pl.BlockSpec((M_TILE, M_TILE), const2)]
    out_specs = [pl.BlockSpec((bg, M_WIDTH, seq_len), lead3),
                 pl.BlockSpec((bg, M_CHAINS, CN_ROWS, M_HD), lead4),
                 pl.BlockSpec((bg, M_CHAINS, 1, LANES), lead4)]
    out_shape = [jax.ShapeDtypeStruct((batch, M_WIDTH, seq_len), F32),
                 jax.ShapeDtypeStruct((batch, M_CHAINS, CN_ROWS, M_HD), F32),
                 jax.ShapeDtypeStruct((batch, M_CHAINS, 1, LANES), F32)]
    return pl.pallas_call(
        functools.partial(_mlstm_kernel, seq_len=seq_len),
        grid=(batch // bg,), in_specs=in_specs, out_specs=out_specs, out_shape=out_shape,
        compiler_params=_params("arbitrary"), name="mlstm",
    )(mqk3, mqk3, vt, mg3, mgt, gb_row, gb_col, c0, n0, m0, tril, triu, ones)


def _post_kernel(x_ref, att_ref, s5_ref, ht_ref, mo_ref, mod_ref, g2_ref, ng_ref, blkm_ref, wo_ref,
                 x1_ref, h2_ref):
    mh = jnp.concatenate([ht_ref[s].T for s in range(ht_ref.shape[0])], axis=0)
    ml = mh * lax.rsqrt(_seg_mean_sq(mh, blkm_ref[...]) + EPS) * ng_ref[...] * _sigmoid(mo_ref[...])
    mix = (_dot(_bf(att_ref[...]), wo_ref[:ATT_WIDTH, :])
           + _dot(_bf(s5_ref[...]), wo_ref[ATT_WIDTH:ATT_WIDTH + S5_WIDTH, :])
           + _dot(_bf(ml), wo_ref[ATT_WIDTH + S5_WIDTH:, :]))
    x1 = x_ref[...] + mod_ref[2:3, :] * mix
    x1_ref[...] = x1
    ms = jnp.mean(x1 * x1, axis=-1, keepdims=True)
    xn = x1 * lax.rsqrt(ms + EPS) * g2_ref[...]
    h2_ref[...] = _bf(xn * (1.0 + mod_ref[4:5, :]) + mod_ref[3:4, :])


def _post_call(x2, att, s5o, ht, proj, mod, g2, ng, blkm, w_out, seq_len, per_batch_mod):
    t = x2.shape[0]
    tm = POST_TM
    seqs_per_tile = tm // seq_len
    if per_batch_mod:
        mod_map = lambda i: (i * seqs_per_tile, 0, 0)
    else:
        mod_map = lambda i: (0, 0, 0)
    row = lambda i: (i, 0)
    const = lambda i: (0, 0)
    in_specs = [pl.BlockSpec((tm, D_MODEL), row),
                pl.BlockSpec((tm, ATT_WIDTH), row),
                pl.BlockSpec((tm, S5_WIDTH), row),
                pl.BlockSpec((seqs_per_tile, M_WIDTH, seq_len), lambda i: (i, 0, 0)),
                pl.BlockSpec((tm, M_WIDTH), lambda i: (i, COL_MO // M_WIDTH)),
                pl.BlockSpec((None, 6, D_MODEL), mod_map),
                pl.BlockSpec((1, D_MODEL), const),
                pl.BlockSpec((1, M_WIDTH), const),
                pl.BlockSpec((M_WIDTH, M_WIDTH), const),
                pl.BlockSpec((D_MODEL, D_MODEL), const)]
    return pl.pallas_call(
        _post_kernel,
        grid=(t // tm,), in_specs=in_specs,
        out_specs=[pl.BlockSpec((tm, D_MODEL), row), pl.BlockSpec((tm, D_MODEL), row)],
        out_shape=[jax.ShapeDtypeStruct((t, D_MODEL), F32), jax.ShapeDtypeStruct((t, D_MODEL), BF16)],
        compiler_params=_params("arbitrary"), name="post",
    )(x2, att, s5o, ht, proj, mod, g2, ng, blkm, w_out)


def _ffn_kernel(x1_ref, h2_ref, hprev_ref, hnext_ref, mod_ref, wg_ref, wu_ref, cw_ref, cb_ref, wd_ref, o_ref,
                gated_ref, *, seq_len):
    tm = h2_ref.shape[0]
    h2 = h2_ref[...]
    h2_ext = jnp.concatenate([h2, hprev_ref[...], hnext_ref[...]], axis=0)
    row = lax.broadcasted_iota(jnp.int32, (tm, FF_CHUNK), 0)
    pos = (pl.program_id(0) * tm + row) % seq_len
    seq_start = pos == 0
    seq_end = pos == seq_len - 1
    tile_start = row == 0
    tile_end = row == tm - 1
    for j in range(D_FF // FF_CHUNK):
        cols = slice(j * FF_CHUNK, (j + 1) * FF_CHUNK)
        a_ext = _dot(h2_ext, wg_ref[:, cols])
        a = a_ext[:tm]
        before = a_ext[tm + BF16_ROWS - 1:tm + BF16_ROWS, :]
        after = a_ext[tm + BF16_ROWS:tm + BF16_ROWS + 1, :]
        a_prev = jnp.where(tile_start, before, pltpu.roll(a, 1, axis=0))
        a_next = jnp.where(tile_end, after, pltpu.roll(a, tm - 1, axis=0))
        a_prev = jnp.where(seq_start, 0.0, a_prev)
        a_next = jnp.where(seq_end, 0.0, a_next)
        ac = a_prev * cw_ref[0:1, cols] + a * cw_ref[1:2, cols] + a_next * cw_ref[2:3, cols] + cb_ref[:, cols]
        up = _dot(h2, wu_ref[:, cols])
        gated_ref[:, cols] = _bf(ac * _sigmoid(ac) * up)
    o_ref[...] = x1_ref[...] + mod_ref[5:6, :] * _dot(gated_ref[...], wd_ref[...])


def _ffn_call(x1, h2, mod, w_gate, w_up, conv_w, conv_b, w_down, seq_len, per_batch_mod):
    t = x1.shape[0]
    tm = FFN_TM
    halo_blocks_per_tile = tm // BF16_ROWS
    last_halo_block = t // BF16_ROWS - 1
    if per_batch_mod:
        mod_map = lambda i: ((i * tm) // seq_len, 0, 0)
    else:
        mod_map = lambda i: (0, 0, 0)
    row = lambda i: (i, 0)
    single = pl.Buffered(1)
    whole = lambda shape: pl.BlockSpec(shape, lambda i: (0, 0), pipeline_mode=single)
    in_specs = [pl.BlockSpec((tm, D_MODEL), row),
                pl.BlockSpec((tm, D_MODEL), row),
                pl.BlockSpec((BF16_ROWS, D_MODEL), lambda i: (jnp.maximum(i * halo_blocks_per_tile - 1, 0), 0)),
                pl.BlockSpec((BF16_ROWS, D_MODEL),
                             lambda i: (jnp.minimum((i + 1) * halo_blocks_per_tile, last_halo_block), 0)),
                pl.BlockSpec((None, 6, D_MODEL), mod_map),
                whole((D_MODEL, D_FF)), whole((D_MODEL, D_FF)), whole((3, D_FF)), whole((1, D_FF)),
                whole((D_FF, D_MODEL))]
    return pl.pallas_call(
        functools.partial(_ffn_kernel, seq_len=seq_len),
        grid=(t // tm,), in_specs=in_specs,
        out_specs=pl.BlockSpec((tm, D_MODEL), row),
        out_shape=jax.ShapeDtypeStruct((t, D_MODEL), F32),
        scratch_shapes=[pltpu.VMEM((tm, D_FF), BF16)],
        compiler_params=_params("arbitrary"), name="ffn",
    )(x1, h2, h2, h2, mod, w_gate, w_up, conv_w, conv_b, w_down)


def _trunk_layer(x2, batch, seq_len, mod, lp, ctx, rope_tabs):
    latent = ctx is not None
    proj, mg, qn, kn, mqk, vt, mgt = _pre_call(x2, mod, lp['g1'], lp['w_main'], lp['w_gatecols'], lp['blkq'],
                                               lp['blkk'], lp['qg'], lp['kg'], rope_tabs if latent else None,
                                               seq_len, latent)
    n_pad = CN_ROWS - M_HD
    if latent:
        ctx_k, ctx_v, s5_re0, s5_im0, c0, n0, m0 = ctx
        past = ctx_k.shape[1]
        cache_k = ctx_k.reshape(batch, past, ATT_KV_WIDTH)
        cache_v = ctx_v.reshape(batch, past, ATT_KV_WIDTH)
        init_re = s5_re0.reshape(batch, N_DIR, S5_FLAT).transpose(1, 0, 2)
        init_im = s5_im0.reshape(batch, N_DIR, S5_FLAT).transpose(1, 0, 2)
        c0 = c0.reshape(batch, M_CHAINS, M_HD, M_HD)
        n0 = jnp.zeros((batch, M_CHAINS, n_pad, M_HD), F32).at[:, :, 0, :].set(n0.reshape(batch, M_CHAINS, M_HD))
        m0 = jnp.broadcast_to(m0.reshape(batch, M_CHAINS, 1, 1), (batch, M_CHAINS, 1, LANES))
    else:
        cache_k = cache_v = None
        init_re = jnp.zeros((N_DIR, batch, S5_FLAT), F32)
        init_im = jnp.zeros((N_DIR, batch, S5_FLAT), F32)
        c0 = jnp.zeros((batch, M_CHAINS, M_HD, M_HD), F32)
        n0 = jnp.zeros((batch, M_CHAINS, n_pad, M_HD), F32)
        m0 = jnp.zeros((batch, M_CHAINS, 1, LANES), F32)

    att = _attn_call(qn, kn, proj, cache_k, cache_v, batch, seq_len)

    s5o, fin_re, fin_im = _s5_call(proj, lp['s5_bmat'], lp['s5_cre'], lp['s5_cim'], lp['s5_ab_re'], lp['s5_ab_im'],
                                   init_re, init_im, lp['s5_d'], lp['s5_glu_w'], lp['s5_glu_b'], batch, seq_len)

    ht, cn_f, m_f = _mlstm_call(mqk.reshape(batch, seq_len, 2 * M_WIDTH), vt, mg.reshape(batch, seq_len, LANES), mgt,
                                lp['m_gb_row'], lp['m_gb_col'], c0, n0, m0, lp['tril'], lp['triu'], lp['ones'],
                                batch, seq_len)

    x1, h2 = _post_call(x2, att, s5o, ht, proj, mod, lp['g2'], lp['m_ng'], lp['blkm'], lp['w_out'], seq_len, latent)
    x_out = _ffn_call(x1, h2, mod, lp['ffn_w_gate'], lp['ffn_w_up'], lp['ffn_conv_w'], lp['ffn_conv_b'],
                      lp['ffn_w_down'], seq_len, latent)
    if latent:
        return x_out, None
    states = (kn.reshape(batch, seq_len, ATT_KV_HEADS, ATT_HD),
              proj[:, COL_V:COL_S5].reshape(batch, seq_len, ATT_KV_HEADS, ATT_HD),
              fin_re.transpose(1, 0, 2).reshape(batch, N_DIR, S5_GROUPS, S5_STATE),
              fin_im.transpose(1, 0, 2).reshape(batch, N_DIR, S5_GROUPS, S5_STATE),
              cn_f[:, :, :M_HD, :].reshape(batch, N_DIR, M_HEADS, M_HD, M_HD),
              cn_f[:, :, M_HD, :].reshape(batch, N_DIR, M_HEADS, M_HD),
              m_f[:, :, 0, 0].reshape(batch, N_DIR, M_HEADS))
    return x_out, states


def _head_block(width, head_dim):
    ids = np.arange(width) // head_dim
    return jnp.asarray((ids[:, None] == ids[None, :]).astype(np.float32) / head_dim, dtype=BF16)


def kernel(x_prompt, x_sample, c, cache_attn_k, cache_attn_v, state_s5_re, state_s5_im, state_mlstm_C, state_mlstm_n, state_mlstm_m, c_ctx, ada_w, ada_b, norm1_g, norm2_g, w_in, q_norm_g, k_norm_g, s5_a_re, s5_a_im, s5_log_dt, s5_b_re, s5_b_im, s5_c_re, s5_c_im, s5_d, s5_glu_w, s5_glu_b, m_gate_b, m_norm_g, w_out, ffn_w_gate, ffn_w_up, ffn_conv_w, ffn_conv_b, ffn_w_down):
    batch, seq = x_prompt.shape[0], x_prompt.shape[1]
    dec_batch, dec_seq = x_sample.shape[0], x_sample.shape[1]

    n_mod_rows = 2 * SUBLANES
    cvec = jnp.zeros((n_mod_rows, D_MODEL), F32).at[0].set(c_ctx).at[1:1 + dec_batch].set(c)
    mod_all = _ada_call(cvec, ada_w, ada_b).reshape(DEPTH, n_mod_rows, 6, D_MODEL)

    ab_re, ab_im, bb_re, bb_im = _s5_disc_call(s5_a_re, s5_a_im, s5_log_dt, s5_b_re, s5_b_im)
    ab_re = ab_re.reshape(DEPTH, N_DIR, 1, S5_FLAT)
    ab_im = ab_im.reshape(DEPTH, N_DIR, 1, S5_FLAT)
    bb_re = bb_re.reshape(DEPTH, N_DIR, S5_GROUPS, S5_STATE, S5_CH)
    bb_im = bb_im.reshape(DEPTH, N_DIR, S5_GROUPS, S5_STATE, S5_CH)

    rope_tabs = _rope_tables(dec_seq)
    tri = np.tril(np.ones((M_TILE, M_TILE), np.float32))
    tril = jnp.asarray(tri, dtype=BF16)
    triu = jnp.asarray(tri.T, dtype=BF16)
    ones = jnp.ones((M_TILE, M_TILE), BF16)
    blkq = _head_block(ATT_WIDTH, ATT_HD)
    blkk = _head_block(ATT_KV_WIDTH, ATT_HD)
    blkm = _head_block(M_WIDTH, M_HD)

    xp = x_prompt.reshape(batch * seq, D_MODEL)
    xs = x_sample.reshape(dec_batch * dec_seq, D_MODEL)
    ctx_out = []
    for l in range(DEPTH):
        w_in_b = _bf(w_in[l])
        gate_cols = jnp.zeros((D_MODEL, LANES), BF16).at[:, :M_GATE_COLS].set(w_in_b[:, PROJ_MAIN:])
        bmat = jnp.stack([jnp.concatenate(
            [_block_diag(jnp.swapaxes(bb_re[l, d], -1, -2)), _block_diag(jnp.swapaxes(bb_im[l, d], -1, -2))],
            axis=1) for d in range(N_DIR)])
        cre = jnp.stack([_block_diag(jnp.swapaxes(s5_c_re[l, d], -1, -2)) for d in range(N_DIR)])
        cim = jnp.stack([_block_diag(jnp.swapaxes(s5_c_im[l, d], -1, -2)) for d in range(N_DIR)])
        gb = m_gate_b[l].reshape(M_GATE_COLS)
        lp = dict(
            g1=norm1_g[l].reshape(1, D_MODEL), g2=norm2_g[l].reshape(1, D_MODEL),
            w_main=w_in_b[:, :PROJ_MAIN], w_gatecols=gate_cols, blkq=blkq, blkk=blkk, blkm=blkm,
            qg=jnp.tile(q_norm_g[l], ATT_HEADS).reshape(1, ATT_WIDTH),
            kg=jnp.tile(k_norm_g[l], ATT_KV_HEADS).reshape(1, ATT_KV_WIDTH),
            s5_bmat=_bf(bmat), s5_cre=_bf(cre), s5_cim=_bf(cim),
            s5_ab_re=jnp.broadcast_to(ab_re[l], (N_DIR, SUBLANES, S5_FLAT)),
            s5_ab_im=jnp.broadcast_to(ab_im[l], (N_DIR, SUBLANES, S5_FLAT)),
            s5_d=s5_d[l].reshape(1, S5_WIDTH), s5_glu_w=_bf(s5_glu_w[l]), s5_glu_b=s5_glu_b[l].reshape(1, S5_WIDTH),
            m_gb_row=jnp.zeros((1, LANES), F32).at[0, :M_GATE_COLS].set(gb), m_gb_col=gb.reshape(M_GATE_COLS, 1),
            m_ng=jnp.tile(m_norm_g[l], M_HEADS).reshape(1, M_WIDTH), tril=tril, triu=triu, ones=ones,
            w_out=_bf(w_out[l]), ffn_w_gate=_bf(ffn_w_gate[l]), ffn_w_up=_bf(ffn_w_up[l]),
            ffn_conv_w=ffn_conv_w[l], ffn_conv_b=ffn_conv_b[l].reshape(1, D_FF), ffn_w_down=_bf(ffn_w_down[l]))
        xp, st = _trunk_layer(xp, batch, seq, mod_all[l, 0:1], lp, None, None)
        ctx_out.append(st)
        ctx = (cache_attn_k[:, l], cache_attn_v[:, l], state_s5_re[:, l], state_s5_im[:, l],
               state_mlstm_C[:, l], state_mlstm_n[:, l], state_mlstm_m[:, l])
        xs, _ = _trunk_layer(xs, dec_batch, dec_seq, mod_all[l, 1:1 + dec_batch], lp, ctx, rope_tabs)
    outs = [jnp.stack([s[i] for s in ctx_out], axis=1) for i in range(7)]
    return (xp.reshape(batch, seq, D_MODEL), xs.reshape(dec_batch, dec_seq, D_MODEL), *outs)
```

```python
import functools

import numpy as np
import jax
import jax.numpy as jnp
from jax import lax
from jax.experimental import pallas as pl
from jax.experimental.pallas import tpu as pltpu

F32 = jnp.float32
BF16 = jnp.bfloat16

D_MODEL = 1024
DEPTH = 2
GRID_W = 64
N_DIR = 2
EPS = 1e-6
ATT_HD = 64
ATT_WIDTH = 512
ATT_HEADS = 8
ATT_KV_HEADS = 2
ATT_GROUP = ATT_HEADS // ATT_KV_HEADS
ATT_KV_WIDTH = ATT_KV_HEADS * ATT_HD
ROPE_THETA = 10000.0
LOG2_E = 1.4426950408889634
S5_CH = 16
S5_STATE = 64
S5_WIDTH = 256
S5_GROUPS = 16
S5_FLAT = S5_GROUPS * S5_STATE
M_HD = 64
M_WIDTH = 256
M_HEADS = 4
M_GATE_COLS = 2 * N_DIR * M_HEADS
M_CHAINS = N_DIR * M_HEADS
D_FF = 2816
PROJ_MAIN = 2048
COL_K = ATT_WIDTH
COL_V = COL_K + ATT_KV_WIDTH
COL_S5 = COL_V + ATT_KV_WIDTH
COL_MQ = COL_S5 + S5_WIDTH
COL_MK = COL_MQ + M_WIDTH
COL_MV = COL_MK + M_WIDTH
COL_MO = COL_MV + M_WIDTH
LANES = 128
SUBLANES = 8
BF16_ROWS = 16
VMEM_LIMIT = 56 * 1024 * 1024

PRE_TM = 512
ATT_TQ = 512
S5_ROWS = 256
S5_SLOTS = 3
M_TILE = 128
M_BG = 4
CN_ROWS = M_HD + BF16_ROWS
POST_TM = 1024
FFN_TM = 512
FF_CHUNK = 256


def _bf(x):
    return x.astype(BF16)


def _dot(a, b):
    return jnp.dot(a, b, preferred_element_type=F32)


def _split(x):
    hi = _bf(x)
    lo = _bf(x - hi.astype(F32))
    return hi, lo


def _seg_mean_sq(x, blk):
    hi, lo = _split(x * x)
    return _dot(hi, blk) + _dot(lo, blk)


def _sigmoid(x):
    return 1.0 / (1.0 + jnp.exp(-x))


def _log_sigmoid(x):
    return -(jnp.maximum(-x, 0.0) + jnp.log1p(jnp.exp(-jnp.abs(x))))


def _params(*sem):
    return pltpu.CompilerParams(dimension_semantics=sem, vmem_limit_bytes=VMEM_LIMIT)


def _ada_kernel(c_ref, w_ref, b_ref, o_ref):
    c = c_ref[...]
    s = c * _sigmoid(c)
    o_ref[...] = _dot(_bf(s), _bf(w_ref[...])) + b_ref[...]


def _ada_call(cvec, ada_w, ada_b):
    rows = cvec.shape[0]
    tn = 1024
    n = ada_w.shape[-1]
    return pl.pallas_call(
        _ada_kernel,
        grid=(DEPTH, n // tn),
        in_specs=[pl.BlockSpec((rows, D_MODEL), lambda l, j: (0, 0)),
                  pl.BlockSpec((None, D_MODEL, tn), lambda l, j: (l, 0, j)),
                  pl.BlockSpec((None, 1, tn), lambda l, j: (l, 0, j))],
        out_specs=pl.BlockSpec((None, rows, tn), lambda l, j: (l, 0, j)),
        out_shape=jax.ShapeDtypeStruct((DEPTH, rows, n), F32),
        compiler_params=_params("arbitrary", "arbitrary"),
        name="adaln",
    )(cvec, ada_w, ada_b.reshape(DEPTH, 1, n))


def _s5_disc_kernel(are_ref, aim_ref, ldt_ref, arex_ref, aimx_ref, bre_ref, bim_ref,
                    abr_ref, abi_ref, bbr_ref, bbi_ref):
    dt = jnp.exp(ldt_ref[...])

    def disc(a_re, a_im):
        mag = jnp.exp(dt * a_re)
        ab_re = mag * jnp.cos(dt * a_im)
        ab_im = mag * jnp.sin(dt * a_im)
        den = a_re * a_re + a_im * a_im
        nr = ab_re - 1.0
        ni = ab_im
        f_re = (nr * a_re + ni * a_im) / den
        f_im = (ni * a_re - nr * a_im) / den
        return ab_re, ab_im, f_re, f_im

    ab_re, ab_im, _, _ = disc(are_ref[...], aim_ref[...])
    abr_ref[...] = ab_re
    abi_ref[...] = ab_im
    _, _, f_re, f_im = disc(arex_ref[...], aimx_ref[...])
    b_re = bre_ref[...]
    b_im = bim_ref[...]
    bbr_ref[...] = f_re * b_re - f_im * b_im
    bbi_ref[...] = f_re * b_im + f_im * b_re


def _s5_disc_call(a_re, a_im, log_dt, b_re, b_im):
    r = DEPTH * N_DIR * S5_GROUPS
    a_re2 = a_re.reshape(r, S5_STATE)
    a_im2 = a_im.reshape(r, S5_STATE)
    wide = S5_STATE * S5_CH
    out_shape = (jax.ShapeDtypeStruct((r, S5_STATE), F32), jax.ShapeDtypeStruct((r, S5_STATE), F32),
                 jax.ShapeDtypeStruct((r, wide), F32), jax.ShapeDtypeStruct((r, wide), F32))
    return pl.pallas_call(_s5_disc_kernel, out_shape=out_shape, name="s5_disc")(
        a_re2, a_im2, log_dt.reshape(r, 1),
        jnp.repeat(a_re2, S5_CH, axis=1), jnp.repeat(a_im2, S5_CH, axis=1),
        b_re.reshape(r, wide), b_im.reshape(r, wide))


def _block_diag(blocks):
    g, r, c = blocks.shape[-3:]
    eye = jnp.eye(g, dtype=blocks.dtype)
    return jnp.einsum('...grc,gh->...grhc', blocks, eye).reshape(blocks.shape[:-3] + (g * r, g * c))


def _rope(x, cos, sin_signed, second):
    width = x.shape[-1]
    quarter = ATT_HD // 4
    partner = jnp.where(second, pltpu.roll(x, quarter, axis=1), pltpu.roll(x, width - quarter, axis=1))
    return x * cos + partner * sin_signed


def _pre_kernel(*refs, rope):
    if rope:
        (x_ref, mod_ref, g1_ref, w_ref, wg_ref, blkq_ref, blkk_ref, qg_ref, kg_ref,
         cosq_ref, sinq_ref, cosk_ref, sink_ref, proj_ref, mg_ref, qn_ref, kn_ref, mqk_ref, vt_ref, mgt_ref) = refs
    else:
        (x_ref, mod_ref, g1_ref, w_ref, wg_ref, blkq_ref, blkk_ref, qg_ref, kg_ref,
         proj_ref, mg_ref, qn_ref, kn_ref, mqk_ref, vt_ref, mgt_ref) = refs
    x = x_ref[...]
    ms = jnp.mean(x * x, axis=-1, keepdims=True)
    xn = x * lax.rsqrt(ms + EPS) * g1_ref[...]
    h = _bf(xn * (1.0 + mod_ref[1:2, :]) + mod_ref[0:1, :])
    proj = _dot(h, w_ref[...])
    proj_ref[...] = proj
    mg = _dot(h, wg_ref[...])
    mg_ref[...] = mg
    mqk_ref[:, :M_WIDTH] = _bf(proj[:, COL_MQ:COL_MK])
    mqk_ref[:, M_WIDTH:] = _bf(proj[:, COL_MK:COL_MV] * (M_HD ** -0.5))
    vt_ref[...] = proj[:, COL_MV:COL_MO].T
    mgt_ref[...] = mg.T[:M_GATE_COLS, :]
    q = proj[:, :COL_K]
    k = proj[:, COL_K:COL_V]
    qn = q * lax.rsqrt(_seg_mean_sq(q, blkq_ref[...]) + EPS) * qg_ref[...]
    kn = k * lax.rsqrt(_seg_mean_sq(k, blkk_ref[...]) + EPS) * kg_ref[...]
    if rope:
        lane_q = lax.broadcasted_iota(jnp.int32, qn.shape, 1)
        lane_k = lax.broadcasted_iota(jnp.int32, kn.shape, 1)
        qn = _rope(qn, cosq_ref[...], sinq_ref[...], (lane_q & (ATT_HD // 4)) != 0)
        kn = _rope(kn, cosk_ref[...], sink_ref[...], (lane_k & (ATT_HD // 4)) != 0)
    qn_ref[...] = qn
    kn_ref[...] = kn


def _rope_tables(seq_len):
    n_rows = seq_len // GRID_W
    row = jnp.repeat(jnp.arange(n_rows), GRID_W)
    col = jnp.tile(jnp.arange(GRID_W), n_rows)
    half = ATT_HD // 2
    inv_freq = 1.0 / (ROPE_THETA ** (jnp.arange(0, half, 2, dtype=F32) / half))

    def tables(pos):
        ang = pos.astype(F32)[:, None] * inv_freq[None, :]
        cos = jnp.cos(ang)
        sin = jnp.sin(ang)
        return jnp.concatenate([cos, cos], axis=-1), jnp.concatenate([-sin, sin], axis=-1)

    cr, sr = tables(row)
    cc, sc = tables(col)
    cos = jnp.concatenate([cr, cc], axis=-1)
    sin = jnp.concatenate([sr, sc], axis=-1)
    return (jnp.tile(cos, (1, ATT_HEADS)), jnp.tile(sin, (1, ATT_HEADS)),
            jnp.tile(cos, (1, ATT_KV_HEADS)), jnp.tile(sin, (1, ATT_KV_HEADS)))


def _layered(layer, block, index_map, **kw):
    return pl.BlockSpec((None,) + tuple(block), lambda *g: (layer,) + tuple(index_map(*g)), **kw)


def _mod_spec(layer, row_of):
    return pl.BlockSpec((None, None, 6, D_MODEL), lambda *g: (layer, row_of(*g), 0, 0))


def _pre_call(x2, layer, pp, rope_tabs, seq_len, per_batch_mod):
    t = x2.shape[0]
    tm = min(PRE_TM, seq_len)
    tiles_per_seq = seq_len // tm
    rope = rope_tabs is not None
    if per_batch_mod:
        mod_row = lambda i: 1 + i // tiles_per_seq
    else:
        mod_row = lambda i: 0
    const = lambda i: (0, 0)
    row = lambda i: (i, 0)
    in_specs = [pl.BlockSpec((tm, D_MODEL), row),
                _mod_spec(layer, mod_row),
                _layered(layer, (1, D_MODEL), const),
                _layered(layer, (D_MODEL, PROJ_MAIN), const),
                _layered(layer, (D_MODEL, LANES), const),
                pl.BlockSpec((ATT_WIDTH, ATT_WIDTH), const),
                pl.BlockSpec((ATT_KV_WIDTH, ATT_KV_WIDTH), const),
                _layered(layer, (1, ATT_WIDTH), const),
                _layered(layer, (1, ATT_KV_WIDTH), const)]
    args = [x2, pp['mod'], pp['g1'], pp['w_in'], pp['w_gatecols'], pp['blkq'], pp['blkk'], pp['qg'], pp['kg']]
    if rope:
        pos_map = lambda i: (i % tiles_per_seq, 0)
        in_specs += [pl.BlockSpec((tm, ATT_WIDTH), pos_map), pl.BlockSpec((tm, ATT_WIDTH), pos_map),
                     pl.BlockSpec((tm, ATT_KV_WIDTH), pos_map), pl.BlockSpec((tm, ATT_KV_WIDTH), pos_map)]
        args += list(rope_tabs)
    time_on_lanes = lambda i: (i // tiles_per_seq, 0, i % tiles_per_seq)
    batch = t // seq_len
    out_specs = [pl.BlockSpec((tm, PROJ_MAIN), row),
                 pl.BlockSpec((tm, LANES), row),
                 pl.BlockSpec((tm, ATT_WIDTH), row),
                 pl.BlockSpec((tm, ATT_KV_WIDTH), row),
                 pl.BlockSpec((tm, 2 * M_WIDTH), row),
                 pl.BlockSpec((None, M_WIDTH, tm), time_on_lanes),
                 pl.BlockSpec((None, M_GATE_COLS, tm), time_on_lanes)]
    out_shape = [jax.ShapeDtypeStruct((t, PROJ_MAIN), F32), jax.ShapeDtypeStruct((t, LANES), F32),
                 jax.ShapeDtypeStruct((t, ATT_WIDTH), F32), jax.ShapeDtypeStruct((t, ATT_KV_WIDTH), F32),
                 jax.ShapeDtypeStruct((t, 2 * M_WIDTH), BF16),
                 jax.ShapeDtypeStruct((batch, M_WIDTH, seq_len), F32),
                 jax.ShapeDtypeStruct((batch, M_GATE_COLS, seq_len), F32)]
    return pl.pallas_call(
        functools.partial(_pre_kernel, rope=rope),
        grid=(t // tm,), in_specs=in_specs, out_specs=out_specs, out_shape=out_shape,
        compiler_params=_params("arbitrary"), name="pre_rope" if rope else "pre",
    )(*args)


def _attn_kernel(*refs, cached):
    if cached:
        q_ref, k_ref, v_ref, ck_ref, cv_ref, o_ref = refs
        k_all = _bf(jnp.concatenate([ck_ref[...], k_ref[...]], axis=0))
        v_all = _bf(jnp.concatenate([cv_ref[...], v_ref[...]], axis=0))
    else:
        q_ref, k_ref, v_ref, o_ref = refs
        k_all = _bf(k_ref[...])
        v_all = _bf(v_ref[...])
    scale = ATT_HD ** -0.5 * LOG2_E
    outs = []
    for kvh in range(ATT_KV_HEADS):
        lo = kvh * ATT_HD
        kb = k_all[:, lo:lo + ATT_HD]
        vb = v_all[:, lo:lo + ATT_HD]
        for g in range(ATT_GROUP):
            c0 = (kvh * ATT_GROUP + g) * ATT_HD
            qb = _bf(q_ref[:, c0:c0 + ATT_HD] * scale)
            s = lax.dot_general(qb, kb, (((1,), (1,)), ((), ())), preferred_element_type=F32)
            m = jnp.max(s, axis=-1, keepdims=True)
            e = jnp.exp2(s - m)
            den = jnp.sum(e, axis=-1, keepdims=True)
            outs.append(_dot(_bf(e), vb) / den)
    o_ref[...] = jnp.concatenate(outs, axis=-1)


def _attn_call(qn, kn, proj, layer, cache_k, cache_v, batch, lq):
    tq = min(ATT_TQ, lq)
    nq = lq // tq
    cached = cache_k is not None
    in_specs = [pl.BlockSpec((tq, ATT_WIDTH), lambda b, i: (b * nq + i, 0)),
                pl.BlockSpec((lq, ATT_KV_WIDTH), lambda b, i: (b, 0)),
                pl.BlockSpec((lq, ATT_KV_WIDTH), lambda b, i: (b, COL_V // ATT_KV_WIDTH))]
    args = [qn, kn, proj]
    if cached:
        past = cache_k.shape[2]
        cache_spec = pl.BlockSpec((None, None, past, ATT_KV_WIDTH), lambda b, i: (b, layer, 0, 0))
        in_specs += [cache_spec, cache_spec]
        args += [cache_k, cache_v]
    return pl.pallas_call(
        functools.partial(_attn_kernel, cached=cached),
        grid=(batch, nq), in_specs=in_specs,
        out_specs=pl.BlockSpec((tq, ATT_WIDTH), lambda b, i: (b * nq + i, 0)),
        out_shape=jax.ShapeDtypeStruct((batch * lq, ATT_WIDTH), F32),
        compiler_params=_params("arbitrary", "arbitrary"), name="attn_cached" if cached else "attn",
    )(*args)


def _s5_kernel(*refs, batch, seq_len, zero_init):
    if zero_init:
        (u_ref, bmat_ref, cre_ref, cim_ref, ar_ref, ai_ref, d_ref, gw_ref, gb_ref,
         out_ref, fr_ref, fi_ref, utb_ref, ytb_ref, xs_ref) = refs
        ir_ref = ii_ref = None
    else:
        (u_ref, bmat_ref, cre_ref, cim_ref, ar_ref, ai_ref, d_ref, gw_ref, gb_ref, ir_ref, ii_ref,
         out_ref, fr_ref, fi_ref, utb_ref, ytb_ref, xs_ref) = refs
    rows = S5_ROWS
    tc = rows // batch
    n_chunks = seq_len // tc
    n_sub = batch // SUBLANES
    halves = S5_WIDTH // LANES

    def aligned(x, m):
        return x if isinstance(x, int) else pl.multiple_of(x, m)

    for b in range(batch):
        for hf in range(halves):
            utb_ref[hf, pl.ds(b, seq_len, stride=batch), :] = (
                u_ref[b * seq_len:(b + 1) * seq_len, hf * LANES:(hf + 1) * LANES])
    for hf in range(halves):
        ytb_ref[hf] = utb_ref[hf] * d_ref[:, hf * LANES:(hf + 1) * LANES]

    def chunk_of(d, i):
        return i if d == 0 else n_chunks - 1 - i

    def stage_in(i, slot):
        for d in range(N_DIR):
            r0 = aligned(chunk_of(d, i) * rows, rows)
            u_c = jnp.concatenate([utb_ref[hf, pl.ds(r0, rows), :] for hf in range(halves)], axis=-1)
            xs_ref[slot, d] = _dot(_bf(u_c), bmat_ref[d])

    def stage_scan(slot, carry):
        new_carry = []
        for d in range(N_DIR):
            a_re = ar_ref[d]
            a_im = ai_ref[d]
            per_sub = []
            for sub in range(n_sub):
                s_re, s_im = carry[d][sub]
                for t in range(tc):
                    r = (t if d == 0 else tc - 1 - t) * batch + sub * SUBLANES
                    x_re = xs_ref[slot, d, r:r + SUBLANES, :S5_FLAT]
                    x_im = xs_ref[slot, d, r:r + SUBLANES, S5_FLAT:]
                    s_re, s_im = (a_re * s_re - a_im * s_im + x_re, a_re * s_im + a_im * s_re + x_im)
                    xs_ref[slot, d, r:r + SUBLANES, :S5_FLAT] = s_re
                    xs_ref[slot, d, r:r + SUBLANES, S5_FLAT:] = s_im
                per_sub.append((s_re, s_im))
            new_carry.append(tuple(per_sub))
        return tuple(new_carry)

    def stage_out(i, slot):
        for d in range(N_DIR):
            r0 = aligned(chunk_of(d, i) * rows, rows)
            y = (_dot(_bf(xs_ref[slot, d, :, :S5_FLAT]), cre_ref[d])
                 - _dot(_bf(xs_ref[slot, d, :, S5_FLAT:]), cim_ref[d]))
            for hf in range(halves):
                ytb_ref[hf, pl.ds(r0, rows), :] += y[:, hf * LANES:(hf + 1) * LANES]

    def start_state(ref, d, sub):
        if zero_init:
            return jnp.zeros((SUBLANES, S5_FLAT), F32)
        return ref[d, sub * SUBLANES:(sub + 1) * SUBLANES, :]

    carry = tuple(tuple((start_state(ir_ref, d, sub), start_state(ii_ref, d, sub)) for sub in range(n_sub))
                  for d in range(N_DIR))
    def step(i, phase, carry):
        stage_in(i + 1, (phase + 1) % S5_SLOTS)
        carry = stage_scan(phase, carry)
        stage_out(i - 1, (phase - 1) % S5_SLOTS)
        return carry

    stage_in(0, 0)
    stage_in(1, 1)
    carry = stage_scan(0, carry)
    n_steady = n_chunks - 2
    n_peeled = n_steady % S5_SLOTS
    for i in range(1, 1 + n_peeled):
        carry = step(i, i % S5_SLOTS, carry)
    first = 1 + n_peeled

    def body(g, carry):
        for k in range(S5_SLOTS):
            carry = step(first + g * S5_SLOTS + k, (first + k) % S5_SLOTS, carry)
        return carry

    carry = lax.fori_loop(0, n_steady // S5_SLOTS, body, carry)
    last = n_chunks - 1
    carry = stage_scan(last % S5_SLOTS, carry)
    stage_out(last - 1, (last - 1) % S5_SLOTS)
    stage_out(last, last % S5_SLOTS)
    for d in range(N_DIR):
        for sub in range(n_sub):
            fr_ref[d, sub * SUBLANES:(sub + 1) * SUBLANES, :] = carry[d][sub][0]
            fi_ref[d, sub * SUBLANES:(sub + 1) * SUBLANES, :] = carry[d][sub][1]

    for b in range(batch):
        z = jax.nn.gelu(jnp.concatenate(
            [ytb_ref[hf, pl.ds(b, seq_len, stride=batch), :] for hf in range(halves)], axis=-1))
        gate = _sigmoid(_dot(_bf(z), gw_ref[...]) + gb_ref[...])
        out_ref[b * seq_len:(b + 1) * seq_len, :] = z * gate


def _s5_call(proj, layer, pp, init_re, init_im, batch, seq_len):
    rows = seq_len * batch
    single = pl.Buffered(1)
    zero_init = init_re is None
    names = ['s5_bmat', 's5_cre', 's5_cim', 's5_ab_re', 's5_ab_im', 's5_d', 's5_glu_w', 's5_glu_b']
    args = [pp[n] for n in names] + ([] if zero_init else [init_re, init_im])
    whole = lambda a: _layered(layer, a.shape[1:], lambda i: (0,) * (a.ndim - 1), pipeline_mode=single)
    in_specs = ([pl.BlockSpec((rows, S5_WIDTH), lambda i: (0, COL_S5 // S5_WIDTH), pipeline_mode=single)]
                + [whole(a) for a in args])
    fin_spec = pl.BlockSpec((N_DIR, batch, S5_FLAT), lambda i: (0, 0, 0))
    out_specs = [pl.BlockSpec((rows, S5_WIDTH), lambda i: (0, 0), pipeline_mode=single), fin_spec, fin_spec]
    out_shape = (jax.ShapeDtypeStruct((rows, S5_WIDTH), F32),
                 jax.ShapeDtypeStruct((N_DIR, batch, S5_FLAT), F32),
                 jax.ShapeDtypeStruct((N_DIR, batch, S5_FLAT), F32))
    return pl.pallas_call(
        functools.partial(_s5_kernel, batch=batch, seq_len=seq_len, zero_init=zero_init),
        grid=(1,), in_specs=in_specs, out_specs=out_specs, out_shape=out_shape,
        scratch_shapes=[pltpu.VMEM((S5_WIDTH // LANES, rows, LANES), F32),
                        pltpu.VMEM((S5_WIDTH // LANES, rows, LANES), F32),
                        pltpu.VMEM((S5_SLOTS, N_DIR, S5_ROWS, 2 * S5_FLAT), F32)],
        compiler_params=_params("arbitrary"), name="s5",
    )(proj, *args)


def _mlstm_kernel(*refs, seq_len, zero_init):
    if zero_init:
        (q_ref, k_ref, vt_ref, mg_ref, mgt_ref, gbr_ref, gbc_ref, tril_ref, triu_ref, ones_ref,
         ht_ref, cn_ref, m_ref) = refs
    else:
        (q_ref, k_ref, vt_ref, mg_ref, mgt_ref, gbr_ref, gbc_ref, tril_ref, triu_ref, ones_ref,
         c0_ref, n0_ref, m0_ref, ht_ref, cn_ref, m_ref) = refs
    _mlstm_body(q_ref, k_ref, vt_ref, mg_ref, mgt_ref, gbr_ref, gbc_ref, tril_ref, triu_ref, ones_ref,
                None if zero_init else (c0_ref, n0_ref, m0_ref), ht_ref, cn_ref, m_ref, seq_len)


def _mlstm_body(q_ref, k_ref, vt_ref, mg_ref, mgt_ref, gbr_ref, gbc_ref, tril_ref, triu_ref, ones_ref,
                init_refs, ht_ref, cn_ref, m_ref, seq_len):
    tile = M_TILE
    n_chunks = seq_len // tile
    bg = q_ref.shape[0]
    if init_refs is None:
        cn_ref[...] = jnp.zeros_like(cn_ref)
        m_ref[...] = jnp.zeros_like(m_ref)
    else:
        c0_ref, n0_ref, m0_ref = init_refs
        cn_ref[:, :, :M_HD, :] = c0_ref[...]
        cn_ref[:, :, M_HD:, :] = n0_ref[...]
        m_ref[...] = m0_ref[...]
    ht_ref[...] = jnp.zeros_like(ht_ref)
    s_ids = lax.broadcasted_iota(jnp.int32, (tile, tile), 0)
    t_ids = lax.broadcasted_iota(jnp.int32, (tile, tile), 1)
    first_row = lax.broadcasted_iota(jnp.int32, (CN_ROWS - M_HD, tile), 0) == 0
    neg_inf = jnp.float32(-jnp.inf)
    ones = ones_ref[...]

    def direction(d, c):
        t0 = pl.multiple_of(c * tile, tile)
        col_mat = tril_ref[...] if d == 0 else triu_ref[...]
        row_mat = triu_ref[...] if d == 0 else tril_ref[...]
        valid = (s_ids <= t_ids) if d == 0 else (s_ids >= t_ids)
        gt = mgt_ref[:, :, pl.ds(t0, tile)] + gbc_ref[...]
        lft_hi, lft_lo = _split(_log_sigmoid(gt).reshape(bg * M_GATE_COLS, tile))
        b_rows = (_dot(lft_hi, row_mat) + _dot(lft_lo, row_mat)).reshape(bg, M_GATE_COLS, tile)
        totals = (_dot(lft_hi, ones) + _dot(lft_lo, ones)).reshape(bg, M_GATE_COLS, tile)
        gc = mg_ref[:, pl.ds(t0, tile), :] + gbr_ref[...]
        lfc_hi, lfc_lo = _split(_log_sigmoid(gc))
        col_b = jnp.broadcast_to(col_mat, (bg, tile, tile))
        b_cols = (jnp.einsum('bts,bsl->btl', col_b, lfc_hi, preferred_element_type=F32)
                  + jnp.einsum('bts,bsl->btl', col_b, lfc_lo, preferred_element_type=F32))
        for h in range(M_HEADS):
            j = d * M_HEADS + h
            fj = M_CHAINS + j
            rows = slice(h * M_HD, (h + 1) * M_HD)
            q_c = q_ref[:, pl.ds(t0, tile), rows]
            k_c = k_ref[:, pl.ds(t0, tile), rows]
            vt_c = vt_ref[:, rows, pl.ds(t0, tile)]
            b_t = b_rows[:, fj:fj + 1, :]
            li_t = gt[:, j:j + 1, :]
            tot = totals[:, fj:fj + 1, :]
            m_prev = m_ref[:, j]
            cn_prev = cn_ref[:, j]

            c_col = gc[:, :, j:j + 1] - b_cols[:, :, fj:fj + 1]
            cm = jnp.where(valid, c_col, neg_inf)
            inter = b_t + m_prev
            m_t = jnp.maximum(inter, b_t + jnp.max(cm, axis=1, keepdims=True))
            w_t = jnp.exp(cm + (b_t - m_t))
            s_inter = jnp.exp(inter - m_t)
            sc_t = jnp.einsum('bse,bte->bst', k_c, q_c, preferred_element_type=F32) * w_t
            num = jnp.einsum('bds,bst->bdt', _bf(vt_c), _bf(sc_t), preferred_element_type=F32)
            ext = jnp.einsum('bre,bte->brt', _bf(cn_prev), q_c, preferred_element_type=F32)
            num = num + s_inter * ext[:, :M_HD, :]
            den = jnp.sum(sc_t, axis=1, keepdims=True) + s_inter * ext[:, M_HD:M_HD + 1, :]
            ht_ref[:, rows, pl.ds(t0, tile)] += num / jnp.maximum(jnp.abs(den), jnp.exp(-m_t))

            g_row = tot - b_t + li_t
            m_new = jnp.maximum(tot + m_prev, jnp.max(g_row, axis=-1, keepdims=True))
            wk = jnp.exp(g_row - m_new)
            decay = jnp.exp(tot + m_prev - m_new)
            vw = jnp.concatenate([vt_c * wk, jnp.where(first_row, wk, 0.0)], axis=1)
            cn_ref[:, j] = (decay[:, :, :M_HD] * cn_prev
                            + jnp.einsum('brs,bse->bre', _bf(vw), k_c, preferred_element_type=F32))
            m_ref[:, j] = m_new

    def body(i, _):
        direction(0, i)
        direction(1, n_chunks - 1 - i)
        return 0

    lax.fori_loop(0, n_chunks, body, 0)


def _mlstm_call(mqk3, vt, mg3, mgt, layer, pp, init, batch, seq_len):
    bg = M_BG
    const2 = lambda g: (0, 0)
    lead3 = lambda g: (g, 0, 0)
    lead4 = lambda g: (g, 0, 0, 0)
    in_specs = [pl.BlockSpec((bg, seq_len, M_WIDTH), lead3),
                pl.BlockSpec((bg, seq_len, M_WIDTH), lambda g: (g, 0, 1)),
                pl.BlockSpec((bg, M_WIDTH, seq_len), lead3),
                pl.BlockSpec((bg, seq_len, LANES), lead3),
                pl.BlockSpec((bg, M_GATE_COLS, seq_len), lead3),
                _layered(layer, (1, LANES), const2),
                _layered(layer, (M_GATE_COLS, 1), const2),
                pl.BlockSpec((M_TILE, M_TILE), const2),
                pl.BlockSpec((M_TILE, M_TILE), const2),
                pl.BlockSpec((M_TILE, M_TILE), const2)]
    args = [mqk3, mqk3, vt, mg3, mgt, pp['m_gb_row'], pp['m_gb_col'], pp['tril'], pp['triu'], pp['ones']]
    if init is not None:
        state_spec = lambda *tail: pl.BlockSpec((bg, None, M_CHAINS) + tail, lambda g: (g, layer, 0, 0, 0))
        in_specs += [state_spec(M_HD, M_HD), state_spec(CN_ROWS - M_HD, M_HD), state_spec(1, LANES)]
        args += list(init)
    out_specs = [pl.BlockSpec((bg, M_WIDTH, seq_len), lead3),
                 pl.BlockSpec((bg, M_CHAINS, CN_ROWS, M_HD), lead4),
                 pl.BlockSpec((bg, M_CHAINS, 1, LANES), lead4)]
    out_shape = [jax.ShapeDtypeStruct((batch, M_WIDTH, seq_len), F32),
                 jax.ShapeDtypeStruct((batch, M_CHAINS, CN_ROWS, M_HD), F32),
                 jax.ShapeDtypeStruct((batch, M_CHAINS, 1, LANES), F32)]
    return pl.pallas_call(
        functools.partial(_mlstm_kernel, seq_len=seq_len, zero_init=init is None),
        grid=(batch // bg,), in_specs=in_specs, out_specs=out_specs, out_shape=out_shape,
        compiler_params=_params("arbitrary"), name="mlstm",
    )(*args)


def _post_kernel(x_ref, att_ref, s5_ref, ht_ref, mo_ref, mod_ref, g2_ref, ng_ref, blkm_ref, wo_ref,
                 x1_ref, h2_ref):
    mh = jnp.concatenate([ht_ref[s].T for s in range(ht_ref.shape[0])], axis=0)
    ml = mh * lax.rsqrt(_seg_mean_sq(mh, blkm_ref[...]) + EPS) * ng_ref[...] * _sigmoid(mo_ref[...])
    mix = (_dot(_bf(att_ref[...]), wo_ref[:ATT_WIDTH, :])
           + _dot(_bf(s5_ref[...]), wo_ref[ATT_WIDTH:ATT_WIDTH + S5_WIDTH, :])
           + _dot(_bf(ml), wo_ref[ATT_WIDTH + S5_WIDTH:, :]))
    x1 = x_ref[...] + mod_ref[2:3, :] * mix
    x1_ref[...] = x1
    ms = jnp.mean(x1 * x1, axis=-1, keepdims=True)
    xn = x1 * lax.rsqrt(ms + EPS) * g2_ref[...]
    h2_ref[...] = _bf(xn * (1.0 + mod_ref[4:5, :]) + mod_ref[3:4, :])


def _post_call(x2, att, s5o, ht, proj, layer, pp, seq_len, per_batch_mod):
    t = x2.shape[0]
    tm = POST_TM
    seqs_per_tile = tm // seq_len
    if per_batch_mod:
        mod_row = lambda i: 1 + i * seqs_per_tile
    else:
        mod_row = lambda i: 0
    row = lambda i: (i, 0)
    const = lambda i: (0, 0)
    in_specs = [pl.BlockSpec((tm, D_MODEL), row),
                pl.BlockSpec((tm, ATT_WIDTH), row),
                pl.BlockSpec((tm, S5_WIDTH), row),
                pl.BlockSpec((seqs_per_tile, M_WIDTH, seq_len), lambda i: (i, 0, 0)),
                pl.BlockSpec((tm, M_WIDTH), lambda i: (i, COL_MO // M_WIDTH)),
                _mod_spec(layer, mod_row),
                _layered(layer, (1, D_MODEL), const),
                _layered(layer, (1, M_WIDTH), const),
                pl.BlockSpec((M_WIDTH, M_WIDTH), const),
                _layered(layer, (D_MODEL, D_MODEL), const)]
    return pl.pallas_call(
        _post_kernel,
        grid=(t // tm,), in_specs=in_specs,
        out_specs=[pl.BlockSpec((tm, D_MODEL), row), pl.BlockSpec((tm, D_MODEL), row)],
        out_shape=[jax.ShapeDtypeStruct((t, D_MODEL), F32), jax.ShapeDtypeStruct((t, D_MODEL), BF16)],
        compiler_params=_params("arbitrary"), name="post",
    )(x2, att, s5o, ht, proj, pp['mod'], pp['g2'], pp['m_ng'], pp['blkm'], pp['w_out'])


def _ffn_kernel(x1_ref, h2_ref, hprev_ref, hnext_ref, mod_ref, wg_ref, wu_ref, cw_ref, cb_ref, wd_ref, o_ref,
                gated_ref, *, seq_len):
    tm = h2_ref.shape[0]
    h2 = h2_ref[...]
    h2_ext = jnp.concatenate([h2, hprev_ref[...], hnext_ref[...]], axis=0)
    row = lax.broadcasted_iota(jnp.int32, (tm, FF_CHUNK), 0)
    pos = (pl.program_id(0) * tm + row) % seq_len
    seq_start = pos == 0
    seq_end = pos == seq_len - 1
    tile_start = row == 0
    tile_end = row == tm - 1
    for j in range(D_FF // FF_CHUNK):
        cols = slice(j * FF_CHUNK, (j + 1) * FF_CHUNK)
        a_ext = _dot(h2_ext, wg_ref[:, cols])
        a = a_ext[:tm]
        before = a_ext[tm + BF16_ROWS - 1:tm + BF16_ROWS, :]
        after = a_ext[tm + BF16_ROWS:tm + BF16_ROWS + 1, :]
        a_prev = jnp.where(tile_start, before, pltpu.roll(a, 1, axis=0))
        a_next = jnp.where(tile_end, after, pltpu.roll(a, tm - 1, axis=0))
        a_prev = jnp.where(seq_start, 0.0, a_prev)
        a_next = jnp.where(seq_end, 0.0, a_next)
        ac = a_prev * cw_ref[0:1, cols] + a * cw_ref[1:2, cols] + a_next * cw_ref[2:3, cols] + cb_ref[:, cols]
        up = _dot(h2, wu_ref[:, cols])
        gated_ref[:, cols] = _bf(ac * _sigmoid(ac) * up)
    o_ref[...] = x1_ref[...] + mod_ref[5:6, :] * _dot(gated_ref[...], wd_ref[...])


def _ffn_call(x1, h2, layer, pp, seq_len, per_batch_mod):
    t = x1.shape[0]
    tm = FFN_TM
    halo_blocks_per_tile = tm // BF16_ROWS
    last_halo_block = t // BF16_ROWS - 1
    if per_batch_mod:
        mod_row = lambda i: 1 + (i * tm) // seq_len
    else:
        mod_row = lambda i: 0
    row = lambda i: (i, 0)
    single = pl.Buffered(1)
    whole = lambda shape: _layered(layer, shape, lambda i: (0, 0), pipeline_mode=single)
    in_specs = [pl.BlockSpec((tm, D_MODEL), row),
                pl.BlockSpec((tm, D_MODEL), row),
                pl.BlockSpec((BF16_ROWS, D_MODEL), lambda i: (jnp.maximum(i * halo_blocks_per_tile - 1, 0), 0)),
                pl.BlockSpec((BF16_ROWS, D_MODEL),
                             lambda i: (jnp.minimum((i + 1) * halo_blocks_per_tile, last_halo_block), 0)),
                _mod_spec(layer, mod_row),
                whole((D_MODEL, D_FF)), whole((D_MODEL, D_FF)), whole((3, D_FF)), whole((1, D_FF)),
                whole((D_FF, D_MODEL))]
    return pl.pallas_call(
        functools.partial(_ffn_kernel, seq_len=seq_len),
        grid=(t // tm,), in_specs=in_specs,
        out_specs=pl.BlockSpec((tm, D_MODEL), row),
        out_shape=jax.ShapeDtypeStruct((t, D_MODEL), F32),
        scratch_shapes=[pltpu.VMEM((tm, D_FF), BF16)],
        compiler_params=_params("arbitrary"), name="ffn",
    )(x1, h2, h2, h2, pp['mod'], pp['ffn_w_gate'], pp['ffn_w_up'], pp['ffn_conv_w'], pp['ffn_conv_b'],
      pp['ffn_w_down'])


def _trunk_layer(x2, batch, seq_len, layer, pp, ctx, rope_tabs):
    latent = ctx is not None
    proj, mg, qn, kn, mqk, vt, mgt = _pre_call(x2, layer, pp, rope_tabs if latent else None, seq_len, latent)
    cache_k, cache_v, init_re, init_im, m_init = ctx if latent else (None,) * 5

    att = _attn_call(qn, kn, proj, layer, cache_k, cache_v, batch, seq_len)
    s5o, fin_re, fin_im = _s5_call(proj, layer, pp, init_re, init_im, batch, seq_len)
    ht, cn_f, m_f = _mlstm_call(mqk.reshape(batch, seq_len, 2 * M_WIDTH), vt, mg.reshape(batch, seq_len, LANES), mgt,
                                layer, pp, m_init, batch, seq_len)
    x1, h2 = _post_call(x2, att, s5o, ht, proj, layer, pp, seq_len, latent)
    x_out = _ffn_call(x1, h2, layer, pp, seq_len, latent)
    if latent:
        return x_out, None
    states = (kn.reshape(batch, seq_len, ATT_KV_HEADS, ATT_HD),
              proj[:, COL_V:COL_S5].reshape(batch, seq_len, ATT_KV_HEADS, ATT_HD),
              fin_re.transpose(1, 0, 2).reshape(batch, N_DIR, S5_GROUPS, S5_STATE),
              fin_im.transpose(1, 0, 2).reshape(batch, N_DIR, S5_GROUPS, S5_STATE),
              cn_f[:, :, :M_HD, :].reshape(batch, N_DIR, M_HEADS, M_HD, M_HD),
              cn_f[:, :, M_HD, :].reshape(batch, N_DIR, M_HEADS, M_HD),
              m_f[:, :, 0, 0].reshape(batch, N_DIR, M_HEADS))
    return x_out, states


def _head_block(width, head_dim):
    ids = np.arange(width) // head_dim
    return jnp.asarray((ids[:, None] == ids[None, :]).astype(np.float32) / head_dim, dtype=BF16)


def kernel(x_prompt, x_sample, c, cache_attn_k, cache_attn_v, state_s5_re, state_s5_im, state_mlstm_C, state_mlstm_n, state_mlstm_m, c_ctx, ada_w, ada_b, norm1_g, norm2_g, w_in, q_norm_g, k_norm_g, s5_a_re, s5_a_im, s5_log_dt, s5_b_re, s5_b_im, s5_c_re, s5_c_im, s5_d, s5_glu_w, s5_glu_b, m_gate_b, m_norm_g, w_out, ffn_w_gate, ffn_w_up, ffn_conv_w, ffn_conv_b, ffn_w_down):
    batch, seq = x_prompt.shape[0], x_prompt.shape[1]
    dec_batch, dec_seq = x_sample.shape[0], x_sample.shape[1]

    n_mod_rows = 2 * SUBLANES
    cvec = jnp.zeros((n_mod_rows, D_MODEL), F32).at[0].set(c_ctx).at[1:1 + dec_batch].set(c)
    mod_all = _ada_call(cvec, ada_w, ada_b).reshape(DEPTH, n_mod_rows, 6, D_MODEL)

    ab_re, ab_im, bb_re, bb_im = _s5_disc_call(s5_a_re, s5_a_im, s5_log_dt, s5_b_re, s5_b_im)
    ab_re = ab_re.reshape(DEPTH, N_DIR, 1, S5_FLAT)
    ab_im = ab_im.reshape(DEPTH, N_DIR, 1, S5_FLAT)
    bb_re = bb_re.reshape(DEPTH, N_DIR, S5_GROUPS, S5_STATE, S5_CH)
    bb_im = bb_im.reshape(DEPTH, N_DIR, S5_GROUPS, S5_STATE, S5_CH)

    rope_tabs = _rope_tables(dec_seq)
    tri = np.tril(np.ones((M_TILE, M_TILE), np.float32))
    tril = jnp.asarray(tri, dtype=BF16)
    triu = jnp.asarray(tri.T, dtype=BF16)
    ones = jnp.ones((M_TILE, M_TILE), BF16)
    blkq = _head_block(ATT_WIDTH, ATT_HD)
    blkk = _head_block(ATT_KV_WIDTH, ATT_HD)
    blkm = _head_block(M_WIDTH, M_HD)

    w_in_b = _bf(w_in)
    gb = m_gate_b.reshape(DEPTH, M_GATE_COLS)
    pp = dict(
        mod=mod_all, g1=norm1_g.reshape(DEPTH, 1, D_MODEL), g2=norm2_g.reshape(DEPTH, 1, D_MODEL),
        w_in=w_in_b,
        w_gatecols=jnp.zeros((DEPTH, D_MODEL, LANES), BF16).at[:, :, :M_GATE_COLS].set(w_in_b[:, :, PROJ_MAIN:]),
        blkq=blkq, blkk=blkk, blkm=blkm,
        qg=jnp.tile(q_norm_g, (1, ATT_HEADS)).reshape(DEPTH, 1, ATT_WIDTH),
        kg=jnp.tile(k_norm_g, (1, ATT_KV_HEADS)).reshape(DEPTH, 1, ATT_KV_WIDTH),
        s5_bmat=_bf(jnp.concatenate([_block_diag(jnp.swapaxes(bb_re, -1, -2)),
                                     _block_diag(jnp.swapaxes(bb_im, -1, -2))], axis=-1)),
        s5_cre=_bf(_block_diag(jnp.swapaxes(s5_c_re, -1, -2))),
        s5_cim=_bf(_block_diag(jnp.swapaxes(s5_c_im, -1, -2))),
        s5_ab_re=jnp.broadcast_to(ab_re, (DEPTH, N_DIR, SUBLANES, S5_FLAT)),
        s5_ab_im=jnp.broadcast_to(ab_im, (DEPTH, N_DIR, SUBLANES, S5_FLAT)),
        s5_d=s5_d.reshape(DEPTH, 1, S5_WIDTH), s5_glu_w=_bf(s5_glu_w), s5_glu_b=s5_glu_b.reshape(DEPTH, 1, S5_WIDTH),
        m_gb_row=jnp.zeros((DEPTH, 1, LANES), F32).at[:, 0, :M_GATE_COLS].set(gb),
        m_gb_col=gb.reshape(DEPTH, M_GATE_COLS, 1),
        m_ng=jnp.tile(m_norm_g, (1, M_HEADS)).reshape(DEPTH, 1, M_WIDTH), tril=tril, triu=triu, ones=ones,
        w_out=_bf(w_out), ffn_w_gate=_bf(ffn_w_gate), ffn_w_up=_bf(ffn_w_up),
        ffn_conv_w=ffn_conv_w, ffn_conv_b=ffn_conv_b.reshape(DEPTH, 1, D_FF), ffn_w_down=_bf(ffn_w_down))

    past = cache_attn_k.shape[2]
    n_pad = CN_ROWS - M_HD
    s5_init = lambda st: st.reshape(dec_batch, DEPTH, N_DIR, S5_FLAT).transpose(1, 2, 0, 3)
    m_init = (state_mlstm_C.reshape(dec_batch, DEPTH, M_CHAINS, M_HD, M_HD),
              jnp.zeros((dec_batch, DEPTH, M_CHAINS, n_pad, M_HD), F32).at[:, :, :, 0, :].set(
                  state_mlstm_n.reshape(dec_batch, DEPTH, M_CHAINS, M_HD)),
              jnp.broadcast_to(state_mlstm_m.reshape(dec_batch, DEPTH, M_CHAINS, 1, 1),
                               (dec_batch, DEPTH, M_CHAINS, 1, LANES)))
    ctx = (cache_attn_k.reshape(dec_batch, DEPTH, past, ATT_KV_WIDTH),
           cache_attn_v.reshape(dec_batch, DEPTH, past, ATT_KV_WIDTH),
           s5_init(state_s5_re), s5_init(state_s5_im), m_init)

    xp = x_prompt.reshape(batch * seq, D_MODEL)
    xs = x_sample.reshape(dec_batch * dec_seq, D_MODEL)
    ctx_out = []
    for l in range(DEPTH):
        xp, st = _trunk_layer(xp, batch, seq, l, pp, None, None)
        ctx_out.append(st)
        xs, _ = _trunk_layer(xs, dec_batch, dec_seq, l, pp, ctx, rope_tabs)
    outs = [jnp.stack([s[i] for s in ctx_out], axis=1) for i in range(7)]
    return (xp.reshape(batch, seq, D_MODEL), xs.reshape(dec_batch, dec_seq, D_MODEL), *outs)
```

```python
import functools

import numpy as np
import jax
import jax.numpy as jnp
from jax import lax
from jax.experimental import pallas as pl
from jax.experimental.pallas import tpu as pltpu

F32 = jnp.float32
BF16 = jnp.bfloat16

D_MODEL = 1024
DEPTH = 2
GRID_W = 64
N_DIR = 2
EPS = 1e-6
ATT_HD = 64
ATT_WIDTH = 512
ATT_HEADS = 8
ATT_KV_HEADS = 2
ATT_GROUP = ATT_HEADS // ATT_KV_HEADS
ATT_KV_WIDTH = ATT_KV_HEADS * ATT_HD
ROPE_THETA = 10000.0
LOG2_E = 1.4426950408889634
S5_CH = 16
S5_STATE = 64
S5_WIDTH = 256
S5_GROUPS = 16
S5_FLAT = S5_GROUPS * S5_STATE
M_HD = 64
M_WIDTH = 256
M_HEADS = 4
M_GATE_COLS = 2 * N_DIR * M_HEADS
M_CHAINS = N_DIR * M_HEADS
D_FF = 2816
PROJ_MAIN = 2048
COL_K = ATT_WIDTH
COL_V = COL_K + ATT_KV_WIDTH
COL_S5 = COL_V + ATT_KV_WIDTH
COL_MQ = COL_S5 + S5_WIDTH
COL_MK = COL_MQ + M_WIDTH
COL_MV = COL_MK + M_WIDTH
COL_MO = COL_MV + M_WIDTH
AUX_S5 = 0
AUX_MO = AUX_S5 + S5_WIDTH
AUX_V = AUX_MO + M_WIDTH
AUX_WIDTH = AUX_V + ATT_KV_WIDTH
LANES = 128
SUBLANES = 8
BF16_ROWS = 16
VMEM_LIMIT = 56 * 1024 * 1024

PRE_TM = 512
ATT_TQ = 512
S5_ROWS = 256
S5_SLOTS = 3
M_TILE = 128
M_BG = 4
CN_ROWS = M_HD + BF16_ROWS
POST_TM = 1024
FFN_TM = 512
FF_CHUNK = 256


def _bf(x):
    return x.astype(BF16)


def _dot(a, b):
    return jnp.dot(a, b, preferred_element_type=F32)


def _split(x):
    hi = _bf(x)
    lo = _bf(x - hi.astype(F32))
    return hi, lo


def _seg_mean_sq(x, blk):
    hi, lo = _split(x * x)
    return _dot(hi, blk) + _dot(lo, blk)


def _sigmoid(x):
    return 1.0 / (1.0 + jnp.exp(-x))


def _log_sigmoid(x):
    return -(jnp.maximum(-x, 0.0) + jnp.log1p(jnp.exp(-jnp.abs(x))))


def _params(*sem):
    return pltpu.CompilerParams(dimension_semantics=sem, vmem_limit_bytes=VMEM_LIMIT)


def _ada_kernel(c_ref, w_ref, b_ref, o_ref):
    c = c_ref[...]
    s = c * _sigmoid(c)
    o_ref[...] = _dot(_bf(s), _bf(w_ref[...])) + b_ref[...]


def _ada_call(cvec, ada_w, ada_b):
    rows = cvec.shape[0]
    tn = 1024
    n = ada_w.shape[-1]
    return pl.pallas_call(
        _ada_kernel,
        grid=(DEPTH, n // tn),
        in_specs=[pl.BlockSpec((rows, D_MODEL), lambda l, j: (0, 0)),
                  pl.BlockSpec((None, D_MODEL, tn), lambda l, j: (l, 0, j)),
                  pl.BlockSpec((None, 1, tn), lambda l, j: (l, 0, j))],
        out_specs=pl.BlockSpec((None, rows, tn), lambda l, j: (l, 0, j)),
        out_shape=jax.ShapeDtypeStruct((DEPTH, rows, n), F32),
        compiler_params=_params("arbitrary", "arbitrary"),
        name="adaln",
    )(cvec, ada_w, ada_b.reshape(DEPTH, 1, n))


def _s5_disc_kernel(are_ref, aim_ref, ldt_ref, arex_ref, aimx_ref, bre_ref, bim_ref,
                    abr_ref, abi_ref, bbr_ref, bbi_ref):
    dt = jnp.exp(ldt_ref[...])

    def disc(a_re, a_im):
        mag = jnp.exp(dt * a_re)
        ab_re = mag * jnp.cos(dt * a_im)
        ab_im = mag * jnp.sin(dt * a_im)
        den = a_re * a_re + a_im * a_im
        nr = ab_re - 1.0
        ni = ab_im
        f_re = (nr * a_re + ni * a_im) / den
        f_im = (ni * a_re - nr * a_im) / den
        return ab_re, ab_im, f_re, f_im

    ab_re, ab_im, _, _ = disc(are_ref[...], aim_ref[...])
    abr_ref[...] = ab_re
    abi_ref[...] = ab_im
    _, _, f_re, f_im = disc(arex_ref[...], aimx_ref[...])
    b_re = bre_ref[...]
    b_im = bim_ref[...]
    bbr_ref[...] = f_re * b_re - f_im * b_im
    bbi_ref[...] = f_re * b_im + f_im * b_re


def _s5_disc_call(a_re, a_im, log_dt, b_re, b_im):
    r = DEPTH * N_DIR * S5_GROUPS
    a_re2 = a_re.reshape(r, S5_STATE)
    a_im2 = a_im.reshape(r, S5_STATE)
    wide = S5_STATE * S5_CH
    out_shape = (jax.ShapeDtypeStruct((r, S5_STATE), F32), jax.ShapeDtypeStruct((r, S5_STATE), F32),
                 jax.ShapeDtypeStruct((r, wide), F32), jax.ShapeDtypeStruct((r, wide), F32))
    return pl.pallas_call(_s5_disc_kernel, out_shape=out_shape, name="s5_disc")(
        a_re2, a_im2, log_dt.reshape(r, 1),
        jnp.repeat(a_re2, S5_CH, axis=1), jnp.repeat(a_im2, S5_CH, axis=1),
        b_re.reshape(r, wide), b_im.reshape(r, wide))


def _block_diag(blocks):
    g, r, c = blocks.shape[-3:]
    lead = blocks.shape[:-3]
    tiled = jnp.tile(blocks.reshape(lead + (g * r, c)), (1,) * len(lead) + (1, g))
    on_diagonal = (np.arange(g * r)[:, None] // r) == (np.arange(g * c)[None, :] // c)
    return jnp.where(on_diagonal, tiled, 0.0)


def _rope(x, cos, sin_signed, second):
    width = x.shape[-1]
    quarter = ATT_HD // 4
    partner = jnp.where(second, pltpu.roll(x, quarter, axis=1), pltpu.roll(x, width - quarter, axis=1))
    return x * cos + partner * sin_signed


def _pre_kernel(*refs, rope):
    if rope:
        (x_ref, mod_ref, g1_ref, w_ref, wg_ref, blkq_ref, blkk_ref, qg_ref, kg_ref,
         cosq_ref, sinq_ref, cosk_ref, sink_ref, proj_ref, mg_ref, qn_ref, kn_ref, mqk_ref, vt_ref, mgt_ref) = refs
    else:
        (x_ref, mod_ref, g1_ref, w_ref, wg_ref, blkq_ref, blkk_ref, qg_ref, kg_ref,
         proj_ref, mg_ref, qn_ref, kn_ref, mqk_ref, vt_ref, mgt_ref) = refs
    x = x_ref[...]
    ms = jnp.mean(x * x, axis=-1, keepdims=True)
    xn = x * lax.rsqrt(ms + EPS) * g1_ref[...]
    h = _bf(xn * (1.0 + mod_ref[1:2, :]) + mod_ref[0:1, :])
    proj = _dot(h, w_ref[...])
    proj_ref[:, AUX_S5:AUX_MO] = proj[:, COL_S5:COL_MQ]
    proj_ref[:, AUX_MO:AUX_V] = proj[:, COL_MO:]
    proj_ref[:, AUX_V:] = proj[:, COL_V:COL_S5]
    mg = _dot(h, wg_ref[...])
    mg_ref[...] = mg
    mqk_ref[:, :M_WIDTH] = _bf(proj[:, COL_MQ:COL_MK])
    mqk_ref[:, M_WIDTH:] = _bf(proj[:, COL_MK:COL_MV] * (M_HD ** -0.5))
    vt_ref[...] = proj[:, COL_MV:COL_MO].T
    mgt_ref[...] = mg.T[:M_GATE_COLS, :]
    q = proj[:, :COL_K]
    k = proj[:, COL_K:COL_V]
    qn = q * lax.rsqrt(_seg_mean_sq(q, blkq_ref[...]) + EPS) * qg_ref[...]
    kn = k * lax.rsqrt(_seg_mean_sq(k, blkk_ref[...]) + EPS) * kg_ref[...]
    if rope:
        lane_q = lax.broadcasted_iota(jnp.int32, qn.shape, 1)
        lane_k = lax.broadcasted_iota(jnp.int32, kn.shape, 1)
        qn = _rope(qn, cosq_ref[...], sinq_ref[...], (lane_q & (ATT_HD // 4)) != 0)
        kn = _rope(kn, cosk_ref[...], sink_ref[...], (lane_k & (ATT_HD // 4)) != 0)
    qn_ref[...] = qn
    kn_ref[...] = kn


def _rope_tables(seq_len):
    n_rows = seq_len // GRID_W
    row = jnp.repeat(jnp.arange(n_rows), GRID_W)
    col = jnp.tile(jnp.arange(GRID_W), n_rows)
    half = ATT_HD // 2
    inv_freq = 1.0 / (ROPE_THETA ** (jnp.arange(0, half, 2, dtype=F32) / half))

    def tables(pos):
        ang = pos.astype(F32)[:, None] * inv_freq[None, :]
        cos = jnp.cos(ang)
        sin = jnp.sin(ang)
        return jnp.concatenate([cos, cos], axis=-1), jnp.concatenate([-sin, sin], axis=-1)

    cr, sr = tables(row)
    cc, sc = tables(col)
    cos = jnp.concatenate([cr, cc], axis=-1)
    sin = jnp.concatenate([sr, sc], axis=-1)
    return (jnp.tile(cos, (1, ATT_HEADS)), jnp.tile(sin, (1, ATT_HEADS)),
            jnp.tile(cos, (1, ATT_KV_HEADS)), jnp.tile(sin, (1, ATT_KV_HEADS)))


def _layered(layer, block, index_map, **kw):
    return pl.BlockSpec((None,) + tuple(block), lambda *g: (layer,) + tuple(index_map(*g)), **kw)


def _mod_spec(layer, row_of):
    return pl.BlockSpec((None, None, 6, D_MODEL), lambda *g: (layer, row_of(*g), 0, 0))


def _pre_call(x2, layer, pp, rope_tabs, seq_len, per_batch_mod):
    t = x2.shape[0]
    tm = min(PRE_TM, seq_len)
    tiles_per_seq = seq_len // tm
    rope = rope_tabs is not None
    if per_batch_mod:
        mod_row = lambda i: 1 + i // tiles_per_seq
    else:
        mod_row = lambda i: 0
    const = lambda i: (0, 0)
    row = lambda i: (i, 0)
    in_specs = [pl.BlockSpec((tm, D_MODEL), row),
                _mod_spec(layer, mod_row),
                _layered(layer, (1, D_MODEL), const),
                _layered(layer, (D_MODEL, PROJ_MAIN), const),
                _layered(layer, (D_MODEL, LANES), const),
                pl.BlockSpec((ATT_WIDTH, ATT_WIDTH), const),
                pl.BlockSpec((ATT_KV_WIDTH, ATT_KV_WIDTH), const),
                _layered(layer, (1, ATT_WIDTH), const),
                _layered(layer, (1, ATT_KV_WIDTH), const)]
    args = [x2, pp['mod'], pp['g1'], pp['w_in'], pp['w_gatecols'], pp['blkq'], pp['blkk'], pp['qg'], pp['kg']]
    if rope:
        pos_map = lambda i: (i % tiles_per_seq, 0)
        in_specs += [pl.BlockSpec((tm, ATT_WIDTH), pos_map), pl.BlockSpec((tm, ATT_WIDTH), pos_map),
                     pl.BlockSpec((tm, ATT_KV_WIDTH), pos_map), pl.BlockSpec((tm, ATT_KV_WIDTH), pos_map)]
        args += list(rope_tabs)
    time_on_lanes = lambda i: (i // tiles_per_seq, 0, i % tiles_per_seq)
    batch = t // seq_len
    out_specs = [pl.BlockSpec((tm, AUX_WIDTH), row),
                 pl.BlockSpec((tm, LANES), row),
                 pl.BlockSpec((tm, ATT_WIDTH), row),
                 pl.BlockSpec((tm, ATT_KV_WIDTH), row),
                 pl.BlockSpec((tm, 2 * M_WIDTH), row),
                 pl.BlockSpec((None, M_WIDTH, tm), time_on_lanes),
                 pl.BlockSpec((None, M_GATE_COLS, tm), time_on_lanes)]
    out_shape = [jax.ShapeDtypeStruct((t, AUX_WIDTH), F32), jax.ShapeDtypeStruct((t, LANES), F32),
                 jax.ShapeDtypeStruct((t, ATT_WIDTH), F32), jax.ShapeDtypeStruct((t, ATT_KV_WIDTH), F32),
                 jax.ShapeDtypeStruct((t, 2 * M_WIDTH), BF16),
                 jax.ShapeDtypeStruct((batch, M_WIDTH, seq_len), F32),
                 jax.ShapeDtypeStruct((batch, M_GATE_COLS, seq_len), F32)]
    return pl.pallas_call(
        functools.partial(_pre_kernel, rope=rope),
        grid=(t // tm,), in_specs=in_specs, out_specs=out_specs, out_shape=out_shape,
        compiler_params=_params("arbitrary"), name="pre_rope" if rope else "pre",
    )(*args)


def _attn_kernel(*refs, cached):
    if cached:
        q_ref, k_ref, v_ref, ck_ref, cv_ref, o_ref = refs
        k_all = _bf(jnp.concatenate([ck_ref[...], k_ref[...]], axis=0))
        v_all = _bf(jnp.concatenate([cv_ref[...], v_ref[...]], axis=0))
    else:
        q_ref, k_ref, v_ref, o_ref = refs
        k_all = _bf(k_ref[...])
        v_all = _bf(v_ref[...])
    scale = ATT_HD ** -0.5 * LOG2_E
    outs = []
    for kvh in range(ATT_KV_HEADS):
        lo = kvh * ATT_HD
        kb = k_all[:, lo:lo + ATT_HD]
        vb = v_all[:, lo:lo + ATT_HD]
        for g in range(ATT_GROUP):
            c0 = (kvh * ATT_GROUP + g) * ATT_HD
            qb = _bf(q_ref[:, c0:c0 + ATT_HD] * scale)
            s = lax.dot_general(qb, kb, (((1,), (1,)), ((), ())), preferred_element_type=F32)
            m = jnp.max(s, axis=-1, keepdims=True)
            e = jnp.exp2(s - m)
            den = jnp.sum(e, axis=-1, keepdims=True)
            outs.append(_dot(_bf(e), vb) / den)
    o_ref[...] = _bf(jnp.concatenate(outs, axis=-1))


def _attn_call(qn, kn, proj, layer, cache_k, cache_v, batch, lq):
    tq = min(ATT_TQ, lq)
    nq = lq // tq
    cached = cache_k is not None
    in_specs = [pl.BlockSpec((tq, ATT_WIDTH), lambda b, i: (b * nq + i, 0)),
                pl.BlockSpec((lq, ATT_KV_WIDTH), lambda b, i: (b, 0)),
                pl.BlockSpec((lq, ATT_KV_WIDTH), lambda b, i: (b, AUX_V // ATT_KV_WIDTH))]
    args = [qn, kn, proj]
    if cached:
        past = cache_k.shape[2]
        cache_spec = pl.BlockSpec((None, None, past, ATT_KV_WIDTH), lambda b, i: (b, layer, 0, 0))
        in_specs += [cache_spec, cache_spec]
        args += [cache_k, cache_v]
    return pl.pallas_call(
        functools.partial(_attn_kernel, cached=cached),
        grid=(batch, nq), in_specs=in_specs,
        out_specs=pl.BlockSpec((tq, ATT_WIDTH), lambda b, i: (b * nq + i, 0)),
        out_shape=jax.ShapeDtypeStruct((batch * lq, ATT_WIDTH), BF16),
        compiler_params=_params("arbitrary", "arbitrary"), name="attn_cached" if cached else "attn",
    )(*args)


def _s5_kernel(*refs, batch, seq_len, zero_init):
    if zero_init:
        (u_ref, bmat_ref, cre_ref, cim_ref, ar_ref, ai_ref, d_ref, gw_ref, gb_ref,
         out_ref, fr_ref, fi_ref, utb_ref, ytb_ref, xs_ref) = refs
        ir_ref = ii_ref = None
    else:
        (u_ref, bmat_ref, cre_ref, cim_ref, ar_ref, ai_ref, d_ref, gw_ref, gb_ref, ir_ref, ii_ref,
         out_ref, fr_ref, fi_ref, utb_ref, ytb_ref, xs_ref) = refs
    rows = S5_ROWS
    tc = rows // batch
    n_chunks = seq_len // tc
    n_sub = batch // SUBLANES
    halves = S5_WIDTH // LANES

    def aligned(x, m):
        return x if isinstance(x, int) else pl.multiple_of(x, m)

    for b in range(batch):
        for hf in range(halves):
            utb_ref[hf, pl.ds(b, seq_len, stride=batch), :] = (
                u_ref[b * seq_len:(b + 1) * seq_len, hf * LANES:(hf + 1) * LANES])
    for hf in range(halves):
        ytb_ref[hf] = utb_ref[hf] * d_ref[:, hf * LANES:(hf + 1) * LANES]

    def chunk_of(d, i):
        return i if d == 0 else n_chunks - 1 - i

    def stage_in(i, slot):
        for d in range(N_DIR):
            r0 = aligned(chunk_of(d, i) * rows, rows)
            u_c = jnp.concatenate([utb_ref[hf, pl.ds(r0, rows), :] for hf in range(halves)], axis=-1)
            xs_ref[slot, d] = _dot(_bf(u_c), bmat_ref[d])

    def stage_scan(slot, carry):
        new_carry = []
        for d in range(N_DIR):
            a_re = ar_ref[d]
            a_im = ai_ref[d]
            per_sub = []
            for sub in range(n_sub):
                s_re, s_im = carry[d][sub]
                for t in range(tc):
                    r = (t if d == 0 else tc - 1 - t) * batch + sub * SUBLANES
                    x_re = xs_ref[slot, d, r:r + SUBLANES, :S5_FLAT]
                    x_im = xs_ref[slot, d, r:r + SUBLANES, S5_FLAT:]
                    s_re, s_im = (a_re * s_re - a_im * s_im + x_re, a_re * s_im + a_im * s_re + x_im)
                    xs_ref[slot, d, r:r + SUBLANES, :S5_FLAT] = s_re
                    xs_ref[slot, d, r:r + SUBLANES, S5_FLAT:] = s_im
                per_sub.append((s_re, s_im))
            new_carry.append(tuple(per_sub))
        return tuple(new_carry)

    def stage_out(i, slot):
        for d in range(N_DIR):
            r0 = aligned(chunk_of(d, i) * rows, rows)
            y = (_dot(_bf(xs_ref[slot, d, :, :S5_FLAT]), cre_ref[d])
                 - _dot(_bf(xs_ref[slot, d, :, S5_FLAT:]), cim_ref[d]))
            for hf in range(halves):
                ytb_ref[hf, pl.ds(r0, rows), :] += y[:, hf * LANES:(hf + 1) * LANES]

    def start_state(ref, d, sub):
        if zero_init:
            return jnp.zeros((SUBLANES, S5_FLAT), F32)
        return ref[d, sub * SUBLANES:(sub + 1) * SUBLANES, :]

    carry = tuple(tuple((start_state(ir_ref, d, sub), start_state(ii_ref, d, sub)) for sub in range(n_sub))
                  for d in range(N_DIR))
    def step(i, phase, carry):
        stage_in(i + 1, (phase + 1) % S5_SLOTS)
        carry = stage_scan(phase, carry)
        stage_out(i - 1, (phase - 1) % S5_SLOTS)
        return carry

    stage_in(0, 0)
    stage_in(1, 1)
    carry = stage_scan(0, carry)
    n_steady = n_chunks - 2
    n_peeled = n_steady % S5_SLOTS
    for i in range(1, 1 + n_peeled):
        carry = step(i, i % S5_SLOTS, carry)
    first = 1 + n_peeled

    def body(g, carry):
        for k in range(S5_SLOTS):
            carry = step(first + g * S5_SLOTS + k, (first + k) % S5_SLOTS, carry)
        return carry

    carry = lax.fori_loop(0, n_steady // S5_SLOTS, body, carry)
    last = n_chunks - 1
    carry = stage_scan(last % S5_SLOTS, carry)
    stage_out(last - 1, (last - 1) % S5_SLOTS)
    stage_out(last, last % S5_SLOTS)
    for d in range(N_DIR):
        for sub in range(n_sub):
            fr_ref[d, sub * SUBLANES:(sub + 1) * SUBLANES, :] = carry[d][sub][0]
            fi_ref[d, sub * SUBLANES:(sub + 1) * SUBLANES, :] = carry[d][sub][1]

    for b in range(batch):
        z = jax.nn.gelu(jnp.concatenate(
            [ytb_ref[hf, pl.ds(b, seq_len, stride=batch), :] for hf in range(halves)], axis=-1))
        gate = _sigmoid(_dot(_bf(z), gw_ref[...]) + gb_ref[...])
        out_ref[b * seq_len:(b + 1) * seq_len, :] = _bf(z * gate)


def _s5_call(proj, layer, pp, init_re, init_im, batch, seq_len):
    rows = seq_len * batch
    single = pl.Buffered(1)
    zero_init = init_re is None
    names = ['s5_bmat', 's5_cre', 's5_cim', 's5_ab_re', 's5_ab_im', 's5_d', 's5_glu_w', 's5_glu_b']
    args = [pp[n] for n in names] + ([] if zero_init else [init_re, init_im])
    whole = lambda a: _layered(layer, a.shape[1:], lambda i: (0,) * (a.ndim - 1), pipeline_mode=single)
    in_specs = ([pl.BlockSpec((rows, S5_WIDTH), lambda i: (0, AUX_S5 // S5_WIDTH), pipeline_mode=single)]
                + [whole(a) for a in args])
    fin_spec = pl.BlockSpec((N_DIR, batch, S5_FLAT), lambda i: (0, 0, 0))
    out_specs = [pl.BlockSpec((rows, S5_WIDTH), lambda i: (0, 0), pipeline_mode=single), fin_spec, fin_spec]
    out_shape = (jax.ShapeDtypeStruct((rows, S5_WIDTH), BF16),
                 jax.ShapeDtypeStruct((N_DIR, batch, S5_FLAT), F32),
                 jax.ShapeDtypeStruct((N_DIR, batch, S5_FLAT), F32))
    return pl.pallas_call(
        functools.partial(_s5_kernel, batch=batch, seq_len=seq_len, zero_init=zero_init),
        grid=(1,), in_specs=in_specs, out_specs=out_specs, out_shape=out_shape,
        scratch_shapes=[pltpu.VMEM((S5_WIDTH // LANES, rows, LANES), F32),
                        pltpu.VMEM((S5_WIDTH // LANES, rows, LANES), F32),
                        pltpu.VMEM((S5_SLOTS, N_DIR, S5_ROWS, 2 * S5_FLAT), F32)],
        compiler_params=_params("arbitrary"), name="s5",
    )(proj, *args)


def _mlstm_kernel(*refs, seq_len, zero_init):
    if zero_init:
        (q_ref, k_ref, vt_ref, mg_ref, mgt_ref, gbr_ref, gbc_ref, tril_ref, triu_ref, ones_ref,
         ht_ref, cn_ref, m_ref) = refs
    else:
        (q_ref, k_ref, vt_ref, mg_ref, mgt_ref, gbr_ref, gbc_ref, tril_ref, triu_ref, ones_ref,
         c0_ref, n0_ref, m0_ref, ht_ref, cn_ref, m_ref) = refs
    _mlstm_body(q_ref, k_ref, vt_ref, mg_ref, mgt_ref, gbr_ref, gbc_ref, tril_ref, triu_ref, ones_ref,
                None if zero_init else (c0_ref, n0_ref, m0_ref), ht_ref, cn_ref, m_ref, seq_len)


def _mlstm_body(q_ref, k_ref, vt_ref, mg_ref, mgt_ref, gbr_ref, gbc_ref, tril_ref, triu_ref, ones_ref,
                init_refs, ht_ref, cn_ref, m_ref, seq_len):
    tile = M_TILE
    n_chunks = seq_len // tile
    bg = q_ref.shape[0]
    if init_refs is None:
        cn_ref[...] = jnp.zeros_like(cn_ref)
        m_ref[...] = jnp.zeros_like(m_ref)
    else:
        c0_ref, n0_ref, m0_ref = init_refs
        cn_ref[:, :, :M_HD, :] = c0_ref[...]
        cn_ref[:, :, M_HD:, :] = n0_ref[...]
        m_ref[...] = m0_ref[...]
    s_ids = lax.broadcasted_iota(jnp.int32, (tile, tile), 0)
    t_ids = lax.broadcasted_iota(jnp.int32, (tile, tile), 1)
    first_row = lax.broadcasted_iota(jnp.int32, (CN_ROWS - M_HD, tile), 0) == 0
    neg_inf = jnp.float32(-jnp.inf)
    ones = ones_ref[...]

    def direction(d, c):
        t0 = pl.multiple_of(c * tile, tile)
        col_mat = tril_ref[...] if d == 0 else triu_ref[...]
        row_mat = triu_ref[...] if d == 0 else tril_ref[...]
        valid = (s_ids <= t_ids) if d == 0 else (s_ids >= t_ids)
        gt = mgt_ref[:, :, pl.ds(t0, tile)] + gbc_ref[...]
        lft_hi, lft_lo = _split(_log_sigmoid(gt).reshape(bg * M_GATE_COLS, tile))
        b_rows = (_dot(lft_hi, row_mat) + _dot(lft_lo, row_mat)).reshape(bg, M_GATE_COLS, tile)
        totals = (_dot(lft_hi, ones) + _dot(lft_lo, ones)).reshape(bg, M_GATE_COLS, tile)
        gc = mg_ref[:, pl.ds(t0, tile), :] + gbr_ref[...]
        lfc_hi, lfc_lo = _split(_log_sigmoid(gc))
        col_b = jnp.broadcast_to(col_mat, (bg, tile, tile))
        b_cols = (jnp.einsum('bts,bsl->btl', col_b, lfc_hi, preferred_element_type=F32)
                  + jnp.einsum('bts,bsl->btl', col_b, lfc_lo, preferred_element_type=F32))
        for h in range(M_HEADS):
            j = d * M_HEADS + h
            fj = M_CHAINS + j
            rows = slice(h * M_HD, (h + 1) * M_HD)
            q_c = q_ref[:, pl.ds(t0, tile), rows]
            k_c = k_ref[:, pl.ds(t0, tile), rows]
            vt_c = vt_ref[:, rows, pl.ds(t0, tile)]
            b_t = b_rows[:, fj:fj + 1, :]
            li_t = gt[:, j:j + 1, :]
            tot = totals[:, fj:fj + 1, :]
            m_prev = m_ref[:, j]
            cn_prev = cn_ref[:, j]

            c_col = gc[:, :, j:j + 1] - b_cols[:, :, fj:fj + 1]
            cm = jnp.where(valid, c_col, neg_inf)
            inter = b_t + m_prev
            m_t = jnp.maximum(inter, b_t + jnp.max(cm, axis=1, keepdims=True))
            w_t = jnp.exp(cm + (b_t - m_t))
            s_inter = jnp.exp(inter - m_t)
            sc_t = jnp.einsum('bse,bte->bst', k_c, q_c, preferred_element_type=F32) * w_t
            num = jnp.einsum('bds,bst->bdt', _bf(vt_c), _bf(sc_t), preferred_element_type=F32)
            ext = jnp.einsum('bre,bte->brt', _bf(cn_prev), q_c, preferred_element_type=F32)
            num = num + s_inter * ext[:, :M_HD, :]
            den = jnp.sum(sc_t, axis=1, keepdims=True) + s_inter * ext[:, M_HD:M_HD + 1, :]
            ht_ref[d, :, rows, pl.ds(t0, tile)] = num / jnp.maximum(jnp.abs(den), jnp.exp(-m_t))

            g_row = tot - b_t + li_t
            m_new = jnp.maximum(tot + m_prev, jnp.max(g_row, axis=-1, keepdims=True))
            wk = jnp.exp(g_row - m_new)
            decay = jnp.exp(tot + m_prev - m_new)
            vw = jnp.concatenate([vt_c * wk, jnp.where(first_row, wk, 0.0)], axis=1)
            cn_ref[:, j] = (decay[:, :, :M_HD] * cn_prev
                            + jnp.einsum('brs,bse->bre', _bf(vw), k_c, preferred_element_type=F32))
            m_ref[:, j] = m_new

    def body(i, _):
        direction(0, i)
        direction(1, n_chunks - 1 - i)
        return 0

    lax.fori_loop(0, n_chunks, body, 0)


def _mlstm_call(mqk3, vt, mg3, mgt, layer, pp, init, batch, seq_len):
    bg = M_BG
    const2 = lambda g: (0, 0)
    lead3 = lambda g: (g, 0, 0)
    lead4 = lambda g: (g, 0, 0, 0)
    in_specs = [pl.BlockSpec((bg, seq_len, M_WIDTH), lead3),
                pl.BlockSpec((bg, seq_len, M_WIDTH), lambda g: (g, 0, 1)),
                pl.BlockSpec((bg, M_WIDTH, seq_len), lead3),
                pl.BlockSpec((bg, seq_len, LANES), lead3),
                pl.BlockSpec((bg, M_GATE_COLS, seq_len), lead3),
                _layered(layer, (1, LANES), const2),
                _layered(layer, (M_GATE_COLS, 1), const2),
                pl.BlockSpec((M_TILE, M_TILE), const2),
                pl.BlockSpec((M_TILE, M_TILE), const2),
                pl.BlockSpec((M_TILE, M_TILE), const2)]
    args = [mqk3, mqk3, vt, mg3, mgt, pp['m_gb_row'], pp['m_gb_col'], pp['tril'], pp['triu'], pp['ones']]
    if init is not None:
        state_spec = lambda *tail: pl.BlockSpec((bg, None, M_CHAINS) + tail, lambda g: (g, layer, 0, 0, 0))
        in_specs += [state_spec(M_HD, M_HD), state_spec(CN_ROWS - M_HD, M_HD), state_spec(1, LANES)]
        args += list(init)
    out_specs = [pl.BlockSpec((N_DIR, bg, M_WIDTH, seq_len), lambda g: (0, g, 0, 0)),
                 pl.BlockSpec((bg, M_CHAINS, CN_ROWS, M_HD), lead4),
                 pl.BlockSpec((bg, M_CHAINS, 1, LANES), lead4)]
    out_shape = [jax.ShapeDtypeStruct((N_DIR, batch, M_WIDTH, seq_len), F32),
                 jax.ShapeDtypeStruct((batch, M_CHAINS, CN_ROWS, M_HD), F32),
                 jax.ShapeDtypeStruct((batch, M_CHAINS, 1, LANES), F32)]
    return pl.pallas_call(
        functools.partial(_mlstm_kernel, seq_len=seq_len, zero_init=init is None),
        grid=(batch // bg,), in_specs=in_specs, out_specs=out_specs, out_shape=out_shape,
        compiler_params=_params("arbitrary"), name="mlstm",
    )(*args)


def _post_kernel(x_ref, att_ref, s5_ref, ht_ref, mo_ref, mod_ref, g2_ref, ng_ref, blkm_ref, wo_ref,
                 x1_ref, h2_ref):
    mh = jnp.concatenate([(ht_ref[0, s] + ht_ref[1, s]).T for s in range(ht_ref.shape[1])], axis=0)
    ml = mh * lax.rsqrt(_seg_mean_sq(mh, blkm_ref[...]) + EPS) * ng_ref[...] * _sigmoid(mo_ref[...])
    mix = (_dot(_bf(att_ref[...]), wo_ref[:ATT_WIDTH, :])
           + _dot(_bf(s5_ref[...]), wo_ref[ATT_WIDTH:ATT_WIDTH + S5_WIDTH, :])
           + _dot(_bf(ml), wo_ref[ATT_WIDTH + S5_WIDTH:, :]))
    x1 = x_ref[...] + mod_ref[2:3, :] * mix
    x1_ref[...] = x1
    ms = jnp.mean(x1 * x1, axis=-1, keepdims=True)
    xn = x1 * lax.rsqrt(ms + EPS) * g2_ref[...]
    h2_ref[...] = _bf(xn * (1.0 + mod_ref[4:5, :]) + mod_ref[3:4, :])


def _post_call(x2, att, s5o, ht, proj, layer, pp, seq_len, per_batch_mod):
    t = x2.shape[0]
    tm = POST_TM
    seqs_per_tile = tm // seq_len
    if per_batch_mod:
        mod_row = lambda i: 1 + i * seqs_per_tile
    else:
        mod_row = lambda i: 0
    row = lambda i: (i, 0)
    const = lambda i: (0, 0)
    in_specs = [pl.BlockSpec((tm, D_MODEL), row),
                pl.BlockSpec((tm, ATT_WIDTH), row),
                pl.BlockSpec((tm, S5_WIDTH), row),
                pl.BlockSpec((N_DIR, seqs_per_tile, M_WIDTH, seq_len), lambda i: (0, i, 0, 0)),
                pl.BlockSpec((tm, M_WIDTH), lambda i: (i, AUX_MO // M_WIDTH)),
                _mod_spec(layer, mod_row),
                _layered(layer, (1, D_MODEL), const),
                _layered(layer, (1, M_WIDTH), const),
                pl.BlockSpec((M_WIDTH, M_WIDTH), const),
                _layered(layer, (D_MODEL, D_MODEL), const)]
    return pl.pallas_call(
        _post_kernel,
        grid=(t // tm,), in_specs=in_specs,
        out_specs=[pl.BlockSpec((tm, D_MODEL), row), pl.BlockSpec((tm, D_MODEL), row)],
        out_shape=[jax.ShapeDtypeStruct((t, D_MODEL), F32), jax.ShapeDtypeStruct((t, D_MODEL), BF16)],
        compiler_params=_params("arbitrary"), name="post",
    )(x2, att, s5o, ht, proj, pp['mod'], pp['g2'], pp['m_ng'], pp['blkm'], pp['w_out'])


def _ffn_kernel(x1_ref, h2_ref, hprev_ref, hnext_ref, mod_ref, wg_ref, wu_ref, cw_ref, cb_ref, wd_ref, o_ref,
                gated_ref, *, seq_len):
    tm = h2_ref.shape[0]
    h2 = h2_ref[...]
    h2_ext = jnp.concatenate([h2, hprev_ref[...], hnext_ref[...]], axis=0)
    row = lax.broadcasted_iota(jnp.int32, (tm, FF_CHUNK), 0)
    pos = (pl.program_id(0) * tm + row) % seq_len
    seq_start = pos == 0
    seq_end = pos == seq_len - 1
    tile_start = row == 0
    tile_end = row == tm - 1
    for j in range(D_FF // FF_CHUNK):
        cols = slice(j * FF_CHUNK, (j + 1) * FF_CHUNK)
        a_ext = _dot(h2_ext, wg_ref[:, cols])
        a = a_ext[:tm]
        before = a_ext[tm + BF16_ROWS - 1:tm + BF16_ROWS, :]
        after = a_ext[tm + BF16_ROWS:tm + BF16_ROWS + 1, :]
        a_prev = jnp.where(tile_start, before, pltpu.roll(a, 1, axis=0))
        a_next = jnp.where(tile_end, after, pltpu.roll(a, tm - 1, axis=0))
        a_prev = jnp.where(seq_start, 0.0, a_prev)
        a_next = jnp.where(seq_end, 0.0, a_next)
        ac = a_prev * cw_ref[0:1, cols] + a * cw_ref[1:2, cols] + a_next * cw_ref[2:3, cols] + cb_ref[:, cols]
        up = _dot(h2, wu_ref[:, cols])
        gated_ref[:, cols] = _bf(ac * _sigmoid(ac) * up)
    o_ref[...] = x1_ref[...] + mod_ref[5:6, :] * _dot(gated_ref[...], wd_ref[...])


def _ffn_call(x1, h2, layer, pp, seq_len, per_batch_mod):
    t = x1.shape[0]
    tm = FFN_TM
    halo_blocks_per_tile = tm // BF16_ROWS
    last_halo_block = t // BF16_ROWS - 1
    if per_batch_mod:
        mod_row = lambda i: 1 + (i * tm) // seq_len
    else:
        mod_row = lambda i: 0
    row = lambda i: (i, 0)
    single = pl.Buffered(1)
    whole = lambda shape: _layered(layer, shape, lambda i: (0, 0), pipeline_mode=single)
    in_specs = [pl.BlockSpec((tm, D_MODEL), row),
                pl.BlockSpec((tm, D_MODEL), row),
                pl.BlockSpec((BF16_ROWS, D_MODEL), lambda i: (jnp.maximum(i * halo_blocks_per_tile - 1, 0), 0)),
                pl.BlockSpec((BF16_ROWS, D_MODEL),
                             lambda i: (jnp.minimum((i + 1) * halo_blocks_per_tile, last_halo_block), 0)),
                _mod_spec(layer, mod_row),
                whole((D_MODEL, D_FF)), whole((D_MODEL, D_FF)), whole((3, D_FF)), whole((1, D_FF)),
                whole((D_FF, D_MODEL))]
    return pl.pallas_call(
        functools.partial(_ffn_kernel, seq_len=seq_len),
        grid=(t // tm,), in_specs=in_specs,
        out_specs=pl.BlockSpec((tm, D_MODEL), row),
        out_shape=jax.ShapeDtypeStruct((t, D_MODEL), F32),
        scratch_shapes=[pltpu.VMEM((tm, D_FF), BF16)],
        compiler_params=_params("arbitrary"), name="ffn",
    )(x1, h2, h2, h2, pp['mod'], pp['ffn_w_gate'], pp['ffn_w_up'], pp['ffn_conv_w'], pp['ffn_conv_b'],
      pp['ffn_w_down'])


def _trunk_layer(x2, batch, seq_len, layer, pp, ctx, rope_tabs):
    latent = ctx is not None
    proj, mg, qn, kn, mqk, vt, mgt = _pre_call(x2, layer, pp, rope_tabs if latent else None, seq_len, latent)
    cache_k, cache_v, init_re, init_im, m_init = ctx if latent else (None,) * 5

    att = _attn_call(qn, kn, proj, layer, cache_k, cache_v, batch, seq_len)
    s5o, fin_re, fin_im = _s5_call(proj, layer, pp, init_re, init_im, batch, seq_len)
    ht, cn_f, m_f = _mlstm_call(mqk.reshape(batch, seq_len, 2 * M_WIDTH), vt, mg.reshape(batch, seq_len, LANES), mgt,
                                layer, pp, m_init, batch, seq_len)
    x1, h2 = _post_call(x2, att, s5o, ht, proj, layer, pp, seq_len, latent)
    x_out = _ffn_call(x1, h2, layer, pp, seq_len, latent)
    if latent:
        return x_out, None
    states = (kn.reshape(batch, seq_len, ATT_KV_HEADS, ATT_HD),
              proj[:, AUX_V:].reshape(batch, seq_len, ATT_KV_HEADS, ATT_HD),
              fin_re.transpose(1, 0, 2).reshape(batch, N_DIR, S5_GROUPS, S5_STATE),
              fin_im.transpose(1, 0, 2).reshape(batch, N_DIR, S5_GROUPS, S5_STATE),
              cn_f[:, :, :M_HD, :].reshape(batch, N_DIR, M_HEADS, M_HD, M_HD),
              cn_f[:, :, M_HD, :].reshape(batch, N_DIR, M_HEADS, M_HD),
              m_f[:, :, 0, 0].reshape(batch, N_DIR, M_HEADS))
    return x_out, states


def _head_block(width, head_dim):
    ids = np.arange(width) // head_dim
    return jnp.asarray((ids[:, None] == ids[None, :]).astype(np.float32) / head_dim, dtype=BF16)


def kernel(x_prompt, x_sample, c, cache_attn_k, cache_attn_v, state_s5_re, state_s5_im, state_mlstm_C, state_mlstm_n, state_mlstm_m, c_ctx, ada_w, ada_b, norm1_g, norm2_g, w_in, q_norm_g, k_norm_g, s5_a_re, s5_a_im, s5_log_dt, s5_b_re, s5_b_im, s5_c_re, s5_c_im, s5_d, s5_glu_w, s5_glu_b, m_gate_b, m_norm_g, w_out, ffn_w_gate, ffn_w_up, ffn_conv_w, ffn_conv_b, ffn_w_down):
    batch, seq = x_prompt.shape[0], x_prompt.shape[1]
    dec_batch, dec_seq = x_sample.shape[0], x_sample.shape[1]

    n_mod_rows = 2 * SUBLANES
    cvec = jnp.zeros((n_mod_rows, D_MODEL), F32).at[0].set(c_ctx).at[1:1 + dec_batch].set(c)
    mod_all = _ada_call(cvec, ada_w, ada_b).reshape(DEPTH, n_mod_rows, 6, D_MODEL)

    ab_re, ab_im, bb_re, bb_im = _s5_disc_call(s5_a_re, s5_a_im, s5_log_dt, s5_b_re, s5_b_im)
    ab_re = ab_re.reshape(DEPTH, N_DIR, 1, S5_FLAT)
    ab_im = ab_im.reshape(DEPTH, N_DIR, 1, S5_FLAT)
    bb_re = bb_re.reshape(DEPTH, N_DIR, S5_GROUPS, S5_STATE, S5_CH)
    bb_im = bb_im.reshape(DEPTH, N_DIR, S5_GROUPS, S5_STATE, S5_CH)

    rope_tabs = _rope_tables(dec_seq)
    tri = np.tril(np.ones((M_TILE, M_TILE), np.float32))
    tril = jnp.asarray(tri, dtype=BF16)
    triu = jnp.asarray(tri.T, dtype=BF16)
    ones = jnp.ones((M_TILE, M_TILE), BF16)
    blkq = _head_block(ATT_WIDTH, ATT_HD)
    blkk = _head_block(ATT_KV_WIDTH, ATT_HD)
    blkm = _head_block(M_WIDTH, M_HD)

    w_in_b = _bf(w_in)
    gb = m_gate_b.reshape(DEPTH, M_GATE_COLS)
    pp = dict(
        mod=mod_all, g1=norm1_g.reshape(DEPTH, 1, D_MODEL), g2=norm2_g.reshape(DEPTH, 1, D_MODEL),
        w_in=w_in_b,
        w_gatecols=jnp.zeros((DEPTH, D_MODEL, LANES), BF16).at[:, :, :M_GATE_COLS].set(w_in_b[:, :, PROJ_MAIN:]),
        blkq=blkq, blkk=blkk, blkm=blkm,
        qg=jnp.tile(q_norm_g, (1, ATT_HEADS)).reshape(DEPTH, 1, ATT_WIDTH),
        kg=jnp.tile(k_norm_g, (1, ATT_KV_HEADS)).reshape(DEPTH, 1, ATT_KV_WIDTH),
        s5_bmat=_bf(jnp.concatenate([_block_diag(jnp.swapaxes(bb_re, -1, -2)),
                                     _block_diag(jnp.swapaxes(bb_im, -1, -2))], axis=-1)),
        s5_cre=_bf(_block_diag(jnp.swapaxes(s5_c_re, -1, -2))),
        s5_cim=_bf(_block_diag(jnp.swapaxes(s5_c_im, -1, -2))),
        s5_ab_re=jnp.broadcast_to(ab_re, (DEPTH, N_DIR, SUBLANES, S5_FLAT)),
        s5_ab_im=jnp.broadcast_to(ab_im, (DEPTH, N_DIR, SUBLANES, S5_FLAT)),
        s5_d=s5_d.reshape(DEPTH, 1, S5_WIDTH), s5_glu_w=_bf(s5_glu_w), s5_glu_b=s5_glu_b.reshape(DEPTH, 1, S5_WIDTH),
        m_gb_row=jnp.zeros((DEPTH, 1, LANES), F32).at[:, 0, :M_GATE_COLS].set(gb),
        m_gb_col=gb.reshape(DEPTH, M_GATE_COLS, 1),
        m_ng=jnp.tile(m_norm_g, (1, M_HEADS)).reshape(DEPTH, 1, M_WIDTH), tril=tril, triu=triu, ones=ones,
        w_out=_bf(w_out), ffn_w_gate=_bf(ffn_w_gate), ffn_w_up=_bf(ffn_w_up),
        ffn_conv_w=ffn_conv_w, ffn_conv_b=ffn_conv_b.reshape(DEPTH, 1, D_FF), ffn_w_down=_bf(ffn_w_down))

    past = cache_attn_k.shape[2]
    n_pad = CN_ROWS - M_HD
    s5_init = lambda st: st.reshape(dec_batch, DEPTH, N_DIR, S5_FLAT).transpose(1, 2, 0, 3)
    m_init = (state_mlstm_C.reshape(dec_batch, DEPTH, M_CHAINS, M_HD, M_HD),
              jnp.zeros((dec_batch, DEPTH, M_CHAINS, n_pad, M_HD), F32).at[:, :, :, 0, :].set(
                  state_mlstm_n.reshape(dec_batch, DEPTH, M_CHAINS, M_HD)),
              jnp.broadcast_to(state_mlstm_m.reshape(dec_batch, DEPTH, M_CHAINS, 1, 1),
                               (dec_batch, DEPTH, M_CHAINS, 1, LANES)))
    ctx = (cache_attn_k.reshape(dec_batch, DEPTH, past, ATT_KV_WIDTH),
           cache_attn_v.reshape(dec_batch, DEPTH, past, ATT_KV_WIDTH),
           s5_init(state_s5_re), s5_init(state_s5_im), m_init)

    xp = x_prompt.reshape(batch * seq, D_MODEL)
    xs = x_sample.reshape(dec_batch * dec_seq, D_MODEL)
    ctx_out = []
    for l in range(DEPTH):
        xp, st = _trunk_layer(xp, batch, seq, l, pp, None, None)
        ctx_out.append(st)
        xs, _ = _trunk_layer(xs, dec_batch, dec_seq, l, pp, ctx, rope_tabs)
    outs = [jnp.stack([s[i] for s in ctx_out], axis=1) for i in range(7)]
    return (xp.reshape(batch, seq, D_MODEL), xs.reshape(dec_batch, dec_seq, D_MODEL), *outs)
```

```python
import functools

import numpy as np
import jax
import jax.numpy as jnp
from jax import lax
from jax.experimental import pallas as pl
from jax.experimental.pallas import tpu as pltpu

F32 = jnp.float32
BF16 = jnp.bfloat16

D_MODEL = 1024
DEPTH = 2
GRID_W = 64
N_DIR = 2
EPS = 1e-6
ATT_HD = 64
ATT_WIDTH = 512
ATT_HEADS = 8
ATT_KV_HEADS = 2
ATT_GROUP = ATT_HEADS // ATT_KV_HEADS
ATT_KV_WIDTH = ATT_KV_HEADS * ATT_HD
ROPE_THETA = 10000.0
LOG2_E = 1.4426950408889634
S5_CH = 16
S5_STATE = 64
S5_WIDTH = 256
S5_GROUPS = 16
S5_FLAT = S5_GROUPS * S5_STATE
M_HD = 64
M_WIDTH = 256
M_HEADS = 4
M_GATE_COLS = 2 * N_DIR * M_HEADS
M_CHAINS = N_DIR * M_HEADS
D_FF = 2816
PROJ_MAIN = 2048
COL_K = ATT_WIDTH
COL_V = COL_K + ATT_KV_WIDTH
COL_S5 = COL_V + ATT_KV_WIDTH
COL_MQ = COL_S5 + S5_WIDTH
COL_MK = COL_MQ + M_WIDTH
COL_MV = COL_MK + M_WIDTH
COL_MO = COL_MV + M_WIDTH
AUX_S5 = 0
AUX_MO = AUX_S5 + S5_WIDTH
AUX_V = AUX_MO + M_WIDTH
AUX_WIDTH = AUX_V + ATT_KV_WIDTH
LANES = 128
SUBLANES = 8
BF16_ROWS = 16
VMEM_LIMIT = 56 * 1024 * 1024

PRE_TM = 512
ATT_TQ = 512
S5_ROWS = 256
S5_SLOTS = 3
M_TILE = 128
M_BG = 4
CN_ROWS = M_HD + BF16_ROWS
POST_TM = 1024
POST_OUT_CHUNK = 256
FFN_TM = 1024
FF_CHUNK = 256


def _bf(x):
    return x.astype(BF16)


def _dot(a, b):
    return jnp.dot(a, b, preferred_element_type=F32)


def _split(x):
    hi = _bf(x)
    lo = _bf(x - hi.astype(F32))
    return hi, lo


def _seg_mean_sq(x, blk):
    hi, lo = _split(x * x)
    return _dot(hi, blk) + _dot(lo, blk)


def _sigmoid(x):
    return 1.0 / (1.0 + jnp.exp(-x))


def _log_sigmoid(x):
    return -(jnp.maximum(-x, 0.0) + jnp.log1p(jnp.exp(-jnp.abs(x))))


def _params(*sem):
    return pltpu.CompilerParams(dimension_semantics=sem, vmem_limit_bytes=VMEM_LIMIT)


def _ada_kernel(c_ref, w_ref, b_ref, o_ref):
    c = c_ref[...]
    s = c * _sigmoid(c)
    o_ref[...] = _dot(_bf(s), _bf(w_ref[...])) + b_ref[...]


def _ada_call(cvec, ada_w, ada_b):
    rows = cvec.shape[0]
    tn = 1024
    n = ada_w.shape[-1]
    return pl.pallas_call(
        _ada_kernel,
        grid=(DEPTH, n // tn),
        in_specs=[pl.BlockSpec((rows, D_MODEL), lambda l, j: (0, 0)),
                  pl.BlockSpec((None, D_MODEL, tn), lambda l, j: (l, 0, j)),
                  pl.BlockSpec((None, 1, tn), lambda l, j: (l, 0, j))],
        out_specs=pl.BlockSpec((None, rows, tn), lambda l, j: (l, 0, j)),
        out_shape=jax.ShapeDtypeStruct((DEPTH, rows, n), F32),
        compiler_params=_params("arbitrary", "arbitrary"),
        name="adaln",
    )(cvec, ada_w, ada_b.reshape(DEPTH, 1, n))


def _s5_disc_kernel(are_ref, aim_ref, ldt_ref, arex_ref, aimx_ref, bre_ref, bim_ref,
                    abr_ref, abi_ref, bbr_ref, bbi_ref):
    dt = jnp.exp(ldt_ref[...])

    def disc(a_re, a_im):
        mag = jnp.exp(dt * a_re)
        ab_re = mag * jnp.cos(dt * a_im)
        ab_im = mag * jnp.sin(dt * a_im)
        den = a_re * a_re + a_im * a_im
        nr = ab_re - 1.0
        ni = ab_im
        f_re = (nr * a_re + ni * a_im) / den
        f_im = (ni * a_re - nr * a_im) / den
        return ab_re, ab_im, f_re, f_im

    ab_re, ab_im, _, _ = disc(are_ref[...], aim_ref[...])
    abr_ref[...] = ab_re
    abi_ref[...] = ab_im
    _, _, f_re, f_im = disc(arex_ref[...], aimx_ref[...])
    b_re = bre_ref[...]
    b_im = bim_ref[...]
    bbr_ref[...] = f_re * b_re - f_im * b_im
    bbi_ref[...] = f_re * b_im + f_im * b_re


def _s5_disc_call(a_re, a_im, log_dt, b_re, b_im):
    r = DEPTH * N_DIR * S5_GROUPS
    a_re2 = a_re.reshape(r, S5_STATE)
    a_im2 = a_im.reshape(r, S5_STATE)
    wide = S5_STATE * S5_CH
    out_shape = (jax.ShapeDtypeStruct((r, S5_STATE), F32), jax.ShapeDtypeStruct((r, S5_STATE), F32),
                 jax.ShapeDtypeStruct((r, wide), F32), jax.ShapeDtypeStruct((r, wide), F32))
    return pl.pallas_call(_s5_disc_kernel, out_shape=out_shape, name="s5_disc")(
        a_re2, a_im2, log_dt.reshape(r, 1),
        jnp.repeat(a_re2, S5_CH, axis=1), jnp.repeat(a_im2, S5_CH, axis=1),
        b_re.reshape(r, wide), b_im.reshape(r, wide))


def _block_diag(blocks):
    g, r, c = blocks.shape[-3:]
    lead = blocks.shape[:-3]
    tiled = jnp.tile(blocks.reshape(lead + (g * r, c)), (1,) * len(lead) + (1, g))
    on_diagonal = (np.arange(g * r)[:, None] // r) == (np.arange(g * c)[None, :] // c)
    return jnp.where(on_diagonal, tiled, 0.0)


def _rope(x, cos, sin_signed, second):
    width = x.shape[-1]
    quarter = ATT_HD // 4
    partner = jnp.where(second, pltpu.roll(x, quarter, axis=1), pltpu.roll(x, width - quarter, axis=1))
    return x * cos + partner * sin_signed


def _pre_kernel(*refs, rope):
    if rope:
        (x_ref, mod_ref, g1_ref, w_ref, wg_ref, blkq_ref, blkk_ref, qg_ref, kg_ref,
         cosq_ref, sinq_ref, cosk_ref, sink_ref, proj_ref, mg_ref, qn_ref, kn_ref, mqk_ref, vt_ref, mgt_ref) = refs
    else:
        (x_ref, mod_ref, g1_ref, w_ref, wg_ref, blkq_ref, blkk_ref, qg_ref, kg_ref,
         proj_ref, mg_ref, qn_ref, kn_ref, mqk_ref, vt_ref, mgt_ref) = refs
    x = x_ref[...]
    ms = jnp.mean(x * x, axis=-1, keepdims=True)
    xn = x * lax.rsqrt(ms + EPS) * g1_ref[...]
    h = _bf(xn * (1.0 + mod_ref[1:2, :]) + mod_ref[0:1, :])
    def proj(c0, c1):
        return _dot(h, w_ref[:, c0:c1])

    q = proj(0, COL_K)
    kvs = proj(COL_K, COL_MQ)
    k = kvs[:, :ATT_KV_WIDTH]
    mqk = proj(COL_MQ, COL_MV)
    mvo = proj(COL_MV, PROJ_MAIN)
    proj_ref[:, AUX_S5:AUX_MO] = kvs[:, 2 * ATT_KV_WIDTH:]
    proj_ref[:, AUX_MO:AUX_V] = mvo[:, M_WIDTH:]
    proj_ref[:, AUX_V:] = kvs[:, ATT_KV_WIDTH:2 * ATT_KV_WIDTH]
    mg = _dot(h, wg_ref[...])
    mg_ref[...] = mg
    mqk_ref[:, :M_WIDTH] = _bf(mqk[:, :M_WIDTH])
    mqk_ref[:, M_WIDTH:] = _bf(mqk[:, M_WIDTH:] * (M_HD ** -0.5))
    vt_ref[...] = mvo[:, :M_WIDTH].T
    mgt_ref[...] = mg.T[:M_GATE_COLS, :]
    qn = q * lax.rsqrt(_seg_mean_sq(q, blkq_ref[...]) + EPS) * qg_ref[...]
    kn = k * lax.rsqrt(_seg_mean_sq(k, blkk_ref[...]) + EPS) * kg_ref[...]
    if rope:
        lane_q = lax.broadcasted_iota(jnp.int32, qn.shape, 1)
        lane_k = lax.broadcasted_iota(jnp.int32, kn.shape, 1)
        qn = _rope(qn, cosq_ref[...], sinq_ref[...], (lane_q & (ATT_HD // 4)) != 0)
        kn = _rope(kn, cosk_ref[...], sink_ref[...], (lane_k & (ATT_HD // 4)) != 0)
    qn_ref[...] = qn
    kn_ref[...] = kn


def _rope_tables(seq_len):
    n_rows = seq_len // GRID_W
    row = jnp.repeat(jnp.arange(n_rows), GRID_W)
    col = jnp.tile(jnp.arange(GRID_W), n_rows)
    half = ATT_HD // 2
    inv_freq = 1.0 / (ROPE_THETA ** (jnp.arange(0, half, 2, dtype=F32) / half))

    def tables(pos):
        ang = pos.astype(F32)[:, None] * inv_freq[None, :]
        cos = jnp.cos(ang)
        sin = jnp.sin(ang)
        return jnp.concatenate([cos, cos], axis=-1), jnp.concatenate([-sin, sin], axis=-1)

    cr, sr = tables(row)
    cc, sc = tables(col)
    cos = jnp.concatenate([cr, cc], axis=-1)
    sin = jnp.concatenate([sr, sc], axis=-1)
    return (jnp.tile(cos, (1, ATT_HEADS)), jnp.tile(sin, (1, ATT_HEADS)),
            jnp.tile(cos, (1, ATT_KV_HEADS)), jnp.tile(sin, (1, ATT_KV_HEADS)))


def _layered(layer, block, index_map, **kw):
    return pl.BlockSpec((None,) + tuple(block), lambda *g: (layer,) + tuple(index_map(*g)), **kw)


def _mod_spec(layer, row_of):
    return pl.BlockSpec((None, None, 6, D_MODEL), lambda *g: (layer, row_of(*g), 0, 0))


def _pre_call(x2, layer, pp, rope_tabs, seq_len, per_batch_mod):
    t = x2.shape[0]
    tm = min(PRE_TM, seq_len)
    tiles_per_seq = seq_len // tm
    rope = rope_tabs is not None
    if per_batch_mod:
        mod_row = lambda i: 1 + i // tiles_per_seq
    else:
        mod_row = lambda i: 0
    const = lambda i: (0, 0)
    row = lambda i: (i, 0)
    in_specs = [pl.BlockSpec((tm, D_MODEL), row),
                _mod_spec(layer, mod_row),
                _layered(layer, (1, D_MODEL), const),
                _layered(layer, (D_MODEL, PROJ_MAIN), const),
                _layered(layer, (D_MODEL, LANES), const),
                pl.BlockSpec((ATT_WIDTH, ATT_WIDTH), const),
                pl.BlockSpec((ATT_KV_WIDTH, ATT_KV_WIDTH), const),
                _layered(layer, (1, ATT_WIDTH), const),
                _layered(layer, (1, ATT_KV_WIDTH), const)]
    args = [x2, pp['mod'], pp['g1'], pp['w_in'], pp['w_gatecols'], pp['blkq'], pp['blkk'], pp['qg'], pp['kg']]
    if rope:
        pos_map = lambda i: (i % tiles_per_seq, 0)
        in_specs += [pl.BlockSpec((tm, ATT_WIDTH), pos_map), pl.BlockSpec((tm, ATT_WIDTH), pos_map),
                     pl.BlockSpec((tm, ATT_KV_WIDTH), pos_map), pl.BlockSpec((tm, ATT_KV_WIDTH), pos_map)]
        args += list(rope_tabs)
    time_on_lanes = lambda i: (i // tiles_per_seq, 0, i % tiles_per_seq)
    batch = t // seq_len
    out_specs = [pl.BlockSpec((tm, AUX_WIDTH), row),
                 pl.BlockSpec((tm, LANES), row),
                 pl.BlockSpec((tm, ATT_WIDTH), row),
                 pl.BlockSpec((tm, ATT_KV_WIDTH), row),
                 pl.BlockSpec((tm, 2 * M_WIDTH), row),
                 pl.BlockSpec((None, M_WIDTH, tm), time_on_lanes),
                 pl.BlockSpec((None, M_GATE_COLS, tm), time_on_lanes)]
    out_shape = [jax.ShapeDtypeStruct((t, AUX_WIDTH), F32), jax.ShapeDtypeStruct((t, LANES), F32),
                 jax.ShapeDtypeStruct((t, ATT_WIDTH), F32), jax.ShapeDtypeStruct((t, ATT_KV_WIDTH), F32),
                 jax.ShapeDtypeStruct((t, 2 * M_WIDTH), BF16),
                 jax.ShapeDtypeStruct((batch, M_WIDTH, seq_len), F32),
                 jax.ShapeDtypeStruct((batch, M_GATE_COLS, seq_len), F32)]
    return pl.pallas_call(
        functools.partial(_pre_kernel, rope=rope),
        grid=(t // tm,), in_specs=in_specs, out_specs=out_specs, out_shape=out_shape,
        compiler_params=_params("arbitrary"), name="pre_rope" if rope else "pre",
    )(*args)


def _attn_kernel(*refs, cached):
    if cached:
        q_ref, k_ref, v_ref, ck_ref, cv_ref, o_ref = refs
        k_all = _bf(jnp.concatenate([ck_ref[...], k_ref[...]], axis=0))
        v_all = _bf(jnp.concatenate([cv_ref[...], v_ref[...]], axis=0))
    else:
        q_ref, k_ref, v_ref, o_ref = refs
        k_all = _bf(k_ref[...])
        v_all = _bf(v_ref[...])
    scale = ATT_HD ** -0.5 * LOG2_E
    outs = []
    for kvh in range(ATT_KV_HEADS):
        lo = kvh * ATT_HD
        kb = k_all[:, lo:lo + ATT_HD]
        vb = v_all[:, lo:lo + ATT_HD]
        for g in range(ATT_GROUP):
            c0 = (kvh * ATT_GROUP + g) * ATT_HD
            qb = _bf(q_ref[:, c0:c0 + ATT_HD] * scale)
            s = lax.dot_general(qb, kb, (((1,), (1,)), ((), ())), preferred_element_type=F32)
            m = jnp.max(s, axis=-1, keepdims=True)
            e = jnp.exp2(s - m)
            den = jnp.sum(e, axis=-1, keepdims=True)
            outs.append(_dot(_bf(e), vb) / den)
    o_ref[...] = _bf(jnp.concatenate(outs, axis=-1))


def _attn_call(qn, kn, proj, layer, cache_k, cache_v, batch, lq):
    tq = min(ATT_TQ, lq)
    nq = lq // tq
    cached = cache_k is not None
    in_specs = [pl.BlockSpec((tq, ATT_WIDTH), lambda b, i: (b * nq + i, 0)),
                pl.BlockSpec((lq, ATT_KV_WIDTH), lambda b, i: (b, 0)),
                pl.BlockSpec((lq, ATT_KV_WIDTH), lambda b, i: (b, AUX_V // ATT_KV_WIDTH))]
    args = [qn, kn, proj]
    if cached:
        past = cache_k.shape[2]
        cache_spec = pl.BlockSpec((None, None, past, ATT_KV_WIDTH), lambda b, i: (b, layer, 0, 0))
        in_specs += [cache_spec, cache_spec]
        args += [cache_k, cache_v]
    return pl.pallas_call(
        functools.partial(_attn_kernel, cached=cached),
        grid=(batch, nq), in_specs=in_specs,
        out_specs=pl.BlockSpec((tq, ATT_WIDTH), lambda b, i: (b * nq + i, 0)),
        out_shape=jax.ShapeDtypeStruct((batch * lq, ATT_WIDTH), BF16),
        compiler_params=_params("arbitrary", "arbitrary"), name="attn_cached" if cached else "attn",
    )(*args)


def _s5_kernel(*refs, batch, seq_len, zero_init):
    if zero_init:
        (u_ref, bmat_ref, cre_ref, cim_ref, ar_ref, ai_ref, d_ref, gw_ref, gb_ref,
         out_ref, fr_ref, fi_ref, utb_ref, ytb_ref, xs_ref) = refs
        ir_ref = ii_ref = None
    else:
        (u_ref, bmat_ref, cre_ref, cim_ref, ar_ref, ai_ref, d_ref, gw_ref, gb_ref, ir_ref, ii_ref,
         out_ref, fr_ref, fi_ref, utb_ref, ytb_ref, xs_ref) = refs
    rows = S5_ROWS
    tc = rows // batch
    n_chunks = seq_len // tc
    n_sub = batch // SUBLANES
    halves = S5_WIDTH // LANES

    def aligned(x, m):
        return x if isinstance(x, int) else pl.multiple_of(x, m)

    for b in range(batch):
        for hf in range(halves):
            utb_ref[hf, pl.ds(b, seq_len, stride=batch), :] = (
                u_ref[b * seq_len:(b + 1) * seq_len, hf * LANES:(hf + 1) * LANES])
    for hf in range(halves):
        ytb_ref[hf] = utb_ref[hf] * d_ref[:, hf * LANES:(hf + 1) * LANES]

    def chunk_of(d, i):
        return i if d == 0 else n_chunks - 1 - i

    def stage_in(i, slot):
        for d in range(N_DIR):
            r0 = aligned(chunk_of(d, i) * rows, rows)
            u_c = jnp.concatenate([utb_ref[hf, pl.ds(r0, rows), :] for hf in range(halves)], axis=-1)
            xs_ref[slot, d] = _dot(_bf(u_c), bmat_ref[d])

    def stage_scan(slot, carry):
        new_carry = []
        for d in range(N_DIR):
            a_re = ar_ref[d]
            a_im = ai_ref[d]
            per_sub = []
            for sub in range(n_sub):
                s_re, s_im = carry[d][sub]
                for t in range(tc):
                    r = (t if d == 0 else tc - 1 - t) * batch + sub * SUBLANES
                    x_re = xs_ref[slot, d, r:r + SUBLANES, :S5_FLAT]
                    x_im = xs_ref[slot, d, r:r + SUBLANES, S5_FLAT:]
                    s_re, s_im = (a_re * s_re - a_im * s_im + x_re, a_re * s_im + a_im * s_re + x_im)
                    xs_ref[slot, d, r:r + SUBLANES, :S5_FLAT] = s_re
                    xs_ref[slot, d, r:r + SUBLANES, S5_FLAT:] = s_im
                per_sub.append((s_re, s_im))
            new_carry.append(tuple(per_sub))
        return tuple(new_carry)

    def stage_out(i, slot):
        for d in range(N_DIR):
            r0 = aligned(chunk_of(d, i) * rows, rows)
            y = (_dot(_bf(xs_ref[slot, d, :, :S5_FLAT]), cre_ref[d])
                 - _dot(_bf(xs_ref[slot, d, :, S5_FLAT:]), cim_ref[d]))
            for hf in range(halves):
                ytb_ref[hf, pl.ds(r0, rows), :] += y[:, hf * LANES:(hf + 1) * LANES]

    def start_state(ref, d, sub):
        if zero_init:
            return jnp.zeros((SUBLANES, S5_FLAT), F32)
        return ref[d, sub * SUBLANES:(sub + 1) * SUBLANES, :]

    carry = tuple(tuple((start_state(ir_ref, d, sub), start_state(ii_ref, d, sub)) for sub in range(n_sub))
                  for d in range(N_DIR))
    def step(i, phase, carry):
        stage_in(i + 1, (phase + 1) % S5_SLOTS)
        carry = stage_scan(phase, carry)
        stage_out(i - 1, (phase - 1) % S5_SLOTS)
        return carry

    stage_in(0, 0)
    stage_in(1, 1)
    carry = stage_scan(0, carry)
    n_steady = n_chunks - 2
    n_peeled = n_steady % S5_SLOTS
    for i in range(1, 1 + n_peeled):
        carry = step(i, i % S5_SLOTS, carry)
    first = 1 + n_peeled

    def body(g, carry):
        for k in range(S5_SLOTS):
            carry = step(first + g * S5_SLOTS + k, (first + k) % S5_SLOTS, carry)
        return carry

    carry = lax.fori_loop(0, n_steady // S5_SLOTS, body, carry)
    last = n_chunks - 1
    carry = stage_scan(last % S5_SLOTS, carry)
    stage_out(last - 1, (last - 1) % S5_SLOTS)
    stage_out(last, last % S5_SLOTS)
    for d in range(N_DIR):
        for sub in range(n_sub):
            fr_ref[d, sub * SUBLANES:(sub + 1) * SUBLANES, :] = carry[d][sub][0]
            fi_ref[d, sub * SUBLANES:(sub + 1) * SUBLANES, :] = carry[d][sub][1]

    for b in range(batch):
        z = jax.nn.gelu(jnp.concatenate(
            [ytb_ref[hf, pl.ds(b, seq_len, stride=batch), :] for hf in range(halves)], axis=-1))
        gate = _sigmoid(_dot(_bf(z), gw_ref[...]) + gb_ref[...])
        out_ref[b * seq_len:(b + 1) * seq_len, :] = _bf(z * gate)


def _s5_call(proj, layer, pp, init_re, init_im, batch, seq_len):
    rows = seq_len * batch
    single = pl.Buffered(1)
    zero_init = init_re is None
    names = ['s5_bmat', 's5_cre', 's5_cim', 's5_ab_re', 's5_ab_im', 's5_d', 's5_glu_w', 's5_glu_b']
    args = [pp[n] for n in names] + ([] if zero_init else [init_re, init_im])
    whole = lambda a: _layered(layer, a.shape[1:], lambda i: (0,) * (a.ndim - 1), pipeline_mode=single)
    in_specs = ([pl.BlockSpec((rows, S5_WIDTH), lambda i: (0, AUX_S5 // S5_WIDTH), pipeline_mode=single)]
                + [whole(a) for a in args])
    fin_spec = pl.BlockSpec((N_DIR, batch, S5_FLAT), lambda i: (0, 0, 0))
    out_specs = [pl.BlockSpec((rows, S5_WIDTH), lambda i: (0, 0), pipeline_mode=single), fin_spec, fin_spec]
    out_shape = (jax.ShapeDtypeStruct((rows, S5_WIDTH), BF16),
                 jax.ShapeDtypeStruct((N_DIR, batch, S5_FLAT), F32),
                 jax.ShapeDtypeStruct((N_DIR, batch, S5_FLAT), F32))
    return pl.pallas_call(
        functools.partial(_s5_kernel, batch=batch, seq_len=seq_len, zero_init=zero_init),
        grid=(1,), in_specs=in_specs, out_specs=out_specs, out_shape=out_shape,
        scratch_shapes=[pltpu.VMEM((S5_WIDTH // LANES, rows, LANES), F32),
                        pltpu.VMEM((S5_WIDTH // LANES, rows, LANES), F32),
                        pltpu.VMEM((S5_SLOTS, N_DIR, S5_ROWS, 2 * S5_FLAT), F32)],
        compiler_params=_params("arbitrary"), name="s5",
    )(proj, *args)


def _mlstm_kernel(*refs, seq_len, zero_init):
    if zero_init:
        (q_ref, k_ref, vt_ref, mg_ref, mgt_ref, gbr_ref, gbc_ref, tril_ref, triu_ref, ones_ref,
         ht_ref, cn_ref, m_ref) = refs
    else:
        (q_ref, k_ref, vt_ref, mg_ref, mgt_ref, gbr_ref, gbc_ref, tril_ref, triu_ref, ones_ref,
         c0_ref, n0_ref, m0_ref, ht_ref, cn_ref, m_ref) = refs
    _mlstm_body(q_ref, k_ref, vt_ref, mg_ref, mgt_ref, gbr_ref, gbc_ref, tril_ref, triu_ref, ones_ref,
                None if zero_init else (c0_ref, n0_ref, m0_ref), ht_ref, cn_ref, m_ref, seq_len)


def _mlstm_body(q_ref, k_ref, vt_ref, mg_ref, mgt_ref, gbr_ref, gbc_ref, tril_ref, triu_ref, ones_ref,
                init_refs, ht_ref, cn_ref, m_ref, seq_len):
    tile = M_TILE
    n_chunks = seq_len // tile
    bg = q_ref.shape[0]
    if init_refs is None:
        cn_ref[...] = jnp.zeros_like(cn_ref)
        m_ref[...] = jnp.zeros_like(m_ref)
    else:
        c0_ref, n0_ref, m0_ref = init_refs
        cn_ref[:, :, :M_HD, :] = c0_ref[...]
        cn_ref[:, :, M_HD:, :] = n0_ref[...]
        m_ref[...] = m0_ref[...]
    s_ids = lax.broadcasted_iota(jnp.int32, (tile, tile), 0)
    t_ids = lax.broadcasted_iota(jnp.int32, (tile, tile), 1)
    first_row = lax.broadcasted_iota(jnp.int32, (CN_ROWS - M_HD, tile), 0) == 0
    neg_inf = jnp.float32(-jnp.inf)
    ones = ones_ref[...]

    def direction(d, c):
        t0 = pl.multiple_of(c * tile, tile)
        col_mat = tril_ref[...] if d == 0 else triu_ref[...]
        row_mat = triu_ref[...] if d == 0 else tril_ref[...]
        valid = (s_ids <= t_ids) if d == 0 else (s_ids >= t_ids)
        gt = mgt_ref[:, :, pl.ds(t0, tile)] + gbc_ref[...]
        lft_hi, lft_lo = _split(_log_sigmoid(gt).reshape(bg * M_GATE_COLS, tile))
        b_rows = (_dot(lft_hi, row_mat) + _dot(lft_lo, row_mat)).reshape(bg, M_GATE_COLS, tile)
        totals = (_dot(lft_hi, ones) + _dot(lft_lo, ones)).reshape(bg, M_GATE_COLS, tile)
        gc = mg_ref[:, pl.ds(t0, tile), :] + gbr_ref[...]
        lfc_hi, lfc_lo = _split(_log_sigmoid(gc))
        col_b = jnp.broadcast_to(col_mat, (bg, tile, tile))
        b_cols = (jnp.einsum('bts,bsl->btl', col_b, lfc_hi, preferred_element_type=F32)
                  + jnp.einsum('bts,bsl->btl', col_b, lfc_lo, preferred_element_type=F32))
        for h in range(M_HEADS):
            j = d * M_HEADS + h
            fj = M_CHAINS + j
            rows = slice(h * M_HD, (h + 1) * M_HD)
            q_c = q_ref[:, pl.ds(t0, tile), rows]
            k_c = k_ref[:, pl.ds(t0, tile), rows]
            vt_c = vt_ref[:, rows, pl.ds(t0, tile)]
            b_t = b_rows[:, fj:fj + 1, :]
            li_t = gt[:, j:j + 1, :]
            tot = totals[:, fj:fj + 1, :]
            m_prev = m_ref[:, j]
            cn_prev = cn_ref[:, j]

            c_col = gc[:, :, j:j + 1] - b_cols[:, :, fj:fj + 1]
            cm = jnp.where(valid, c_col, neg_inf)
            inter = b_t + m_prev
            m_t = jnp.maximum(inter, b_t + jnp.max(cm, axis=1, keepdims=True))
            w_t = jnp.exp(cm + (b_t - m_t))
            s_inter = jnp.exp(inter - m_t)
            sc_t = jnp.einsum('bse,bte->bst', k_c, q_c, preferred_element_type=F32) * w_t
            num = jnp.einsum('bds,bst->bdt', _bf(vt_c), _bf(sc_t), preferred_element_type=F32)
            ext = jnp.einsum('bre,bte->brt', _bf(cn_prev), q_c, preferred_element_type=F32)
            num = num + s_inter * ext[:, :M_HD, :]
            den = jnp.sum(sc_t, axis=1, keepdims=True) + s_inter * ext[:, M_HD:M_HD + 1, :]
            ht_ref[d, :, rows, pl.ds(t0, tile)] = num / jnp.maximum(jnp.abs(den), jnp.exp(-m_t))

            g_row = tot - b_t + li_t
            m_new = jnp.maximum(tot + m_prev, jnp.max(g_row, axis=-1, keepdims=True))
            wk = jnp.exp(g_row - m_new)
            decay = jnp.exp(tot + m_prev - m_new)
            vw = jnp.concatenate([vt_c * wk, jnp.where(first_row, wk, 0.0)], axis=1)
            cn_ref[:, j] = (decay[:, :, :M_HD] * cn_prev
                            + jnp.einsum('brs,bse->bre', _bf(vw), k_c, preferred_element_type=F32))
            m_ref[:, j] = m_new

    def body(i, _):
        direction(0, i)
        direction(1, n_chunks - 1 - i)
        return 0

    lax.fori_loop(0, n_chunks, body, 0)


def _mlstm_call(mqk3, vt, mg3, mgt, layer, pp, init, batch, seq_len):
    bg = M_BG
    const2 = lambda g: (0, 0)
    lead3 = lambda g: (g, 0, 0)
    lead4 = lambda g: (g, 0, 0, 0)
    in_specs = [pl.BlockSpec((bg, seq_len, M_WIDTH), lead3),
                pl.BlockSpec((bg, seq_len, M_WIDTH), lambda g: (g, 0, 1)),
                pl.BlockSpec((bg, M_WIDTH, seq_len), lead3),
                pl.BlockSpec((bg, seq_len, LANES), lead3),
                pl.BlockSpec((bg, M_GATE_COLS, seq_len), lead3),
                _layered(layer, (1, LANES), const2),
                _layered(layer, (M_GATE_COLS, 1), const2),
                pl.BlockSpec((M_TILE, M_TILE), const2),
                pl.BlockSpec((M_TILE, M_TILE), const2),
                pl.BlockSpec((M_TILE, M_TILE), const2)]
    args = [mqk3, mqk3, vt, mg3, mgt, pp['m_gb_row'], pp['m_gb_col'], pp['tril'], pp['triu'], pp['ones']]
    if init is not None:
        state_spec = lambda *tail: pl.BlockSpec((bg, None, M_CHAINS) + tail, lambda g: (g, layer, 0, 0, 0))
        in_specs += [state_spec(M_HD, M_HD), state_spec(CN_ROWS - M_HD, M_HD), state_spec(1, LANES)]
        args += list(init)
    out_specs = [pl.BlockSpec((N_DIR, bg, M_WIDTH, seq_len), lambda g: (0, g, 0, 0)),
                 pl.BlockSpec((bg, M_CHAINS, CN_ROWS, M_HD), lead4),
                 pl.BlockSpec((bg, M_CHAINS, 1, LANES), lead4)]
    out_shape = [jax.ShapeDtypeStruct((N_DIR, batch, M_WIDTH, seq_len), F32),
                 jax.ShapeDtypeStruct((batch, M_CHAINS, CN_ROWS, M_HD), F32),
                 jax.ShapeDtypeStruct((batch, M_CHAINS, 1, LANES), F32)]
    return pl.pallas_call(
        functools.partial(_mlstm_kernel, seq_len=seq_len, zero_init=init is None),
        grid=(batch // bg,), in_specs=in_specs, out_specs=out_specs, out_shape=out_shape,
        compiler_params=_params("arbitrary"), name="mlstm",
    )(*args)


def _post_kernel(x_ref, att_ref, s5_ref, ht_ref, mo_ref, mod_ref, g2_ref, ng_ref, blkm_ref, wo_ref,
                 x1_ref, h2_ref):
    mh = jnp.concatenate([(ht_ref[0, s] + ht_ref[1, s]).T for s in range(ht_ref.shape[1])], axis=0)
    ml = mh * lax.rsqrt(_seg_mean_sq(mh, blkm_ref[...]) + EPS) * ng_ref[...] * _sigmoid(mo_ref[...])
    mixers = jnp.concatenate([att_ref[...], s5_ref[...], _bf(ml)], axis=-1)
    for n0 in range(0, D_MODEL, POST_OUT_CHUNK):
        cols = slice(n0, n0 + POST_OUT_CHUNK)
        x1_ref[:, cols] = x_ref[:, cols] + mod_ref[2:3, cols] * _dot(mixers, wo_ref[:, cols])
    x1 = x1_ref[...]
    ms = jnp.mean(x1 * x1, axis=-1, keepdims=True)
    xn = x1 * lax.rsqrt(ms + EPS) * g2_ref[...]
    h2_ref[...] = _bf(xn * (1.0 + mod_ref[4:5, :]) + mod_ref[3:4, :])


def _post_call(x2, att, s5o, ht, proj, layer, pp, seq_len, per_batch_mod):
    t = x2.shape[0]
    tm = POST_TM
    seqs_per_tile = tm // seq_len
    if per_batch_mod:
        mod_row = lambda i: 1 + i * seqs_per_tile
    else:
        mod_row = lambda i: 0
    row = lambda i: (i, 0)
    const = lambda i: (0, 0)
    in_specs = [pl.BlockSpec((tm, D_MODEL), row),
                pl.BlockSpec((tm, ATT_WIDTH), row),
                pl.BlockSpec((tm, S5_WIDTH), row),
                pl.BlockSpec((N_DIR, seqs_per_tile, M_WIDTH, seq_len), lambda i: (0, i, 0, 0)),
                pl.BlockSpec((tm, M_WIDTH), lambda i: (i, AUX_MO // M_WIDTH)),
                _mod_spec(layer, mod_row),
                _layered(layer, (1, D_MODEL), const),
                _layered(layer, (1, M_WIDTH), const),
                pl.BlockSpec((M_WIDTH, M_WIDTH), const),
                _layered(layer, (D_MODEL, D_MODEL), const)]
    return pl.pallas_call(
        _post_kernel,
        grid=(t // tm,), in_specs=in_specs,
        out_specs=[pl.BlockSpec((tm, D_MODEL), row), pl.BlockSpec((tm, D_MODEL), row)],
        out_shape=[jax.ShapeDtypeStruct((t, D_MODEL), F32), jax.ShapeDtypeStruct((t, D_MODEL), BF16)],
        compiler_params=_params("arbitrary"), name="post",
    )(x2, att, s5o, ht, proj, pp['mod'], pp['g2'], pp['m_ng'], pp['blkm'], pp['w_out'])


def _ffn_kernel(x1_ref, h2_ref, hprev_ref, hnext_ref, mod_ref, wg_ref, wu_ref, cw_ref, cb_ref, wd_ref, o_ref,
                gated_ref, *, seq_len):
    tm = h2_ref.shape[0]
    h2 = h2_ref[...]
    h2_ext = jnp.concatenate([h2, hprev_ref[...], hnext_ref[...]], axis=0)
    row = lax.broadcasted_iota(jnp.int32, (tm, FF_CHUNK), 0)
    pos = (pl.program_id(0) * tm + row) % seq_len
    seq_start = pos == 0
    seq_end = pos == seq_len - 1
    tile_start = row == 0
    tile_end = row == tm - 1
    for j in range(D_FF // FF_CHUNK):
        cols = slice(j * FF_CHUNK, (j + 1) * FF_CHUNK)
        a_ext = _dot(h2_ext, wg_ref[:, cols])
        a = a_ext[:tm]
        before = a_ext[tm + BF16_ROWS - 1:tm + BF16_ROWS, :]
        after = a_ext[tm + BF16_ROWS:tm + BF16_ROWS + 1, :]
        a_prev = jnp.where(tile_start, before, pltpu.roll(a, 1, axis=0))
        a_next = jnp.where(tile_end, after, pltpu.roll(a, tm - 1, axis=0))
        a_prev = jnp.where(seq_start, 0.0, a_prev)
        a_next = jnp.where(seq_end, 0.0, a_next)
        ac = a_prev * cw_ref[0:1, cols] + a * cw_ref[1:2, cols] + a_next * cw_ref[2:3, cols] + cb_ref[:, cols]
        up = _dot(h2, wu_ref[:, cols])
        gated_ref[:, cols] = _bf(ac * _sigmoid(ac) * up)
    o_ref[...] = x1_ref[...] + mod_ref[5:6, :] * _dot(gated_ref[...], wd_ref[...])


def _ffn_call(x1, h2, layer, pp, seq_len, per_batch_mod):
    t = x1.shape[0]
    tm = FFN_TM
    halo_blocks_per_tile = tm // BF16_ROWS
    last_halo_block = t // BF16_ROWS - 1
    if per_batch_mod:
        mod_row = lambda i: 1 + (i * tm) // seq_len
    else:
        mod_row = lambda i: 0
    row = lambda i: (i, 0)
    single = pl.Buffered(1)
    whole = lambda shape: _layered(layer, shape, lambda i: (0, 0), pipeline_mode=single)
    in_specs = [pl.BlockSpec((tm, D_MODEL), row),
                pl.BlockSpec((tm, D_MODEL), row),
                pl.BlockSpec((BF16_ROWS, D_MODEL), lambda i: (jnp.maximum(i * halo_blocks_per_tile - 1, 0), 0)),
                pl.BlockSpec((BF16_ROWS, D_MODEL),
                             lambda i: (jnp.minimum((i + 1) * halo_blocks_per_tile, last_halo_block), 0)),
                _mod_spec(layer, mod_row),
                whole((D_MODEL, D_FF)), whole((D_MODEL, D_FF)), whole((3, D_FF)), whole((1, D_FF)),
                whole((D_FF, D_MODEL))]
    return pl.pallas_call(
        functools.partial(_ffn_kernel, seq_len=seq_len),
        grid=(t // tm,), in_specs=in_specs,
        out_specs=pl.BlockSpec((tm, D_MODEL), row),
        out_shape=jax.ShapeDtypeStruct((t, D_MODEL), F32),
        scratch_shapes=[pltpu.VMEM((tm, D_FF), BF16)],
        compiler_params=_params("arbitrary"), name="ffn",
    )(x1, h2, h2, h2, pp['mod'], pp['ffn_w_gate'], pp['ffn_w_up'], pp['ffn_conv_w'], pp['ffn_conv_b'],
      pp['ffn_w_down'])


def _trunk_layer(x2, batch, seq_len, layer, pp, ctx, rope_tabs):
    latent = ctx is not None
    proj, mg, qn, kn, mqk, vt, mgt = _pre_call(x2, layer, pp, rope_tabs if latent else None, seq_len, latent)
    cache_k, cache_v, init_re, init_im, m_init = ctx if latent else (None,) * 5

    att = _attn_call(qn, kn, proj, layer, cache_k, cache_v, batch, seq_len)
    s5o, fin_re, fin_im = _s5_call(proj, layer, pp, init_re, init_im, batch, seq_len)
    ht, cn_f, m_f = _mlstm_call(mqk.reshape(batch, seq_len, 2 * M_WIDTH), vt, mg.reshape(batch, seq_len, LANES), mgt,
                                layer, pp, m_init, batch, seq_len)
    x1, h2 = _post_call(x2, att, s5o, ht, proj, layer, pp, seq_len, latent)
    x_out = _ffn_call(x1, h2, layer, pp, seq_len, latent)
    if latent:
        return x_out, None
    states = (kn.reshape(batch, seq_len, ATT_KV_HEADS, ATT_HD),
              proj[:, AUX_V:].reshape(batch, seq_len, ATT_KV_HEADS, ATT_HD),
              fin_re.transpose(1, 0, 2).reshape(batch, N_DIR, S5_GROUPS, S5_STATE),
              fin_im.transpose(1, 0, 2).reshape(batch, N_DIR, S5_GROUPS, S5_STATE),
              cn_f[:, :, :M_HD, :].reshape(batch, N_DIR, M_HEADS, M_HD, M_HD),
              cn_f[:, :, M_HD, :].reshape(batch, N_DIR, M_HEADS, M_HD),
              m_f[:, :, 0, 0].reshape(batch, N_DIR, M_HEADS))
    return x_out, states


def _head_block(width, head_dim):
    ids = np.arange(width) // head_dim
    return jnp.asarray((ids[:, None] == ids[None, :]).astype(np.float32) / head_dim, dtype=BF16)


def kernel(x_prompt, x_sample, c, cache_attn_k, cache_attn_v, state_s5_re, state_s5_im, state_mlstm_C, state_mlstm_n, state_mlstm_m, c_ctx, ada_w, ada_b, norm1_g, norm2_g, w_in, q_norm_g, k_norm_g, s5_a_re, s5_a_im, s5_log_dt, s5_b_re, s5_b_im, s5_c_re, s5_c_im, s5_d, s5_glu_w, s5_glu_b, m_gate_b, m_norm_g, w_out, ffn_w_gate, ffn_w_up, ffn_conv_w, ffn_conv_b, ffn_w_down):
    batch, seq = x_prompt.shape[0], x_prompt.shape[1]
    dec_batch, dec_seq = x_sample.shape[0], x_sample.shape[1]

    n_mod_rows = 2 * SUBLANES
    cvec = jnp.zeros((n_mod_rows, D_MODEL), F32).at[0].set(c_ctx).at[1:1 + dec_batch].set(c)
    mod_all = _ada_call(cvec, ada_w, ada_b).reshape(DEPTH, n_mod_rows, 6, D_MODEL)

    ab_re, ab_im, bb_re, bb_im = _s5_disc_call(s5_a_re, s5_a_im, s5_log_dt, s5_b_re, s5_b_im)
    ab_re = ab_re.reshape(DEPTH, N_DIR, 1, S5_FLAT)
    ab_im = ab_im.reshape(DEPTH, N_DIR, 1, S5_FLAT)
    bb_re = bb_re.reshape(DEPTH, N_DIR, S5_GROUPS, S5_STATE, S5_CH)
    bb_im = bb_im.reshape(DEPTH, N_DIR, S5_GROUPS, S5_STATE, S5_CH)

    rope_tabs = _rope_tables(dec_seq)
    tri = np.tril(np.ones((M_TILE, M_TILE), np.float32))
    tril = jnp.asarray(tri, dtype=BF16)
    triu = jnp.asarray(tri.T, dtype=BF16)
    ones = jnp.ones((M_TILE, M_TILE), BF16)
    blkq = _head_block(ATT_WIDTH, ATT_HD)
    blkk = _head_block(ATT_KV_WIDTH, ATT_HD)
    blkm = _head_block(M_WIDTH, M_HD)

    w_in_b = _bf(w_in)
    gb = m_gate_b.reshape(DEPTH, M_GATE_COLS)
    pp = dict(
        mod=mod_all, g1=norm1_g.reshape(DEPTH, 1, D_MODEL), g2=norm2_g.reshape(DEPTH, 1, D_MODEL),
        w_in=w_in_b,
        w_gatecols=jnp.zeros((DEPTH, D_MODEL, LANES), BF16).at[:, :, :M_GATE_COLS].set(w_in_b[:, :, PROJ_MAIN:]),
        blkq=blkq, blkk=blkk, blkm=blkm,
        qg=jnp.tile(q_norm_g, (1, ATT_HEADS)).reshape(DEPTH, 1, ATT_WIDTH),
        kg=jnp.tile(k_norm_g, (1, ATT_KV_HEADS)).reshape(DEPTH, 1, ATT_KV_WIDTH),
        s5_bmat=_bf(jnp.concatenate([_block_diag(jnp.swapaxes(bb_re, -1, -2)),
                                     _block_diag(jnp.swapaxes(bb_im, -1, -2))], axis=-1)),
        s5_cre=_bf(_block_diag(jnp.swapaxes(s5_c_re, -1, -2))),
        s5_cim=_bf(_block_diag(jnp.swapaxes(s5_c_im, -1, -2))),
        s5_ab_re=jnp.broadcast_to(ab_re, (DEPTH, N_DIR, SUBLANES, S5_FLAT)),
        s5_ab_im=jnp.broadcast_to(ab_im, (DEPTH, N_DIR, SUBLANES, S5_FLAT)),
        s5_d=s5_d.reshape(DEPTH, 1, S5_WIDTH), s5_glu_w=_bf(s5_glu_w), s5_glu_b=s5_glu_b.reshape(DEPTH, 1, S5_WIDTH),
        m_gb_row=jnp.zeros((DEPTH, 1, LANES), F32).at[:, 0, :M_GATE_COLS].set(gb),
        m_gb_col=gb.reshape(DEPTH, M_GATE_COLS, 1),
        m_ng=jnp.tile(m_norm_g, (1, M_HEADS)).reshape(DEPTH, 1, M_WIDTH), tril=tril, triu=triu, ones=ones,
        w_out=_bf(w_out), ffn_w_gate=_bf(ffn_w_gate), ffn_w_up=_bf(ffn_w_up),
        ffn_conv_w=ffn_conv_w, ffn_conv_b=ffn_conv_b.reshape(DEPTH, 1, D_FF), ffn_w_down=_bf(ffn_w_down))

    past = cache_attn_k.shape[2]
    n_pad = CN_ROWS - M_HD
    s5_init = lambda st: st.reshape(dec_batch, DEPTH, N_DIR, S5_FLAT).transpose(1, 2, 0, 3)
    m_init = (state_mlstm_C.reshape(dec_batch, DEPTH, M_CHAINS, M_HD, M_HD),
              jnp.zeros((dec_batch, DEPTH, M_CHAINS, n_pad, M_HD), F32).at[:, :, :, 0, :].set(
                  state_mlstm_n.reshape(dec_batch, DEPTH, M_CHAINS, M_HD)),
              jnp.broadcast_to(state_mlstm_m.reshape(dec_batch, DEPTH, M_CHAINS, 1, 1),
                               (dec_batch, DEPTH, M_CHAINS, 1, LANES)))
    ctx = (cache_attn_k.reshape(dec_batch, DEPTH, past, ATT_KV_WIDTH),
           cache_attn_v.reshape(dec_batch, DEPTH, past, ATT_KV_WIDTH),
           s5_init(state_s5_re), s5_init(state_s5_im), m_init)

    xp = x_prompt.reshape(batch * seq, D_MODEL)
    xs = x_sample.reshape(dec_batch * dec_seq, D_MODEL)
    ctx_out = []
    for l in range(DEPTH):
        xp, st = _trunk_layer(xp, batch, seq, l, pp, None, None)
        ctx_out.append(st)
        xs, _ = _trunk_layer(xs, dec_batch, dec_seq, l, pp, ctx, rope_tabs)
    outs = [jnp.stack([s[i] for s in ctx_out], axis=1) for i in range(7)]
    return (xp.reshape(batch, seq, D_MODEL), xs.reshape(dec_batch, dec_seq, D_MODEL), *outs)
```

```python
import functools

import numpy as np
import jax
import jax.numpy as jnp
from jax import lax
from jax.experimental import pallas as pl
from jax.experimental.pallas import tpu as pltpu

F32 = jnp.float32
BF16 = jnp.bfloat16

D_MODEL = 1024
DEPTH = 2
GRID_W = 64
N_DIR = 2
EPS = 1e-6
ATT_HD = 64
ATT_WIDTH = 512
ATT_HEADS = 8
ATT_KV_HEADS = 2
ATT_GROUP = ATT_HEADS // ATT_KV_HEADS
ATT_KV_WIDTH = ATT_KV_HEADS * ATT_HD
ROPE_THETA = 10000.0
LOG2_E = 1.4426950408889634
S5_CH = 16
S5_STATE = 64
S5_WIDTH = 256
S5_GROUPS = 16
S5_FLAT = S5_GROUPS * S5_STATE
M_HD = 64
M_WIDTH = 256
M_HEADS = 4
M_GATE_COLS = 2 * N_DIR * M_HEADS
M_CHAINS = N_DIR * M_HEADS
D_FF = 2816
PROJ_MAIN = 2048
COL_K = ATT_WIDTH
COL_V = COL_K + ATT_KV_WIDTH
COL_S5 = COL_V + ATT_KV_WIDTH
COL_MQ = COL_S5 + S5_WIDTH
COL_MK = COL_MQ + M_WIDTH
COL_MV = COL_MK + M_WIDTH
COL_MO = COL_MV + M_WIDTH
AUX_S5 = 0
AUX_MO = AUX_S5 + S5_WIDTH
AUX_V = AUX_MO + M_WIDTH
AUX_WIDTH = AUX_V + ATT_KV_WIDTH
LANES = 128
SUBLANES = 8
BF16_ROWS = 16
VMEM_LIMIT = 56 * 1024 * 1024

PRE_TM = 512
ATT_TQ = 512
S5_ROWS = 256
S5_SLOTS = 3
M_TILE = 128
M_BG = 4
M_BG_SEQ = 1024
CN_ROWS = M_HD + BF16_ROWS
POST_TM = 1024
POST_OUT_CHUNK = 256
FFN_TM = 1024
FF_CHUNK = 256


def _bf(x):
    return x.astype(BF16)


def _dot(a, b):
    return jnp.dot(a, b, preferred_element_type=F32)


def _split(x):
    hi = _bf(x)
    lo = _bf(x - hi.astype(F32))
    return hi, lo


def _seg_mean_sq(x, blk):
    hi, lo = _split(x * x)
    return _dot(hi, blk) + _dot(lo, blk)


def _sigmoid(x):
    return 1.0 / (1.0 + jnp.exp(-x))


def _log_sigmoid(x):
    return -(jnp.maximum(-x, 0.0) + jnp.log1p(jnp.exp(-jnp.abs(x))))


def _params(*sem):
    return pltpu.CompilerParams(dimension_semantics=sem, vmem_limit_bytes=VMEM_LIMIT)


def _ada_kernel(c_ref, w_ref, b_ref, o_ref):
    c = c_ref[...]
    s = c * _sigmoid(c)
    o_ref[...] = _dot(_bf(s), _bf(w_ref[...])) + b_ref[...]


def _ada_call(cvec, ada_w, ada_b):
    rows = cvec.shape[0]
    tn = 1024
    n = ada_w.shape[-1]
    return pl.pallas_call(
        _ada_kernel,
        grid=(DEPTH, n // tn),
        in_specs=[pl.BlockSpec((rows, D_MODEL), lambda l, j: (0, 0)),
                  pl.BlockSpec((None, D_MODEL, tn), lambda l, j: (l, 0, j)),
                  pl.BlockSpec((None, 1, tn), lambda l, j: (l, 0, j))],
        out_specs=pl.BlockSpec((None, rows, tn), lambda l, j: (l, 0, j)),
        out_shape=jax.ShapeDtypeStruct((DEPTH, rows, n), F32),
        compiler_params=_params("arbitrary", "arbitrary"),
        name="adaln",
    )(cvec, ada_w, ada_b.reshape(DEPTH, 1, n))


def _s5_disc_kernel(are_ref, aim_ref, ldt_ref, arex_ref, aimx_ref, bre_ref, bim_ref,
                    abr_ref, abi_ref, bbr_ref, bbi_ref):
    dt = jnp.exp(ldt_ref[...])

    def disc(a_re, a_im):
        mag = jnp.exp(dt * a_re)
        ab_re = mag * jnp.cos(dt * a_im)
        ab_im = mag * jnp.sin(dt * a_im)
        den = a_re * a_re + a_im * a_im
        nr = ab_re - 1.0
        ni = ab_im
        f_re = (nr * a_re + ni * a_im) / den
        f_im = (ni * a_re - nr * a_im) / den
        return ab_re, ab_im, f_re, f_im

    ab_re, ab_im, _, _ = disc(are_ref[...], aim_ref[...])
    abr_ref[...] = ab_re
    abi_ref[...] = ab_im
    _, _, f_re, f_im = disc(arex_ref[...], aimx_ref[...])
    b_re = bre_ref[...]
    b_im = bim_ref[...]
    bbr_ref[...] = f_re * b_re - f_im * b_im
    bbi_ref[...] = f_re * b_im + f_im * b_re


def _s5_disc_call(a_re, a_im, log_dt, b_re, b_im):
    r = DEPTH * N_DIR * S5_GROUPS
    a_re2 = a_re.reshape(r, S5_STATE)
    a_im2 = a_im.reshape(r, S5_STATE)
    wide = S5_STATE * S5_CH
    out_shape = (jax.ShapeDtypeStruct((r, S5_STATE), F32), jax.ShapeDtypeStruct((r, S5_STATE), F32),
                 jax.ShapeDtypeStruct((r, wide), F32), jax.ShapeDtypeStruct((r, wide), F32))
    return pl.pallas_call(_s5_disc_kernel, out_shape=out_shape, name="s5_disc")(
        a_re2, a_im2, log_dt.reshape(r, 1),
        jnp.repeat(a_re2, S5_CH, axis=1), jnp.repeat(a_im2, S5_CH, axis=1),
        b_re.reshape(r, wide), b_im.reshape(r, wide))


def _block_diag(blocks):
    g, r, c = blocks.shape[-3:]
    lead = blocks.shape[:-3]
    tiled = jnp.tile(blocks.reshape(lead + (g * r, c)), (1,) * len(lead) + (1, g))
    on_diagonal = (np.arange(g * r)[:, None] // r) == (np.arange(g * c)[None, :] // c)
    return jnp.where(on_diagonal, tiled, 0.0)


def _rope(x, cos, sin_signed, second):
    width = x.shape[-1]
    quarter = ATT_HD // 4
    partner = jnp.where(second, pltpu.roll(x, quarter, axis=1), pltpu.roll(x, width - quarter, axis=1))
    return x * cos + partner * sin_signed


def _pre_kernel(*refs, rope):
    if rope:
        (x_ref, mod_ref, g1_ref, w_ref, wg_ref, blkq_ref, blkk_ref, qg_ref, kg_ref,
         cosq_ref, sinq_ref, cosk_ref, sink_ref, proj_ref, mg_ref, qn_ref, kn_ref, mqk_ref, vt_ref, mgt_ref) = refs
    else:
        (x_ref, mod_ref, g1_ref, w_ref, wg_ref, blkq_ref, blkk_ref, qg_ref, kg_ref,
         proj_ref, mg_ref, qn_ref, kn_ref, mqk_ref, vt_ref, mgt_ref) = refs
    x = x_ref[...]
    ms = jnp.mean(x * x, axis=-1, keepdims=True)
    xn = x * lax.rsqrt(ms + EPS) * g1_ref[...]
    h = _bf(xn * (1.0 + mod_ref[1:2, :]) + mod_ref[0:1, :])
    def proj(c0, c1):
        return _dot(h, w_ref[:, c0:c1])

    q = proj(0, COL_K)
    kvs = proj(COL_K, COL_MQ)
    k = kvs[:, :ATT_KV_WIDTH]
    mqk = proj(COL_MQ, COL_MV)
    mvo = proj(COL_MV, PROJ_MAIN)
    proj_ref[:, AUX_S5:AUX_MO] = kvs[:, 2 * ATT_KV_WIDTH:]
    proj_ref[:, AUX_MO:AUX_V] = mvo[:, M_WIDTH:]
    proj_ref[:, AUX_V:] = kvs[:, ATT_KV_WIDTH:2 * ATT_KV_WIDTH]
    mg = _dot(h, wg_ref[...])
    mg_ref[...] = mg
    mqk_ref[:, :M_WIDTH] = _bf(mqk[:, :M_WIDTH])
    mqk_ref[:, M_WIDTH:] = _bf(mqk[:, M_WIDTH:] * (M_HD ** -0.5))
    vt_ref[...] = mvo[:, :M_WIDTH].T
    mgt_ref[...] = mg.T[:M_GATE_COLS, :]
    qn = q * lax.rsqrt(_seg_mean_sq(q, blkq_ref[...]) + EPS) * qg_ref[...]
    kn = k * lax.rsqrt(_seg_mean_sq(k, blkk_ref[...]) + EPS) * kg_ref[...]
    if rope:
        lane_q = lax.broadcasted_iota(jnp.int32, qn.shape, 1)
        lane_k = lax.broadcasted_iota(jnp.int32, kn.shape, 1)
        qn = _rope(qn, cosq_ref[...], sinq_ref[...], (lane_q & (ATT_HD // 4)) != 0)
        kn = _rope(kn, cosk_ref[...], sink_ref[...], (lane_k & (ATT_HD // 4)) != 0)
    qn_ref[...] = qn
    kn_ref[...] = kn


def _rope_tables(seq_len):
    n_rows = seq_len // GRID_W
    row = np.repeat(np.arange(n_rows), GRID_W)
    col = np.tile(np.arange(GRID_W), n_rows)
    half = ATT_HD // 2
    inv_freq = (1.0 / (np.float32(ROPE_THETA) ** (np.arange(0, half, 2, dtype=np.float32) / half))).astype(np.float32)

    def tables(pos):
        ang = pos.astype(np.float32)[:, None] * inv_freq[None, :]
        cos = np.cos(ang)
        sin = np.sin(ang)
        return np.concatenate([cos, cos], axis=-1), np.concatenate([-sin, sin], axis=-1)

    cr, sr = tables(row)
    cc, sc = tables(col)
    cos = np.concatenate([cr, cc], axis=-1)
    sin = np.concatenate([sr, sc], axis=-1)
    return tuple(jnp.asarray(np.tile(t, (1, n)), dtype=F32)
                 for t, n in ((cos, ATT_HEADS), (sin, ATT_HEADS), (cos, ATT_KV_HEADS), (sin, ATT_KV_HEADS)))


def _layered(layer, block, index_map, **kw):
    return pl.BlockSpec((None,) + tuple(block), lambda *g: (layer,) + tuple(index_map(*g)), **kw)


def _mod_spec(layer, row_of):
    return pl.BlockSpec((None, None, 6, D_MODEL), lambda *g: (layer, row_of(*g), 0, 0))


def _pre_call(x2, layer, pp, rope_tabs, seq_len, per_batch_mod):
    t = x2.shape[0]
    tm = min(PRE_TM, seq_len)
    tiles_per_seq = seq_len // tm
    rope = rope_tabs is not None
    if per_batch_mod:
        mod_row = lambda i: 1 + i // tiles_per_seq
    else:
        mod_row = lambda i: 0
    const = lambda i: (0, 0)
    row = lambda i: (i, 0)
    in_specs = [pl.BlockSpec((tm, D_MODEL), row),
                _mod_spec(layer, mod_row),
                _layered(layer, (1, D_MODEL), const),
                _layered(layer, (D_MODEL, PROJ_MAIN), const),
                _layered(layer, (D_MODEL, LANES), const),
                pl.BlockSpec((ATT_WIDTH, ATT_WIDTH), const),
                pl.BlockSpec((ATT_KV_WIDTH, ATT_KV_WIDTH), const),
                _layered(layer, (1, ATT_WIDTH), const),
                _layered(layer, (1, ATT_KV_WIDTH), const)]
    args = [x2, pp['mod'], pp['g1'], pp['w_in'], pp['w_gatecols'], pp['blkq'], pp['blkk'], pp['qg'], pp['kg']]
    if rope:
        pos_map = lambda i: (i % tiles_per_seq, 0)
        in_specs += [pl.BlockSpec((tm, ATT_WIDTH), pos_map), pl.BlockSpec((tm, ATT_WIDTH), pos_map),
                     pl.BlockSpec((tm, ATT_KV_WIDTH), pos_map), pl.BlockSpec((tm, ATT_KV_WIDTH), pos_map)]
        args += list(rope_tabs)
    time_on_lanes = lambda i: (i // tiles_per_seq, 0, i % tiles_per_seq)
    batch = t // seq_len
    out_specs = [pl.BlockSpec((tm, AUX_WIDTH), row),
                 pl.BlockSpec((tm, LANES), row),
                 pl.BlockSpec((tm, ATT_WIDTH), row),
                 pl.BlockSpec((tm, ATT_KV_WIDTH), row),
                 pl.BlockSpec((tm, 2 * M_WIDTH), row),
                 pl.BlockSpec((None, M_WIDTH, tm), time_on_lanes),
                 pl.BlockSpec((None, M_GATE_COLS, tm), time_on_lanes)]
    out_shape = [jax.ShapeDtypeStruct((t, AUX_WIDTH), F32), jax.ShapeDtypeStruct((t, LANES), F32),
                 jax.ShapeDtypeStruct((t, ATT_WIDTH), F32), jax.ShapeDtypeStruct((t, ATT_KV_WIDTH), F32),
                 jax.ShapeDtypeStruct((t, 2 * M_WIDTH), BF16),
                 jax.ShapeDtypeStruct((batch, M_WIDTH, seq_len), F32),
                 jax.ShapeDtypeStruct((batch, M_GATE_COLS, seq_len), F32)]
    return pl.pallas_call(
        functools.partial(_pre_kernel, rope=rope),
        grid=(t // tm,), in_specs=in_specs, out_specs=out_specs, out_shape=out_shape,
        compiler_params=_params("arbitrary"), name="pre_rope" if rope else "pre",
    )(*args)


def _attn_kernel(*refs, cached):
    if cached:
        q_ref, k_ref, v_ref, ck_ref, cv_ref, o_ref = refs
        k_all = _bf(jnp.concatenate([ck_ref[...], k_ref[...]], axis=0))
        v_all = _bf(jnp.concatenate([cv_ref[...], v_ref[...]], axis=0))
    else:
        q_ref, k_ref, v_ref, o_ref = refs
        k_all = _bf(k_ref[...])
        v_all = _bf(v_ref[...])
    scale = ATT_HD ** -0.5 * LOG2_E
    outs = []
    for kvh in range(ATT_KV_HEADS):
        lo = kvh * ATT_HD
        kb = k_all[:, lo:lo + ATT_HD]
        vb = v_all[:, lo:lo + ATT_HD]
        for g in range(ATT_GROUP):
            c0 = (kvh * ATT_GROUP + g) * ATT_HD
            qb = _bf(q_ref[:, c0:c0 + ATT_HD] * scale)
            s = lax.dot_general(qb, kb, (((1,), (1,)), ((), ())), preferred_element_type=F32)
            m = jnp.max(s, axis=-1, keepdims=True)
            e = jnp.exp2(s - m)
            den = jnp.sum(e, axis=-1, keepdims=True)
            outs.append(_dot(_bf(e), vb) / den)
    o_ref[...] = _bf(jnp.concatenate(outs, axis=-1))


def _attn_call(qn, kn, proj, layer, cache_k, cache_v, batch, lq):
    tq = min(ATT_TQ, lq)
    nq = lq // tq
    cached = cache_k is not None
    in_specs = [pl.BlockSpec((tq, ATT_WIDTH), lambda b, i: (b * nq + i, 0)),
                pl.BlockSpec((lq, ATT_KV_WIDTH), lambda b, i: (b, 0)),
                pl.BlockSpec((lq, ATT_KV_WIDTH), lambda b, i: (b, AUX_V // ATT_KV_WIDTH))]
    args = [qn, kn, proj]
    if cached:
        past = cache_k.shape[2]
        cache_spec = pl.BlockSpec((None, None, past, ATT_KV_WIDTH), lambda b, i: (b, layer, 0, 0))
        in_specs += [cache_spec, cache_spec]
        args += [cache_k, cache_v]
    return pl.pallas_call(
        functools.partial(_attn_kernel, cached=cached),
        grid=(batch, nq), in_specs=in_specs,
        out_specs=pl.BlockSpec((tq, ATT_WIDTH), lambda b, i: (b * nq + i, 0)),
        out_shape=jax.ShapeDtypeStruct((batch * lq, ATT_WIDTH), BF16),
        compiler_params=_params("arbitrary", "arbitrary"), name="attn_cached" if cached else "attn",
    )(*args)


def _s5_kernel(*refs, batch, seq_len, zero_init):
    if zero_init:
        (u_ref, bmat_ref, cre_ref, cim_ref, ar_ref, ai_ref, d_ref, gw_ref, gb_ref,
         out_ref, fr_ref, fi_ref, utb_ref, ytb_ref, xs_ref) = refs
        ir_ref = ii_ref = None
    else:
        (u_ref, bmat_ref, cre_ref, cim_ref, ar_ref, ai_ref, d_ref, gw_ref, gb_ref, ir_ref, ii_ref,
         out_ref, fr_ref, fi_ref, utb_ref, ytb_ref, xs_ref) = refs
    rows = S5_ROWS
    tc = rows // batch
    n_chunks = seq_len // tc
    n_sub = batch // SUBLANES
    halves = S5_WIDTH // LANES

    def aligned(x, m):
        return x if isinstance(x, int) else pl.multiple_of(x, m)

    for b in range(batch):
        for hf in range(halves):
            utb_ref[hf, pl.ds(b, seq_len, stride=batch), :] = (
                u_ref[b * seq_len:(b + 1) * seq_len, hf * LANES:(hf + 1) * LANES])
    for hf in range(halves):
        ytb_ref[hf] = utb_ref[hf] * d_ref[:, hf * LANES:(hf + 1) * LANES]

    def chunk_of(d, i):
        return i if d == 0 else n_chunks - 1 - i

    def stage_in(i, slot):
        for d in range(N_DIR):
            r0 = aligned(chunk_of(d, i) * rows, rows)
            u_c = jnp.concatenate([utb_ref[hf, pl.ds(r0, rows), :] for hf in range(halves)], axis=-1)
            xs_ref[slot, d] = _dot(_bf(u_c), bmat_ref[d])

    def stage_scan(slot, carry):
        new_carry = []
        for d in range(N_DIR):
            a_re = ar_ref[d]
            a_im = ai_ref[d]
            per_sub = []
            for sub in range(n_sub):
                s_re, s_im = carry[d][sub]
                for t in range(tc):
                    r = (t if d == 0 else tc - 1 - t) * batch + sub * SUBLANES
                    x_re = xs_ref[slot, d, r:r + SUBLANES, :S5_FLAT]
                    x_im = xs_ref[slot, d, r:r + SUBLANES, S5_FLAT:]
                    s_re, s_im = (a_re * s_re - a_im * s_im + x_re, a_re * s_im + a_im * s_re + x_im)
                    xs_ref[slot, d, r:r + SUBLANES, :S5_FLAT] = s_re
                    xs_ref[slot, d, r:r + SUBLANES, S5_FLAT:] = s_im
                per_sub.append((s_re, s_im))
            new_carry.append(tuple(per_sub))
        return tuple(new_carry)

    def stage_out(i, slot):
        for d in range(N_DIR):
            r0 = aligned(chunk_of(d, i) * rows, rows)
            y = (_dot(_bf(xs_ref[slot, d, :, :S5_FLAT]), cre_ref[d])
                 - _dot(_bf(xs_ref[slot, d, :, S5_FLAT:]), cim_ref[d]))
            for hf in range(halves):
                ytb_ref[hf, pl.ds(r0, rows), :] += y[:, hf * LANES:(hf + 1) * LANES]

    def start_state(ref, d, sub):
        if zero_init:
            return jnp.zeros((SUBLANES, S5_FLAT), F32)
        return ref[d, sub * SUBLANES:(sub + 1) * SUBLANES, :]

    carry = tuple(tuple((start_state(ir_ref, d, sub), start_state(ii_ref, d, sub)) for sub in range(n_sub))
                  for d in range(N_DIR))
    def step(i, phase, carry):
        stage_in(i + 1, (phase + 1) % S5_SLOTS)
        carry = stage_scan(phase, carry)
        stage_out(i - 1, (phase - 1) % S5_SLOTS)
        return carry

    stage_in(0, 0)
    stage_in(1, 1)
    carry = stage_scan(0, carry)
    n_steady = n_chunks - 2
    n_peeled = n_steady % S5_SLOTS
    for i in range(1, 1 + n_peeled):
        carry = step(i, i % S5_SLOTS, carry)
    first = 1 + n_peeled

    def body(g, carry):
        for k in range(S5_SLOTS):
            carry = step(first + g * S5_SLOTS + k, (first + k) % S5_SLOTS, carry)
        return carry

    carry = lax.fori_loop(0, n_steady // S5_SLOTS, body, carry)
    last = n_chunks - 1
    carry = stage_scan(last % S5_SLOTS, carry)
    stage_out(last - 1, (last - 1) % S5_SLOTS)
    stage_out(last, last % S5_SLOTS)
    for d in range(N_DIR):
        for sub in range(n_sub):
            fr_ref[d, sub * SUBLANES:(sub + 1) * SUBLANES, :] = carry[d][sub][0]
            fi_ref[d, sub * SUBLANES:(sub + 1) * SUBLANES, :] = carry[d][sub][1]

    for b in range(batch):
        z = jax.nn.gelu(jnp.concatenate(
            [ytb_ref[hf, pl.ds(b, seq_len, stride=batch), :] for hf in range(halves)], axis=-1))
        gate = _sigmoid(_dot(_bf(z), gw_ref[...]) + gb_ref[...])
        out_ref[b * seq_len:(b + 1) * seq_len, :] = _bf(z * gate)


def _s5_call(proj, layer, pp, init_re, init_im, batch, seq_len):
    rows = seq_len * batch
    single = pl.Buffered(1)
    zero_init = init_re is None
    names = ['s5_bmat', 's5_cre', 's5_cim', 's5_ab_re', 's5_ab_im', 's5_d', 's5_glu_w', 's5_glu_b']
    args = [pp[n] for n in names] + ([] if zero_init else [init_re, init_im])
    whole = lambda a: _layered(layer, a.shape[1:], lambda i: (0,) * (a.ndim - 1), pipeline_mode=single)
    in_specs = ([pl.BlockSpec((rows, S5_WIDTH), lambda i: (0, AUX_S5 // S5_WIDTH), pipeline_mode=single)]
                + [whole(a) for a in args])
    fin_spec = pl.BlockSpec((N_DIR, batch, S5_FLAT), lambda i: (0, 0, 0))
    out_specs = [pl.BlockSpec((rows, S5_WIDTH), lambda i: (0, 0), pipeline_mode=single), fin_spec, fin_spec]
    out_shape = (jax.ShapeDtypeStruct((rows, S5_WIDTH), BF16),
                 jax.ShapeDtypeStruct((N_DIR, batch, S5_FLAT), F32),
                 jax.ShapeDtypeStruct((N_DIR, batch, S5_FLAT), F32))
    return pl.pallas_call(
        functools.partial(_s5_kernel, batch=batch, seq_len=seq_len, zero_init=zero_init),
        grid=(1,), in_specs=in_specs, out_specs=out_specs, out_shape=out_shape,
        scratch_shapes=[pltpu.VMEM((S5_WIDTH // LANES, rows, LANES), F32),
                        pltpu.VMEM((S5_WIDTH // LANES, rows, LANES), F32),
                        pltpu.VMEM((S5_SLOTS, N_DIR, S5_ROWS, 2 * S5_FLAT), F32)],
        compiler_params=_params("arbitrary"), name="s5",
    )(proj, *args)


def _mlstm_kernel(*refs, seq_len, zero_init):
    if zero_init:
        (q_ref, k_ref, vt_ref, mg_ref, mgt_ref, gbr_ref, gbc_ref, tril_ref, triu_ref, ones_ref,
         ht_ref, cn_ref, m_ref) = refs
    else:
        (q_ref, k_ref, vt_ref, mg_ref, mgt_ref, gbr_ref, gbc_ref, tril_ref, triu_ref, ones_ref,
         c0_ref, n0_ref, m0_ref, ht_ref, cn_ref, m_ref) = refs
    _mlstm_body(q_ref, k_ref, vt_ref, mg_ref, mgt_ref, gbr_ref, gbc_ref, tril_ref, triu_ref, ones_ref,
                None if zero_init else (c0_ref, n0_ref, m0_ref), ht_ref, cn_ref, m_ref, seq_len)


def _mlstm_body(q_ref, k_ref, vt_ref, mg_ref, mgt_ref, gbr_ref, gbc_ref, tril_ref, triu_ref, ones_ref,
                init_refs, ht_ref, cn_ref, m_ref, seq_len):
    tile = M_TILE
    n_chunks = seq_len // tile
    bg = q_ref.shape[0]
    if init_refs is None:
        cn_ref[...] = jnp.zeros_like(cn_ref)
        m_ref[...] = jnp.zeros_like(m_ref)
    else:
        c0_ref, n0_ref, m0_ref = init_refs
        cn_ref[:, :, :M_HD, :] = c0_ref[...]
        cn_ref[:, :, M_HD:, :] = n0_ref[...]
        m_ref[...] = m0_ref[...]
    s_ids = lax.broadcasted_iota(jnp.int32, (tile, tile), 0)
    t_ids = lax.broadcasted_iota(jnp.int32, (tile, tile), 1)
    first_row = lax.broadcasted_iota(jnp.int32, (CN_ROWS - M_HD, tile), 0) == 0
    neg_inf = jnp.float32(-jnp.inf)
    ones = ones_ref[...]

    def direction(d, c):
        t0 = pl.multiple_of(c * tile, tile)
        col_mat = tril_ref[...] if d == 0 else triu_ref[...]
        row_mat = triu_ref[...] if d == 0 else tril_ref[...]
        valid = (s_ids <= t_ids) if d == 0 else (s_ids >= t_ids)
        gt = mgt_ref[:, :, pl.ds(t0, tile)] + gbc_ref[...]
        lft_hi, lft_lo = _split(_log_sigmoid(gt).reshape(bg * M_GATE_COLS, tile))
        b_rows = (_dot(lft_hi, row_mat) + _dot(lft_lo, row_mat)).reshape(bg, M_GATE_COLS, tile)
        totals = (_dot(lft_hi, ones) + _dot(lft_lo, ones)).reshape(bg, M_GATE_COLS, tile)
        gc = mg_ref[:, pl.ds(t0, tile), :] + gbr_ref[...]
        lfc_hi, lfc_lo = _split(_log_sigmoid(gc))
        col_b = jnp.broadcast_to(col_mat, (bg, tile, tile))
        b_cols = (jnp.einsum('bts,bsl->btl', col_b, lfc_hi, preferred_element_type=F32)
                  + jnp.einsum('bts,bsl->btl', col_b, lfc_lo, preferred_element_type=F32))
        for h in range(M_HEADS):
            j = d * M_HEADS + h
            fj = M_CHAINS + j
            rows = slice(h * M_HD, (h + 1) * M_HD)
            q_c = q_ref[:, pl.ds(t0, tile), rows]
            k_c = k_ref[:, pl.ds(t0, tile), rows]
            vt_c = vt_ref[:, rows, pl.ds(t0, tile)]
            b_t = b_rows[:, fj:fj + 1, :]
            li_t = gt[:, j:j + 1, :]
            tot = totals[:, fj:fj + 1, :]
            m_prev = m_ref[:, j]
            cn_prev = cn_ref[:, j]

            c_col = gc[:, :, j:j + 1] - b_cols[:, :, fj:fj + 1]
            cm = jnp.where(valid, c_col, neg_inf)
            inter = b_t + m_prev
            m_t = jnp.maximum(inter, b_t + jnp.max(cm, axis=1, keepdims=True))
            w_t = jnp.exp(cm + (b_t - m_t))
            s_inter = jnp.exp(inter - m_t)
            sc_t = jnp.einsum('bse,bte->bst', k_c, q_c, preferred_element_type=F32) * w_t
            num = jnp.einsum('bds,bst->bdt', _bf(vt_c), _bf(sc_t), preferred_element_type=F32)
            ext = jnp.einsum('bre,bte->brt', _bf(cn_prev), q_c, preferred_element_type=F32)
            num = num + s_inter * ext[:, :M_HD, :]
            den = jnp.sum(sc_t, axis=1, keepdims=True) + s_inter * ext[:, M_HD:M_HD + 1, :]
            ht_ref[d, :, rows, pl.ds(t0, tile)] = num / jnp.maximum(jnp.abs(den), jnp.exp(-m_t))

            g_row = tot - b_t + li_t
            m_new = jnp.maximum(tot + m_prev, jnp.max(g_row, axis=-1, keepdims=True))
            wk = jnp.exp(g_row - m_new)
            decay = jnp.exp(tot + m_prev - m_new)
            vw = jnp.concatenate([vt_c * wk, jnp.where(first_row, wk, 0.0)], axis=1)
            cn_ref[:, j] = (decay[:, :, :M_HD] * cn_prev
                            + jnp.einsum('brs,bse->bre', _bf(vw), k_c, preferred_element_type=F32))
            m_ref[:, j] = m_new

    def body(i, _):
        direction(0, i)
        direction(1, n_chunks - 1 - i)
        return 0

    lax.fori_loop(0, n_chunks, body, 0)


def _mlstm_call(mqk3, vt, mg3, mgt, layer, pp, init, batch, seq_len):
    bg = min(batch // 2, M_BG * M_BG_SEQ // seq_len)
    const2 = lambda g: (0, 0)
    lead3 = lambda g: (g, 0, 0)
    lead4 = lambda g: (g, 0, 0, 0)
    in_specs = [pl.BlockSpec((bg, seq_len, M_WIDTH), lead3),
                pl.BlockSpec((bg, seq_len, M_WIDTH), lambda g: (g, 0, 1)),
                pl.BlockSpec((bg, M_WIDTH, seq_len), lead3),
                pl.BlockSpec((bg, seq_len, LANES), lead3),
                pl.BlockSpec((bg, M_GATE_COLS, seq_len), lead3),
                _layered(layer, (1, LANES), const2),
                _layered(layer, (M_GATE_COLS, 1), const2),
                pl.BlockSpec((M_TILE, M_TILE), const2),
                pl.BlockSpec((M_TILE, M_TILE), const2),
                pl.BlockSpec((M_TILE, M_TILE), const2)]
    args = [mqk3, mqk3, vt, mg3, mgt, pp['m_gb_row'], pp['m_gb_col'], pp['tril'], pp['triu'], pp['ones']]
    if init is not None:
        state_spec = lambda *tail: pl.BlockSpec((bg, None, M_CHAINS) + tail, lambda g: (g, layer, 0, 0, 0))
        in_specs += [state_spec(M_HD, M_HD), state_spec(CN_ROWS - M_HD, M_HD), state_spec(1, LANES)]
        args += list(init)
    out_specs = [pl.BlockSpec((N_DIR, bg, M_WIDTH, seq_len), lambda g: (0, g, 0, 0)),
                 pl.BlockSpec((bg, M_CHAINS, CN_ROWS, M_HD), lead4),
                 pl.BlockSpec((bg, M_CHAINS, 1, LANES), lead4)]
    out_shape = [jax.ShapeDtypeStruct((N_DIR, batch, M_WIDTH, seq_len), F32),
                 jax.ShapeDtypeStruct((batch, M_CHAINS, CN_ROWS, M_HD), F32),
                 jax.ShapeDtypeStruct((batch, M_CHAINS, 1, LANES), F32)]
    return pl.pallas_call(
        functools.partial(_mlstm_kernel, seq_len=seq_len, zero_init=init is None),
        grid=(batch // bg,), in_specs=in_specs, out_specs=out_specs, out_shape=out_shape,
        compiler_params=_params("arbitrary"), name="mlstm",
    )(*args)


def _post_kernel(x_ref, att_ref, s5_ref, ht_ref, mo_ref, mod_ref, g2_ref, ng_ref, blkm_ref, wo_ref,
                 x1_ref, h2_ref):
    mh = jnp.concatenate([(ht_ref[0, s] + ht_ref[1, s]).T for s in range(ht_ref.shape[1])], axis=0)
    ml = mh * lax.rsqrt(_seg_mean_sq(mh, blkm_ref[...]) + EPS) * ng_ref[...] * _sigmoid(mo_ref[...])
    mixers = jnp.concatenate([att_ref[...], s5_ref[...], _bf(ml)], axis=-1)
    for n0 in range(0, D_MODEL, POST_OUT_CHUNK):
        cols = slice(n0, n0 + POST_OUT_CHUNK)
        x1_ref[:, cols] = x_ref[:, cols] + mod_ref[2:3, cols] * _dot(mixers, wo_ref[:, cols])
    x1 = x1_ref[...]
    ms = jnp.mean(x1 * x1, axis=-1, keepdims=True)
    xn = x1 * lax.rsqrt(ms + EPS) * g2_ref[...]
    h2_ref[...] = _bf(xn * (1.0 + mod_ref[4:5, :]) + mod_ref[3:4, :])


def _post_call(x2, att, s5o, ht, proj, layer, pp, seq_len, per_batch_mod):
    t = x2.shape[0]
    tm = POST_TM
    seqs_per_tile = tm // seq_len
    if per_batch_mod:
        mod_row = lambda i: 1 + i * seqs_per_tile
    else:
        mod_row = lambda i: 0
    row = lambda i: (i, 0)
    const = lambda i: (0, 0)
    in_specs = [pl.BlockSpec((tm, D_MODEL), row),
                pl.BlockSpec((tm, ATT_WIDTH), row),
                pl.BlockSpec((tm, S5_WIDTH), row),
                pl.BlockSpec((N_DIR, seqs_per_tile, M_WIDTH, seq_len), lambda i: (0, i, 0, 0)),
                pl.BlockSpec((tm, M_WIDTH), lambda i: (i, AUX_MO // M_WIDTH)),
                _mod_spec(layer, mod_row),
                _layered(layer, (1, D_MODEL), const),
                _layered(layer, (1, M_WIDTH), const),
                pl.BlockSpec((M_WIDTH, M_WIDTH), const),
                _layered(layer, (D_MODEL, D_MODEL), const)]
    return pl.pallas_call(
        _post_kernel,
        grid=(t // tm,), in_specs=in_specs,
        out_specs=[pl.BlockSpec((tm, D_MODEL), row), pl.BlockSpec((tm, D_MODEL), row)],
        out_shape=[jax.ShapeDtypeStruct((t, D_MODEL), F32), jax.ShapeDtypeStruct((t, D_MODEL), BF16)],
        compiler_params=_params("arbitrary"), name="post",
    )(x2, att, s5o, ht, proj, pp['mod'], pp['g2'], pp['m_ng'], pp['blkm'], pp['w_out'])


def _ffn_kernel(x1_ref, h2_ref, hprev_ref, hnext_ref, mod_ref, wg_ref, wu_ref, cw_ref, cb_ref, wd_ref, o_ref,
                gated_ref, *, seq_len):
    tm = h2_ref.shape[0]
    h2 = h2_ref[...]
    h2_ext = jnp.concatenate([h2, hprev_ref[...], hnext_ref[...]], axis=0)
    row = lax.broadcasted_iota(jnp.int32, (tm, FF_CHUNK), 0)
    pos = (pl.program_id(0) * tm + row) % seq_len
    seq_start = pos == 0
    seq_end = pos == seq_len - 1
    tile_start = row == 0
    tile_end = row == tm - 1
    for j in range(D_FF // FF_CHUNK):
        cols = slice(j * FF_CHUNK, (j + 1) * FF_CHUNK)
        a_ext = _dot(h2_ext, wg_ref[:, cols])
        a = a_ext[:tm]
        before = a_ext[tm + BF16_ROWS - 1:tm + BF16_ROWS, :]
        after = a_ext[tm + BF16_ROWS:tm + BF16_ROWS + 1, :]
        a_prev = jnp.where(tile_start, before, pltpu.roll(a, 1, axis=0))
        a_next = jnp.where(tile_end, after, pltpu.roll(a, tm - 1, axis=0))
        a_prev = jnp.where(seq_start, 0.0, a_prev)
        a_next = jnp.where(seq_end, 0.0, a_next)
        ac = a_prev * cw_ref[0:1, cols] + a * cw_ref[1:2, cols] + a_next * cw_ref[2:3, cols] + cb_ref[:, cols]
        up = _dot(h2, wu_ref[:, cols])
        gated_ref[:, cols] = _bf(ac * _sigmoid(ac) * up)
    o_ref[...] = x1_ref[...] + mod_ref[5:6, :] * _dot(gated_ref[...], wd_ref[...])


def _ffn_call(x1, h2, layer, pp, seq_len, per_batch_mod):
    t = x1.shape[0]
    tm = FFN_TM
    halo_blocks_per_tile = tm // BF16_ROWS
    last_halo_block = t // BF16_ROWS - 1
    if per_batch_mod:
        mod_row = lambda i: 1 + (i * tm) // seq_len
    else:
        mod_row = lambda i: 0
    row = lambda i: (i, 0)
    single = pl.Buffered(1)
    whole = lambda shape: _layered(layer, shape, lambda i: (0, 0), pipeline_mode=single)
    in_specs = [pl.BlockSpec((tm, D_MODEL), row),
                pl.BlockSpec((tm, D_MODEL), row),
                pl.BlockSpec((BF16_ROWS, D_MODEL), lambda i: (jnp.maximum(i * halo_blocks_per_tile - 1, 0), 0)),
                pl.BlockSpec((BF16_ROWS, D_MODEL),
                             lambda i: (jnp.minimum((i + 1) * halo_blocks_per_tile, last_halo_block), 0)),
                _mod_spec(layer, mod_row),
                whole((D_MODEL, D_FF)), whole((D_MODEL, D_FF)), whole((3, D_FF)), whole((1, D_FF)),
                whole((D_FF, D_MODEL))]
    return pl.pallas_call(
        functools.partial(_ffn_kernel, seq_len=seq_len),
        grid=(t // tm,), in_specs=in_specs,
        out_specs=pl.BlockSpec((tm, D_MODEL), row),
        out_shape=jax.ShapeDtypeStruct((t, D_MODEL), F32),
        scratch_shapes=[pltpu.VMEM((tm, D_FF), BF16)],
        compiler_params=_params("arbitrary"), name="ffn",
    )(x1, h2, h2, h2, pp['mod'], pp['ffn_w_gate'], pp['ffn_w_up'], pp['ffn_conv_w'], pp['ffn_conv_b'],
      pp['ffn_w_down'])


def _trunk_layer(x2, batch, seq_len, layer, pp, ctx, rope_tabs):
    latent = ctx is not None
    proj, mg, qn, kn, mqk, vt, mgt = _pre_call(x2, layer, pp, rope_tabs if latent else None, seq_len, latent)
    cache_k, cache_v, init_re, init_im, m_init = ctx if latent else (None,) * 5

    att = _attn_call(qn, kn, proj, layer, cache_k, cache_v, batch, seq_len)
    s5o, fin_re, fin_im = _s5_call(proj, layer, pp, init_re, init_im, batch, seq_len)
    ht, cn_f, m_f = _mlstm_call(mqk.reshape(batch, seq_len, 2 * M_WIDTH), vt, mg.reshape(batch, seq_len, LANES), mgt,
                                layer, pp, m_init, batch, seq_len)
    x1, h2 = _post_call(x2, att, s5o, ht, proj, layer, pp, seq_len, latent)
    x_out = _ffn_call(x1, h2, layer, pp, seq_len, latent)
    if latent:
        return x_out, None
    states = (kn.reshape(batch, seq_len, ATT_KV_HEADS, ATT_HD),
              proj[:, AUX_V:].reshape(batch, seq_len, ATT_KV_HEADS, ATT_HD),
              fin_re.transpose(1, 0, 2).reshape(batch, N_DIR, S5_GROUPS, S5_STATE),
              fin_im.transpose(1, 0, 2).reshape(batch, N_DIR, S5_GROUPS, S5_STATE),
              cn_f[:, :, :M_HD, :].reshape(batch, N_DIR, M_HEADS, M_HD, M_HD),
              cn_f[:, :, M_HD, :].reshape(batch, N_DIR, M_HEADS, M_HD),
              m_f[:, :, 0, 0].reshape(batch, N_DIR, M_HEADS))
    return x_out, states


def _head_block(width, head_dim):
    ids = np.arange(width) // head_dim
    return jnp.asarray((ids[:, None] == ids[None, :]).astype(np.float32) / head_dim, dtype=BF16)


def kernel(x_prompt, x_sample, c, cache_attn_k, cache_attn_v, state_s5_re, state_s5_im, state_mlstm_C, state_mlstm_n, state_mlstm_m, c_ctx, ada_w, ada_b, norm1_g, norm2_g, w_in, q_norm_g, k_norm_g, s5_a_re, s5_a_im, s5_log_dt, s5_b_re, s5_b_im, s5_c_re, s5_c_im, s5_d, s5_glu_w, s5_glu_b, m_gate_b, m_norm_g, w_out, ffn_w_gate, ffn_w_up, ffn_conv_w, ffn_conv_b, ffn_w_down):
    batch, seq = x_prompt.shape[0], x_prompt.shape[1]
    dec_batch, dec_seq = x_sample.shape[0], x_sample.shape[1]

    n_mod_rows = 2 * SUBLANES
    cvec = jnp.zeros((n_mod_rows, D_MODEL), F32).at[0].set(c_ctx).at[1:1 + dec_batch].set(c)
    mod_all = _ada_call(cvec, ada_w, ada_b).reshape(DEPTH, n_mod_rows, 6, D_MODEL)

    ab_re, ab_im, bb_re, bb_im = _s5_disc_call(s5_a_re, s5_a_im, s5_log_dt, s5_b_re, s5_b_im)
    ab_re = ab_re.reshape(DEPTH, N_DIR, 1, S5_FLAT)
    ab_im = ab_im.reshape(DEPTH, N_DIR, 1, S5_FLAT)
    bb_re = bb_re.reshape(DEPTH, N_DIR, S5_GROUPS, S5_STATE, S5_CH)
    bb_im = bb_im.reshape(DEPTH, N_DIR, S5_GROUPS, S5_STATE, S5_CH)

    rope_tabs = _rope_tables(dec_seq)
    tri = np.tril(np.ones((M_TILE, M_TILE), np.float32))
    tril = jnp.asarray(tri, dtype=BF16)
    triu = jnp.asarray(tri.T, dtype=BF16)
    ones = jnp.ones((M_TILE, M_TILE), BF16)
    blkq = _head_block(ATT_WIDTH, ATT_HD)
    blkk = _head_block(ATT_KV_WIDTH, ATT_HD)
    blkm = _head_block(M_WIDTH, M_HD)

    w_in_b = _bf(w_in)
    gb = m_gate_b.reshape(DEPTH, M_GATE_COLS)
    pp = dict(
        mod=mod_all, g1=norm1_g.reshape(DEPTH, 1, D_MODEL), g2=norm2_g.reshape(DEPTH, 1, D_MODEL),
        w_in=w_in_b,
        w_gatecols=jnp.zeros((DEPTH, D_MODEL, LANES), BF16).at[:, :, :M_GATE_COLS].set(w_in_b[:, :, PROJ_MAIN:]),
        blkq=blkq, blkk=blkk, blkm=blkm,
        qg=jnp.tile(q_norm_g, (1, ATT_HEADS)).reshape(DEPTH, 1, ATT_WIDTH),
        kg=jnp.tile(k_norm_g, (1, ATT_KV_HEADS)).reshape(DEPTH, 1, ATT_KV_WIDTH),
        s5_bmat=_bf(jnp.concatenate([_block_diag(jnp.swapaxes(bb_re, -1, -2)),
                                     _block_diag(jnp.swapaxes(bb_im, -1, -2))], axis=-1)),
        s5_cre=_bf(_block_diag(jnp.swapaxes(s5_c_re, -1, -2))),
        s5_cim=_bf(_block_diag(jnp.swapaxes(s5_c_im, -1, -2))),
        s5_ab_re=jnp.broadcast_to(ab_re, (DEPTH, N_DIR, SUBLANES, S5_FLAT)),
        s5_ab_im=jnp.broadcast_to(ab_im, (DEPTH, N_DIR, SUBLANES, S5_FLAT)),
        s5_d=s5_d.reshape(DEPTH, 1, S5_WIDTH), s5_glu_w=_bf(s5_glu_w), s5_glu_b=s5_glu_b.reshape(DEPTH, 1, S5_WIDTH),
        m_gb_row=jnp.zeros((DEPTH, 1, LANES), F32).at[:, 0, :M_GATE_COLS].set(gb),
        m_gb_col=gb.reshape(DEPTH, M_GATE_COLS, 1),
        m_ng=jnp.tile(m_norm_g, (1, M_HEADS)).reshape(DEPTH, 1, M_WIDTH), tril=tril, triu=triu, ones=ones,
        w_out=_bf(w_out), ffn_w_gate=_bf(ffn_w_gate), ffn_w_up=_bf(ffn_w_up),
        ffn_conv_w=ffn_conv_w, ffn_conv_b=ffn_conv_b.reshape(DEPTH, 1, D_FF), ffn_w_down=_bf(ffn_w_down))

    past = cache_attn_k.shape[2]
    n_pad = CN_ROWS - M_HD
    s5_init = lambda st: st.reshape(dec_batch, DEPTH, N_DIR, S5_FLAT).transpose(1, 2, 0, 3)
    m_init = (state_mlstm_C.reshape(dec_batch, DEPTH, M_CHAINS, M_HD, M_HD),
              jnp.zeros((dec_batch, DEPTH, M_CHAINS, n_pad, M_HD), F32).at[:, :, :, 0, :].set(
                  state_mlstm_n.reshape(dec_batch, DEPTH, M_CHAINS, M_HD)),
              jnp.broadcast_to(state_mlstm_m.reshape(dec_batch, DEPTH, M_CHAINS, 1, 1),
                               (dec_batch, DEPTH, M_CHAINS, 1, LANES)))
    ctx = (cache_attn_k.reshape(dec_batch, DEPTH, past, ATT_KV_WIDTH),
           cache_attn_v.reshape(dec_batch, DEPTH, past, ATT_KV_WIDTH),
           s5_init(state_s5_re), s5_init(state_s5_im), m_init)

    xp = x_prompt.reshape(batch * seq, D_MODEL)
    xs = x_sample.reshape(dec_batch * dec_seq, D_MODEL)
    ctx_out = []
    for l in range(DEPTH):
        xp, st = _trunk_layer(xp, batch, seq, l, pp, None, None)
        ctx_out.append(st)
        xs, _ = _trunk_layer(xs, dec_batch, dec_seq, l, pp, ctx, rope_tabs)
    outs = [jnp.stack([s[i] for s in ctx_out], axis=1) for i in range(7)]
    return (xp.reshape(batch, seq, D_MODEL), xs.reshape(dec_batch, dec_seq, D_MODEL), *outs)
```

```python
import functools

import numpy as np
import jax
import jax.numpy as jnp
from jax import lax
from jax.experimental import pallas as pl
from jax.experimental.pallas import tpu as pltpu

F32 = jnp.float32
BF16 = jnp.bfloat16

D_MODEL = 1024
DEPTH = 2
GRID_W = 64
N_DIR = 2
EPS = 1e-6
ATT_HD = 64
ATT_WIDTH = 512
ATT_HEADS = 8
ATT_KV_HEADS = 2
ATT_GROUP = ATT_HEADS // ATT_KV_HEADS
ATT_KV_WIDTH = ATT_KV_HEADS * ATT_HD
ROPE_THETA = 10000.0
LOG2_E = 1.4426950408889634
S5_CH = 16
S5_STATE = 64
S5_WIDTH = 256
S5_GROUPS = 16
S5_FLAT = S5_GROUPS * S5_STATE
M_HD = 64
M_WIDTH = 256
M_HEADS = 4
M_GATE_COLS = 2 * N_DIR * M_HEADS
M_CHAINS = N_DIR * M_HEADS
D_FF = 2816
PROJ_MAIN = 2048
COL_K = ATT_WIDTH
COL_V = COL_K + ATT_KV_WIDTH
COL_S5 = COL_V + ATT_KV_WIDTH
COL_MQ = COL_S5 + S5_WIDTH
COL_MK = COL_MQ + M_WIDTH
COL_MV = COL_MK + M_WIDTH
COL_MO = COL_MV + M_WIDTH
AUX_S5 = 0
AUX_MO = AUX_S5 + S5_WIDTH
AUX_V = AUX_MO + M_WIDTH
AUX_WIDTH = AUX_V + ATT_KV_WIDTH
LANES = 128
SUBLANES = 8
BF16_ROWS = 16
VMEM_LIMIT = 56 * 1024 * 1024

PRE_TM = 512
ATT_TQ = 512
S5_ROWS = 256
S5_SLOTS = 3
M_TILE = 128
M_BG = 4
M_BG_SEQ = 1024
CN_ROWS = M_HD + BF16_ROWS
POST_TM = 1024
POST_OUT_CHUNK = 256
FFN_TM = 1024
TAIL_HALF = 512
FF_CHUNK = 256


def _bf(x):
    return x.astype(BF16)


def _dot(a, b):
    return jnp.dot(a, b, preferred_element_type=F32)


def _split(x):
    hi = _bf(x)
    lo = _bf(x - hi.astype(F32))
    return hi, lo


def _seg_mean_sq(x, blk):
    hi, lo = _split(x * x)
    return _dot(hi, blk) + _dot(lo, blk)


def _sigmoid(x):
    return 1.0 / (1.0 + jnp.exp(-x))


def _log_sigmoid(x):
    return -(jnp.maximum(-x, 0.0) + jnp.log1p(jnp.exp(-jnp.abs(x))))


def _params(*sem):
    return pltpu.CompilerParams(dimension_semantics=sem, vmem_limit_bytes=VMEM_LIMIT)


def _ada_kernel(c_ref, w_ref, b_ref, o_ref):
    c = c_ref[...]
    s = c * _sigmoid(c)
    o_ref[...] = _dot(_bf(s), _bf(w_ref[...])) + b_ref[...]


def _ada_call(cvec, ada_w, ada_b):
    rows = cvec.shape[0]
    tn = 1024
    n = ada_w.shape[-1]
    return pl.pallas_call(
        _ada_kernel,
        grid=(DEPTH, n // tn),
        in_specs=[pl.BlockSpec((rows, D_MODEL), lambda l, j: (0, 0)),
                  pl.BlockSpec((None, D_MODEL, tn), lambda l, j: (l, 0, j)),
                  pl.BlockSpec((None, 1, tn), lambda l, j: (l, 0, j))],
        out_specs=pl.BlockSpec((None, rows, tn), lambda l, j: (l, 0, j)),
        out_shape=jax.ShapeDtypeStruct((DEPTH, rows, n), F32),
        compiler_params=_params("arbitrary", "arbitrary"),
        name="adaln",
    )(cvec, ada_w, ada_b.reshape(DEPTH, 1, n))


def _s5_disc_kernel(are_ref, aim_ref, ldt_ref, arex_ref, aimx_ref, bre_ref, bim_ref,
                    abr_ref, abi_ref, bbr_ref, bbi_ref):
    dt = jnp.exp(ldt_ref[...])

    def disc(a_re, a_im):
        mag = jnp.exp(dt * a_re)
        ab_re = mag * jnp.cos(dt * a_im)
        ab_im = mag * jnp.sin(dt * a_im)
        den = a_re * a_re + a_im * a_im
        nr = ab_re - 1.0
        ni = ab_im
        f_re = (nr * a_re + ni * a_im) / den
        f_im = (ni * a_re - nr * a_im) / den
        return ab_re, ab_im, f_re, f_im

    ab_re, ab_im, _, _ = disc(are_ref[...], aim_ref[...])
    abr_ref[...] = ab_re
    abi_ref[...] = ab_im
    _, _, f_re, f_im = disc(arex_ref[...], aimx_ref[...])
    b_re = bre_ref[...]
    b_im = bim_ref[...]
    bbr_ref[...] = f_re * b_re - f_im * b_im
    bbi_ref[...] = f_re * b_im + f_im * b_re


def _s5_disc_call(a_re, a_im, log_dt, b_re, b_im):
    r = DEPTH * N_DIR * S5_GROUPS
    a_re2 = a_re.reshape(r, S5_STATE)
    a_im2 = a_im.reshape(r, S5_STATE)
    wide = S5_STATE * S5_CH
    out_shape = (jax.ShapeDtypeStruct((r, S5_STATE), F32), jax.ShapeDtypeStruct((r, S5_STATE), F32),
                 jax.ShapeDtypeStruct((r, wide), F32), jax.ShapeDtypeStruct((r, wide), F32))
    return pl.pallas_call(_s5_disc_kernel, out_shape=out_shape, name="s5_disc")(
        a_re2, a_im2, log_dt.reshape(r, 1),
        jnp.repeat(a_re2, S5_CH, axis=1), jnp.repeat(a_im2, S5_CH, axis=1),
        b_re.reshape(r, wide), b_im.reshape(r, wide))


def _block_diag(blocks):
    g, r, c = blocks.shape[-3:]
    lead = blocks.shape[:-3]
    tiled = jnp.tile(blocks.reshape(lead + (g * r, c)), (1,) * len(lead) + (1, g))
    on_diagonal = (np.arange(g * r)[:, None] // r) == (np.arange(g * c)[None, :] // c)
    return jnp.where(on_diagonal, tiled, 0.0)


def _rope(x, cos, sin_signed, second):
    width = x.shape[-1]
    quarter = ATT_HD // 4
    partner = jnp.where(second, pltpu.roll(x, quarter, axis=1), pltpu.roll(x, width - quarter, axis=1))
    return x * cos + partner * sin_signed


def _pre_kernel(*refs, rope):
    if rope:
        (x_ref, mod_ref, g1_ref, w_ref, wg_ref, blkq_ref, blkk_ref, qg_ref, kg_ref,
         cosq_ref, sinq_ref, cosk_ref, sink_ref, proj_ref, mg_ref, qn_ref, kn_ref, mqk_ref, vt_ref, mgt_ref) = refs
    else:
        (x_ref, mod_ref, g1_ref, w_ref, wg_ref, blkq_ref, blkk_ref, qg_ref, kg_ref,
         proj_ref, mg_ref, qn_ref, kn_ref, mqk_ref, vt_ref, mgt_ref) = refs
    x = x_ref[...]
    ms = jnp.mean(x * x, axis=-1, keepdims=True)
    xn = x * lax.rsqrt(ms + EPS) * g1_ref[...]
    h = _bf(xn * (1.0 + mod_ref[1:2, :]) + mod_ref[0:1, :])
    def proj(c0, c1):
        return _dot(h, w_ref[:, c0:c1])

    q = proj(0, COL_K)
    kvs = proj(COL_K, COL_MQ)
    k = kvs[:, :ATT_KV_WIDTH]
    mqk = proj(COL_MQ, COL_MV)
    mvo = proj(COL_MV, PROJ_MAIN)
    proj_ref[:, AUX_S5:AUX_MO] = kvs[:, 2 * ATT_KV_WIDTH:]
    proj_ref[:, AUX_MO:AUX_V] = mvo[:, M_WIDTH:]
    proj_ref[:, AUX_V:] = kvs[:, ATT_KV_WIDTH:2 * ATT_KV_WIDTH]
    mg = _dot(h, wg_ref[...])
    mg_ref[...] = mg
    mqk_ref[:, :M_WIDTH] = _bf(mqk[:, :M_WIDTH])
    mqk_ref[:, M_WIDTH:] = _bf(mqk[:, M_WIDTH:] * (M_HD ** -0.5))
    vt_ref[...] = mvo[:, :M_WIDTH].T
    mgt_ref[...] = mg.T[:M_GATE_COLS, :]
    qn = q * lax.rsqrt(_seg_mean_sq(q, blkq_ref[...]) + EPS) * qg_ref[...]
    kn = k * lax.rsqrt(_seg_mean_sq(k, blkk_ref[...]) + EPS) * kg_ref[...]
    if rope:
        lane_q = lax.broadcasted_iota(jnp.int32, qn.shape, 1)
        lane_k = lax.broadcasted_iota(jnp.int32, kn.shape, 1)
        qn = _rope(qn, cosq_ref[...], sinq_ref[...], (lane_q & (ATT_HD // 4)) != 0)
        kn = _rope(kn, cosk_ref[...], sink_ref[...], (lane_k & (ATT_HD // 4)) != 0)
    qn_ref[...] = qn
    kn_ref[...] = kn


def _rope_tables(seq_len):
    n_rows = seq_len // GRID_W
    row = np.repeat(np.arange(n_rows), GRID_W)
    col = np.tile(np.arange(GRID_W), n_rows)
    half = ATT_HD // 2
    inv_freq = (1.0 / (np.float32(ROPE_THETA) ** (np.arange(0, half, 2, dtype=np.float32) / half))).astype(np.float32)

    def tables(pos):
        ang = pos.astype(np.float32)[:, None] * inv_freq[None, :]
        cos = np.cos(ang)
        sin = np.sin(ang)
        return np.concatenate([cos, cos], axis=-1), np.concatenate([-sin, sin], axis=-1)

    cr, sr = tables(row)
    cc, sc = tables(col)
    cos = np.concatenate([cr, cc], axis=-1)
    sin = np.concatenate([sr, sc], axis=-1)
    return tuple(jnp.asarray(np.tile(t, (1, n)), dtype=F32)
                 for t, n in ((cos, ATT_HEADS), (sin, ATT_HEADS), (cos, ATT_KV_HEADS), (sin, ATT_KV_HEADS)))


def _layered(layer, block, index_map, **kw):
    return pl.BlockSpec((None,) + tuple(block), lambda *g: (layer,) + tuple(index_map(*g)), **kw)


def _mod_spec(layer, row_of):
    return pl.BlockSpec((None, None, 6, D_MODEL), lambda *g: (layer, row_of(*g), 0, 0))


def _pre_call(x2, layer, pp, rope_tabs, seq_len, per_batch_mod):
    t = x2.shape[0]
    tm = min(PRE_TM, seq_len)
    tiles_per_seq = seq_len // tm
    rope = rope_tabs is not None
    if per_batch_mod:
        mod_row = lambda i: 1 + i // tiles_per_seq
    else:
        mod_row = lambda i: 0
    const = lambda i: (0, 0)
    row = lambda i: (i, 0)
    in_specs = [pl.BlockSpec((tm, D_MODEL), row),
                _mod_spec(layer, mod_row),
                _layered(layer, (1, D_MODEL), const),
                _layered(layer, (D_MODEL, PROJ_MAIN), const),
                _layered(layer, (D_MODEL, LANES), const),
                pl.BlockSpec((ATT_WIDTH, ATT_WIDTH), const),
                pl.BlockSpec((ATT_KV_WIDTH, ATT_KV_WIDTH), const),
                _layered(layer, (1, ATT_WIDTH), const),
                _layered(layer, (1, ATT_KV_WIDTH), const)]
    args = [x2, pp['mod'], pp['g1'], pp['w_in'], pp['w_gatecols'], pp['blkq'], pp['blkk'], pp['qg'], pp['kg']]
    if rope:
        pos_map = lambda i: (i % tiles_per_seq, 0)
        in_specs += [pl.BlockSpec((tm, ATT_WIDTH), pos_map), pl.BlockSpec((tm, ATT_WIDTH), pos_map),
                     pl.BlockSpec((tm, ATT_KV_WIDTH), pos_map), pl.BlockSpec((tm, ATT_KV_WIDTH), pos_map)]
        args += list(rope_tabs)
    time_on_lanes = lambda i: (i // tiles_per_seq, 0, i % tiles_per_seq)
    batch = t // seq_len
    out_specs = [pl.BlockSpec((tm, AUX_WIDTH), row),
                 pl.BlockSpec((tm, LANES), row),
                 pl.BlockSpec((tm, ATT_WIDTH), row),
                 pl.BlockSpec((tm, ATT_KV_WIDTH), row),
                 pl.BlockSpec((tm, 2 * M_WIDTH), row),
                 pl.BlockSpec((None, M_WIDTH, tm), time_on_lanes),
                 pl.BlockSpec((None, M_GATE_COLS, tm), time_on_lanes)]
    out_shape = [jax.ShapeDtypeStruct((t, AUX_WIDTH), F32), jax.ShapeDtypeStruct((t, LANES), F32),
                 jax.ShapeDtypeStruct((t, ATT_WIDTH), F32), jax.ShapeDtypeStruct((t, ATT_KV_WIDTH), F32),
                 jax.ShapeDtypeStruct((t, 2 * M_WIDTH), BF16),
                 jax.ShapeDtypeStruct((batch, M_WIDTH, seq_len), F32),
                 jax.ShapeDtypeStruct((batch, M_GATE_COLS, seq_len), F32)]
    return pl.pallas_call(
        functools.partial(_pre_kernel, rope=rope),
        grid=(t // tm,), in_specs=in_specs, out_specs=out_specs, out_shape=out_shape,
        compiler_params=_params("arbitrary"), name="pre_rope" if rope else "pre",
    )(*args)


def _attn_kernel(*refs, cached):
    if cached:
        q_ref, k_ref, v_ref, ck_ref, cv_ref, o_ref = refs
        k_all = _bf(jnp.concatenate([ck_ref[...], k_ref[...]], axis=0))
        v_all = _bf(jnp.concatenate([cv_ref[...], v_ref[...]], axis=0))
    else:
        q_ref, k_ref, v_ref, o_ref = refs
        k_all = _bf(k_ref[...])
        v_all = _bf(v_ref[...])
    scale = ATT_HD ** -0.5 * LOG2_E
    outs = []
    for kvh in range(ATT_KV_HEADS):
        lo = kvh * ATT_HD
        kb = k_all[:, lo:lo + ATT_HD]
        vb = v_all[:, lo:lo + ATT_HD]
        for g in range(ATT_GROUP):
            c0 = (kvh * ATT_GROUP + g) * ATT_HD
            qb = _bf(q_ref[:, c0:c0 + ATT_HD] * scale)
            s = lax.dot_general(qb, kb, (((1,), (1,)), ((), ())), preferred_element_type=F32)
            m = jnp.max(s, axis=-1, keepdims=True)
            e = jnp.exp2(s - m)
            den = jnp.sum(e, axis=-1, keepdims=True)
            outs.append(_dot(_bf(e), vb) / den)
    o_ref[...] = _bf(jnp.concatenate(outs, axis=-1))


def _attn_call(qn, kn, proj, layer, cache_k, cache_v, batch, lq):
    tq = min(ATT_TQ, lq)
    nq = lq // tq
    cached = cache_k is not None
    in_specs = [pl.BlockSpec((tq, ATT_WIDTH), lambda b, i: (b * nq + i, 0)),
                pl.BlockSpec((lq, ATT_KV_WIDTH), lambda b, i: (b, 0)),
                pl.BlockSpec((lq, ATT_KV_WIDTH), lambda b, i: (b, AUX_V // ATT_KV_WIDTH))]
    args = [qn, kn, proj]
    if cached:
        past = cache_k.shape[2]
        cache_spec = pl.BlockSpec((None, None, past, ATT_KV_WIDTH), lambda b, i: (b, layer, 0, 0))
        in_specs += [cache_spec, cache_spec]
        args += [cache_k, cache_v]
    return pl.pallas_call(
        functools.partial(_attn_kernel, cached=cached),
        grid=(batch, nq), in_specs=in_specs,
        out_specs=pl.BlockSpec((tq, ATT_WIDTH), lambda b, i: (b * nq + i, 0)),
        out_shape=jax.ShapeDtypeStruct((batch * lq, ATT_WIDTH), BF16),
        compiler_params=_params("arbitrary", "arbitrary"), name="attn_cached" if cached else "attn",
    )(*args)


def _s5_kernel(*refs, batch, seq_len, zero_init):
    if zero_init:
        (u_ref, bmat_ref, cre_ref, cim_ref, ar_ref, ai_ref, d_ref, gw_ref, gb_ref,
         out_ref, fr_ref, fi_ref, utb_ref, ytb_ref, xs_ref) = refs
        ir_ref = ii_ref = None
    else:
        (u_ref, bmat_ref, cre_ref, cim_ref, ar_ref, ai_ref, d_ref, gw_ref, gb_ref, ir_ref, ii_ref,
         out_ref, fr_ref, fi_ref, utb_ref, ytb_ref, xs_ref) = refs
    rows = S5_ROWS
    tc = rows // batch
    n_chunks = seq_len // tc
    n_sub = batch // SUBLANES
    halves = S5_WIDTH // LANES

    def aligned(x, m):
        return x if isinstance(x, int) else pl.multiple_of(x, m)

    for b in range(batch):
        for hf in range(halves):
            utb_ref[hf, pl.ds(b, seq_len, stride=batch), :] = (
                u_ref[b * seq_len:(b + 1) * seq_len, hf * LANES:(hf + 1) * LANES])
    for hf in range(halves):
        ytb_ref[hf] = utb_ref[hf] * d_ref[:, hf * LANES:(hf + 1) * LANES]

    def chunk_of(d, i):
        return i if d == 0 else n_chunks - 1 - i

    def stage_in(i, slot):
        for d in range(N_DIR):
            r0 = aligned(chunk_of(d, i) * rows, rows)
            u_c = jnp.concatenate([utb_ref[hf, pl.ds(r0, rows), :] for hf in range(halves)], axis=-1)
            xs_ref[slot, d] = _dot(_bf(u_c), bmat_ref[d])

    def stage_scan(slot, carry):
        new_carry = []
        for d in range(N_DIR):
            a_re = ar_ref[d]
            a_im = ai_ref[d]
            per_sub = []
            for sub in range(n_sub):
                s_re, s_im = carry[d][sub]
                for t in range(tc):
                    r = (t if d == 0 else tc - 1 - t) * batch + sub * SUBLANES
                    x_re = xs_ref[slot, d, r:r + SUBLANES, :S5_FLAT]
                    x_im = xs_ref[slot, d, r:r + SUBLANES, S5_FLAT:]
                    s_re, s_im = (a_re * s_re - a_im * s_im + x_re, a_re * s_im + a_im * s_re + x_im)
                    xs_ref[slot, d, r:r + SUBLANES, :S5_FLAT] = s_re
                    xs_ref[slot, d, r:r + SUBLANES, S5_FLAT:] = s_im
                per_sub.append((s_re, s_im))
            new_carry.append(tuple(per_sub))
        return tuple(new_carry)

    def stage_out(i, slot):
        for d in range(N_DIR):
            r0 = aligned(chunk_of(d, i) * rows, rows)
            y = (_dot(_bf(xs_ref[slot, d, :, :S5_FLAT]), cre_ref[d])
                 - _dot(_bf(xs_ref[slot, d, :, S5_FLAT:]), cim_ref[d]))
            for hf in range(halves):
                ytb_ref[hf, pl.ds(r0, rows), :] += y[:, hf * LANES:(hf + 1) * LANES]

    def start_state(ref, d, sub):
        if zero_init:
            return jnp.zeros((SUBLANES, S5_FLAT), F32)
        return ref[d, sub * SUBLANES:(sub + 1) * SUBLANES, :]

    carry = tuple(tuple((start_state(ir_ref, d, sub), start_state(ii_ref, d, sub)) for sub in range(n_sub))
                  for d in range(N_DIR))
    def step(i, phase, carry):
        stage_in(i + 1, (phase + 1) % S5_SLOTS)
        carry = stage_scan(phase, carry)
        stage_out(i - 1, (phase - 1) % S5_SLOTS)
        return carry

    stage_in(0, 0)
    stage_in(1, 1)
    carry = stage_scan(0, carry)
    n_steady = n_chunks - 2
    n_peeled = n_steady % S5_SLOTS
    for i in range(1, 1 + n_peeled):
        carry = step(i, i % S5_SLOTS, carry)
    first = 1 + n_peeled

    def body(g, carry):
        for k in range(S5_SLOTS):
            carry = step(first + g * S5_SLOTS + k, (first + k) % S5_SLOTS, carry)
        return carry

    carry = lax.fori_loop(0, n_steady // S5_SLOTS, body, carry)
    last = n_chunks - 1
    carry = stage_scan(last % S5_SLOTS, carry)
    stage_out(last - 1, (last - 1) % S5_SLOTS)
    stage_out(last, last % S5_SLOTS)
    for d in range(N_DIR):
        for sub in range(n_sub):
            fr_ref[d, sub * SUBLANES:(sub + 1) * SUBLANES, :] = carry[d][sub][0]
            fi_ref[d, sub * SUBLANES:(sub + 1) * SUBLANES, :] = carry[d][sub][1]

    for b in range(batch):
        z = jax.nn.gelu(jnp.concatenate(
            [ytb_ref[hf, pl.ds(b, seq_len, stride=batch), :] for hf in range(halves)], axis=-1))
        gate = _sigmoid(_dot(_bf(z), gw_ref[...]) + gb_ref[...])
        out_ref[b * seq_len:(b + 1) * seq_len, :] = _bf(z * gate)


def _s5_call(proj, layer, pp, init_re, init_im, batch, seq_len):
    rows = seq_len * batch
    single = pl.Buffered(1)
    zero_init = init_re is None
    names = ['s5_bmat', 's5_cre', 's5_cim', 's5_ab_re', 's5_ab_im', 's5_d', 's5_glu_w', 's5_glu_b']
    args = [pp[n] for n in names] + ([] if zero_init else [init_re, init_im])
    whole = lambda a: _layered(layer, a.shape[1:], lambda i: (0,) * (a.ndim - 1), pipeline_mode=single)
    in_specs = ([pl.BlockSpec((rows, S5_WIDTH), lambda i: (0, AUX_S5 // S5_WIDTH), pipeline_mode=single)]
                + [whole(a) for a in args])
    fin_spec = pl.BlockSpec((N_DIR, batch, S5_FLAT), lambda i: (0, 0, 0))
    out_specs = [pl.BlockSpec((rows, S5_WIDTH), lambda i: (0, 0), pipeline_mode=single), fin_spec, fin_spec]
    out_shape = (jax.ShapeDtypeStruct((rows, S5_WIDTH), BF16),
                 jax.ShapeDtypeStruct((N_DIR, batch, S5_FLAT), F32),
                 jax.ShapeDtypeStruct((N_DIR, batch, S5_FLAT), F32))
    return pl.pallas_call(
        functools.partial(_s5_kernel, batch=batch, seq_len=seq_len, zero_init=zero_init),
        grid=(1,), in_specs=in_specs, out_specs=out_specs, out_shape=out_shape,
        scratch_shapes=[pltpu.VMEM((S5_WIDTH // LANES, rows, LANES), F32),
                        pltpu.VMEM((S5_WIDTH // LANES, rows, LANES), F32),
                        pltpu.VMEM((S5_SLOTS, N_DIR, S5_ROWS, 2 * S5_FLAT), F32)],
        compiler_params=_params("arbitrary"), name="s5",
    )(proj, *args)


def _mlstm_kernel(*refs, seq_len, zero_init):
    if zero_init:
        (q_ref, k_ref, vt_ref, mg_ref, mgt_ref, gbr_ref, gbc_ref, tril_ref, triu_ref, ones_ref,
         ht_ref, cn_ref, m_ref) = refs
    else:
        (q_ref, k_ref, vt_ref, mg_ref, mgt_ref, gbr_ref, gbc_ref, tril_ref, triu_ref, ones_ref,
         c0_ref, n0_ref, m0_ref, ht_ref, cn_ref, m_ref) = refs
    _mlstm_body(q_ref, k_ref, vt_ref, mg_ref, mgt_ref, gbr_ref, gbc_ref, tril_ref, triu_ref, ones_ref,
                None if zero_init else (c0_ref, n0_ref, m0_ref), ht_ref, cn_ref, m_ref, seq_len)


def _mlstm_body(q_ref, k_ref, vt_ref, mg_ref, mgt_ref, gbr_ref, gbc_ref, tril_ref, triu_ref, ones_ref,
                init_refs, ht_ref, cn_ref, m_ref, seq_len):
    tile = M_TILE
    n_chunks = seq_len // tile
    bg = q_ref.shape[0]
    if init_refs is None:
        cn_ref[...] = jnp.zeros_like(cn_ref)
        m_ref[...] = jnp.zeros_like(m_ref)
    else:
        c0_ref, n0_ref, m0_ref = init_refs
        cn_ref[:, :, :M_HD, :] = c0_ref[...]
        cn_ref[:, :, M_HD:, :] = n0_ref[...]
        m_ref[...] = m0_ref[...]
    s_ids = lax.broadcasted_iota(jnp.int32, (tile, tile), 0)
    t_ids = lax.broadcasted_iota(jnp.int32, (tile, tile), 1)
    first_row = lax.broadcasted_iota(jnp.int32, (CN_ROWS - M_HD, tile), 0) == 0
    neg_inf = jnp.float32(-jnp.inf)
    ones = ones_ref[...]

    def direction(d, c):
        t0 = pl.multiple_of(c * tile, tile)
        col_mat = tril_ref[...] if d == 0 else triu_ref[...]
        row_mat = triu_ref[...] if d == 0 else tril_ref[...]
        valid = (s_ids <= t_ids) if d == 0 else (s_ids >= t_ids)
        gt = mgt_ref[:, :, pl.ds(t0, tile)] + gbc_ref[...]
        lft_hi, lft_lo = _split(_log_sigmoid(gt).reshape(bg * M_GATE_COLS, tile))
        b_rows = (_dot(lft_hi, row_mat) + _dot(lft_lo, row_mat)).reshape(bg, M_GATE_COLS, tile)
        totals = (_dot(lft_hi, ones) + _dot(lft_lo, ones)).reshape(bg, M_GATE_COLS, tile)
        gc = mg_ref[:, pl.ds(t0, tile), :] + gbr_ref[...]
        lfc_hi, lfc_lo = _split(_log_sigmoid(gc))
        col_b = jnp.broadcast_to(col_mat, (bg, tile, tile))
        b_cols = (jnp.einsum('bts,bsl->btl', col_b, lfc_hi, preferred_element_type=F32)
                  + jnp.einsum('bts,bsl->btl', col_b, lfc_lo, preferred_element_type=F32))
        for h in range(M_HEADS):
            j = d * M_HEADS + h
            fj = M_CHAINS + j
            rows = slice(h * M_HD, (h + 1) * M_HD)
            q_c = q_ref[:, pl.ds(t0, tile), rows]
            k_c = k_ref[:, pl.ds(t0, tile), rows]
            vt_c = vt_ref[:, rows, pl.ds(t0, tile)]
            b_t = b_rows[:, fj:fj + 1, :]
            li_t = gt[:, j:j + 1, :]
            tot = totals[:, fj:fj + 1, :]
            m_prev = m_ref[:, j]
            cn_prev = cn_ref[:, j]

            c_col = gc[:, :, j:j + 1] - b_cols[:, :, fj:fj + 1]
            cm = jnp.where(valid, c_col, neg_inf)
            inter = b_t + m_prev
            m_t = jnp.maximum(inter, b_t + jnp.max(cm, axis=1, keepdims=True))
            w_t = jnp.exp(cm + (b_t - m_t))
            s_inter = jnp.exp(inter - m_t)
            sc_t = jnp.einsum('bse,bte->bst', k_c, q_c, preferred_element_type=F32) * w_t
            num = jnp.einsum('bds,bst->bdt', _bf(vt_c), _bf(sc_t), preferred_element_type=F32)
            ext = jnp.einsum('bre,bte->brt', _bf(cn_prev), q_c, preferred_element_type=F32)
            num = num + s_inter * ext[:, :M_HD, :]
            den = jnp.sum(sc_t, axis=1, keepdims=True) + s_inter * ext[:, M_HD:M_HD + 1, :]
            ht_ref[d, :, rows, pl.ds(t0, tile)] = num / jnp.maximum(jnp.abs(den), jnp.exp(-m_t))

            g_row = tot - b_t + li_t
            m_new = jnp.maximum(tot + m_prev, jnp.max(g_row, axis=-1, keepdims=True))
            wk = jnp.exp(g_row - m_new)
            decay = jnp.exp(tot + m_prev - m_new)
            vw = jnp.concatenate([vt_c * wk, jnp.where(first_row, wk, 0.0)], axis=1)
            cn_ref[:, j] = (decay[:, :, :M_HD] * cn_prev
                            + jnp.einsum('brs,bse->bre', _bf(vw), k_c, preferred_element_type=F32))
            m_ref[:, j] = m_new

    def body(i, _):
        direction(0, i)
        direction(1, n_chunks - 1 - i)
        return 0

    lax.fori_loop(0, n_chunks, body, 0)


def _mlstm_call(mqk3, vt, mg3, mgt, layer, pp, init, batch, seq_len):
    bg = min(batch // 2, M_BG * M_BG_SEQ // seq_len)
    const2 = lambda g: (0, 0)
    lead3 = lambda g: (g, 0, 0)
    lead4 = lambda g: (g, 0, 0, 0)
    in_specs = [pl.BlockSpec((bg, seq_len, M_WIDTH), lead3),
                pl.BlockSpec((bg, seq_len, M_WIDTH), lambda g: (g, 0, 1)),
                pl.BlockSpec((bg, M_WIDTH, seq_len), lead3),
                pl.BlockSpec((bg, seq_len, LANES), lead3),
                pl.BlockSpec((bg, M_GATE_COLS, seq_len), lead3),
                _layered(layer, (1, LANES), const2),
                _layered(layer, (M_GATE_COLS, 1), const2),
                pl.BlockSpec((M_TILE, M_TILE), const2),
                pl.BlockSpec((M_TILE, M_TILE), const2),
                pl.BlockSpec((M_TILE, M_TILE), const2)]
    args = [mqk3, mqk3, vt, mg3, mgt, pp['m_gb_row'], pp['m_gb_col'], pp['tril'], pp['triu'], pp['ones']]
    if init is not None:
        state_spec = lambda *tail: pl.BlockSpec((bg, None, M_CHAINS) + tail, lambda g: (g, layer, 0, 0, 0))
        in_specs += [state_spec(M_HD, M_HD), state_spec(CN_ROWS - M_HD, M_HD), state_spec(1, LANES)]
        args += list(init)
    out_specs = [pl.BlockSpec((N_DIR, bg, M_WIDTH, seq_len), lambda g: (0, g, 0, 0)),
                 pl.BlockSpec((bg, M_CHAINS, CN_ROWS, M_HD), lead4),
                 pl.BlockSpec((bg, M_CHAINS, 1, LANES), lead4)]
    out_shape = [jax.ShapeDtypeStruct((N_DIR, batch, M_WIDTH, seq_len), F32),
                 jax.ShapeDtypeStruct((batch, M_CHAINS, CN_ROWS, M_HD), F32),
                 jax.ShapeDtypeStruct((batch, M_CHAINS, 1, LANES), F32)]
    return pl.pallas_call(
        functools.partial(_mlstm_kernel, seq_len=seq_len, zero_init=init is None),
        grid=(batch // bg,), in_specs=in_specs, out_specs=out_specs, out_shape=out_shape,
        compiler_params=_params("arbitrary"), name="mlstm",
    )(*args)


def _post_kernel(x_ref, att_ref, s5_ref, ht_ref, mo_ref, mod_ref, g2_ref, ng_ref, blkm_ref, wo_ref,
                 x1_ref, h2_ref):
    mh = jnp.concatenate([(ht_ref[0, s] + ht_ref[1, s]).T for s in range(ht_ref.shape[1])], axis=0)
    ml = mh * lax.rsqrt(_seg_mean_sq(mh, blkm_ref[...]) + EPS) * ng_ref[...] * _sigmoid(mo_ref[...])
    mixers = jnp.concatenate([att_ref[...], s5_ref[...], _bf(ml)], axis=-1)
    for n0 in range(0, D_MODEL, POST_OUT_CHUNK):
        cols = slice(n0, n0 + POST_OUT_CHUNK)
        x1_ref[:, cols] = x_ref[:, cols] + mod_ref[2:3, cols] * _dot(mixers, wo_ref[:, cols])
    x1 = x1_ref[...]
    ms = jnp.mean(x1 * x1, axis=-1, keepdims=True)
    xn = x1 * lax.rsqrt(ms + EPS) * g2_ref[...]
    h2_ref[...] = _bf(xn * (1.0 + mod_ref[4:5, :]) + mod_ref[3:4, :])


def _post_call(x2, att, s5o, ht, proj, layer, pp, seq_len, per_batch_mod):
    t = x2.shape[0]
    tm = POST_TM
    seqs_per_tile = tm // seq_len
    if per_batch_mod:
        mod_row = lambda i: 1 + i * seqs_per_tile
    else:
        mod_row = lambda i: 0
    row = lambda i: (i, 0)
    const = lambda i: (0, 0)
    in_specs = [pl.BlockSpec((tm, D_MODEL), row),
                pl.BlockSpec((tm, ATT_WIDTH), row),
                pl.BlockSpec((tm, S5_WIDTH), row),
                pl.BlockSpec((N_DIR, seqs_per_tile, M_WIDTH, seq_len), lambda i: (0, i, 0, 0)),
                pl.BlockSpec((tm, M_WIDTH), lambda i: (i, AUX_MO // M_WIDTH)),
                _mod_spec(layer, mod_row),
                _layered(layer, (1, D_MODEL), const),
                _layered(layer, (1, M_WIDTH), const),
                pl.BlockSpec((M_WIDTH, M_WIDTH), const),
                _layered(layer, (D_MODEL, D_MODEL), const)]
    return pl.pallas_call(
        _post_kernel,
        grid=(t // tm,), in_specs=in_specs,
        out_specs=[pl.BlockSpec((tm, D_MODEL), row), pl.BlockSpec((tm, D_MODEL), row)],
        out_shape=[jax.ShapeDtypeStruct((t, D_MODEL), F32), jax.ShapeDtypeStruct((t, D_MODEL), BF16)],
        compiler_params=_params("arbitrary"), name="post",
    )(x2, att, s5o, ht, proj, pp['mod'], pp['g2'], pp['m_ng'], pp['blkm'], pp['w_out'])


def _ffn_kernel(x1_ref, h2_ref, hprev_ref, hnext_ref, mod_ref, wg_ref, wu_ref, cw_ref, cb_ref, wd_ref, o_ref,
                gated_ref, *, seq_len):
    tm = h2_ref.shape[0]
    h2 = h2_ref[...]
    h2_ext = jnp.concatenate([h2, hprev_ref[...], hnext_ref[...]], axis=0)
    row = lax.broadcasted_iota(jnp.int32, (tm, FF_CHUNK), 0)
    pos = (pl.program_id(0) * tm + row) % seq_len
    seq_start = pos == 0
    seq_end = pos == seq_len - 1
    tile_start = row == 0
    tile_end = row == tm - 1
    for j in range(D_FF // FF_CHUNK):
        cols = slice(j * FF_CHUNK, (j + 1) * FF_CHUNK)
        a_ext = _dot(h2_ext, wg_ref[:, cols])
        a = a_ext[:tm]
        before = a_ext[tm + BF16_ROWS - 1:tm + BF16_ROWS, :]
        after = a_ext[tm + BF16_ROWS:tm + BF16_ROWS + 1, :]
        a_prev = jnp.where(tile_start, before, pltpu.roll(a, 1, axis=0))
        a_next = jnp.where(tile_end, after, pltpu.roll(a, tm - 1, axis=0))
        a_prev = jnp.where(seq_start, 0.0, a_prev)
        a_next = jnp.where(seq_end, 0.0, a_next)
        ac = a_prev * cw_ref[0:1, cols] + a * cw_ref[1:2, cols] + a_next * cw_ref[2:3, cols] + cb_ref[:, cols]
        up = _dot(h2, wu_ref[:, cols])
        gated_ref[:, cols] = _bf(ac * _sigmoid(ac) * up)
    o_ref[...] = x1_ref[...] + mod_ref[5:6, :] * _dot(gated_ref[...], wd_ref[...])


def _ffn_call(x1, h2, layer, pp, seq_len, per_batch_mod):
    t = x1.shape[0]
    tm = FFN_TM
    halo_blocks_per_tile = tm // BF16_ROWS
    last_halo_block = t // BF16_ROWS - 1
    if per_batch_mod:
        mod_row = lambda i: 1 + (i * tm) // seq_len
    else:
        mod_row = lambda i: 0
    row = lambda i: (i, 0)
    single = pl.Buffered(1)
    whole = lambda shape: _layered(layer, shape, lambda i: (0, 0), pipeline_mode=single)
    in_specs = [pl.BlockSpec((tm, D_MODEL), row),
                pl.BlockSpec((tm, D_MODEL), row),
                pl.BlockSpec((BF16_ROWS, D_MODEL), lambda i: (jnp.maximum(i * halo_blocks_per_tile - 1, 0), 0)),
                pl.BlockSpec((BF16_ROWS, D_MODEL),
                             lambda i: (jnp.minimum((i + 1) * halo_blocks_per_tile, last_halo_block), 0)),
                _mod_spec(layer, mod_row),
                whole((D_MODEL, D_FF)), whole((D_MODEL, D_FF)), whole((3, D_FF)), whole((1, D_FF)),
                whole((D_FF, D_MODEL))]
    return pl.pallas_call(
        functools.partial(_ffn_kernel, seq_len=seq_len),
        grid=(t // tm,), in_specs=in_specs,
        out_specs=pl.BlockSpec((tm, D_MODEL), row),
        out_shape=jax.ShapeDtypeStruct((t, D_MODEL), F32),
        scratch_shapes=[pltpu.VMEM((tm, D_FF), BF16)],
        compiler_params=_params("arbitrary"), name="ffn",
    )(x1, h2, h2, h2, pp['mod'], pp['ffn_w_gate'], pp['ffn_w_up'], pp['ffn_conv_w'], pp['ffn_conv_b'],
      pp['ffn_w_down'])


def _tail_kernel(x_ref, att_ref, s5_ref, ht_ref, mo_ref, mod_ref, g2_ref, ng_ref, blkm_ref, wo_ref,
                 wg_ref, wu_ref, cw_ref, cb_ref, wd_ref, o_ref, h2_ref, gated_ref, *, seq_len):
    tm = x_ref.shape[0]
    mh = jnp.concatenate([(ht_ref[0, s] + ht_ref[1, s]).T for s in range(ht_ref.shape[1])], axis=0)
    ml = mh * lax.rsqrt(_seg_mean_sq(mh, blkm_ref[...]) + EPS) * ng_ref[...] * _sigmoid(mo_ref[...])
    mixers = jnp.concatenate([att_ref[...], s5_ref[...], _bf(ml)], axis=-1)
    for n0 in range(0, D_MODEL, POST_OUT_CHUNK):
        cols = slice(n0, n0 + POST_OUT_CHUNK)
        o_ref[:, cols] = x_ref[:, cols] + mod_ref[2:3, cols] * _dot(mixers, wo_ref[:, cols])
    x1 = o_ref[...]
    ms = jnp.mean(x1 * x1, axis=-1, keepdims=True)
    xn = x1 * lax.rsqrt(ms + EPS) * g2_ref[...]
    h2_ref[...] = _bf(xn * (1.0 + mod_ref[4:5, :]) + mod_ref[3:4, :])

    half = TAIL_HALF
    row = lax.broadcasted_iota(jnp.int32, (half, FF_CHUNK), 0)
    half_start = row == 0
    half_end = row == half - 1
    for r0 in range(0, tm, half):
        pos = (r0 + row) % seq_len
        seq_start = pos == 0
        seq_end = pos == seq_len - 1
        h2 = h2_ref[r0:r0 + half, :]
        p0 = max(r0 - BF16_ROWS, 0)
        n0 = min(r0 + half, tm - BF16_ROWS)
        h2_ext = jnp.concatenate([h2, h2_ref[p0:p0 + BF16_ROWS, :], h2_ref[n0:n0 + BF16_ROWS, :]], axis=0)
        for j in range(D_FF // FF_CHUNK):
            cols = slice(j * FF_CHUNK, (j + 1) * FF_CHUNK)
            a_ext = _dot(h2_ext, wg_ref[:, cols])
            a = a_ext[:half]
            before = a_ext[half + BF16_ROWS - 1:half + BF16_ROWS, :]
            after = a_ext[half + BF16_ROWS:half + BF16_ROWS + 1, :]
            a_prev = jnp.where(half_start, before, pltpu.roll(a, 1, axis=0))
            a_next = jnp.where(half_end, after, pltpu.roll(a, half - 1, axis=0))
            a_prev = jnp.where(seq_start, 0.0, a_prev)
            a_next = jnp.where(seq_end, 0.0, a_next)
            ac = a_prev * cw_ref[0:1, cols] + a * cw_ref[1:2, cols] + a_next * cw_ref[2:3, cols] + cb_ref[:, cols]
            up = _dot(h2, wu_ref[:, cols])
            gated_ref[:, cols] = _bf(ac * _sigmoid(ac) * up)
        o_ref[r0:r0 + half, :] = o_ref[r0:r0 + half, :] + mod_ref[5:6, :] * _dot(gated_ref[...], wd_ref[...])


def _tail_call(x2, att, s5o, ht, proj, layer, pp, seq_len, per_batch_mod):
    t = x2.shape[0]
    tm = POST_TM
    seqs_per_tile = tm // seq_len
    if per_batch_mod:
        mod_row = lambda i: 1 + i * seqs_per_tile
    else:
        mod_row = lambda i: 0
    row = lambda i: (i, 0)
    const = lambda i: (0, 0)
    single = pl.Buffered(1)
    whole = lambda shape: _layered(layer, shape, const, pipeline_mode=single)
    in_specs = [pl.BlockSpec((tm, D_MODEL), row),
                pl.BlockSpec((tm, ATT_WIDTH), row),
                pl.BlockSpec((tm, S5_WIDTH), row),
                pl.BlockSpec((N_DIR, seqs_per_tile, M_WIDTH, seq_len), lambda i: (0, i, 0, 0)),
                pl.BlockSpec((tm, M_WIDTH), lambda i: (i, AUX_MO // M_WIDTH)),
                _mod_spec(layer, mod_row),
                _layered(layer, (1, D_MODEL), const),
                _layered(layer, (1, M_WIDTH), const),
                pl.BlockSpec((M_WIDTH, M_WIDTH), const),
                whole((D_MODEL, D_MODEL)),
                whole((D_MODEL, D_FF)), whole((D_MODEL, D_FF)), whole((3, D_FF)), whole((1, D_FF)),
                whole((D_FF, D_MODEL))]
    return pl.pallas_call(
        functools.partial(_tail_kernel, seq_len=seq_len),
        grid=(t // tm,), in_specs=in_specs,
        out_specs=pl.BlockSpec((tm, D_MODEL), row),
        out_shape=jax.ShapeDtypeStruct((t, D_MODEL), F32),
        scratch_shapes=[pltpu.VMEM((tm, D_MODEL), BF16), pltpu.VMEM((TAIL_HALF, D_FF), BF16)],
        compiler_params=_params("arbitrary"), name="tail",
    )(x2, att, s5o, ht, proj, pp['mod'], pp['g2'], pp['m_ng'], pp['blkm'], pp['w_out'],
      pp['ffn_w_gate'], pp['ffn_w_up'], pp['ffn_conv_w'], pp['ffn_conv_b'], pp['ffn_w_down'])


def _trunk_layer(x2, batch, seq_len, layer, pp, ctx, rope_tabs):
    latent = ctx is not None
    proj, mg, qn, kn, mqk, vt, mgt = _pre_call(x2, layer, pp, rope_tabs if latent else None, seq_len, latent)
    cache_k, cache_v, init_re, init_im, m_init = ctx if latent else (None,) * 5

    att = _attn_call(qn, kn, proj, layer, cache_k, cache_v, batch, seq_len)
    s5o, fin_re, fin_im = _s5_call(proj, layer, pp, init_re, init_im, batch, seq_len)
    ht, cn_f, m_f = _mlstm_call(mqk.reshape(batch, seq_len, 2 * M_WIDTH), vt, mg.reshape(batch, seq_len, LANES), mgt,
                                layer, pp, m_init, batch, seq_len)
    x_out = _tail_call(x2, att, s5o, ht, proj, layer, pp, seq_len, latent)
    if latent:
        return x_out, None
    states = (kn.reshape(batch, seq_len, ATT_KV_HEADS, ATT_HD),
              proj[:, AUX_V:].reshape(batch, seq_len, ATT_KV_HEADS, ATT_HD),
              fin_re.transpose(1, 0, 2).reshape(batch, N_DIR, S5_GROUPS, S5_STATE),
              fin_im.transpose(1, 0, 2).reshape(batch, N_DIR, S5_GROUPS, S5_STATE),
              cn_f[:, :, :M_HD, :].reshape(batch, N_DIR, M_HEADS, M_HD, M_HD),
              cn_f[:, :, M_HD, :].reshape(batch, N_DIR, M_HEADS, M_HD),
              m_f[:, :, 0, 0].reshape(batch, N_DIR, M_HEADS))
    return x_out, states


def _head_block(width, head_dim):
    ids = np.arange(width) // head_dim
    return jnp.asarray((ids[:, None] == ids[None, :]).astype(np.float32) / head_dim, dtype=BF16)


def kernel(x_prompt, x_sample, c, cache_attn_k, cache_attn_v, state_s5_re, state_s5_im, state_mlstm_C, state_mlstm_n, state_mlstm_m, c_ctx, ada_w, ada_b, norm1_g, norm2_g, w_in, q_norm_g, k_norm_g, s5_a_re, s5_a_im, s5_log_dt, s5_b_re, s5_b_im, s5_c_re, s5_c_im, s5_d, s5_glu_w, s5_glu_b, m_gate_b, m_norm_g, w_out, ffn_w_gate, ffn_w_up, ffn_conv_w, ffn_conv_b, ffn_w_down):
    batch, seq = x_prompt.shape[0], x_prompt.shape[1]
    dec_batch, dec_seq = x_sample.shape[0], x_sample.shape[1]

    n_mod_rows = 2 * SUBLANES
    cvec = jnp.zeros((n_mod_rows, D_MODEL), F32).at[0].set(c_ctx).at[1:1 + dec_batch].set(c)
    mod_all = _ada_call(cvec, ada_w, ada_b).reshape(DEPTH, n_mod_rows, 6, D_MODEL)

    ab_re, ab_im, bb_re, bb_im = _s5_disc_call(s5_a_re, s5_a_im, s5_log_dt, s5_b_re, s5_b_im)
    ab_re = ab_re.reshape(DEPTH, N_DIR, 1, S5_FLAT)
    ab_im = ab_im.reshape(DEPTH, N_DIR, 1, S5_FLAT)
    bb_re = bb_re.reshape(DEPTH, N_DIR, S5_GROUPS, S5_STATE, S5_CH)
    bb_im = bb_im.reshape(DEPTH, N_DIR, S5_GROUPS, S5_STATE, S5_CH)

    rope_tabs = _rope_tables(dec_seq)
    tri = np.tril(np.ones((M_TILE, M_TILE), np.float32))
    tril = jnp.asarray(tri, dtype=BF16)
    triu = jnp.asarray(tri.T, dtype=BF16)
    ones = jnp.ones((M_TILE, M_TILE), BF16)
    blkq = _head_block(ATT_WIDTH, ATT_HD)
    blkk = _head_block(ATT_KV_WIDTH, ATT_HD)
    blkm = _head_block(M_WIDTH, M_HD)

    w_in_b = _bf(w_in)
    gb = m_gate_b.reshape(DEPTH, M_GATE_COLS)
    pp = dict(
        mod=mod_all, g1=norm1_g.reshape(DEPTH, 1, D_MODEL), g2=norm2_g.reshape(DEPTH, 1, D_MODEL),
        w_in=w_in_b,
        w_gatecols=jnp.zeros((DEPTH, D_MODEL, LANES), BF16).at[:, :, :M_GATE_COLS].set(w_in_b[:, :, PROJ_MAIN:]),
        blkq=blkq, blkk=blkk, blkm=blkm,
        qg=jnp.tile(q_norm_g, (1, ATT_HEADS)).reshape(DEPTH, 1, ATT_WIDTH),
        kg=jnp.tile(k_norm_g, (1, ATT_KV_HEADS)).reshape(DEPTH, 1, ATT_KV_WIDTH),
        s5_bmat=_bf(jnp.concatenate([_block_diag(jnp.swapaxes(bb_re, -1, -2)),
                                     _block_diag(jnp.swapaxes(bb_im, -1, -2))], axis=-1)),
        s5_cre=_bf(_block_diag(jnp.swapaxes(s5_c_re, -1, -2))),
        s5_cim=_bf(_block_diag(jnp.swapaxes(s5_c_im, -1, -2))),
        s5_ab_re=jnp.broadcast_to(ab_re, (DEPTH, N_DIR, SUBLANES, S5_FLAT)),
        s5_ab_im=jnp.broadcast_to(ab_im, (DEPTH, N_DIR, SUBLANES, S5_FLAT)),
        s5_d=s5_d.reshape(DEPTH, 1, S5_WIDTH), s5_glu_w=_bf(s5_glu_w), s5_glu_b=s5_glu_b.reshape(DEPTH, 1, S5_WIDTH),
        m_gb_row=jnp.zeros((DEPTH, 1, LANES), F32).at[:, 0, :M_GATE_COLS].set(gb),
        m_gb_col=gb.reshape(DEPTH, M_GATE_COLS, 1),
        m_ng=jnp.tile(m_norm_g, (1, M_HEADS)).reshape(DEPTH, 1, M_WIDTH), tril=tril, triu=triu, ones=ones,
        w_out=_bf(w_out), ffn_w_gate=_bf(ffn_w_gate), ffn_w_up=_bf(ffn_w_up),
        ffn_conv_w=ffn_conv_w, ffn_conv_b=ffn_conv_b.reshape(DEPTH, 1, D_FF), ffn_w_down=_bf(ffn_w_down))

    past = cache_attn_k.shape[2]
    n_pad = CN_ROWS - M_HD
    s5_init = lambda st: st.reshape(dec_batch, DEPTH, N_DIR, S5_FLAT).transpose(1, 2, 0, 3)
    m_init = (state_mlstm_C.reshape(dec_batch, DEPTH, M_CHAINS, M_HD, M_HD),
              jnp.zeros((dec_batch, DEPTH, M_CHAINS, n_pad, M_HD), F32).at[:, :, :, 0, :].set(
                  state_mlstm_n.reshape(dec_batch, DEPTH, M_CHAINS, M_HD)),
              jnp.broadcast_to(state_mlstm_m.reshape(dec_batch, DEPTH, M_CHAINS, 1, 1),
                               (dec_batch, DEPTH, M_CHAINS, 1, LANES)))
    ctx = (cache_attn_k.reshape(dec_batch, DEPTH, past, ATT_KV_WIDTH),
           cache_attn_v.reshape(dec_batch, DEPTH, past, ATT_KV_WIDTH),
           s5_init(state_s5_re), s5_init(state_s5_im), m_init)

    xp = x_prompt.reshape(batch * seq, D_MODEL)
    xs = x_sample.reshape(dec_batch * dec_seq, D_MODEL)
    ctx_out = []
    for l in range(DEPTH):
        xp, st = _trunk_layer(xp, batch, seq, l, pp, None, None)
        ctx_out.append(st)
        xs, _ = _trunk_layer(xs, dec_batch, dec_seq, l, pp, ctx, rope_tabs)
    outs = [jnp.stack([s[i] for s in ctx_out], axis=1) for i in range(7)]
    return (xp.reshape(batch, seq, D_MODEL), xs.reshape(dec_batch, dec_seq, D_MODEL), *outs)
```

```python
import functools

import numpy as np
import jax
import jax.numpy as jnp
from jax import lax
from jax.experimental import pallas as pl
from jax.experimental.pallas import tpu as pltpu

F32 = jnp.float32
BF16 = jnp.bfloat16

D_MODEL = 1024
DEPTH = 2
GRID_W = 64
N_DIR = 2
EPS = 1e-6
ATT_HD = 64
ATT_WIDTH = 512
ATT_HEADS = 8
ATT_KV_HEADS = 2
ATT_GROUP = ATT_HEADS // ATT_KV_HEADS
ATT_KV_WIDTH = ATT_KV_HEADS * ATT_HD
ROPE_THETA = 10000.0
LOG2_E = 1.4426950408889634
S5_CH = 16
S5_STATE = 64
S5_WIDTH = 256
S5_GROUPS = 16
S5_FLAT = S5_GROUPS * S5_STATE
M_HD = 64
M_WIDTH = 256
M_HEADS = 4
M_GATE_COLS = 2 * N_DIR * M_HEADS
M_CHAINS = N_DIR * M_HEADS
D_FF = 2816
PROJ_MAIN = 2048
COL_K = ATT_WIDTH
COL_V = COL_K + ATT_KV_WIDTH
COL_S5 = COL_V + ATT_KV_WIDTH
COL_MQ = COL_S5 + S5_WIDTH
COL_MK = COL_MQ + M_WIDTH
COL_MV = COL_MK + M_WIDTH
COL_MO = COL_MV + M_WIDTH
AUX_S5 = 0
AUX_MO = AUX_S5 + S5_WIDTH
AUX_V = AUX_MO + M_WIDTH
AUX_WIDTH = AUX_V + ATT_KV_WIDTH
LANES = 128
SUBLANES = 8
BF16_ROWS = 16
VMEM_LIMIT = 56 * 1024 * 1024

ADA_TN = 1024
PRE_TM = 512
ATT_TQ = 512
S5_ROWS = 256
S5_SLOTS = 3
M_TILE = 128
M_BG = 4
M_BG_SEQ = 1024
CN_ROWS = M_HD + BF16_ROWS
TAIL_TM = 1024
TAIL_OUT_CHUNK = 256
TAIL_HALF = 512
FF_CHUNK = 256


def _bf(x):
    return x.astype(BF16)


def _dot(a, b):
    return jnp.dot(a, b, preferred_element_type=F32)


def _split(x):
    hi = _bf(x)
    lo = _bf(x - hi.astype(F32))
    return hi, lo


def _seg_mean_sq(x, blk):
    hi, lo = _split(x * x)
    return _dot(hi, blk) + _dot(lo, blk)


def _sigmoid(x):
    return 1.0 / (1.0 + jnp.exp(-x))


def _log_sigmoid(x):
    return -(jnp.maximum(-x, 0.0) + jnp.log1p(jnp.exp(-jnp.abs(x))))


def _params(*sem):
    return pltpu.CompilerParams(dimension_semantics=sem, vmem_limit_bytes=VMEM_LIMIT)


def _ada_kernel(c_ref, w_ref, b_ref, o_ref):
    c = c_ref[...]
    s = c * _sigmoid(c)
    o_ref[...] = _dot(_bf(s), _bf(w_ref[...])) + b_ref[...]


def _ada_call(cvec, ada_w, ada_b):
    rows = cvec.shape[0]
    tn = ADA_TN
    n = ada_w.shape[-1]
    return pl.pallas_call(
        _ada_kernel,
        grid=(DEPTH, n // tn),
        in_specs=[pl.BlockSpec((rows, D_MODEL), lambda l, j: (0, 0)),
                  pl.BlockSpec((None, D_MODEL, tn), lambda l, j: (l, 0, j)),
                  pl.BlockSpec((None, 1, tn), lambda l, j: (l, 0, j))],
        out_specs=pl.BlockSpec((None, rows, tn), lambda l, j: (l, 0, j)),
        out_shape=jax.ShapeDtypeStruct((DEPTH, rows, n), F32),
        compiler_params=_params("arbitrary", "arbitrary"),
        name="adaln",
    )(cvec, ada_w, ada_b.reshape(DEPTH, 1, n))


def _s5_disc_kernel(are_ref, aim_ref, ldt_ref, arex_ref, aimx_ref, bre_ref, bim_ref,
                    abr_ref, abi_ref, bbr_ref, bbi_ref):
    dt = jnp.exp(ldt_ref[...])

    def disc(a_re, a_im):
        mag = jnp.exp(dt * a_re)
        ab_re = mag * jnp.cos(dt * a_im)
        ab_im = mag * jnp.sin(dt * a_im)
        den = a_re * a_re + a_im * a_im
        nr = ab_re - 1.0
        ni = ab_im
        f_re = (nr * a_re + ni * a_im) / den
        f_im = (ni * a_re - nr * a_im) / den
        return ab_re, ab_im, f_re, f_im

    ab_re, ab_im, _, _ = disc(are_ref[...], aim_ref[...])
    abr_ref[...] = ab_re
    abi_ref[...] = ab_im
    _, _, f_re, f_im = disc(arex_ref[...], aimx_ref[...])
    b_re = bre_ref[...]
    b_im = bim_ref[...]
    bbr_ref[...] = f_re * b_re - f_im * b_im
    bbi_ref[...] = f_re * b_im + f_im * b_re


def _s5_disc_call(a_re, a_im, log_dt, b_re, b_im):
    r = DEPTH * N_DIR * S5_GROUPS
    a_re2 = a_re.reshape(r, S5_STATE)
    a_im2 = a_im.reshape(r, S5_STATE)
    wide = S5_STATE * S5_CH
    out_shape = (jax.ShapeDtypeStruct((r, S5_STATE), F32), jax.ShapeDtypeStruct((r, S5_STATE), F32),
                 jax.ShapeDtypeStruct((r, wide), F32), jax.ShapeDtypeStruct((r, wide), F32))
    return pl.pallas_call(_s5_disc_kernel, out_shape=out_shape, name="s5_disc")(
        a_re2, a_im2, log_dt.reshape(r, 1),
        jnp.repeat(a_re2, S5_CH, axis=1), jnp.repeat(a_im2, S5_CH, axis=1),
        b_re.reshape(r, wide), b_im.reshape(r, wide))


def _block_diag(blocks):
    g, r, c = blocks.shape[-3:]
    lead = blocks.shape[:-3]
    tiled = jnp.tile(blocks.reshape(lead + (g * r, c)), (1,) * len(lead) + (1, g))
    on_diagonal = (np.arange(g * r)[:, None] // r) == (np.arange(g * c)[None, :] // c)
    return jnp.where(on_diagonal, tiled, 0.0)


def _rope(x, cos, sin_signed, second):
    width = x.shape[-1]
    quarter = ATT_HD // 4
    partner = jnp.where(second, pltpu.roll(x, quarter, axis=1), pltpu.roll(x, width - quarter, axis=1))
    return x * cos + partner * sin_signed


def _pre_kernel(*refs, rope):
    if rope:
        (x_ref, mod_ref, g1_ref, w_ref, wg_ref, blkq_ref, blkk_ref, qg_ref, kg_ref,
         cosq_ref, sinq_ref, cosk_ref, sink_ref, proj_ref, mg_ref, qn_ref, kn_ref, mqk_ref, vt_ref, mgt_ref) = refs
    else:
        (x_ref, mod_ref, g1_ref, w_ref, wg_ref, blkq_ref, blkk_ref, qg_ref, kg_ref,
         proj_ref, mg_ref, qn_ref, kn_ref, mqk_ref, vt_ref, mgt_ref) = refs
    x = x_ref[...]
    ms = jnp.mean(x * x, axis=-1, keepdims=True)
    xn = x * lax.rsqrt(ms + EPS) * g1_ref[...]
    h = _bf(xn * (1.0 + mod_ref[1:2, :]) + mod_ref[0:1, :])
    def proj(c0, c1):
        return _dot(h, w_ref[:, c0:c1])

    q = proj(0, COL_K)
    kvs = proj(COL_K, COL_MQ)
    k = kvs[:, :ATT_KV_WIDTH]
    mqk = proj(COL_MQ, COL_MV)
    mvo = proj(COL_MV, PROJ_MAIN)
    proj_ref[:, AUX_S5:AUX_MO] = kvs[:, 2 * ATT_KV_WIDTH:]
    proj_ref[:, AUX_MO:AUX_V] = mvo[:, M_WIDTH:]
    proj_ref[:, AUX_V:] = kvs[:, ATT_KV_WIDTH:2 * ATT_KV_WIDTH]
    mg = _dot(h, wg_ref[...])
    mg_ref[...] = mg
    mqk_ref[:, :M_WIDTH] = _bf(mqk[:, :M_WIDTH])
    mqk_ref[:, M_WIDTH:] = _bf(mqk[:, M_WIDTH:] * (M_HD ** -0.5))
    vt_ref[...] = mvo[:, :M_WIDTH].T
    mgt_ref[...] = mg.T[:M_GATE_COLS, :]
    qn = q * lax.rsqrt(_seg_mean_sq(q, blkq_ref[...]) + EPS) * qg_ref[...]
    kn = k * lax.rsqrt(_seg_mean_sq(k, blkk_ref[...]) + EPS) * kg_ref[...]
    if rope:
        lane_q = lax.broadcasted_iota(jnp.int32, qn.shape, 1)
        lane_k = lax.broadcasted_iota(jnp.int32, kn.shape, 1)
        qn = _rope(qn, cosq_ref[...], sinq_ref[...], (lane_q & (ATT_HD // 4)) != 0)
        kn = _rope(kn, cosk_ref[...], sink_ref[...], (lane_k & (ATT_HD // 4)) != 0)
    qn_ref[...] = qn
    kn_ref[...] = kn


def _rope_tables(seq_len):
    n_rows = seq_len // GRID_W
    row = np.repeat(np.arange(n_rows), GRID_W)
    col = np.tile(np.arange(GRID_W), n_rows)
    half = ATT_HD // 2
    inv_freq = (1.0 / (np.float32(ROPE_THETA) ** (np.arange(0, half, 2, dtype=np.float32) / half))).astype(np.float32)

    def tables(pos):
        ang = pos.astype(np.float32)[:, None] * inv_freq[None, :]
        cos = np.cos(ang)
        sin = np.sin(ang)
        return np.concatenate([cos, cos], axis=-1), np.concatenate([-sin, sin], axis=-1)

    cr, sr = tables(row)
    cc, sc = tables(col)
    cos = np.concatenate([cr, cc], axis=-1)
    sin = np.concatenate([sr, sc], axis=-1)
    return tuple(jnp.asarray(np.tile(t, (1, n)), dtype=F32)
                 for t, n in ((cos, ATT_HEADS), (sin, ATT_HEADS), (cos, ATT_KV_HEADS), (sin, ATT_KV_HEADS)))


def _layered(layer, block, index_map, **kw):
    return pl.BlockSpec((None,) + tuple(block), lambda *g: (layer,) + tuple(index_map(*g)), **kw)


def _mod_spec(layer, row_of):
    return pl.BlockSpec((None, None, 6, D_MODEL), lambda *g: (layer, row_of(*g), 0, 0))


def _pre_call(x2, layer, pp, rope_tabs, seq_len, per_batch_mod):
    t = x2.shape[0]
    tm = min(PRE_TM, seq_len)
    tiles_per_seq = seq_len // tm
    rope = rope_tabs is not None
    if per_batch_mod:
        mod_row = lambda i: 1 + i // tiles_per_seq
    else:
        mod_row = lambda i: 0
    const = lambda i: (0, 0)
    row = lambda i: (i, 0)
    in_specs = [pl.BlockSpec((tm, D_MODEL), row),
                _mod_spec(layer, mod_row),
                _layered(layer, (1, D_MODEL), const),
                _layered(layer, (D_MODEL, PROJ_MAIN), const),
                _layered(layer, (D_MODEL, LANES), const),
                pl.BlockSpec((ATT_WIDTH, ATT_WIDTH), const),
                pl.BlockSpec((ATT_KV_WIDTH, ATT_KV_WIDTH), const),
                _layered(layer, (1, ATT_WIDTH), const),
                _layered(layer, (1, ATT_KV_WIDTH), const)]
    args = [x2, pp['mod'], pp['g1'], pp['w_in'], pp['w_gatecols'], pp['blkq'], pp['blkk'], pp['qg'], pp['kg']]
    if rope:
        pos_map = lambda i: (i % tiles_per_seq, 0)
        in_specs += [pl.BlockSpec((tm, ATT_WIDTH), pos_map), pl.BlockSpec((tm, ATT_WIDTH), pos_map),
                     pl.BlockSpec((tm, ATT_KV_WIDTH), pos_map), pl.BlockSpec((tm, ATT_KV_WIDTH), pos_map)]
        args += list(rope_tabs)
    time_on_lanes = lambda i: (i // tiles_per_seq, 0, i % tiles_per_seq)
    batch = t // seq_len
    out_specs = [pl.BlockSpec((tm, AUX_WIDTH), row),
                 pl.BlockSpec((tm, LANES), row),
                 pl.BlockSpec((tm, ATT_WIDTH), row),
                 pl.BlockSpec((tm, ATT_KV_WIDTH), row),
                 pl.BlockSpec((tm, 2 * M_WIDTH), row),
                 pl.BlockSpec((None, M_WIDTH, tm), time_on_lanes),
                 pl.BlockSpec((None, M_GATE_COLS, tm), time_on_lanes)]
    out_shape = [jax.ShapeDtypeStruct((t, AUX_WIDTH), F32), jax.ShapeDtypeStruct((t, LANES), F32),
                 jax.ShapeDtypeStruct((t, ATT_WIDTH), F32), jax.ShapeDtypeStruct((t, ATT_KV_WIDTH), F32),
                 jax.ShapeDtypeStruct((t, 2 * M_WIDTH), BF16),
                 jax.ShapeDtypeStruct((batch, M_WIDTH, seq_len), F32),
                 jax.ShapeDtypeStruct((batch, M_GATE_COLS, seq_len), F32)]
    return pl.pallas_call(
        functools.partial(_pre_kernel, rope=rope),
        grid=(t // tm,), in_specs=in_specs, out_specs=out_specs, out_shape=out_shape,
        compiler_params=_params("arbitrary"), name="pre_rope" if rope else "pre",
    )(*args)


def _attn_kernel(*refs, cached, kv_len):
    if cached:
        q_ref, k_ref, v_ref, ck_ref, cv_ref, o_ref = refs
        units = [(slice(None), _bf(jnp.concatenate([ck_ref[...], k_ref[...]], axis=0)),
                  _bf(jnp.concatenate([cv_ref[...], v_ref[...]], axis=0)))]
    else:
        q_ref, k_ref, v_ref, o_ref = refs
        units = []
        for r0 in range(0, q_ref.shape[0], kv_len):
            rows = slice(r0, r0 + kv_len)
            units.append((rows, _bf(k_ref[rows, :]), _bf(v_ref[rows, :])))
    scale = ATT_HD ** -0.5 * LOG2_E
    for rows, k_all, v_all in units:
        outs = []
        for kvh in range(ATT_KV_HEADS):
            lo = kvh * ATT_HD
            kb = k_all[:, lo:lo + ATT_HD]
            vb = v_all[:, lo:lo + ATT_HD]
            for g in range(ATT_GROUP):
                c0 = (kvh * ATT_GROUP + g) * ATT_HD
                qb = _bf(q_ref[rows, c0:c0 + ATT_HD] * scale)
                s = lax.dot_general(qb, kb, (((1,), (1,)), ((), ())), preferred_element_type=F32)
                m = jnp.max(s, axis=-1, keepdims=True)
                e = jnp.exp2(s - m)
                den = jnp.sum(e, axis=-1, keepdims=True)
                outs.append(_dot(_bf(e), vb) / den)
        o_ref[rows, :] = _bf(jnp.concatenate(outs, axis=-1))


def _attn_call(qn, kn, proj, layer, cache_k, cache_v, batch, lq):
    cached = cache_k is not None
    tq = min(ATT_TQ, lq)
    nq = lq // tq
    seqs = 1 if cached else max(1, ATT_TQ // lq)
    total_rows = batch * lq
    kv_rows = seqs * lq
    tq *= seqs
    in_specs = [pl.BlockSpec((tq, ATT_WIDTH), lambda b, i: (b * nq + i, 0)),
                pl.BlockSpec((kv_rows, ATT_KV_WIDTH), lambda b, i: (b, 0)),
                pl.BlockSpec((kv_rows, ATT_KV_WIDTH), lambda b, i: (b, AUX_V // ATT_KV_WIDTH))]
    args = [qn, kn, proj]
    if cached:
        past = cache_k.shape[2]
        cache_spec = pl.BlockSpec((None, None, past, ATT_KV_WIDTH), lambda b, i: (b, layer, 0, 0))
        in_specs += [cache_spec, cache_spec]
        args += [cache_k, cache_v]
    return pl.pallas_call(
        functools.partial(_attn_kernel, cached=cached, kv_len=lq),
        grid=(batch // seqs, nq), in_specs=in_specs,
        out_specs=pl.BlockSpec((tq, ATT_WIDTH), lambda b, i: (b * nq + i, 0)),
        out_shape=jax.ShapeDtypeStruct((total_rows, ATT_WIDTH), BF16),
        compiler_params=_params("arbitrary", "arbitrary"), name="attn_cached" if cached else "attn",
    )(*args)


def _s5_kernel(*refs, batch, seq_len, zero_init):
    if zero_init:
        (u_ref, bmat_ref, cre_ref, cim_ref, ar_ref, ai_ref, d_ref, gw_ref, gb_ref,
         out_ref, fr_ref, fi_ref, utb_ref, ytb_ref, xs_ref) = refs
        ir_ref = ii_ref = None
    else:
        (u_ref, bmat_ref, cre_ref, cim_ref, ar_ref, ai_ref, d_ref, gw_ref, gb_ref, ir_ref, ii_ref,
         out_ref, fr_ref, fi_ref, utb_ref, ytb_ref, xs_ref) = refs
    rows = S5_ROWS
    tc = rows // batch
    n_chunks = seq_len // tc
    n_sub = batch // SUBLANES
    halves = S5_WIDTH // LANES

    def aligned(x, m):
        return x if isinstance(x, int) else pl.multiple_of(x, m)

    for b in range(batch):
        for hf in range(halves):
            utb_ref[hf, pl.ds(b, seq_len, stride=batch), :] = (
                u_ref[b * seq_len:(b + 1) * seq_len, hf * LANES:(hf + 1) * LANES])
    for hf in range(halves):
        ytb_ref[hf] = utb_ref[hf] * d_ref[:, hf * LANES:(hf + 1) * LANES]

    def chunk_of(d, i):
        return i if d == 0 else n_chunks - 1 - i

    def stage_in(i, slot):
        for d in range(N_DIR):
            r0 = aligned(chunk_of(d, i) * rows, rows)
            u_c = jnp.concatenate([utb_ref[hf, pl.ds(r0, rows), :] for hf in range(halves)], axis=-1)
            xs_ref[slot, d] = _dot(_bf(u_c), bmat_ref[d])

    def stage_scan(slot, carry):
        new_carry = []
        for d in range(N_DIR):
            a_re = ar_ref[d]
            a_im = ai_ref[d]
            per_sub = []
            for sub in range(n_sub):
                s_re, s_im = carry[d][sub]
                for t in range(tc):
                    r = (t if d == 0 else tc - 1 - t) * batch + sub * SUBLANES
                    x_re = xs_ref[slot, d, r:r + SUBLANES, :S5_FLAT]
                    x_im = xs_ref[slot, d, r:r + SUBLANES, S5_FLAT:]
                    s_re, s_im = (a_re * s_re - a_im * s_im + x_re, a_re * s_im + a_im * s_re + x_im)
                    xs_ref[slot, d, r:r + SUBLANES, :S5_FLAT] = s_re
                    xs_ref[slot, d, r:r + SUBLANES, S5_FLAT:] = s_im
                per_sub.append((s_re, s_im))
            new_carry.append(tuple(per_sub))
        return tuple(new_carry)

    def stage_out(i, slot):
        for d in range(N_DIR):
            r0 = aligned(chunk_of(d, i) * rows, rows)
            y = (_dot(_bf(xs_ref[slot, d, :, :S5_FLAT]), cre_ref[d])
                 - _dot(_bf(xs_ref[slot, d, :, S5_FLAT:]), cim_ref[d]))
            for hf in range(halves):
                ytb_ref[hf, pl.ds(r0, rows), :] += y[:, hf * LANES:(hf + 1) * LANES]

    def start_state(ref, d, sub):
        if zero_init:
            return jnp.zeros((SUBLANES, S5_FLAT), F32)
        return ref[d, sub * SUBLANES:(sub + 1) * SUBLANES, :]

    carry = tuple(tuple((start_state(ir_ref, d, sub), start_state(ii_ref, d, sub)) for sub in range(n_sub))
                  for d in range(N_DIR))
    def step(i, phase, carry):
        stage_in(i + 1, (phase + 1) % S5_SLOTS)
        carry = stage_scan(phase, carry)
        stage_out(i - 1, (phase - 1) % S5_SLOTS)
        return carry

    stage_in(0, 0)
    stage_in(1, 1)
    carry = stage_scan(0, carry)
    n_steady = n_chunks - 2
    n_peeled = n_steady % S5_SLOTS
    for i in range(1, 1 + n_peeled):
        carry = step(i, i % S5_SLOTS, carry)
    first = 1 + n_peeled

    def body(g, carry):
        for k in range(S5_SLOTS):
            carry = step(first + g * S5_SLOTS + k, (first + k) % S5_SLOTS, carry)
        return carry

    carry = lax.fori_loop(0, n_steady // S5_SLOTS, body, carry)
    last = n_chunks - 1
    carry = stage_scan(last % S5_SLOTS, carry)
    stage_out(last - 1, (last - 1) % S5_SLOTS)
    stage_out(last, last % S5_SLOTS)
    for d in range(N_DIR):
        for sub in range(n_sub):
            fr_ref[d, sub * SUBLANES:(sub + 1) * SUBLANES, :] = carry[d][sub][0]
            fi_ref[d, sub * SUBLANES:(sub + 1) * SUBLANES, :] = carry[d][sub][1]

    for b in range(batch):
        z = jax.nn.gelu(jnp.concatenate(
            [ytb_ref[hf, pl.ds(b, seq_len, stride=batch), :] for hf in range(halves)], axis=-1))
        gate = _sigmoid(_dot(_bf(z), gw_ref[...]) + gb_ref[...])
        out_ref[b * seq_len:(b + 1) * seq_len, :] = _bf(z * gate)


def _s5_call(proj, layer, pp, init_re, init_im, batch, seq_len):
    rows = seq_len * batch
    single = pl.Buffered(1)
    zero_init = init_re is None
    names = ['s5_bmat', 's5_cre', 's5_cim', 's5_ab_re', 's5_ab_im', 's5_d', 's5_glu_w', 's5_glu_b']
    args = [pp[n] for n in names] + ([] if zero_init else [init_re, init_im])
    whole = lambda a: _layered(layer, a.shape[1:], lambda i: (0,) * (a.ndim - 1), pipeline_mode=single)
    in_specs = ([pl.BlockSpec((rows, S5_WIDTH), lambda i: (0, AUX_S5 // S5_WIDTH), pipeline_mode=single)]
                + [whole(a) for a in args])
    fin_spec = pl.BlockSpec((N_DIR, batch, S5_FLAT), lambda i: (0, 0, 0))
    out_specs = [pl.BlockSpec((rows, S5_WIDTH), lambda i: (0, 0), pipeline_mode=single), fin_spec, fin_spec]
    out_shape = (jax.ShapeDtypeStruct((rows, S5_WIDTH), BF16),
                 jax.ShapeDtypeStruct((N_DIR, batch, S5_FLAT), F32),
                 jax.ShapeDtypeStruct((N_DIR, batch, S5_FLAT), F32))
    return pl.pallas_call(
        functools.partial(_s5_kernel, batch=batch, seq_len=seq_len, zero_init=zero_init),
        grid=(1,), in_specs=in_specs, out_specs=out_specs, out_shape=out_shape,
        scratch_shapes=[pltpu.VMEM((S5_WIDTH // LANES, rows, LANES), F32),
                        pltpu.VMEM((S5_WIDTH // LANES, rows, LANES), F32),
                        pltpu.VMEM((S5_SLOTS, N_DIR, S5_ROWS, 2 * S5_FLAT), F32)],
        compiler_params=_params("arbitrary"), name="s5",
    )(proj, *args)


def _mlstm_kernel(*refs, seq_len, zero_init):
    if zero_init:
        (q_ref, k_ref, vt_ref, mg_ref, mgt_ref, gbr_ref, gbc_ref, tril_ref, triu_ref, ones_ref,
         ht_ref, cn_ref, m_ref) = refs
    else:
        (q_ref, k_ref, vt_ref, mg_ref, mgt_ref, gbr_ref, gbc_ref, tril_ref, triu_ref, ones_ref,
         c0_ref, n0_ref, m0_ref, ht_ref, cn_ref, m_ref) = refs
    _mlstm_body(q_ref, k_ref, vt_ref, mg_ref, mgt_ref, gbr_ref, gbc_ref, tril_ref, triu_ref, ones_ref,
                None if zero_init else (c0_ref, n0_ref, m0_ref), ht_ref, cn_ref, m_ref, seq_len)


def _mlstm_body(q_ref, k_ref, vt_ref, mg_ref, mgt_ref, gbr_ref, gbc_ref, tril_ref, triu_ref, ones_ref,
                init_refs, ht_ref, cn_ref, m_ref, seq_len):
    tile = M_TILE
    n_chunks = seq_len // tile
    bg = q_ref.shape[0]
    if init_refs is None:
        cn_ref[...] = jnp.zeros_like(cn_ref)
        m_ref[...] = jnp.zeros_like(m_ref)
    else:
        c0_ref, n0_ref, m0_ref = init_refs
        cn_ref[:, :, :M_HD, :] = c0_ref[...]
        cn_ref[:, :, M_HD:, :] = n0_ref[...]
        m_ref[...] = m0_ref[...]
    s_ids = lax.broadcasted_iota(jnp.int32, (tile, tile), 0)
    t_ids = lax.broadcasted_iota(jnp.int32, (tile, tile), 1)
    first_row = lax.broadcasted_iota(jnp.int32, (CN_ROWS - M_HD, tile), 0) == 0
    neg_inf = jnp.float32(-jnp.inf)
    ones = ones_ref[...]

    def direction(d, c):
        t0 = pl.multiple_of(c * tile, tile)
        col_mat = tril_ref[...] if d == 0 else triu_ref[...]
        row_mat = triu_ref[...] if d == 0 else tril_ref[...]
        valid = (s_ids <= t_ids) if d == 0 else (s_ids >= t_ids)
        gt = mgt_ref[:, :, pl.ds(t0, tile)] + gbc_ref[...]
        lft_hi, lft_lo = _split(_log_sigmoid(gt).reshape(bg * M_GATE_COLS, tile))
        b_rows = (_dot(lft_hi, row_mat) + _dot(lft_lo, row_mat)).reshape(bg, M_GATE_COLS, tile)
        totals = (_dot(lft_hi, ones) + _dot(lft_lo, ones)).reshape(bg, M_GATE_COLS, tile)
        gc = mg_ref[:, pl.ds(t0, tile), :] + gbr_ref[...]
        lfc_hi, lfc_lo = _split(_log_sigmoid(gc))
        col_b = jnp.broadcast_to(col_mat, (bg, tile, tile))
        b_cols = (jnp.einsum('bts,bsl->btl', col_b, lfc_hi, preferred_element_type=F32)
                  + jnp.einsum('bts,bsl->btl', col_b, lfc_lo, preferred_element_type=F32))
        for h in range(M_HEADS):
            j = d * M_HEADS + h
            fj = M_CHAINS + j
            rows = slice(h * M_HD, (h + 1) * M_HD)
            q_c = q_ref[:, pl.ds(t0, tile), rows]
            k_c = k_ref[:, pl.ds(t0, tile), rows]
            vt_c = vt_ref[:, rows, pl.ds(t0, tile)]
            b_t = b_rows[:, fj:fj + 1, :]
            li_t = gt[:, j:j + 1, :]
            tot = totals[:, fj:fj + 1, :]
            m_prev = m_ref[:, j]
            cn_prev = cn_ref[:, j]

            c_col = gc[:, :, j:j + 1] - b_cols[:, :, fj:fj + 1]
            cm = jnp.where(valid, c_col, neg_inf)
            inter = b_t + m_prev
            m_t = jnp.maximum(inter, b_t + jnp.max(cm, axis=1, keepdims=True))
            w_t = jnp.exp(cm + (b_t - m_t))
            s_inter = jnp.exp(inter - m_t)
            sc_t = jnp.einsum('bse,bte->bst', k_c, q_c, preferred_element_type=F32) * w_t
            num = jnp.einsum('bds,bst->bdt', _bf(vt_c), _bf(sc_t), preferred_element_type=F32)
            ext = jnp.einsum('bre,bte->brt', _bf(cn_prev), q_c, preferred_element_type=F32)
            num = num + s_inter * ext[:, :M_HD, :]
            den = jnp.sum(sc_t, axis=1, keepdims=True) + s_inter * ext[:, M_HD:M_HD + 1, :]
            ht_ref[d, :, rows, pl.ds(t0, tile)] = num / jnp.maximum(jnp.abs(den), jnp.exp(-m_t))

            g_row = tot - b_t + li_t
            m_new = jnp.maximum(tot + m_prev, jnp.max(g_row, axis=-1, keepdims=True))
            wk = jnp.exp(g_row - m_new)
            decay = jnp.exp(tot + m_prev - m_new)
            vw = jnp.concatenate([vt_c * wk, jnp.where(first_row, wk, 0.0)], axis=1)
            cn_ref[:, j] = (decay[:, :, :M_HD] * cn_prev
                            + jnp.einsum('brs,bse->bre', _bf(vw), k_c, preferred_element_type=F32))
            m_ref[:, j] = m_new

    def body(i, _):
        direction(0, i)
        direction(1, n_chunks - 1 - i)
        return 0

    lax.fori_loop(0, n_chunks, body, 0)


def _mlstm_call(mqk3, vt, mg3, mgt, layer, pp, init, batch, seq_len):
    bg = min(batch // 2, M_BG * M_BG_SEQ // seq_len)
    const2 = lambda g: (0, 0)
    lead3 = lambda g: (g, 0, 0)
    lead4 = lambda g: (g, 0, 0, 0)
    in_specs = [pl.BlockSpec((bg, seq_len, M_WIDTH), lead3),
                pl.BlockSpec((bg, seq_len, M_WIDTH), lambda g: (g, 0, 1)),
                pl.BlockSpec((bg, M_WIDTH, seq_len), lead3),
                pl.BlockSpec((bg, seq_len, LANES), lead3),
                pl.BlockSpec((bg, M_GATE_COLS, seq_len), lead3),
                _layered(layer, (1, LANES), const2),
                _layered(layer, (M_GATE_COLS, 1), const2),
                pl.BlockSpec((M_TILE, M_TILE), const2),
                pl.BlockSpec((M_TILE, M_TILE), const2),
                pl.BlockSpec((M_TILE, M_TILE), const2)]
    args = [mqk3, mqk3, vt, mg3, mgt, pp['m_gb_row'], pp['m_gb_col'], pp['tril'], pp['triu'], pp['ones']]
    if init is not None:
        state_spec = lambda *tail: pl.BlockSpec((bg, None, M_CHAINS) + tail, lambda g: (g, layer, 0, 0, 0))
        in_specs += [state_spec(M_HD, M_HD), state_spec(CN_ROWS - M_HD, M_HD), state_spec(1, LANES)]
        args += list(init)
    out_specs = [pl.BlockSpec((N_DIR, bg, M_WIDTH, seq_len), lambda g: (0, g, 0, 0)),
                 pl.BlockSpec((bg, M_CHAINS, CN_ROWS, M_HD), lead4),
                 pl.BlockSpec((bg, M_CHAINS, 1, LANES), lead4)]
    out_shape = [jax.ShapeDtypeStruct((N_DIR, batch, M_WIDTH, seq_len), F32),
                 jax.ShapeDtypeStruct((batch, M_CHAINS, CN_ROWS, M_HD), F32),
                 jax.ShapeDtypeStruct((batch, M_CHAINS, 1, LANES), F32)]
    return pl.pallas_call(
        functools.partial(_mlstm_kernel, seq_len=seq_len, zero_init=init is None),
        grid=(batch // bg,), in_specs=in_specs, out_specs=out_specs, out_shape=out_shape,
        compiler_params=_params("arbitrary"), name="mlstm",
    )(*args)


def _tail_kernel(x_ref, att_ref, s5_ref, ht_ref, mo_ref, mod_ref, g2_ref, ng_ref, blkm_ref, wo_ref,
                 wg_ref, wu_ref, cw_ref, cb_ref, wd_ref, o_ref, h2_ref, gated_ref, *, seq_len):
    tm = x_ref.shape[0]
    mh = jnp.concatenate([(ht_ref[0, s] + ht_ref[1, s]).T for s in range(ht_ref.shape[1])], axis=0)
    ml = mh * lax.rsqrt(_seg_mean_sq(mh, blkm_ref[...]) + EPS) * ng_ref[...] * _sigmoid(mo_ref[...])
    mixers = jnp.concatenate([att_ref[...], s5_ref[...], _bf(ml)], axis=-1)
    for c0 in range(0, D_MODEL, TAIL_OUT_CHUNK):
        cols = slice(c0, c0 + TAIL_OUT_CHUNK)
        o_ref[:, cols] = x_ref[:, cols] + mod_ref[2:3, cols] * _dot(mixers, wo_ref[:, cols])
    x1 = o_ref[...]
    ms = jnp.mean(x1 * x1, axis=-1, keepdims=True)
    xn = x1 * lax.rsqrt(ms + EPS) * g2_ref[...]
    h2_ref[...] = _bf(xn * (1.0 + mod_ref[4:5, :]) + mod_ref[3:4, :])

    half = TAIL_HALF
    row = lax.broadcasted_iota(jnp.int32, (half, FF_CHUNK), 0)
    half_start = row == 0
    half_end = row == half - 1
    for r0 in range(0, tm, half):
        pos = (r0 + row) % seq_len
        seq_start = pos == 0
        seq_end = pos == seq_len - 1
        h2 = h2_ref[r0:r0 + half, :]
        p0 = max(r0 - BF16_ROWS, 0)
        n0 = min(r0 + half, tm - BF16_ROWS)
        h2_ext = jnp.concatenate([h2, h2_ref[p0:p0 + BF16_ROWS, :], h2_ref[n0:n0 + BF16_ROWS, :]], axis=0)
        for j in range(D_FF // FF_CHUNK):
            cols = slice(j * FF_CHUNK, (j + 1) * FF_CHUNK)
            a_ext = _dot(h2_ext, wg_ref[:, cols])
            a = a_ext[:half]
            before = a_ext[half + BF16_ROWS - 1:half + BF16_ROWS, :]
            after = a_ext[half + BF16_ROWS:half + BF16_ROWS + 1, :]
            a_prev = jnp.where(half_start, before, pltpu.roll(a, 1, axis=0))
            a_next = jnp.where(half_end, after, pltpu.roll(a, half - 1, axis=0))
            a_prev = jnp.where(seq_start, 0.0, a_prev)
            a_next = jnp.where(seq_end, 0.0, a_next)
            ac = a_prev * cw_ref[0:1, cols] + a * cw_ref[1:2, cols] + a_next * cw_ref[2:3, cols] + cb_ref[:, cols]
            up = _dot(h2, wu_ref[:, cols])
            gated_ref[:, cols] = _bf(ac * _sigmoid(ac) * up)
        o_ref[r0:r0 + half, :] = o_ref[r0:r0 + half, :] + mod_ref[5:6, :] * _dot(gated_ref[...], wd_ref[...])


def _tail_call(x2, att, s5o, ht, proj, layer, pp, seq_len, per_batch_mod):
    t = x2.shape[0]
    tm = TAIL_TM
    seqs_per_tile = tm // seq_len
    if per_batch_mod:
        mod_row = lambda i: 1 + i * seqs_per_tile
    else:
        mod_row = lambda i: 0
    row = lambda i: (i, 0)
    const = lambda i: (0, 0)
    single = pl.Buffered(1)
    whole = lambda shape: _layered(layer, shape, const, pipeline_mode=single)
    in_specs = [pl.BlockSpec((tm, D_MODEL), row),
                pl.BlockSpec((tm, ATT_WIDTH), row),
                pl.BlockSpec((tm, S5_WIDTH), row),
                pl.BlockSpec((N_DIR, seqs_per_tile, M_WIDTH, seq_len), lambda i: (0, i, 0, 0)),
                pl.BlockSpec((tm, M_WIDTH), lambda i: (i, AUX_MO // M_WIDTH)),
                _mod_spec(layer, mod_row),
                _layered(layer, (1, D_MODEL), const),
                _layered(layer, (1, M_WIDTH), const),
                pl.BlockSpec((M_WIDTH, M_WIDTH), const),
                whole((D_MODEL, D_MODEL)),
                whole((D_MODEL, D_FF)), whole((D_MODEL, D_FF)), whole((3, D_FF)), whole((1, D_FF)),
                whole((D_FF, D_MODEL))]
    return pl.pallas_call(
        functools.partial(_tail_kernel, seq_len=seq_len),
        grid=(t // tm,), in_specs=in_specs,
        out_specs=pl.BlockSpec((tm, D_MODEL), row),
        out_shape=jax.ShapeDtypeStruct((t, D_MODEL), F32),
        scratch_shapes=[pltpu.VMEM((tm, D_MODEL), BF16), pltpu.VMEM((TAIL_HALF, D_FF), BF16)],
        compiler_params=_params("arbitrary"), name="tail",
    )(x2, att, s5o, ht, proj, pp['mod'], pp['g2'], pp['m_ng'], pp['blkm'], pp['w_out'],
      pp['ffn_w_gate'], pp['ffn_w_up'], pp['ffn_conv_w'], pp['ffn_conv_b'], pp['ffn_w_down'])


def _trunk_layer(x2, batch, seq_len, layer, pp, ctx, rope_tabs):
    latent = ctx is not None
    proj, mg, qn, kn, mqk, vt, mgt = _pre_call(x2, layer, pp, rope_tabs if latent else None, seq_len, latent)
    cache_k, cache_v, init_re, init_im, m_init = ctx if latent else (None,) * 5

    att = _attn_call(qn, kn, proj, layer, cache_k, cache_v, batch, seq_len)
    s5o, fin_re, fin_im = _s5_call(proj, layer, pp, init_re, init_im, batch, seq_len)
    ht, cn_f, m_f = _mlstm_call(mqk.reshape(batch, seq_len, 2 * M_WIDTH), vt, mg.reshape(batch, seq_len, LANES), mgt,
                                layer, pp, m_init, batch, seq_len)
    x_out = _tail_call(x2, att, s5o, ht, proj, layer, pp, seq_len, latent)
    if latent:
        return x_out, None
    states = (kn.reshape(batch, seq_len, ATT_KV_HEADS, ATT_HD),
              proj[:, AUX_V:].reshape(batch, seq_len, ATT_KV_HEADS, ATT_HD),
              fin_re.transpose(1, 0, 2).reshape(batch, N_DIR, S5_GROUPS, S5_STATE),
              fin_im.transpose(1, 0, 2).reshape(batch, N_DIR, S5_GROUPS, S5_STATE),
              cn_f[:, :, :M_HD, :].reshape(batch, N_DIR, M_HEADS, M_HD, M_HD),
              cn_f[:, :, M_HD, :].reshape(batch, N_DIR, M_HEADS, M_HD),
              m_f[:, :, 0, 0].reshape(batch, N_DIR, M_HEADS))
    return x_out, states


def _head_block(width, head_dim):
    ids = np.arange(width) // head_dim
    return jnp.asarray((ids[:, None] == ids[None, :]).astype(np.float32) / head_dim, dtype=BF16)


def kernel(x_prompt, x_sample, c, cache_attn_k, cache_attn_v, state_s5_re, state_s5_im, state_mlstm_C, state_mlstm_n, state_mlstm_m, c_ctx, ada_w, ada_b, norm1_g, norm2_g, w_in, q_norm_g, k_norm_g, s5_a_re, s5_a_im, s5_log_dt, s5_b_re, s5_b_im, s5_c_re, s5_c_im, s5_d, s5_glu_w, s5_glu_b, m_gate_b, m_norm_g, w_out, ffn_w_gate, ffn_w_up, ffn_conv_w, ffn_conv_b, ffn_w_down):
    batch, seq = x_prompt.shape[0], x_prompt.shape[1]
    dec_batch, dec_seq = x_sample.shape[0], x_sample.shape[1]

    n_mod_rows = 2 * SUBLANES
    cvec = jnp.zeros((n_mod_rows, D_MODEL), F32).at[0].set(c_ctx).at[1:1 + dec_batch].set(c)
    mod_all = _ada_call(cvec, ada_w, ada_b).reshape(DEPTH, n_mod_rows, 6, D_MODEL)

    ab_re, ab_im, bb_re, bb_im = _s5_disc_call(s5_a_re, s5_a_im, s5_log_dt, s5_b_re, s5_b_im)
    ab_re = ab_re.reshape(DEPTH, N_DIR, 1, S5_FLAT)
    ab_im = ab_im.reshape(DEPTH, N_DIR, 1, S5_FLAT)
    bb_re = bb_re.reshape(DEPTH, N_DIR, S5_GROUPS, S5_STATE, S5_CH)
    bb_im = bb_im.reshape(DEPTH, N_DIR, S5_GROUPS, S5_STATE, S5_CH)

    rope_tabs = _rope_tables(dec_seq)
    tri = np.tril(np.ones((M_TILE, M_TILE), np.float32))
    tril = jnp.asarray(tri, dtype=BF16)
    triu = jnp.asarray(tri.T, dtype=BF16)
    ones = jnp.ones((M_TILE, M_TILE), BF16)
    blkq = _head_block(ATT_WIDTH, ATT_HD)
    blkk = _head_block(ATT_KV_WIDTH, ATT_HD)
    blkm = _head_block(M_WIDTH, M_HD)

    w_in_b = _bf(w_in)
    gb = m_gate_b.reshape(DEPTH, M_GATE_COLS)
    pp = dict(
        mod=mod_all, g1=norm1_g.reshape(DEPTH, 1, D_MODEL), g2=norm2_g.reshape(DEPTH, 1, D_MODEL),
        w_in=w_in_b,
        w_gatecols=jnp.zeros((DEPTH, D_MODEL, LANES), BF16).at[:, :, :M_GATE_COLS].set(w_in_b[:, :, PROJ_MAIN:]),
        blkq=blkq, blkk=blkk, blkm=blkm,
        qg=jnp.tile(q_norm_g, (1, ATT_HEADS)).reshape(DEPTH, 1, ATT_WIDTH),
        kg=jnp.tile(k_norm_g, (1, ATT_KV_HEADS)).reshape(DEPTH, 1, ATT_KV_WIDTH),
        s5_bmat=_bf(jnp.concatenate([_block_diag(jnp.swapaxes(bb_re, -1, -2)),
                                     _block_diag(jnp.swapaxes(bb_im, -1, -2))], axis=-1)),
        s5_cre=_bf(_block_diag(jnp.swapaxes(s5_c_re, -1, -2))),
        s5_cim=_bf(_block_diag(jnp.swapaxes(s5_c_im, -1, -2))),
        s5_ab_re=jnp.broadcast_to(ab_re, (DEPTH, N_DIR, SUBLANES, S5_FLAT)),
        s5_ab_im=jnp.broadcast_to(ab_im, (DEPTH, N_DIR, SUBLANES, S5_FLAT)),
        s5_d=s5_d.reshape(DEPTH, 1, S5_WIDTH), s5_glu_w=_bf(s5_glu_w), s5_glu_b=s5_glu_b.reshape(DEPTH, 1, S5_WIDTH),
        m_gb_row=jnp.zeros((DEPTH, 1, LANES), F32).at[:, 0, :M_GATE_COLS].set(gb),
        m_gb_col=gb.reshape(DEPTH, M_GATE_COLS, 1),
        m_ng=jnp.tile(m_norm_g, (1, M_HEADS)).reshape(DEPTH, 1, M_WIDTH), tril=tril, triu=triu, ones=ones,
        w_out=_bf(w_out), ffn_w_gate=_bf(ffn_w_gate), ffn_w_up=_bf(ffn_w_up),
        ffn_conv_w=ffn_conv_w, ffn_conv_b=ffn_conv_b.reshape(DEPTH, 1, D_FF), ffn_w_down=_bf(ffn_w_down))

    past = cache_attn_k.shape[2]
    n_pad = CN_ROWS - M_HD
    s5_init = lambda st: st.reshape(dec_batch, DEPTH, N_DIR, S5_FLAT).transpose(1, 2, 0, 3)
    m_init = (state_mlstm_C.reshape(dec_batch, DEPTH, M_CHAINS, M_HD, M_HD),
              jnp.zeros((dec_batch, DEPTH, M_CHAINS, n_pad, M_HD), F32).at[:, :, :, 0, :].set(
                  state_mlstm_n.reshape(dec_batch, DEPTH, M_CHAINS, M_HD)),
              jnp.broadcast_to(state_mlstm_m.reshape(dec_batch, DEPTH, M_CHAINS, 1, 1),
                               (dec_batch, DEPTH, M_CHAINS, 1, LANES)))
    ctx = (cache_attn_k.reshape(dec_batch, DEPTH, past, ATT_KV_WIDTH),
           cache_attn_v.reshape(dec_batch, DEPTH, past, ATT_KV_WIDTH),
           s5_init(state_s5_re), s5_init(state_s5_im), m_init)

    xp = x_prompt.reshape(batch * seq, D_MODEL)
    xs = x_sample.reshape(dec_batch * dec_seq, D_MODEL)
    ctx_out = []
    for l in range(DEPTH):
        xp, st = _trunk_layer(xp, batch, seq, l, pp, None, None)
        ctx_out.append(st)
        xs, _ = _trunk_layer(xs, dec_batch, dec_seq, l, pp, ctx, rope_tabs)
    outs = [jnp.stack([s[i] for s in ctx_out], axis=1) for i in range(7)]
    return (xp.reshape(batch, seq, D_MODEL), xs.reshape(dec_batch, dec_seq, D_MODEL), *outs)
```

```python
import functools

import numpy as np
import jax
import jax.numpy as jnp
from jax import lax
from jax.experimental import pallas as pl
from jax.experimental.pallas import tpu as pltpu

F32 = jnp.float32
BF16 = jnp.bfloat16

D_MODEL = 1024
DEPTH = 2
GRID_W = 64
N_DIR = 2
EPS = 1e-6
ATT_HD = 64
ATT_WIDTH = 512
ATT_HEADS = 8
ATT_KV_HEADS = 2
ATT_GROUP = ATT_HEADS // ATT_KV_HEADS
ATT_KV_WIDTH = ATT_KV_HEADS * ATT_HD
ROPE_THETA = 10000.0
LOG2_E = 1.4426950408889634
S5_CH = 16
S5_STATE = 64
S5_WIDTH = 256
S5_GROUPS = 16
S5_FLAT = S5_GROUPS * S5_STATE
M_HD = 64
M_WIDTH = 256
M_HEADS = 4
M_GATE_COLS = 2 * N_DIR * M_HEADS
M_CHAINS = N_DIR * M_HEADS
D_FF = 2816
PROJ_MAIN = 2048
COL_K = ATT_WIDTH
COL_V = COL_K + ATT_KV_WIDTH
COL_S5 = COL_V + ATT_KV_WIDTH
COL_MQ = COL_S5 + S5_WIDTH
COL_MK = COL_MQ + M_WIDTH
COL_MV = COL_MK + M_WIDTH
COL_MO = COL_MV + M_WIDTH
AUX_S5 = 0
AUX_MO = AUX_S5 + S5_WIDTH
AUX_V = AUX_MO + M_WIDTH
AUX_WIDTH = AUX_V + ATT_KV_WIDTH
LANES = 128
SUBLANES = 8
BF16_ROWS = 16
VMEM_LIMIT = 56 * 1024 * 1024

ADA_TN = 1024
PRE_TM = 512
ATT_TQ = 512
S5_ROWS = 256
S5_SLOTS = 3
M_TILE = 128
M_BG = 4
M_BG_SEQ = 1024
CN_ROWS = M_HD + BF16_ROWS
TAIL_TM = 1024
TAIL_OUT_CHUNK = 256
TAIL_HALF = 512
FF_CHUNK = 256


def _bf(x):
    return x.astype(BF16)


def _dot(a, b):
    return jnp.dot(a, b, preferred_element_type=F32)


def _split(x):
    hi = _bf(x)
    lo = _bf(x - hi.astype(F32))
    return hi, lo


def _seg_mean_sq(x, blk):
    hi, lo = _split(x * x)
    return _dot(hi, blk) + _dot(lo, blk)


def _sigmoid(x):
    return 1.0 / (1.0 + jnp.exp(-x))


def _log_sigmoid(x):
    return -(jnp.maximum(-x, 0.0) + jnp.log1p(jnp.exp(-jnp.abs(x))))


def _params(*sem):
    return pltpu.CompilerParams(dimension_semantics=sem, vmem_limit_bytes=VMEM_LIMIT)


def _ada_kernel(c_ref, w_ref, b_ref, o_ref):
    c = c_ref[...]
    s = c * _sigmoid(c)
    o_ref[...] = _dot(_bf(s), _bf(w_ref[...])) + b_ref[...]


def _ada_call(cvec, ada_w, ada_b):
    rows = cvec.shape[0]
    tn = ADA_TN
    n = ada_w.shape[-1]
    return pl.pallas_call(
        _ada_kernel,
        grid=(DEPTH, n // tn),
        in_specs=[pl.BlockSpec((rows, D_MODEL), lambda l, j: (0, 0)),
                  pl.BlockSpec((None, D_MODEL, tn), lambda l, j: (l, 0, j)),
                  pl.BlockSpec((None, 1, tn), lambda l, j: (l, 0, j))],
        out_specs=pl.BlockSpec((None, rows, tn), lambda l, j: (l, 0, j)),
        out_shape=jax.ShapeDtypeStruct((DEPTH, rows, n), F32),
        compiler_params=_params("arbitrary", "arbitrary"),
        name="adaln",
    )(cvec, ada_w, ada_b.reshape(DEPTH, 1, n))


def _s5_disc_kernel(are_ref, aim_ref, ldt_ref, arex_ref, aimx_ref, bre_ref, bim_ref,
                    abr_ref, abi_ref, bbr_ref, bbi_ref):
    dt = jnp.exp(ldt_ref[...])

    def disc(a_re, a_im):
        mag = jnp.exp(dt * a_re)
        ab_re = mag * jnp.cos(dt * a_im)
        ab_im = mag * jnp.sin(dt * a_im)
        den = a_re * a_re + a_im * a_im
        nr = ab_re - 1.0
        ni = ab_im
        f_re = (nr * a_re + ni * a_im) / den
        f_im = (ni * a_re - nr * a_im) / den
        return ab_re, ab_im, f_re, f_im

    ab_re, ab_im, _, _ = disc(are_ref[...], aim_ref[...])
    abr_ref[...] = ab_re
    abi_ref[...] = ab_im
    _, _, f_re, f_im = disc(arex_ref[...], aimx_ref[...])
    b_re = bre_ref[...]
    b_im = bim_ref[...]
    bbr_ref[...] = f_re * b_re - f_im * b_im
    bbi_ref[...] = f_re * b_im + f_im * b_re


def _s5_disc_call(a_re, a_im, log_dt, b_re, b_im):
    r = DEPTH * N_DIR * S5_GROUPS
    a_re2 = a_re.reshape(r, S5_STATE)
    a_im2 = a_im.reshape(r, S5_STATE)
    wide = S5_STATE * S5_CH
    out_shape = (jax.ShapeDtypeStruct((r, S5_STATE), F32), jax.ShapeDtypeStruct((r, S5_STATE), F32),
                 jax.ShapeDtypeStruct((r, wide), F32), jax.ShapeDtypeStruct((r, wide), F32))
    return pl.pallas_call(_s5_disc_kernel, out_shape=out_shape, name="s5_disc")(
        a_re2, a_im2, log_dt.reshape(r, 1),
        jnp.repeat(a_re2, S5_CH, axis=1), jnp.repeat(a_im2, S5_CH, axis=1),
        b_re.reshape(r, wide), b_im.reshape(r, wide))


def _block_diag(blocks):
    g, r, c = blocks.shape[-3:]
    lead = blocks.shape[:-3]
    tiled = jnp.tile(blocks.reshape(lead + (g * r, c)), (1,) * len(lead) + (1, g))
    on_diagonal = (np.arange(g * r)[:, None] // r) == (np.arange(g * c)[None, :] // c)
    return jnp.where(on_diagonal, tiled, 0.0)


def _rope(x, cos, sin_signed, second):
    width = x.shape[-1]
    quarter = ATT_HD // 4
    partner = jnp.where(second, pltpu.roll(x, quarter, axis=1), pltpu.roll(x, width - quarter, axis=1))
    return x * cos + partner * sin_signed


def _pre_kernel(*refs, rope):
    if rope:
        (x_ref, mod_ref, g1_ref, w_ref, wg_ref, blkq_ref, blkk_ref, qg_ref, kg_ref,
         cosq_ref, sinq_ref, cosk_ref, sink_ref, proj_ref, mg_ref, qn_ref, kn_ref, mqk_ref, vt_ref, mgt_ref) = refs
    else:
        (x_ref, mod_ref, g1_ref, w_ref, wg_ref, blkq_ref, blkk_ref, qg_ref, kg_ref,
         proj_ref, mg_ref, qn_ref, kn_ref, mqk_ref, vt_ref, mgt_ref) = refs
    x = x_ref[...]
    ms = jnp.mean(x * x, axis=-1, keepdims=True)
    xn = x * lax.rsqrt(ms + EPS) * g1_ref[...]
    h = _bf(xn * (1.0 + mod_ref[1:2, :]) + mod_ref[0:1, :])
    def proj(c0, c1):
        return _dot(h, w_ref[:, c0:c1])

    q = proj(0, COL_K)
    kvs = proj(COL_K, COL_MQ)
    k = kvs[:, :ATT_KV_WIDTH]
    mqk = proj(COL_MQ, COL_MV)
    mvo = proj(COL_MV, PROJ_MAIN)
    proj_ref[:, AUX_S5:AUX_MO] = kvs[:, 2 * ATT_KV_WIDTH:]
    proj_ref[:, AUX_MO:AUX_V] = mvo[:, M_WIDTH:]
    proj_ref[:, AUX_V:] = kvs[:, ATT_KV_WIDTH:2 * ATT_KV_WIDTH]
    mg = _dot(h, wg_ref[...])
    mg_ref[...] = mg
    mqk_ref[:, :M_WIDTH] = _bf(mqk[:, :M_WIDTH])
    mqk_ref[:, M_WIDTH:] = _bf(mqk[:, M_WIDTH:] * (M_HD ** -0.5))
    vt_ref[...] = mvo[:, :M_WIDTH].T
    mgt_ref[...] = mg.T[:M_GATE_COLS, :]
    qn = q * lax.rsqrt(_seg_mean_sq(q, blkq_ref[...]) + EPS) * qg_ref[...]
    kn = k * lax.rsqrt(_seg_mean_sq(k, blkk_ref[...]) + EPS) * kg_ref[...]
    if rope:
        lane_q = lax.broadcasted_iota(jnp.int32, qn.shape, 1)
        lane_k = lax.broadcasted_iota(jnp.int32, kn.shape, 1)
        qn = _rope(qn, cosq_ref[...], sinq_ref[...], (lane_q & (ATT_HD // 4)) != 0)
        kn = _rope(kn, cosk_ref[...], sink_ref[...], (lane_k & (ATT_HD // 4)) != 0)
    qn_ref[...] = qn
    kn_ref[...] = kn


def _rope_tables(seq_len):
    n_rows = seq_len // GRID_W
    row = np.repeat(np.arange(n_rows), GRID_W)
    col = np.tile(np.arange(GRID_W), n_rows)
    half = ATT_HD // 2
    inv_freq = (1.0 / (np.float32(ROPE_THETA) ** (np.arange(0, half, 2, dtype=np.float32) / half))).astype(np.float32)

    def tables(pos):
        ang = pos.astype(np.float32)[:, None] * inv_freq[None, :]
        cos = np.cos(ang)
        sin = np.sin(ang)
        return np.concatenate([cos, cos], axis=-1), np.concatenate([-sin, sin], axis=-1)

    cr, sr = tables(row)
    cc, sc = tables(col)
    cos = np.concatenate([cr, cc], axis=-1)
    sin = np.concatenate([sr, sc], axis=-1)
    return tuple(jnp.asarray(np.tile(t, (1, n)), dtype=F32)
                 for t, n in ((cos, ATT_HEADS), (sin, ATT_HEADS), (cos, ATT_KV_HEADS), (sin, ATT_KV_HEADS)))


def _layered(layer, block, index_map, **kw):
    return pl.BlockSpec((None,) + tuple(block), lambda *g: (layer,) + tuple(index_map(*g)), **kw)


def _mod_spec(layer, row_of):
    return pl.BlockSpec((None, None, 6, D_MODEL), lambda *g: (layer, row_of(*g), 0, 0))


def _pre_call(x2, layer, pp, rope_tabs, seq_len, per_batch_mod):
    t = x2.shape[0]
    tm = min(PRE_TM, seq_len)
    tiles_per_seq = seq_len // tm
    rope = rope_tabs is not None
    if per_batch_mod:
        mod_row = lambda i: 1 + i // tiles_per_seq
    else:
        mod_row = lambda i: 0
    const = lambda i: (0, 0)
    row = lambda i: (i, 0)
    in_specs = [pl.BlockSpec((tm, D_MODEL), row),
                _mod_spec(layer, mod_row),
                _layered(layer, (1, D_MODEL), const),
                _layered(layer, (D_MODEL, PROJ_MAIN), const),
                _layered(layer, (D_MODEL, LANES), const),
                pl.BlockSpec((ATT_WIDTH, ATT_WIDTH), const),
                pl.BlockSpec((ATT_KV_WIDTH, ATT_KV_WIDTH), const),
                _layered(layer, (1, ATT_WIDTH), const),
                _layered(layer, (1, ATT_KV_WIDTH), const)]
    args = [x2, pp['mod'], pp['g1'], pp['w_in'], pp['w_gatecols'], pp['blkq'], pp['blkk'], pp['qg'], pp['kg']]
    if rope:
        pos_map = lambda i: (i % tiles_per_seq, 0)
        in_specs += [pl.BlockSpec((tm, ATT_WIDTH), pos_map), pl.BlockSpec((tm, ATT_WIDTH), pos_map),
                     pl.BlockSpec((tm, ATT_KV_WIDTH), pos_map), pl.BlockSpec((tm, ATT_KV_WIDTH), pos_map)]
        args += list(rope_tabs)
    time_on_lanes = lambda i: (i // tiles_per_seq, 0, i % tiles_per_seq)
    batch = t // seq_len
    out_specs = [pl.BlockSpec((tm, AUX_WIDTH), row),
                 pl.BlockSpec((tm, LANES), row),
                 pl.BlockSpec((tm, ATT_WIDTH), row),
                 pl.BlockSpec((tm, ATT_KV_WIDTH), row),
                 pl.BlockSpec((tm, 2 * M_WIDTH), row),
                 pl.BlockSpec((None, M_WIDTH, tm), time_on_lanes),
                 pl.BlockSpec((None, M_GATE_COLS, tm), time_on_lanes)]
    out_shape = [jax.ShapeDtypeStruct((t, AUX_WIDTH), F32), jax.ShapeDtypeStruct((t, LANES), F32),
                 jax.ShapeDtypeStruct((t, ATT_WIDTH), F32), jax.ShapeDtypeStruct((t, ATT_KV_WIDTH), F32),
                 jax.ShapeDtypeStruct((t, 2 * M_WIDTH), BF16),
                 jax.ShapeDtypeStruct((batch, M_WIDTH, seq_len), F32),
                 jax.ShapeDtypeStruct((batch, M_GATE_COLS, seq_len), F32)]
    return pl.pallas_call(
        functools.partial(_pre_kernel, rope=rope),
        grid=(t // tm,), in_specs=in_specs, out_specs=out_specs, out_shape=out_shape,
        compiler_params=_params("arbitrary"), name="pre_rope" if rope else "pre",
    )(*args)


def _attn_kernel(*refs, cached):
    if cached:
        q_ref, k_ref, v_ref, ck_ref, cv_ref, o_ref = refs
        k_all = _bf(jnp.concatenate([ck_ref[...], k_ref[...]], axis=0))
        v_all = _bf(jnp.concatenate([cv_ref[...], v_ref[...]], axis=0))
    else:
        q_ref, k_ref, v_ref, o_ref = refs
        k_all = _bf(k_ref[...])
        v_all = _bf(v_ref[...])
    scale = ATT_HD ** -0.5 * LOG2_E
    outs = []
    for kvh in range(ATT_KV_HEADS):
        lo = kvh * ATT_HD
        kb = k_all[:, lo:lo + ATT_HD]
        vb = v_all[:, lo:lo + ATT_HD]
        for g in range(ATT_GROUP):
            c0 = (kvh * ATT_GROUP + g) * ATT_HD
            qb = _bf(q_ref[:, c0:c0 + ATT_HD] * scale)
            s = lax.dot_general(qb, kb, (((1,), (1,)), ((), ())), preferred_element_type=F32)
            m = jnp.max(s, axis=-1, keepdims=True)
            e = jnp.exp2(s - m)
            den = jnp.sum(e, axis=-1, keepdims=True)
            outs.append(_dot(_bf(e), vb) / den)
    o_ref[...] = _bf(jnp.concatenate(outs, axis=-1))


def _attn_call(qn, kn, proj, layer, cache_k, cache_v, batch, lq):
    cached = cache_k is not None
    tq = min(ATT_TQ, lq)
    nq = lq // tq
    in_specs = [pl.BlockSpec((tq, ATT_WIDTH), lambda b, i: (b * nq + i, 0)),
                pl.BlockSpec((lq, ATT_KV_WIDTH), lambda b, i: (b, 0)),
                pl.BlockSpec((lq, ATT_KV_WIDTH), lambda b, i: (b, AUX_V // ATT_KV_WIDTH))]
    args = [qn, kn, proj]
    if cached:
        past = cache_k.shape[2]
        cache_spec = pl.BlockSpec((None, None, past, ATT_KV_WIDTH), lambda b, i: (b, layer, 0, 0))
        in_specs += [cache_spec, cache_spec]
        args += [cache_k, cache_v]
    return pl.pallas_call(
        functools.partial(_attn_kernel, cached=cached),
        grid=(batch, nq), in_specs=in_specs,
        out_specs=pl.BlockSpec((tq, ATT_WIDTH), lambda b, i: (b * nq + i, 0)),
        out_shape=jax.ShapeDtypeStruct((batch * lq, ATT_WIDTH), BF16),
        compiler_params=_params("arbitrary", "arbitrary"), name="attn_cached" if cached else "attn",
    )(*args)


def _s5_kernel(*refs, batch, seq_len, zero_init):
    if zero_init:
        (u_ref, bmat_ref, cre_ref, cim_ref, ar_ref, ai_ref, d_ref, gw_ref, gb_ref,
         out_ref, fr_ref, fi_ref, utb_ref, ytb_ref, xs_ref) = refs
        ir_ref = ii_ref = None
    else:
        (u_ref, bmat_ref, cre_ref, cim_ref, ar_ref, ai_ref, d_ref, gw_ref, gb_ref, ir_ref, ii_ref,
         out_ref, fr_ref, fi_ref, utb_ref, ytb_ref, xs_ref) = refs
    rows = S5_ROWS
    tc = rows // batch
    n_chunks = seq_len // tc
    n_sub = batch // SUBLANES
    halves = S5_WIDTH // LANES

    def aligned(x, m):
        return x if isinstance(x, int) else pl.multiple_of(x, m)

    for b in range(batch):
        for hf in range(halves):
            utb_ref[hf, pl.ds(b, seq_len, stride=batch), :] = (
                u_ref[b * seq_len:(b + 1) * seq_len, hf * LANES:(hf + 1) * LANES])
    for hf in range(halves):
        ytb_ref[hf] = utb_ref[hf] * d_ref[:, hf * LANES:(hf + 1) * LANES]

    def chunk_of(d, i):
        return i if d == 0 else n_chunks - 1 - i

    def stage_in(i, slot):
        for d in range(N_DIR):
            r0 = aligned(chunk_of(d, i) * rows, rows)
            u_c = jnp.concatenate([utb_ref[hf, pl.ds(r0, rows), :] for hf in range(halves)], axis=-1)
            xs_ref[slot, d] = _dot(_bf(u_c), bmat_ref[d])

    def stage_scan(slot, carry):
        new_carry = []
        for d in range(N_DIR):
            a_re = ar_ref[d]
            a_im = ai_ref[d]
            per_sub = []
            for sub in range(n_sub):
                s_re, s_im = carry[d][sub]
                for t in range(tc):
                    r = (t if d == 0 else tc - 1 - t) * batch + sub * SUBLANES
                    x_re = xs_ref[slot, d, r:r + SUBLANES, :S5_FLAT]
                    x_im = xs_ref[slot, d, r:r + SUBLANES, S5_FLAT:]
                    s_re, s_im = (a_re * s_re - a_im * s_im + x_re, a_re * s_im + a_im * s_re + x_im)
                    xs_ref[slot, d, r:r + SUBLANES, :S5_FLAT] = s_re
                    xs_ref[slot, d, r:r + SUBLANES, S5_FLAT:] = s_im
                per_sub.append((s_re, s_im))
            new_carry.append(tuple(per_sub))
        return tuple(new_carry)

    def stage_out(i, slot):
        for d in range(N_DIR):
            r0 = aligned(chunk_of(d, i) * rows, rows)
            y = (_dot(_bf(xs_ref[slot, d, :, :S5_FLAT]), cre_ref[d])
                 - _dot(_bf(xs_ref[slot, d, :, S5_FLAT:]), cim_ref[d]))
            for hf in range(halves):
                ytb_ref[hf, pl.ds(r0, rows), :] += y[:, hf * LANES:(hf + 1) * LANES]

    def start_state(ref, d, sub):
        if zero_init:
            return jnp.zeros((SUBLANES, S5_FLAT), F32)
        return ref[d, sub * SUBLANES:(sub + 1) * SUBLANES, :]

    carry = tuple(tuple((start_state(ir_ref, d, sub), start_state(ii_ref, d, sub)) for sub in range(n_sub))
                  for d in range(N_DIR))
    def step(i, phase, carry):
        stage_in(i + 1, (phase + 1) % S5_SLOTS)
        carry = stage_scan(phase, carry)
        stage_out(i - 1, (phase - 1) % S5_SLOTS)
        return carry

    stage_in(0, 0)
    stage_in(1, 1)
    carry = stage_scan(0, carry)
    n_steady = n_chunks - 2
    n_peeled = n_steady % S5_SLOTS
    for i in range(1, 1 + n_peeled):
        carry = step(i, i % S5_SLOTS, carry)
    first = 1 + n_peeled

    def body(g, carry):
        for k in range(S5_SLOTS):
            carry = step(first + g * S5_SLOTS + k, (first + k) % S5_SLOTS, carry)
        return carry

    carry = lax.fori_loop(0, n_steady // S5_SLOTS, body, carry)
    last = n_chunks - 1
    carry = stage_scan(last % S5_SLOTS, carry)
    stage_out(last - 1, (last - 1) % S5_SLOTS)
    stage_out(last, last % S5_SLOTS)
    for d in range(N_DIR):
        for sub in range(n_sub):
            fr_ref[d, sub * SUBLANES:(sub + 1) * SUBLANES, :] = carry[d][sub][0]
            fi_ref[d, sub * SUBLANES:(sub + 1) * SUBLANES, :] = carry[d][sub][1]

    for b in range(batch):
        z = jax.nn.gelu(jnp.concatenate(
            [ytb_ref[hf, pl.ds(b, seq_len, stride=batch), :] for hf in range(halves)], axis=-1))
        gate = _sigmoid(_dot(_bf(z), gw_ref[...]) + gb_ref[...])
        out_ref[b * seq_len:(b + 1) * seq_len, :] = _bf(z * gate)


def _s5_call(proj, layer, pp, init_re, init_im, batch, seq_len):
    rows = seq_len * batch
    single = pl.Buffered(1)
    zero_init = init_re is None
    names = ['s5_bmat', 's5_cre', 's5_cim', 's5_ab_re', 's5_ab_im', 's5_d', 's5_glu_w', 's5_glu_b']
    args = [pp[n] for n in names] + ([] if zero_init else [init_re, init_im])
    whole = lambda a: _layered(layer, a.shape[1:], lambda i: (0,) * (a.ndim - 1), pipeline_mode=single)
    in_specs = ([pl.BlockSpec((rows, S5_WIDTH), lambda i: (0, AUX_S5 // S5_WIDTH), pipeline_mode=single)]
                + [whole(a) for a in args])
    fin_spec = pl.BlockSpec((N_DIR, batch, S5_FLAT), lambda i: (0, 0, 0))
    out_specs = [pl.BlockSpec((rows, S5_WIDTH), lambda i: (0, 0), pipeline_mode=single), fin_spec, fin_spec]
    out_shape = (jax.ShapeDtypeStruct((rows, S5_WIDTH), BF16),
                 jax.ShapeDtypeStruct((N_DIR, batch, S5_FLAT), F32),
                 jax.ShapeDtypeStruct((N_DIR, batch, S5_FLAT), F32))
    return pl.pallas_call(
        functools.partial(_s5_kernel, batch=batch, seq_len=seq_len, zero_init=zero_init),
        grid=(1,), in_specs=in_specs, out_specs=out_specs, out_shape=out_shape,
        scratch_shapes=[pltpu.VMEM((S5_WIDTH // LANES, rows, LANES), F32),
                        pltpu.VMEM((S5_WIDTH // LANES, rows, LANES), F32),
                        pltpu.VMEM((S5_SLOTS, N_DIR, S5_ROWS, 2 * S5_FLAT), F32)],
        compiler_params=_params("arbitrary"), name="s5",
    )(proj, *args)


def _mlstm_kernel(*refs, seq_len, zero_init):
    if zero_init:
        (q_ref, k_ref, vt_ref, mg_ref, mgt_ref, gbr_ref, gbc_ref, tril_ref, triu_ref, ones_ref,
         ht_ref, cn_ref, m_ref) = refs
    else:
        (q_ref, k_ref, vt_ref, mg_ref, mgt_ref, gbr_ref, gbc_ref, tril_ref, triu_ref, ones_ref,
         c0_ref, n0_ref, m0_ref, ht_ref, cn_ref, m_ref) = refs
    _mlstm_body(q_ref, k_ref, vt_ref, mg_ref, mgt_ref, gbr_ref, gbc_ref, tril_ref, triu_ref, ones_ref,
                None if zero_init else (c0_ref, n0_ref, m0_ref), ht_ref, cn_ref, m_ref, seq_len)


def _mlstm_body(q_ref, k_ref, vt_ref, mg_ref, mgt_ref, gbr_ref, gbc_ref, tril_ref, triu_ref, ones_ref,
                init_refs, ht_ref, cn_ref, m_ref, seq_len):
    tile = M_TILE
    n_chunks = seq_len // tile
    bg = q_ref.shape[0]
    if init_refs is None:
        cn_ref[...] = jnp.zeros_like(cn_ref)
        m_ref[...] = jnp.zeros_like(m_ref)
    else:
        c0_ref, n0_ref, m0_ref = init_refs
        cn_ref[:, :, :M_HD, :] = c0_ref[...]
        cn_ref[:, :, M_HD:, :] = n0_ref[...]
        m_ref[...] = m0_ref[...]
    s_ids = lax.broadcasted_iota(jnp.int32, (tile, tile), 0)
    t_ids = lax.broadcasted_iota(jnp.int32, (tile, tile), 1)
    first_row = lax.broadcasted_iota(jnp.int32, (CN_ROWS - M_HD, tile), 0) == 0
    neg_inf = jnp.float32(-jnp.inf)
    ones = ones_ref[...]

    def direction(d, c):
        t0 = pl.multiple_of(c * tile, tile)
        col_mat = tril_ref[...] if d == 0 else triu_ref[...]
        row_mat = triu_ref[...] if d == 0 else tril_ref[...]
        valid = (s_ids <= t_ids) if d == 0 else (s_ids >= t_ids)
        gt = mgt_ref[:, :, pl.ds(t0, tile)] + gbc_ref[...]
        lft_hi, lft_lo = _split(_log_sigmoid(gt).reshape(bg * M_GATE_COLS, tile))
        b_rows = (_dot(lft_hi, row_mat) + _dot(lft_lo, row_mat)).reshape(bg, M_GATE_COLS, tile)
        totals = (_dot(lft_hi, ones) + _dot(lft_lo, ones)).reshape(bg, M_GATE_COLS, tile)
        gc = mg_ref[:, pl.ds(t0, tile), :] + gbr_ref[...]
        lfc_hi, lfc_lo = _split(_log_sigmoid(gc))
        col_b = jnp.broadcast_to(col_mat, (bg, tile, tile))
        b_cols = (jnp.einsum('bts,bsl->btl', col_b, lfc_hi, preferred_element_type=F32)
                  + jnp.einsum('bts,bsl->btl', col_b, lfc_lo, preferred_element_type=F32))
        for h in range(M_HEADS):
            j = d * M_HEADS + h
            fj = M_CHAINS + j
            rows = slice(h * M_HD, (h + 1) * M_HD)
            q_c = q_ref[:, pl.ds(t0, tile), rows]
            k_c = k_ref[:, pl.ds(t0, tile), rows]
            vt_c = vt_ref[:, rows, pl.ds(t0, tile)]
            b_t = b_rows[:, fj:fj + 1, :]
            li_t = gt[:, j:j + 1, :]
            tot = totals[:, fj:fj + 1, :]
            m_prev = m_ref[:, j]
            cn_prev = cn_ref[:, j]

            c_col = gc[:, :, j:j + 1] - b_cols[:, :, fj:fj + 1]
            cm = jnp.where(valid, c_col, neg_inf)
            inter = b_t + m_prev
            m_t = jnp.maximum(inter, b_t + jnp.max(cm, axis=1, keepdims=True))
            w_t = jnp.exp(cm + (b_t - m_t))
            s_inter = jnp.exp(inter - m_t)
            sc_t = jnp.einsum('bse,bte->bst', k_c, q_c, preferred_element_type=F32) * w_t
            num = jnp.einsum('bds,bst->bdt', _bf(vt_c), _bf(sc_t), preferred_element_type=F32)
            ext = jnp.einsum('bre,bte->brt', _bf(cn_prev), q_c, preferred_element_type=F32)
            num = num + s_inter * ext[:, :M_HD, :]
            den = jnp.sum(sc_t, axis=1, keepdims=True) + s_inter * ext[:, M_HD:M_HD + 1, :]
            ht_ref[d, :, rows, pl.ds(t0, tile)] = num / jnp.maximum(jnp.abs(den), jnp.exp(-m_t))

            g_row = tot - b_t + li_t
            m_new = jnp.maximum(tot + m_prev, jnp.max(g_row, axis=-1, keepdims=True))
            wk = jnp.exp(g_row - m_new)
            decay = jnp.exp(tot + m_prev - m_new)
            vw = jnp.concatenate([vt_c * wk, jnp.where(first_row, wk, 0.0)], axis=1)
            cn_ref[:, j] = (decay[:, :, :M_HD] * cn_prev
                            + jnp.einsum('brs,bse->bre', _bf(vw), k_c, preferred_element_type=F32))
            m_ref[:, j] = m_new

    def body(i, _):
        direction(0, i)
        direction(1, n_chunks - 1 - i)
        return 0

    lax.fori_loop(0, n_chunks, body, 0)


def _mlstm_call(mqk3, vt, mg3, mgt, layer, pp, init, batch, seq_len):
    bg = min(batch // 2, M_BG * M_BG_SEQ // seq_len)
    const2 = lambda g: (0, 0)
    lead3 = lambda g: (g, 0, 0)
    lead4 = lambda g: (g, 0, 0, 0)
    in_specs = [pl.BlockSpec((bg, seq_len, M_WIDTH), lead3),
                pl.BlockSpec((bg, seq_len, M_WIDTH), lambda g: (g, 0, 1)),
                pl.BlockSpec((bg, M_WIDTH, seq_len), lead3),
                pl.BlockSpec((bg, seq_len, LANES), lead3),
                pl.BlockSpec((bg, M_GATE_COLS, seq_len), lead3),
                _layered(layer, (1, LANES), const2),
                _layered(layer, (M_GATE_COLS, 1), const2),
                pl.BlockSpec((M_TILE, M_TILE), const2),
                pl.BlockSpec((M_TILE, M_TILE), const2),
                pl.BlockSpec((M_TILE, M_TILE), const2)]
    args = [mqk3, mqk3, vt, mg3, mgt, pp['m_gb_row'], pp['m_gb_col'], pp['tril'], pp['triu'], pp['ones']]
    if init is not None:
        state_spec = lambda *tail: pl.BlockSpec((bg, None, M_CHAINS) + tail, lambda g: (g, layer, 0, 0, 0))
        in_specs += [state_spec(M_HD, M_HD), state_spec(CN_ROWS - M_HD, M_HD), state_spec(1, LANES)]
        args += list(init)
    out_specs = [pl.BlockSpec((N_DIR, bg, M_WIDTH, seq_len), lambda g: (0, g, 0, 0)),
                 pl.BlockSpec((bg, M_CHAINS, CN_ROWS, M_HD), lead4),
                 pl.BlockSpec((bg, M_CHAINS, 1, LANES), lead4)]
    out_shape = [jax.ShapeDtypeStruct((N_DIR, batch, M_WIDTH, seq_len), F32),
                 jax.ShapeDtypeStruct((batch, M_CHAINS, CN_ROWS, M_HD), F32),
                 jax.ShapeDtypeStruct((batch, M_CHAINS, 1, LANES), F32)]
    return pl.pallas_call(
        functools.partial(_mlstm_kernel, seq_len=seq_len, zero_init=init is None),
        grid=(batch // bg,), in_specs=in_specs, out_specs=out_specs, out_shape=out_shape,
        compiler_params=_params("arbitrary"), name="mlstm",
    )(*args)


def _tail_kernel(x_ref, att_ref, s5_ref, ht_ref, mo_ref, mod_ref, g2_ref, ng_ref, blkm_ref, wo_ref,
                 wg_ref, wu_ref, cw_ref, cb_ref, wd_ref, o_ref, h2_ref, gated_ref, *, seq_len):
    tm = x_ref.shape[0]
    mh = jnp.concatenate([(ht_ref[0, s] + ht_ref[1, s]).T for s in range(ht_ref.shape[1])], axis=0)
    ml = mh * lax.rsqrt(_seg_mean_sq(mh, blkm_ref[...]) + EPS) * ng_ref[...] * _sigmoid(mo_ref[...])
    mixers = jnp.concatenate([att_ref[...], s5_ref[...], _bf(ml)], axis=-1)
    for c0 in range(0, D_MODEL, TAIL_OUT_CHUNK):
        cols = slice(c0, c0 + TAIL_OUT_CHUNK)
        o_ref[:, cols] = x_ref[:, cols] + mod_ref[2:3, cols] * _dot(mixers, wo_ref[:, cols])
    x1 = o_ref[...]
    ms = jnp.mean(x1 * x1, axis=-1, keepdims=True)
    xn = x1 * lax.rsqrt(ms + EPS) * g2_ref[...]
    h2_ref[...] = _bf(xn * (1.0 + mod_ref[4:5, :]) + mod_ref[3:4, :])

    half = TAIL_HALF
    row = lax.broadcasted_iota(jnp.int32, (half, FF_CHUNK), 0)
    half_start = row == 0
    half_end = row == half - 1
    for r0 in range(0, tm, half):
        pos = (r0 + row) % seq_len
        seq_start = pos == 0
        seq_end = pos == seq_len - 1
        h2 = h2_ref[r0:r0 + half, :]
        p0 = max(r0 - BF16_ROWS, 0)
        n0 = min(r0 + half, tm - BF16_ROWS)
        h2_ext = jnp.concatenate([h2, h2_ref[p0:p0 + BF16_ROWS, :], h2_ref[n0:n0 + BF16_ROWS, :]], axis=0)
        for j in range(D_FF // FF_CHUNK):
            cols = slice(j * FF_CHUNK, (j + 1) * FF_CHUNK)
            a_ext = _dot(h2_ext, wg_ref[:, cols])
            a = a_ext[:half]
            before = a_ext[half + BF16_ROWS - 1:half + BF16_ROWS, :]
            after = a_ext[half + BF16_ROWS:half + BF16_ROWS + 1, :]
            a_prev = jnp.where(half_start, before, pltpu.roll(a, 1, axis=0))
            a_next = jnp.where(half_end, after, pltpu.roll(a, half - 1, axis=0))
            a_prev = jnp.where(seq_start, 0.0, a_prev)
            a_next = jnp.where(seq_end, 0.0, a_next)
            ac = a_prev * cw_ref[0:1, cols] + a * cw_ref[1:2, cols] + a_next * cw_ref[2:3, cols] + cb_ref[:, cols]
            up = _dot(h2, wu_ref[:, cols])
            gated_ref[:, cols] = _bf(ac * _sigmoid(ac) * up)
        o_ref[r0:r0 + half, :] = o_ref[r0:r0 + half, :] + mod_ref[5:6, :] * _dot(gated_ref[...], wd_ref[...])


def _tail_call(x2, att, s5o, ht, proj, layer, pp, seq_len, per_batch_mod):
    t = x2.shape[0]
    tm = TAIL_TM
    seqs_per_tile = tm // seq_len
    if per_batch_mod:
        mod_row = lambda i: 1 + i * seqs_per_tile
    else:
        mod_row = lambda i: 0
    row = lambda i: (i, 0)
    const = lambda i: (0, 0)
    single = pl.Buffered(1)
    whole = lambda shape: _layered(layer, shape, const, pipeline_mode=single)
    in_specs = [pl.BlockSpec((tm, D_MODEL), row),
                pl.BlockSpec((tm, ATT_WIDTH), row),
                pl.BlockSpec((tm, S5_WIDTH), row),
                pl.BlockSpec((N_DIR, seqs_per_tile, M_WIDTH, seq_len), lambda i: (0, i, 0, 0)),
                pl.BlockSpec((tm, M_WIDTH), lambda i: (i, AUX_MO // M_WIDTH)),
                _mod_spec(layer, mod_row),
                _layered(layer, (1, D_MODEL), const),
                _layered(layer, (1, M_WIDTH), const),
                pl.BlockSpec((M_WIDTH, M_WIDTH), const),
                whole((D_MODEL, D_MODEL)),
                whole((D_MODEL, D_FF)), whole((D_MODEL, D_FF)), whole((3, D_FF)), whole((1, D_FF)),
                whole((D_FF, D_MODEL))]
    return pl.pallas_call(
        functools.partial(_tail_kernel, seq_len=seq_len),
        grid=(t // tm,), in_specs=in_specs,
        out_specs=pl.BlockSpec((tm, D_MODEL), row),
        out_shape=jax.ShapeDtypeStruct((t, D_MODEL), F32),
        scratch_shapes=[pltpu.VMEM((tm, D_MODEL), BF16), pltpu.VMEM((TAIL_HALF, D_FF), BF16)],
        compiler_params=_params("arbitrary"), name="tail",
    )(x2, att, s5o, ht, proj, pp['mod'], pp['g2'], pp['m_ng'], pp['blkm'], pp['w_out'],
      pp['ffn_w_gate'], pp['ffn_w_up'], pp['ffn_conv_w'], pp['ffn_conv_b'], pp['ffn_w_down'])


def _trunk_layer(x2, batch, seq_len, layer, pp, ctx, rope_tabs):
    latent = ctx is not None
    proj, mg, qn, kn, mqk, vt, mgt = _pre_call(x2, layer, pp, rope_tabs if latent else None, seq_len, latent)
    cache_k, cache_v, init_re, init_im, m_init = ctx if latent else (None,) * 5

    att = _attn_call(qn, kn, proj, layer, cache_k, cache_v, batch, seq_len)
    s5o, fin_re, fin_im = _s5_call(proj, layer, pp, init_re, init_im, batch, seq_len)
    ht, cn_f, m_f = _mlstm_call(mqk.reshape(batch, seq_len, 2 * M_WIDTH), vt, mg.reshape(batch, seq_len, LANES), mgt,
                                layer, pp, m_init, batch, seq_len)
    x_out = _tail_call(x2, att, s5o, ht, proj, layer, pp, seq_len, latent)
    if latent:
        return x_out, None
    states = (kn.reshape(batch, seq_len, ATT_KV_HEADS, ATT_HD),
              proj[:, AUX_V:].reshape(batch, seq_len, ATT_KV_HEADS, ATT_HD),
              fin_re.transpose(1, 0, 2).reshape(batch, N_DIR, S5_GROUPS, S5_STATE),
              fin_im.transpose(1, 0, 2).reshape(batch, N_DIR, S5_GROUPS, S5_STATE),
              cn_f[:, :, :M_HD, :].reshape(batch, N_DIR, M_HEADS, M_HD, M_HD),
              cn_f[:, :, M_HD, :].reshape(batch, N_DIR, M_HEADS, M_HD),
              m_f[:, :, 0, 0].reshape(batch, N_DIR, M_HEADS))
    return x_out, states


def _head_block(width, head_dim):
    ids = np.arange(width) // head_dim
    return jnp.asarray((ids[:, None] == ids[None, :]).astype(np.float32) / head_dim, dtype=BF16)


def kernel(x_prompt, x_sample, c, cache_attn_k, cache_attn_v, state_s5_re, state_s5_im, state_mlstm_C, state_mlstm_n, state_mlstm_m, c_ctx, ada_w, ada_b, norm1_g, norm2_g, w_in, q_norm_g, k_norm_g, s5_a_re, s5_a_im, s5_log_dt, s5_b_re, s5_b_im, s5_c_re, s5_c_im, s5_d, s5_glu_w, s5_glu_b, m_gate_b, m_norm_g, w_out, ffn_w_gate, ffn_w_up, ffn_conv_w, ffn_conv_b, ffn_w_down):
    batch, seq = x_prompt.shape[0], x_prompt.shape[1]
    dec_batch, dec_seq = x_sample.shape[0], x_sample.shape[1]

    n_mod_rows = 2 * SUBLANES
    cvec = jnp.zeros((n_mod_rows, D_MODEL), F32).at[0].set(c_ctx).at[1:1 + dec_batch].set(c)
    mod_all = _ada_call(cvec, ada_w, ada_b).reshape(DEPTH, n_mod_rows, 6, D_MODEL)

    ab_re, ab_im, bb_re, bb_im = _s5_disc_call(s5_a_re, s5_a_im, s5_log_dt, s5_b_re, s5_b_im)
    ab_re = ab_re.reshape(DEPTH, N_DIR, 1, S5_FLAT)
    ab_im = ab_im.reshape(DEPTH, N_DIR, 1, S5_FLAT)
    bb_re = bb_re.reshape(DEPTH, N_DIR, S5_GROUPS, S5_STATE, S5_CH)
    bb_im = bb_im.reshape(DEPTH, N_DIR, S5_GROUPS, S5_STATE, S5_CH)

    rope_tabs = _rope_tables(dec_seq)
    tri = np.tril(np.ones((M_TILE, M_TILE), np.float32))
    tril = jnp.asarray(tri, dtype=BF16)
    triu = jnp.asarray(tri.T, dtype=BF16)
    ones = jnp.ones((M_TILE, M_TILE), BF16)
    blkq = _head_block(ATT_WIDTH, ATT_HD)
    blkk = _head_block(ATT_KV_WIDTH, ATT_HD)
    blkm = _head_block(M_WIDTH, M_HD)

    w_in_b = _bf(w_in)
    gb = m_gate_b.reshape(DEPTH, M_GATE_COLS)
    pp = dict(
        mod=mod_all, g1=norm1_g.reshape(DEPTH, 1, D_MODEL), g2=norm2_g.reshape(DEPTH, 1, D_MODEL),
        w_in=w_in_b,
        w_gatecols=jnp.zeros((DEPTH, D_MODEL, LANES), BF16).at[:, :, :M_GATE_COLS].set(w_in_b[:, :, PROJ_MAIN:]),
        blkq=blkq, blkk=blkk, blkm=blkm,
        qg=jnp.tile(q_norm_g, (1, ATT_HEADS)).reshape(DEPTH, 1, ATT_WIDTH),
        kg=jnp.tile(k_norm_g, (1, ATT_KV_HEADS)).reshape(DEPTH, 1, ATT_KV_WIDTH),
        s5_bmat=_bf(jnp.concatenate([_block_diag(jnp.swapaxes(bb_re, -1, -2)),
                                     _block_diag(jnp.swapaxes(bb_im, -1, -2))], axis=-1)),
        s5_cre=_bf(_block_diag(jnp.swapaxes(s5_c_re, -1, -2))),
        s5_cim=_bf(_block_diag(jnp.swapaxes(s5_c_im, -1, -2))),
        s5_ab_re=jnp.broadcast_to(ab_re, (DEPTH, N_DIR, SUBLANES, S5_FLAT)),
        s5_ab_im=jnp.broadcast_to(ab_im, (DEPTH, N_DIR, SUBLANES, S5_FLAT)),
        s5_d=s5_d.reshape(DEPTH, 1, S5_WIDTH), s5_glu_w=_bf(s5_glu_w), s5_glu_b=s5_glu_b.reshape(DEPTH, 1, S5_WIDTH),
        m_gb_row=jnp.zeros((DEPTH, 1, LANES), F32).at[:, 0, :M_GATE_COLS].set(gb),
        m_gb_col=gb.reshape(DEPTH, M_GATE_COLS, 1),
        m_ng=jnp.tile(m_norm_g, (1, M_HEADS)).reshape(DEPTH, 1, M_WIDTH), tril=tril, triu=triu, ones=ones,
        w_out=_bf(w_out), ffn_w_gate=_bf(ffn_w_gate), ffn_w_up=_bf(ffn_w_up),
        ffn_conv_w=ffn_conv_w, ffn_conv_b=ffn_conv_b.reshape(DEPTH, 1, D_FF), ffn_w_down=_bf(ffn_w_down))

    past = cache_attn_k.shape[2]
    n_pad = CN_ROWS - M_HD
    s5_init = lambda st: st.reshape(dec_batch, DEPTH, N_DIR, S5_FLAT).transpose(1, 2, 0, 3)
    m_init = (state_mlstm_C.reshape(dec_batch, DEPTH, M_CHAINS, M_HD, M_HD),
              jnp.zeros((dec_batch, DEPTH, M_CHAINS, n_pad, M_HD), F32).at[:, :, :, 0, :].set(
                  state_mlstm_n.reshape(dec_batch, DEPTH, M_CHAINS, M_HD)),
              jnp.broadcast_to(state_mlstm_m.reshape(dec_batch, DEPTH, M_CHAINS, 1, 1),
                               (dec_batch, DEPTH, M_CHAINS, 1, LANES)))
    ctx = (cache_attn_k.reshape(dec_batch, DEPTH, past, ATT_KV_WIDTH),
           cache_attn_v.reshape(dec_batch, DEPTH, past, ATT_KV_WIDTH),
           s5_init(state_s5_re), s5_init(state_s5_im), m_init)

    xp = x_prompt.reshape(batch * seq, D_MODEL)
    xs = x_sample.reshape(dec_batch * dec_seq, D_MODEL)
    ctx_out = []
    for l in range(DEPTH):
        xp, st = _trunk_layer(xp, batch, seq, l, pp, None, None)
        ctx_out.append(st)
        xs, _ = _trunk_layer(xs, dec_batch, dec_seq, l, pp, ctx, rope_tabs)
    outs = [jnp.stack([s[i] for s in ctx_out], axis=1) for i in range(7)]
    return (xp.reshape(batch, seq, D_MODEL), xs.reshape(dec_batch, dec_seq, D_MODEL), *outs)
```

```python
import functools

import numpy as np
import jax
import jax.numpy as jnp
from jax import lax
from jax.experimental import pallas as pl
from jax.experimental.pallas import tpu as pltpu

F32 = jnp.float32
BF16 = jnp.bfloat16

D_MODEL = 1024
DEPTH = 2
GRID_W = 64
N_DIR = 2
EPS = 1e-6
ATT_HD = 64
ATT_WIDTH = 512
ATT_HEADS = 8
ATT_KV_HEADS = 2
ATT_GROUP = ATT_HEADS // ATT_KV_HEADS
ATT_KV_WIDTH = ATT_KV_HEADS * ATT_HD
ROPE_THETA = 10000.0
LOG2_E = 1.4426950408889634
S5_CH = 16
S5_STATE = 64
S5_WIDTH = 256
S5_GROUPS = 16
S5_FLAT = S5_GROUPS * S5_STATE
M_HD = 64
M_WIDTH = 256
M_HEADS = 4
M_GATE_COLS = 2 * N_DIR * M_HEADS
M_CHAINS = N_DIR * M_HEADS
D_FF = 2816
PROJ_MAIN = 2048
COL_K = ATT_WIDTH
COL_V = COL_K + ATT_KV_WIDTH
COL_S5 = COL_V + ATT_KV_WIDTH
COL_MQ = COL_S5 + S5_WIDTH
COL_MK = COL_MQ + M_WIDTH
COL_MV = COL_MK + M_WIDTH
COL_MO = COL_MV + M_WIDTH
AUX_S5 = 0
AUX_MO = AUX_S5 + S5_WIDTH
AUX_V = AUX_MO + M_WIDTH
AUX_WIDTH = AUX_V + ATT_KV_WIDTH
LANES = 128
SUBLANES = 8
BF16_ROWS = 16
VMEM_LIMIT = 56 * 1024 * 1024

ADA_TN = 1024
PRE_TM = 512
ATT_TQ = 512
S5_ROWS = 256
S5_SLOTS = 3
M_TILE = 128
M_BG = 4
M_BG_SEQ = 1024
CN_ROWS = M_HD + BF16_ROWS
TAIL_TM = 1024
TAIL_OUT_CHUNK = 256
TAIL_HALF = 512
FF_CHUNK = 512


def _bf(x):
    return x.astype(BF16)


def _dot(a, b):
    return jnp.dot(a, b, preferred_element_type=F32)


def _split(x):
    hi = _bf(x)
    lo = _bf(x - hi.astype(F32))
    return hi, lo


def _seg_mean_sq(x, blk):
    hi, lo = _split(x * x)
    return _dot(hi, blk) + _dot(lo, blk)


def _sigmoid(x):
    return 1.0 / (1.0 + jnp.exp(-x))


def _log_sigmoid(x):
    return -(jnp.maximum(-x, 0.0) + jnp.log1p(jnp.exp(-jnp.abs(x))))


def _params(*sem):
    return pltpu.CompilerParams(dimension_semantics=sem, vmem_limit_bytes=VMEM_LIMIT)


def _ada_kernel(c_ref, w_ref, b_ref, o_ref):
    c = c_ref[...]
    s = c * _sigmoid(c)
    o_ref[...] = _dot(_bf(s), _bf(w_ref[...])) + b_ref[...]


def _ada_call(cvec, ada_w, ada_b):
    rows = cvec.shape[0]
    tn = ADA_TN
    n = ada_w.shape[-1]
    return pl.pallas_call(
        _ada_kernel,
        grid=(DEPTH, n // tn),
        in_specs=[pl.BlockSpec((rows, D_MODEL), lambda l, j: (0, 0)),
                  pl.BlockSpec((None, D_MODEL, tn), lambda l, j: (l, 0, j)),
                  pl.BlockSpec((None, 1, tn), lambda l, j: (l, 0, j))],
        out_specs=pl.BlockSpec((None, rows, tn), lambda l, j: (l, 0, j)),
        out_shape=jax.ShapeDtypeStruct((DEPTH, rows, n), F32),
        compiler_params=_params("arbitrary", "arbitrary"),
        name="adaln",
    )(cvec, ada_w, ada_b.reshape(DEPTH, 1, n))


def _s5_disc_kernel(are_ref, aim_ref, ldt_ref, arex_ref, aimx_ref, bre_ref, bim_ref,
                    abr_ref, abi_ref, bbr_ref, bbi_ref):
    dt = jnp.exp(ldt_ref[...])

    def disc(a_re, a_im):
        mag = jnp.exp(dt * a_re)
        ab_re = mag * jnp.cos(dt * a_im)
        ab_im = mag * jnp.sin(dt * a_im)
        den = a_re * a_re + a_im * a_im
        nr = ab_re - 1.0
        ni = ab_im
        f_re = (nr * a_re + ni * a_im) / den
        f_im = (ni * a_re - nr * a_im) / den
        return ab_re, ab_im, f_re, f_im

    ab_re, ab_im, _, _ = disc(are_ref[...], aim_ref[...])
    abr_ref[...] = ab_re
    abi_ref[...] = ab_im
    _, _, f_re, f_im = disc(arex_ref[...], aimx_ref[...])
    b_re = bre_ref[...]
    b_im = bim_ref[...]
    bbr_ref[...] = f_re * b_re - f_im * b_im
    bbi_ref[...] = f_re * b_im + f_im * b_re


def _s5_disc_call(a_re, a_im, log_dt, b_re, b_im):
    r = DEPTH * N_DIR * S5_GROUPS
    a_re2 = a_re.reshape(r, S5_STATE)
    a_im2 = a_im.reshape(r, S5_STATE)
    wide = S5_STATE * S5_CH
    out_shape = (jax.ShapeDtypeStruct((r, S5_STATE), F32), jax.ShapeDtypeStruct((r, S5_STATE), F32),
                 jax.ShapeDtypeStruct((r, wide), F32), jax.ShapeDtypeStruct((r, wide), F32))
    return pl.pallas_call(_s5_disc_kernel, out_shape=out_shape, name="s5_disc")(
        a_re2, a_im2, log_dt.reshape(r, 1),
        jnp.repeat(a_re2, S5_CH, axis=1), jnp.repeat(a_im2, S5_CH, axis=1),
        b_re.reshape(r, wide), b_im.reshape(r, wide))


def _block_diag(blocks):
    g, r, c = blocks.shape[-3:]
    lead = blocks.shape[:-3]
    tiled = jnp.tile(blocks.reshape(lead + (g * r, c)), (1,) * len(lead) + (1, g))
    on_diagonal = (np.arange(g * r)[:, None] // r) == (np.arange(g * c)[None, :] // c)
    return jnp.where(on_diagonal, tiled, 0.0)


def _rope(x, cos, sin_signed, second):
    width = x.shape[-1]
    quarter = ATT_HD // 4
    partner = jnp.where(second, pltpu.roll(x, quarter, axis=1), pltpu.roll(x, width - quarter, axis=1))
    return x * cos + partner * sin_signed


def _pre_kernel(*refs, rope):
    if rope:
        (x_ref, mod_ref, g1_ref, w_ref, wg_ref, blkq_ref, blkk_ref, qg_ref, kg_ref,
         cosq_ref, sinq_ref, cosk_ref, sink_ref, proj_ref, mg_ref, qn_ref, kn_ref, mqk_ref, vt_ref, mgt_ref) = refs
    else:
        (x_ref, mod_ref, g1_ref, w_ref, wg_ref, blkq_ref, blkk_ref, qg_ref, kg_ref,
         proj_ref, mg_ref, qn_ref, kn_ref, mqk_ref, vt_ref, mgt_ref) = refs
    x = x_ref[...]
    ms = jnp.mean(x * x, axis=-1, keepdims=True)
    xn = x * lax.rsqrt(ms + EPS) * g1_ref[...]
    h = _bf(xn * (1.0 + mod_ref[1:2, :]) + mod_ref[0:1, :])
    def proj(c0, c1):
        return _dot(h, w_ref[:, c0:c1])

    q = proj(0, COL_K)
    kvs = proj(COL_K, COL_MQ)
    k = kvs[:, :ATT_KV_WIDTH]
    mqk = proj(COL_MQ, COL_MV)
    mvo = proj(COL_MV, PROJ_MAIN)
    proj_ref[:, AUX_S5:AUX_MO] = kvs[:, 2 * ATT_KV_WIDTH:]
    proj_ref[:, AUX_MO:AUX_V] = mvo[:, M_WIDTH:]
    proj_ref[:, AUX_V:] = kvs[:, ATT_KV_WIDTH:2 * ATT_KV_WIDTH]
    mg = _dot(h, wg_ref[...])
    mg_ref[...] = mg
    mqk_ref[:, :M_WIDTH] = _bf(mqk[:, :M_WIDTH])
    mqk_ref[:, M_WIDTH:] = _bf(mqk[:, M_WIDTH:] * (M_HD ** -0.5))
    vt_ref[...] = mvo[:, :M_WIDTH].T
    mgt_ref[...] = mg.T[:M_GATE_COLS, :]
    qn = q * lax.rsqrt(_seg_mean_sq(q, blkq_ref[...]) + EPS) * qg_ref[...]
    kn = k * lax.rsqrt(_seg_mean_sq(k, blkk_ref[...]) + EPS) * kg_ref[...]
    if rope:
        lane_q = lax.broadcasted_iota(jnp.int32, qn.shape, 1)
        lane_k = lax.broadcasted_iota(jnp.int32, kn.shape, 1)
        qn = _rope(qn, cosq_ref[...], sinq_ref[...], (lane_q & (ATT_HD // 4)) != 0)
        kn = _rope(kn, cosk_ref[...], sink_ref[...], (lane_k & (ATT_HD // 4)) != 0)
    qn_ref[...] = qn
    kn_ref[...] = kn


def _rope_tables(seq_len):
    n_rows = seq_len // GRID_W
    row = np.repeat(np.arange(n_rows), GRID_W)
    col = np.tile(np.arange(GRID_W), n_rows)
    half = ATT_HD // 2
    inv_freq = (1.0 / (np.float32(ROPE_THETA) ** (np.arange(0, half, 2, dtype=np.float32) / half))).astype(np.float32)

    def tables(pos):
        ang = pos.astype(np.float32)[:, None] * inv_freq[None, :]
        cos = np.cos(ang)
        sin = np.sin(ang)
        return np.concatenate([cos, cos], axis=-1), np.concatenate([-sin, sin], axis=-1)

    cr, sr = tables(row)
    cc, sc = tables(col)
    cos = np.concatenate([cr, cc], axis=-1)
    sin = np.concatenate([sr, sc], axis=-1)
    return tuple(jnp.asarray(np.tile(t, (1, n)), dtype=F32)
                 for t, n in ((cos, ATT_HEADS), (sin, ATT_HEADS), (cos, ATT_KV_HEADS), (sin, ATT_KV_HEADS)))


def _layered(layer, block, index_map, **kw):
    return pl.BlockSpec((None,) + tuple(block), lambda *g: (layer,) + tuple(index_map(*g)), **kw)


def _mod_spec(layer, row_of):
    return pl.BlockSpec((None, None, 6, D_MODEL), lambda *g: (layer, row_of(*g), 0, 0))


def _pre_call(x2, layer, pp, rope_tabs, seq_len, per_batch_mod):
    t = x2.shape[0]
    tm = min(PRE_TM, seq_len)
    tiles_per_seq = seq_len // tm
    rope = rope_tabs is not None
    if per_batch_mod:
        mod_row = lambda i: 1 + i // tiles_per_seq
    else:
        mod_row = lambda i: 0
    const = lambda i: (0, 0)
    row = lambda i: (i, 0)
    in_specs = [pl.BlockSpec((tm, D_MODEL), row),
                _mod_spec(layer, mod_row),
                _layered(layer, (1, D_MODEL), const),
                _layered(layer, (D_MODEL, PROJ_MAIN), const),
                _layered(layer, (D_MODEL, LANES), const),
                pl.BlockSpec((ATT_WIDTH, ATT_WIDTH), const),
                pl.BlockSpec((ATT_KV_WIDTH, ATT_KV_WIDTH), const),
                _layered(layer, (1, ATT_WIDTH), const),
                _layered(layer, (1, ATT_KV_WIDTH), const)]
    args = [x2, pp['mod'], pp['g1'], pp['w_in'], pp['w_gatecols'], pp['blkq'], pp['blkk'], pp['qg'], pp['kg']]
    if rope:
        pos_map = lambda i: (i % tiles_per_seq, 0)
        in_specs += [pl.BlockSpec((tm, ATT_WIDTH), pos_map), pl.BlockSpec((tm, ATT_WIDTH), pos_map),
                     pl.BlockSpec((tm, ATT_KV_WIDTH), pos_map), pl.BlockSpec((tm, ATT_KV_WIDTH), pos_map)]
        args += list(rope_tabs)
    time_on_lanes = lambda i: (i // tiles_per_seq, 0, i % tiles_per_seq)
    batch = t // seq_len
    out_specs = [pl.BlockSpec((tm, AUX_WIDTH), row),
                 pl.BlockSpec((tm, LANES), row),
                 pl.BlockSpec((tm, ATT_WIDTH), row),
                 pl.BlockSpec((tm, ATT_KV_WIDTH), row),
                 pl.BlockSpec((tm, 2 * M_WIDTH), row),
                 pl.BlockSpec((None, M_WIDTH, tm), time_on_lanes),
                 pl.BlockSpec((None, M_GATE_COLS, tm), time_on_lanes)]
    out_shape = [jax.ShapeDtypeStruct((t, AUX_WIDTH), F32), jax.ShapeDtypeStruct((t, LANES), F32),
                 jax.ShapeDtypeStruct((t, ATT_WIDTH), F32), jax.ShapeDtypeStruct((t, ATT_KV_WIDTH), F32),
                 jax.ShapeDtypeStruct((t, 2 * M_WIDTH), BF16),
                 jax.ShapeDtypeStruct((batch, M_WIDTH, seq_len), F32),
                 jax.ShapeDtypeStruct((batch, M_GATE_COLS, seq_len), F32)]
    return pl.pallas_call(
        functools.partial(_pre_kernel, rope=rope),
        grid=(t // tm,), in_specs=in_specs, out_specs=out_specs, out_shape=out_shape,
        compiler_params=_params("arbitrary"), name="pre_rope" if rope else "pre",
    )(*args)


def _attn_kernel(*refs, cached):
    if cached:
        q_ref, k_ref, v_ref, ck_ref, cv_ref, o_ref = refs
        k_all = _bf(jnp.concatenate([ck_ref[...], k_ref[...]], axis=0))
        v_all = _bf(jnp.concatenate([cv_ref[...], v_ref[...]], axis=0))
    else:
        q_ref, k_ref, v_ref, o_ref = refs
        k_all = _bf(k_ref[...])
        v_all = _bf(v_ref[...])
    scale = ATT_HD ** -0.5 * LOG2_E
    outs = []
    for kvh in range(ATT_KV_HEADS):
        lo = kvh * ATT_HD
        kb = k_all[:, lo:lo + ATT_HD]
        vb = v_all[:, lo:lo + ATT_HD]
        for g in range(ATT_GROUP):
            c0 = (kvh * ATT_GROUP + g) * ATT_HD
            qb = _bf(q_ref[:, c0:c0 + ATT_HD] * scale)
            s = lax.dot_general(qb, kb, (((1,), (1,)), ((), ())), preferred_element_type=F32)
            m = jnp.max(s, axis=-1, keepdims=True)
            e = jnp.exp2(s - m)
            den = jnp.sum(e, axis=-1, keepdims=True)
            outs.append(_dot(_bf(e), vb) / den)
    o_ref[...] = _bf(jnp.concatenate(outs, axis=-1))


def _attn_call(qn, kn, proj, layer, cache_k, cache_v, batch, lq):
    cached = cache_k is not None
    tq = min(ATT_TQ, lq)
    nq = lq // tq
    in_specs = [pl.BlockSpec((tq, ATT_WIDTH), lambda b, i: (b * nq + i, 0)),
                pl.BlockSpec((lq, ATT_KV_WIDTH), lambda b, i: (b, 0)),
                pl.BlockSpec((lq, ATT_KV_WIDTH), lambda b, i: (b, AUX_V // ATT_KV_WIDTH))]
    args = [qn, kn, proj]
    if cached:
        past = cache_k.shape[2]
        cache_spec = pl.BlockSpec((None, None, past, ATT_KV_WIDTH), lambda b, i: (b, layer, 0, 0))
        in_specs += [cache_spec, cache_spec]
        args += [cache_k, cache_v]
    return pl.pallas_call(
        functools.partial(_attn_kernel, cached=cached),
        grid=(batch, nq), in_specs=in_specs,
        out_specs=pl.BlockSpec((tq, ATT_WIDTH), lambda b, i: (b * nq + i, 0)),
        out_shape=jax.ShapeDtypeStruct((batch * lq, ATT_WIDTH), BF16),
        compiler_params=_params("arbitrary", "arbitrary"), name="attn_cached" if cached else "attn",
    )(*args)


def _s5_kernel(*refs, batch, seq_len, zero_init):
    if zero_init:
        (u_ref, bmat_ref, cre_ref, cim_ref, ar_ref, ai_ref, d_ref, gw_ref, gb_ref,
         out_ref, fr_ref, fi_ref, utb_ref, ytb_ref, xs_ref) = refs
        ir_ref = ii_ref = None
    else:
        (u_ref, bmat_ref, cre_ref, cim_ref, ar_ref, ai_ref, d_ref, gw_ref, gb_ref, ir_ref, ii_ref,
         out_ref, fr_ref, fi_ref, utb_ref, ytb_ref, xs_ref) = refs
    rows = S5_ROWS
    tc = rows // batch
    n_chunks = seq_len // tc
    n_sub = batch // SUBLANES
    halves = S5_WIDTH // LANES

    def aligned(x, m):
        return x if isinstance(x, int) else pl.multiple_of(x, m)

    for b in range(batch):
        for hf in range(halves):
            utb_ref[hf, pl.ds(b, seq_len, stride=batch), :] = (
                u_ref[b * seq_len:(b + 1) * seq_len, hf * LANES:(hf + 1) * LANES])
    for hf in range(halves):
        ytb_ref[hf] = utb_ref[hf] * d_ref[:, hf * LANES:(hf + 1) * LANES]

    def chunk_of(d, i):
        return i if d == 0 else n_chunks - 1 - i

    def stage_in(i, slot):
        for d in range(N_DIR):
            r0 = aligned(chunk_of(d, i) * rows, rows)
            u_c = jnp.concatenate([utb_ref[hf, pl.ds(r0, rows), :] for hf in range(halves)], axis=-1)
            xs_ref[slot, d] = _dot(_bf(u_c), bmat_ref[d])

    def stage_scan(slot, carry):
        new_carry = []
        for d in range(N_DIR):
            a_re = ar_ref[d]
            a_im = ai_ref[d]
            per_sub = []
            for sub in range(n_sub):
                s_re, s_im = carry[d][sub]
                for t in range(tc):
                    r = (t if d == 0 else tc - 1 - t) * batch + sub * SUBLANES
                    x_re = xs_ref[slot, d, r:r + SUBLANES, :S5_FLAT]
                    x_im = xs_ref[slot, d, r:r + SUBLANES, S5_FLAT:]
                    s_re, s_im = (a_re * s_re - a_im * s_im + x_re, a_re * s_im + a_im * s_re + x_im)
                    xs_ref[slot, d, r:r + SUBLANES, :S5_FLAT] = s_re
                    xs_ref[slot, d, r:r + SUBLANES, S5_FLAT:] = s_im
                per_sub.append((s_re, s_im))
            new_carry.append(tuple(per_sub))
        return tuple(new_carry)

    def stage_out(i, slot):
        for d in range(N_DIR):
            r0 = aligned(chunk_of(d, i) * rows, rows)
            y = (_dot(_bf(xs_ref[slot, d, :, :S5_FLAT]), cre_ref[d])
                 - _dot(_bf(xs_ref[slot, d, :, S5_FLAT:]), cim_ref[d]))
            for hf in range(halves):
                ytb_ref[hf, pl.ds(r0, rows), :] += y[:, hf * LANES:(hf + 1) * LANES]

    def start_state(ref, d, sub):
        if zero_init:
            return jnp.zeros((SUBLANES, S5_FLAT), F32)
        return ref[d, sub * SUBLANES:(sub + 1) * SUBLANES, :]

    carry = tuple(tuple((start_state(ir_ref, d, sub), start_state(ii_ref, d, sub)) for sub in range(n_sub))
                  for d in range(N_DIR))
    def step(i, phase, carry):
        stage_in(i + 1, (phase + 1) % S5_SLOTS)
        carry = stage_scan(phase, carry)
        stage_out(i - 1, (phase - 1) % S5_SLOTS)
        return carry

    stage_in(0, 0)
    stage_in(1, 1)
    carry = stage_scan(0, carry)
    n_steady = n_chunks - 2
    n_peeled = n_steady % S5_SLOTS
    for i in range(1, 1 + n_peeled):
        carry = step(i, i % S5_SLOTS, carry)
    first = 1 + n_peeled

    def body(g, carry):
        for k in range(S5_SLOTS):
            carry = step(first + g * S5_SLOTS + k, (first + k) % S5_SLOTS, carry)
        return carry

    carry = lax.fori_loop(0, n_steady // S5_SLOTS, body, carry)
    last = n_chunks - 1
    carry = stage_scan(last % S5_SLOTS, carry)
    stage_out(last - 1, (last - 1) % S5_SLOTS)
    stage_out(last, last % S5_SLOTS)
    for d in range(N_DIR):
        for sub in range(n_sub):
            fr_ref[d, sub * SUBLANES:(sub + 1) * SUBLANES, :] = carry[d][sub][0]
            fi_ref[d, sub * SUBLANES:(sub + 1) * SUBLANES, :] = carry[d][sub][1]

    for b in range(batch):
        z = jax.nn.gelu(jnp.concatenate(
            [ytb_ref[hf, pl.ds(b, seq_len, stride=batch), :] for hf in range(halves)], axis=-1))
        gate = _sigmoid(_dot(_bf(z), gw_ref[...]) + gb_ref[...])
        out_ref[b * seq_len:(b + 1) * seq_len, :] = _bf(z * gate)


def _s5_call(proj, layer, pp, init_re, init_im, batch, seq_len):
    rows = seq_len * batch
    single = pl.Buffered(1)
    zero_init = init_re is None
    names = ['s5_bmat', 's5_cre', 's5_cim', 's5_ab_re', 's5_ab_im', 's5_d', 's5_glu_w', 's5_glu_b']
    args = [pp[n] for n in names] + ([] if zero_init else [init_re, init_im])
    whole = lambda a: _layered(layer, a.shape[1:], lambda i: (0,) * (a.ndim - 1), pipeline_mode=single)
    in_specs = ([pl.BlockSpec((rows, S5_WIDTH), lambda i: (0, AUX_S5 // S5_WIDTH), pipeline_mode=single)]
                + [whole(a) for a in args])
    fin_spec = pl.BlockSpec((N_DIR, batch, S5_FLAT), lambda i: (0, 0, 0))
    out_specs = [pl.BlockSpec((rows, S5_WIDTH), lambda i: (0, 0), pipeline_mode=single), fin_spec, fin_spec]
    out_shape = (jax.ShapeDtypeStruct((rows, S5_WIDTH), BF16),
                 jax.ShapeDtypeStruct((N_DIR, batch, S5_FLAT), F32),
                 jax.ShapeDtypeStruct((N_DIR, batch, S5_FLAT), F32))
    return pl.pallas_call(
        functools.partial(_s5_kernel, batch=batch, seq_len=seq_len, zero_init=zero_init),
        grid=(1,), in_specs=in_specs, out_specs=out_specs, out_shape=out_shape,
        scratch_shapes=[pltpu.VMEM((S5_WIDTH // LANES, rows, LANES), F32),
                        pltpu.VMEM((S5_WIDTH // LANES, rows, LANES), F32),
                        pltpu.VMEM((S5_SLOTS, N_DIR, S5_ROWS, 2 * S5_FLAT), F32)],
        compiler_params=_params("arbitrary"), name="s5",
    )(proj, *args)


def _mlstm_kernel(*refs, seq_len, zero_init):
    if zero_init:
        (q_ref, k_ref, vt_ref, mg_ref, mgt_ref, gbr_ref, gbc_ref, tril_ref, triu_ref, ones_ref,
         ht_ref, cn_ref, m_ref) = refs
    else:
        (q_ref, k_ref, vt_ref, mg_ref, mgt_ref, gbr_ref, gbc_ref, tril_ref, triu_ref, ones_ref,
         c0_ref, n0_ref, m0_ref, ht_ref, cn_ref, m_ref) = refs
    _mlstm_body(q_ref, k_ref, vt_ref, mg_ref, mgt_ref, gbr_ref, gbc_ref, tril_ref, triu_ref, ones_ref,
                None if zero_init else (c0_ref, n0_ref, m0_ref), ht_ref, cn_ref, m_ref, seq_len)


def _mlstm_body(q_ref, k_ref, vt_ref, mg_ref, mgt_ref, gbr_ref, gbc_ref, tril_ref, triu_ref, ones_ref,
                init_refs, ht_ref, cn_ref, m_ref, seq_len):
    tile = M_TILE
    n_chunks = seq_len // tile
    bg = q_ref.shape[0]
    if init_refs is None:
        cn_ref[...] = jnp.zeros_like(cn_ref)
        m_ref[...] = jnp.zeros_like(m_ref)
    else:
        c0_ref, n0_ref, m0_ref = init_refs
        cn_ref[:, :, :M_HD, :] = c0_ref[...]
        cn_ref[:, :, M_HD:, :] = n0_ref[...]
        m_ref[...] = m0_ref[...]
    s_ids = lax.broadcasted_iota(jnp.int32, (tile, tile), 0)
    t_ids = lax.broadcasted_iota(jnp.int32, (tile, tile), 1)
    first_row = lax.broadcasted_iota(jnp.int32, (CN_ROWS - M_HD, tile), 0) == 0
    neg_inf = jnp.float32(-jnp.inf)
    ones = ones_ref[...]

    def direction(d, c):
        t0 = pl.multiple_of(c * tile, tile)
        col_mat = tril_ref[...] if d == 0 else triu_ref[...]
        row_mat = triu_ref[...] if d == 0 else tril_ref[...]
        valid = (s_ids <= t_ids) if d == 0 else (s_ids >= t_ids)
        gt = mgt_ref[:, :, pl.ds(t0, tile)] + gbc_ref[...]
        lft_hi, lft_lo = _split(_log_sigmoid(gt).reshape(bg * M_GATE_COLS, tile))
        b_rows = (_dot(lft_hi, row_mat) + _dot(lft_lo, row_mat)).reshape(bg, M_GATE_COLS, tile)
        totals = (_dot(lft_hi, ones) + _dot(lft_lo, ones)).reshape(bg, M_GATE_COLS, tile)
        gc = mg_ref[:, pl.ds(t0, tile), :] + gbr_ref[...]
        lfc_hi, lfc_lo = _split(_log_sigmoid(gc))
        col_b = jnp.broadcast_to(col_mat, (bg, tile, tile))
        b_cols = (jnp.einsum('bts,bsl->btl', col_b, lfc_hi, preferred_element_type=F32)
                  + jnp.einsum('bts,bsl->btl', col_b, lfc_lo, preferred_element_type=F32))
        for h in range(M_HEADS):
            j = d * M_HEADS + h
            fj = M_CHAINS + j
            rows = slice(h * M_HD, (h + 1) * M_HD)
            q_c = q_ref[:, pl.ds(t0, tile), rows]
            k_c = k_ref[:, pl.ds(t0, tile), rows]
            vt_c = vt_ref[:, rows, pl.ds(t0, tile)]
            b_t = b_rows[:, fj:fj + 1, :]
            li_t = gt[:, j:j + 1, :]
            tot = totals[:, fj:fj + 1, :]
            m_prev = m_ref[:, j]
            cn_prev = cn_ref[:, j]

            c_col = gc[:, :, j:j + 1] - b_cols[:, :, fj:fj + 1]
            cm = jnp.where(valid, c_col, neg_inf)
            inter = b_t + m_prev
            m_t = jnp.maximum(inter, b_t + jnp.max(cm, axis=1, keepdims=True))
            w_t = jnp.exp(cm + (b_t - m_t))
            s_inter = jnp.exp(inter - m_t)
            sc_t = jnp.einsum('bse,bte->bst', k_c, q_c, preferred_element_type=F32) * w_t
            num = jnp.einsum('bds,bst->bdt', _bf(vt_c), _bf(sc_t), preferred_element_type=F32)
            ext = jnp.einsum('bre,bte->brt', _bf(cn_prev), q_c, preferred_element_type=F32)
            num = num + s_inter * ext[:, :M_HD, :]
            den = jnp.sum(sc_t, axis=1, keepdims=True) + s_inter * ext[:, M_HD:M_HD + 1, :]
            ht_ref[d, :, rows, pl.ds(t0, tile)] = num / jnp.maximum(jnp.abs(den), jnp.exp(-m_t))

            g_row = tot - b_t + li_t
            m_new = jnp.maximum(tot + m_prev, jnp.max(g_row, axis=-1, keepdims=True))
            wk = jnp.exp(g_row - m_new)
            decay = jnp.exp(tot + m_prev - m_new)
            vw = jnp.concatenate([vt_c * wk, jnp.where(first_row, wk, 0.0)], axis=1)
            cn_ref[:, j] = (decay[:, :, :M_HD] * cn_prev
                            + jnp.einsum('brs,bse->bre', _bf(vw), k_c, preferred_element_type=F32))
            m_ref[:, j] = m_new

    def body(i, _):
        direction(0, i)
        direction(1, n_chunks - 1 - i)
        return 0

    lax.fori_loop(0, n_chunks, body, 0)


def _mlstm_call(mqk3, vt, mg3, mgt, layer, pp, init, batch, seq_len):
    bg = min(batch // 2, M_BG * M_BG_SEQ // seq_len)
    const2 = lambda g: (0, 0)
    lead3 = lambda g: (g, 0, 0)
    lead4 = lambda g: (g, 0, 0, 0)
    in_specs = [pl.BlockSpec((bg, seq_len, M_WIDTH), lead3),
                pl.BlockSpec((bg, seq_len, M_WIDTH), lambda g: (g, 0, 1)),
                pl.BlockSpec((bg, M_WIDTH, seq_len), lead3),
                pl.BlockSpec((bg, seq_len, LANES), lead3),
                pl.BlockSpec((bg, M_GATE_COLS, seq_len), lead3),
                _layered(layer, (1, LANES), const2),
                _layered(layer, (M_GATE_COLS, 1), const2),
                pl.BlockSpec((M_TILE, M_TILE), const2),
                pl.BlockSpec((M_TILE, M_TILE), const2),
                pl.BlockSpec((M_TILE, M_TILE), const2)]
    args = [mqk3, mqk3, vt, mg3, mgt, pp['m_gb_row'], pp['m_gb_col'], pp['tril'], pp['triu'], pp['ones']]
    if init is not None:
        state_spec = lambda *tail: pl.BlockSpec((bg, None, M_CHAINS) + tail, lambda g: (g, layer, 0, 0, 0))
        in_specs += [state_spec(M_HD, M_HD), state_spec(CN_ROWS - M_HD, M_HD), state_spec(1, LANES)]
        args += list(init)
    out_specs = [pl.BlockSpec((N_DIR, bg, M_WIDTH, seq_len), lambda g: (0, g, 0, 0)),
                 pl.BlockSpec((bg, M_CHAINS, CN_ROWS, M_HD), lead4),
                 pl.BlockSpec((bg, M_CHAINS, 1, LANES), lead4)]
    out_shape = [jax.ShapeDtypeStruct((N_DIR, batch, M_WIDTH, seq_len), F32),
                 jax.ShapeDtypeStruct((batch, M_CHAINS, CN_ROWS, M_HD), F32),
                 jax.ShapeDtypeStruct((batch, M_CHAINS, 1, LANES), F32)]
    return pl.pallas_call(
        functools.partial(_mlstm_kernel, seq_len=seq_len, zero_init=init is None),
        grid=(batch // bg,), in_specs=in_specs, out_specs=out_specs, out_shape=out_shape,
        compiler_params=_params("arbitrary"), name="mlstm",
    )(*args)


def _tail_kernel(x_ref, att_ref, s5_ref, ht_ref, mo_ref, mod_ref, g2_ref, ng_ref, blkm_ref, wo_ref,
                 wg_ref, wu_ref, cw_ref, cb_ref, wd_ref, o_ref, h2_ref, gated_ref, *, seq_len):
    tm = x_ref.shape[0]
    mh = jnp.concatenate([(ht_ref[0, s] + ht_ref[1, s]).T for s in range(ht_ref.shape[1])], axis=0)
    ml = mh * lax.rsqrt(_seg_mean_sq(mh, blkm_ref[...]) + EPS) * ng_ref[...] * _sigmoid(mo_ref[...])
    mixers = jnp.concatenate([att_ref[...], s5_ref[...], _bf(ml)], axis=-1)
    for c0 in range(0, D_MODEL, TAIL_OUT_CHUNK):
        cols = slice(c0, c0 + TAIL_OUT_CHUNK)
        o_ref[:, cols] = x_ref[:, cols] + mod_ref[2:3, cols] * _dot(mixers, wo_ref[:, cols])
    x1 = o_ref[...]
    ms = jnp.mean(x1 * x1, axis=-1, keepdims=True)
    xn = x1 * lax.rsqrt(ms + EPS) * g2_ref[...]
    h2_ref[...] = _bf(xn * (1.0 + mod_ref[4:5, :]) + mod_ref[3:4, :])

    half = TAIL_HALF
    for r0 in range(0, tm, half):
        h2 = h2_ref[r0:r0 + half, :]
        p0 = max(r0 - BF16_ROWS, 0)
        n0 = min(r0 + half, tm - BF16_ROWS)
        h2_ext = jnp.concatenate([h2, h2_ref[p0:p0 + BF16_ROWS, :], h2_ref[n0:n0 + BF16_ROWS, :]], axis=0)
        for c0 in range(0, D_FF, FF_CHUNK):
            width = min(FF_CHUNK, D_FF - c0)
            cols = slice(c0, c0 + width)
            row = lax.broadcasted_iota(jnp.int32, (half, width), 0)
            half_start = row == 0
            half_end = row == half - 1
            pos = (r0 + row) % seq_len
            seq_start = pos == 0
            seq_end = pos == seq_len - 1
            a_ext = _dot(h2_ext, wg_ref[:, cols])
            a = a_ext[:half]
            before = a_ext[half + BF16_ROWS - 1:half + BF16_ROWS, :]
            after = a_ext[half + BF16_ROWS:half + BF16_ROWS + 1, :]
            a_prev = jnp.where(half_start, before, pltpu.roll(a, 1, axis=0))
            a_next = jnp.where(half_end, after, pltpu.roll(a, half - 1, axis=0))
            a_prev = jnp.where(seq_start, 0.0, a_prev)
            a_next = jnp.where(seq_end, 0.0, a_next)
            ac = a_prev * cw_ref[0:1, cols] + a * cw_ref[1:2, cols] + a_next * cw_ref[2:3, cols] + cb_ref[:, cols]
            up = _dot(h2, wu_ref[:, cols])
            gated_ref[:, cols] = _bf(ac * _sigmoid(ac) * up)
        o_ref[r0:r0 + half, :] = o_ref[r0:r0 + half, :] + mod_ref[5:6, :] * _dot(gated_ref[...], wd_ref[...])


def _tail_call(x2, att, s5o, ht, proj, layer, pp, seq_len, per_batch_mod):
    t = x2.shape[0]
    tm = TAIL_TM
    seqs_per_tile = tm // seq_len
    if per_batch_mod:
        mod_row = lambda i: 1 + i * seqs_per_tile
    else:
        mod_row = lambda i: 0
    row = lambda i: (i, 0)
    const = lambda i: (0, 0)
    single = pl.Buffered(1)
    whole = lambda shape: _layered(layer, shape, const, pipeline_mode=single)
    in_specs = [pl.BlockSpec((tm, D_MODEL), row),
                pl.BlockSpec((tm, ATT_WIDTH), row),
                pl.BlockSpec((tm, S5_WIDTH), row),
                pl.BlockSpec((N_DIR, seqs_per_tile, M_WIDTH, seq_len), lambda i: (0, i, 0, 0)),
                pl.BlockSpec((tm, M_WIDTH), lambda i: (i, AUX_MO // M_WIDTH)),
                _mod_spec(layer, mod_row),
                _layered(layer, (1, D_MODEL), const),
                _layered(layer, (1, M_WIDTH), const),
                pl.BlockSpec((M_WIDTH, M_WIDTH), const),
                whole((D_MODEL, D_MODEL)),
                whole((D_MODEL, D_FF)), whole((D_MODEL, D_FF)), whole((3, D_FF)), whole((1, D_FF)),
                whole((D_FF, D_MODEL))]
    return pl.pallas_call(
        functools.partial(_tail_kernel, seq_len=seq_len),
        grid=(t // tm,), in_specs=in_specs,
        out_specs=pl.BlockSpec((tm, D_MODEL), row),
        out_shape=jax.ShapeDtypeStruct((t, D_MODEL), F32),
        scratch_shapes=[pltpu.VMEM((tm, D_MODEL), BF16), pltpu.VMEM((TAIL_HALF, D_FF), BF16)],
        compiler_params=_params("arbitrary"), name="tail",
    )(x2, att, s5o, ht, proj, pp['mod'], pp['g2'], pp['m_ng'], pp['blkm'], pp['w_out'],
      pp['ffn_w_gate'], pp['ffn_w_up'], pp['ffn_conv_w'], pp['ffn_conv_b'], pp['ffn_w_down'])


def _trunk_layer(x2, batch, seq_len, layer, pp, ctx, rope_tabs):
    latent = ctx is not None
    proj, mg, qn, kn, mqk, vt, mgt = _pre_call(x2, layer, pp, rope_tabs if latent else None, seq_len, latent)
    cache_k, cache_v, init_re, init_im, m_init = ctx if latent else (None,) * 5

    att = _attn_call(qn, kn, proj, layer, cache_k, cache_v, batch, seq_len)
    s5o, fin_re, fin_im = _s5_call(proj, layer, pp, init_re, init_im, batch, seq_len)
    ht, cn_f, m_f = _mlstm_call(mqk.reshape(batch, seq_len, 2 * M_WIDTH), vt, mg.reshape(batch, seq_len, LANES), mgt,
                                layer, pp, m_init, batch, seq_len)
    x_out = _tail_call(x2, att, s5o, ht, proj, layer, pp, seq_len, latent)
    if latent:
        return x_out, None
    states = (kn.reshape(batch, seq_len, ATT_KV_HEADS, ATT_HD),
              proj[:, AUX_V:].reshape(batch, seq_len, ATT_KV_HEADS, ATT_HD),
              fin_re.transpose(1, 0, 2).reshape(batch, N_DIR, S5_GROUPS, S5_STATE),
              fin_im.transpose(1, 0, 2).reshape(batch, N_DIR, S5_GROUPS, S5_STATE),
              cn_f[:, :, :M_HD, :].reshape(batch, N_DIR, M_HEADS, M_HD, M_HD),
              cn_f[:, :, M_HD, :].reshape(batch, N_DIR, M_HEADS, M_HD),
              m_f[:, :, 0, 0].reshape(batch, N_DIR, M_HEADS))
    return x_out, states


def _head_block(width, head_dim):
    ids = np.arange(width) // head_dim
    return jnp.asarray((ids[:, None] == ids[None, :]).astype(np.float32) / head_dim, dtype=BF16)


def kernel(x_prompt, x_sample, c, cache_attn_k, cache_attn_v, state_s5_re, state_s5_im, state_mlstm_C, state_mlstm_n, state_mlstm_m, c_ctx, ada_w, ada_b, norm1_g, norm2_g, w_in, q_norm_g, k_norm_g, s5_a_re, s5_a_im, s5_log_dt, s5_b_re, s5_b_im, s5_c_re, s5_c_im, s5_d, s5_glu_w, s5_glu_b, m_gate_b, m_norm_g, w_out, ffn_w_gate, ffn_w_up, ffn_conv_w, ffn_conv_b, ffn_w_down):
    batch, seq = x_prompt.shape[0], x_prompt.shape[1]
    dec_batch, dec_seq = x_sample.shape[0], x_sample.shape[1]

    n_mod_rows = 2 * SUBLANES
    cvec = jnp.zeros((n_mod_rows, D_MODEL), F32).at[0].set(c_ctx).at[1:1 + dec_batch].set(c)
    mod_all = _ada_call(cvec, ada_w, ada_b).reshape(DEPTH, n_mod_rows, 6, D_MODEL)

    ab_re, ab_im, bb_re, bb_im = _s5_disc_call(s5_a_re, s5_a_im, s5_log_dt, s5_b_re, s5_b_im)
    ab_re = ab_re.reshape(DEPTH, N_DIR, 1, S5_FLAT)
    ab_im = ab_im.reshape(DEPTH, N_DIR, 1, S5_FLAT)
    bb_re = bb_re.reshape(DEPTH, N_DIR, S5_GROUPS, S5_STATE, S5_CH)
    bb_im = bb_im.reshape(DEPTH, N_DIR, S5_GROUPS, S5_STATE, S5_CH)

    rope_tabs = _rope_tables(dec_seq)
    tri = np.tril(np.ones((M_TILE, M_TILE), np.float32))
    tril = jnp.asarray(tri, dtype=BF16)
    triu = jnp.asarray(tri.T, dtype=BF16)
    ones = jnp.ones((M_TILE, M_TILE), BF16)
    blkq = _head_block(ATT_WIDTH, ATT_HD)
    blkk = _head_block(ATT_KV_WIDTH, ATT_HD)
    blkm = _head_block(M_WIDTH, M_HD)

    w_in_b = _bf(w_in)
    gb = m_gate_b.reshape(DEPTH, M_GATE_COLS)
    pp = dict(
        mod=mod_all, g1=norm1_g.reshape(DEPTH, 1, D_MODEL), g2=norm2_g.reshape(DEPTH, 1, D_MODEL),
        w_in=w_in_b,
        w_gatecols=jnp.zeros((DEPTH, D_MODEL, LANES), BF16).at[:, :, :M_GATE_COLS].set(w_in_b[:, :, PROJ_MAIN:]),
        blkq=blkq, blkk=blkk, blkm=blkm,
        qg=jnp.tile(q_norm_g, (1, ATT_HEADS)).reshape(DEPTH, 1, ATT_WIDTH),
        kg=jnp.tile(k_norm_g, (1, ATT_KV_HEADS)).reshape(DEPTH, 1, ATT_KV_WIDTH),
        s5_bmat=_bf(jnp.concatenate([_block_diag(jnp.swapaxes(bb_re, -1, -2)),
                                     _block_diag(jnp.swapaxes(bb_im, -1, -2))], axis=-1)),
        s5_cre=_bf(_block_diag(jnp.swapaxes(s5_c_re, -1, -2))),
        s5_cim=_bf(_block_diag(jnp.swapaxes(s5_c_im, -1, -2))),
        s5_ab_re=jnp.broadcast_to(ab_re, (DEPTH, N_DIR, SUBLANES, S5_FLAT)),
        s5_ab_im=jnp.broadcast_to(ab_im, (DEPTH, N_DIR, SUBLANES, S5_FLAT)),
        s5_d=s5_d.reshape(DEPTH, 1, S5_WIDTH), s5_glu_w=_bf(s5_glu_w), s5_glu_b=s5_glu_b.reshape(DEPTH, 1, S5_WIDTH),
        m_gb_row=jnp.zeros((DEPTH, 1, LANES), F32).at[:, 0, :M_GATE_COLS].set(gb),
        m_gb_col=gb.reshape(DEPTH, M_GATE_COLS, 1),
        m_ng=jnp.tile(m_norm_g, (1, M_HEADS)).reshape(DEPTH, 1, M_WIDTH), tril=tril, triu=triu, ones=ones,
        w_out=_bf(w_out), ffn_w_gate=_bf(ffn_w_gate), ffn_w_up=_bf(ffn_w_up),
        ffn_conv_w=ffn_conv_w, ffn_conv_b=ffn_conv_b.reshape(DEPTH, 1, D_FF), ffn_w_down=_bf(ffn_w_down))

    past = cache_attn_k.shape[2]
    n_pad = CN_ROWS - M_HD
    s5_init = lambda st: st.reshape(dec_batch, DEPTH, N_DIR, S5_FLAT).transpose(1, 2, 0, 3)
    m_init = (state_mlstm_C.reshape(dec_batch, DEPTH, M_CHAINS, M_HD, M_HD),
              jnp.zeros((dec_batch, DEPTH, M_CHAINS, n_pad, M_HD), F32).at[:, :, :, 0, :].set(
                  state_mlstm_n.reshape(dec_batch, DEPTH, M_CHAINS, M_HD)),
              jnp.broadcast_to(state_mlstm_m.reshape(dec_batch, DEPTH, M_CHAINS, 1, 1),
                               (dec_batch, DEPTH, M_CHAINS, 1, LANES)))
    ctx = (cache_attn_k.reshape(dec_batch, DEPTH, past, ATT_KV_WIDTH),
           cache_attn_v.reshape(dec_batch, DEPTH, past, ATT_KV_WIDTH),
           s5_init(state_s5_re), s5_init(state_s5_im), m_init)

    xp = x_prompt.reshape(batch * seq, D_MODEL)
    xs = x_sample.reshape(dec_batch * dec_seq, D_MODEL)
    ctx_out = []
    for l in range(DEPTH):
        xp, st = _trunk_layer(xp, batch, seq, l, pp, None, None)
        ctx_out.append(st)
        xs, _ = _trunk_layer(xs, dec_batch, dec_seq, l, pp, ctx, rope_tabs)
    outs = [jnp.stack([s[i] for s in ctx_out], axis=1) for i in range(7)]
    return (xp.reshape(batch, seq, D_MODEL), xs.reshape(dec_batch, dec_seq, D_MODEL), *outs)
```

```python
import functools

import numpy as np
import jax
import jax.numpy as jnp
from jax import lax
from jax.experimental import pallas as pl
from jax.experimental.pallas import tpu as pltpu

F32 = jnp.float32
BF16 = jnp.bfloat16

D_MODEL = 1024
DEPTH = 2
GRID_W = 64
N_DIR = 2
EPS = 1e-6
ATT_HD = 64
ATT_WIDTH = 512
ATT_HEADS = 8
ATT_KV_HEADS = 2
ATT_GROUP = ATT_HEADS // ATT_KV_HEADS
ATT_KV_WIDTH = ATT_KV_HEADS * ATT_HD
ROPE_THETA = 10000.0
LOG2_E = 1.4426950408889634
S5_CH = 16
S5_STATE = 64
S5_WIDTH = 256
S5_GROUPS = 16
S5_FLAT = S5_GROUPS * S5_STATE
M_HD = 64
M_WIDTH = 256
M_HEADS = 4
M_GATE_COLS = 2 * N_DIR * M_HEADS
M_CHAINS = N_DIR * M_HEADS
D_FF = 2816
PROJ_MAIN = 2048
COL_K = ATT_WIDTH
COL_V = COL_K + ATT_KV_WIDTH
COL_S5 = COL_V + ATT_KV_WIDTH
COL_MQ = COL_S5 + S5_WIDTH
COL_MK = COL_MQ + M_WIDTH
COL_MV = COL_MK + M_WIDTH
COL_MO = COL_MV + M_WIDTH
AUX_S5 = 0
AUX_MO = AUX_S5 + S5_WIDTH
AUX_V = AUX_MO + M_WIDTH
AUX_WIDTH = AUX_V + ATT_KV_WIDTH
LANES = 128
SUBLANES = 8
BF16_ROWS = 16
VMEM_LIMIT = 56 * 1024 * 1024

ADA_TN = 2048
PRE_TM = 1024
ATT_TQ = 512
S5_ROWS = 256
S5_SLOTS = 3
M_TILE = 128
M_BG = 4
M_BG_SEQ = 1024
CN_ROWS = M_HD + BF16_ROWS
TAIL_TM = 1024
TAIL_OUT_CHUNK = 256
TAIL_HALF = 512
FF_CHUNK = 512


def _bf(x):
    return x.astype(BF16)


def _dot(a, b):
    return jnp.dot(a, b, preferred_element_type=F32)


def _split(x):
    hi = _bf(x)
    lo = _bf(x - hi.astype(F32))
    return hi, lo


def _seg_mean_sq(x, blk):
    hi, lo = _split(x * x)
    return _dot(hi, blk) + _dot(lo, blk)


def _sigmoid(x):
    return 1.0 / (1.0 + jnp.exp(-x))


def _log_sigmoid(x):
    return -(jnp.maximum(-x, 0.0) + jnp.log1p(jnp.exp(-jnp.abs(x))))


def _params(*sem):
    return pltpu.CompilerParams(dimension_semantics=sem, vmem_limit_bytes=VMEM_LIMIT)


def _ada_kernel(c_ref, w_ref, b_ref, o_ref):
    c = c_ref[...]
    s = c * _sigmoid(c)
    o_ref[...] = _dot(_bf(s), _bf(w_ref[...])) + b_ref[...]


def _ada_call(cvec, ada_w, ada_b):
    rows = cvec.shape[0]
    tn = ADA_TN
    n = ada_w.shape[-1]
    return pl.pallas_call(
        _ada_kernel,
        grid=(DEPTH, n // tn),
        in_specs=[pl.BlockSpec((rows, D_MODEL), lambda l, j: (0, 0)),
                  pl.BlockSpec((None, D_MODEL, tn), lambda l, j: (l, 0, j)),
                  pl.BlockSpec((None, 1, tn), lambda l, j: (l, 0, j))],
        out_specs=pl.BlockSpec((None, rows, tn), lambda l, j: (l, 0, j)),
        out_shape=jax.ShapeDtypeStruct((DEPTH, rows, n), F32),
        compiler_params=_params("arbitrary", "arbitrary"),
        name="adaln",
    )(cvec, ada_w, ada_b.reshape(DEPTH, 1, n))


def _s5_disc_kernel(are_ref, aim_ref, ldt_ref, arex_ref, aimx_ref, bre_ref, bim_ref,
                    abr_ref, abi_ref, bbr_ref, bbi_ref):
    dt = jnp.exp(ldt_ref[...])

    def disc(a_re, a_im):
        mag = jnp.exp(dt * a_re)
        ab_re = mag * jnp.cos(dt * a_im)
        ab_im = mag * jnp.sin(dt * a_im)
        den = a_re * a_re + a_im * a_im
        nr = ab_re - 1.0
        ni = ab_im
        f_re = (nr * a_re + ni * a_im) / den
        f_im = (ni * a_re - nr * a_im) / den
        return ab_re, ab_im, f_re, f_im

    ab_re, ab_im, _, _ = disc(are_ref[...], aim_ref[...])
    abr_ref[...] = ab_re
    abi_ref[...] = ab_im
    _, _, f_re, f_im = disc(arex_ref[...], aimx_ref[...])
    b_re = bre_ref[...]
    b_im = bim_ref[...]
    bbr_ref[...] = f_re * b_re - f_im * b_im
    bbi_ref[...] = f_re * b_im + f_im * b_re


def _s5_disc_call(a_re, a_im, log_dt, b_re, b_im):
    r = DEPTH * N_DIR * S5_GROUPS
    a_re2 = a_re.reshape(r, S5_STATE)
    a_im2 = a_im.reshape(r, S5_STATE)
    wide = S5_STATE * S5_CH
    out_shape = (jax.ShapeDtypeStruct((r, S5_STATE), F32), jax.ShapeDtypeStruct((r, S5_STATE), F32),
                 jax.ShapeDtypeStruct((r, wide), F32), jax.ShapeDtypeStruct((r, wide), F32))
    return pl.pallas_call(_s5_disc_kernel, out_shape=out_shape, name="s5_disc")(
        a_re2, a_im2, log_dt.reshape(r, 1),
        jnp.repeat(a_re2, S5_CH, axis=1), jnp.repeat(a_im2, S5_CH, axis=1),
        b_re.reshape(r, wide), b_im.reshape(r, wide))


def _block_diag(blocks):
    g, r, c = blocks.shape[-3:]
    lead = blocks.shape[:-3]
    tiled = jnp.tile(blocks.reshape(lead + (g * r, c)), (1,) * len(lead) + (1, g))
    on_diagonal = (np.arange(g * r)[:, None] // r) == (np.arange(g * c)[None, :] // c)
    return jnp.where(on_diagonal, tiled, 0.0)


def _rope(x, cos, sin_signed, second):
    width = x.shape[-1]
    quarter = ATT_HD // 4
    partner = jnp.where(second, pltpu.roll(x, quarter, axis=1), pltpu.roll(x, width - quarter, axis=1))
    return x * cos + partner * sin_signed


def _pre_kernel(*refs, rope):
    if rope:
        (x_ref, mod_ref, g1_ref, w_ref, wg_ref, blkq_ref, blkk_ref, qg_ref, kg_ref,
         cosq_ref, sinq_ref, cosk_ref, sink_ref, proj_ref, mg_ref, qn_ref, kn_ref, mqk_ref, vt_ref, mgt_ref) = refs
    else:
        (x_ref, mod_ref, g1_ref, w_ref, wg_ref, blkq_ref, blkk_ref, qg_ref, kg_ref,
         proj_ref, mg_ref, qn_ref, kn_ref, mqk_ref, vt_ref, mgt_ref) = refs
    x = x_ref[...]
    ms = jnp.mean(x * x, axis=-1, keepdims=True)
    xn = x * lax.rsqrt(ms + EPS) * g1_ref[...]
    h = _bf(xn * (1.0 + mod_ref[1:2, :]) + mod_ref[0:1, :])
    def proj(c0, c1):
        return _dot(h, w_ref[:, c0:c1])

    q = proj(0, COL_K)
    kvs = proj(COL_K, COL_MQ)
    k = kvs[:, :ATT_KV_WIDTH]
    mqk = proj(COL_MQ, COL_MV)
    mvo = proj(COL_MV, PROJ_MAIN)
    proj_ref[:, AUX_S5:AUX_MO] = kvs[:, 2 * ATT_KV_WIDTH:]
    proj_ref[:, AUX_MO:AUX_V] = mvo[:, M_WIDTH:]
    proj_ref[:, AUX_V:] = kvs[:, ATT_KV_WIDTH:2 * ATT_KV_WIDTH]
    mg = _dot(h, wg_ref[...])
    mg_ref[...] = mg
    mqk_ref[:, :M_WIDTH] = _bf(mqk[:, :M_WIDTH])
    mqk_ref[:, M_WIDTH:] = _bf(mqk[:, M_WIDTH:] * (M_HD ** -0.5))
    vt_ref[...] = mvo[:, :M_WIDTH].T
    mgt_ref[...] = mg.T[:M_GATE_COLS, :]
    qn = q * lax.rsqrt(_seg_mean_sq(q, blkq_ref[...]) + EPS) * qg_ref[...]
    kn = k * lax.rsqrt(_seg_mean_sq(k, blkk_ref[...]) + EPS) * kg_ref[...]
    if rope:
        lane_q = lax.broadcasted_iota(jnp.int32, qn.shape, 1)
        lane_k = lax.broadcasted_iota(jnp.int32, kn.shape, 1)
        qn = _rope(qn, cosq_ref[...], sinq_ref[...], (lane_q & (ATT_HD // 4)) != 0)
        kn = _rope(kn, cosk_ref[...], sink_ref[...], (lane_k & (ATT_HD // 4)) != 0)
    qn_ref[...] = qn
    kn_ref[...] = kn


def _rope_tables(seq_len):
    n_rows = seq_len // GRID_W
    row = np.repeat(np.arange(n_rows), GRID_W)
    col = np.tile(np.arange(GRID_W), n_rows)
    half = ATT_HD // 2
    inv_freq = (1.0 / (np.float32(ROPE_THETA) ** (np.arange(0, half, 2, dtype=np.float32) / half))).astype(np.float32)

    def tables(pos):
        ang = pos.astype(np.float32)[:, None] * inv_freq[None, :]
        cos = np.cos(ang)
        sin = np.sin(ang)
        return np.concatenate([cos, cos], axis=-1), np.concatenate([-sin, sin], axis=-1)

    cr, sr = tables(row)
    cc, sc = tables(col)
    cos = np.concatenate([cr, cc], axis=-1)
    sin = np.concatenate([sr, sc], axis=-1)
    return tuple(jnp.asarray(np.tile(t, (1, n)), dtype=F32)
                 for t, n in ((cos, ATT_HEADS), (sin, ATT_HEADS), (cos, ATT_KV_HEADS), (sin, ATT_KV_HEADS)))


def _layered(layer, block, index_map, **kw):
    return pl.BlockSpec((None,) + tuple(block), lambda *g: (layer,) + tuple(index_map(*g)), **kw)


def _mod_spec(layer, row_of):
    return pl.BlockSpec((None, None, 6, D_MODEL), lambda *g: (layer, row_of(*g), 0, 0))


def _pre_call(x2, layer, pp, rope_tabs, seq_len, per_batch_mod):
    t = x2.shape[0]
    tm = min(PRE_TM, seq_len)
    tiles_per_seq = seq_len // tm
    rope = rope_tabs is not None
    if per_batch_mod:
        mod_row = lambda i: 1 + i // tiles_per_seq
    else:
        mod_row = lambda i: 0
    const = lambda i: (0, 0)
    row = lambda i: (i, 0)
    in_specs = [pl.BlockSpec((tm, D_MODEL), row),
                _mod_spec(layer, mod_row),
                _layered(layer, (1, D_MODEL), const),
                _layered(layer, (D_MODEL, PROJ_MAIN), const),
                _layered(layer, (D_MODEL, LANES), const),
                pl.BlockSpec((ATT_WIDTH, ATT_WIDTH), const),
                pl.BlockSpec((ATT_KV_WIDTH, ATT_KV_WIDTH), const),
                _layered(layer, (1, ATT_WIDTH), const),
                _layered(layer, (1, ATT_KV_WIDTH), const)]
    args = [x2, pp['mod'], pp['g1'], pp['w_in'], pp['w_gatecols'], pp['blkq'], pp['blkk'], pp['qg'], pp['kg']]
    if rope:
        pos_map = lambda i: (i % tiles_per_seq, 0)
        in_specs += [pl.BlockSpec((tm, ATT_WIDTH), pos_map), pl.BlockSpec((tm, ATT_WIDTH), pos_map),
                     pl.BlockSpec((tm, ATT_KV_WIDTH), pos_map), pl.BlockSpec((tm, ATT_KV_WIDTH), pos_map)]
        args += list(rope_tabs)
    time_on_lanes = lambda i: (i // tiles_per_seq, 0, i % tiles_per_seq)
    batch = t // seq_len
    out_specs = [pl.BlockSpec((tm, AUX_WIDTH), row),
                 pl.BlockSpec((tm, LANES), row),
                 pl.BlockSpec((tm, ATT_WIDTH), row),
                 pl.BlockSpec((tm, ATT_KV_WIDTH), row),
                 pl.BlockSpec((tm, 2 * M_WIDTH), row),
                 pl.BlockSpec((None, M_WIDTH, tm), time_on_lanes),
                 pl.BlockSpec((None, M_GATE_COLS, tm), time_on_lanes)]
    out_shape = [jax.ShapeDtypeStruct((t, AUX_WIDTH), F32), jax.ShapeDtypeStruct((t, LANES), F32),
                 jax.ShapeDtypeStruct((t, ATT_WIDTH), F32), jax.ShapeDtypeStruct((t, ATT_KV_WIDTH), F32),
                 jax.ShapeDtypeStruct((t, 2 * M_WIDTH), BF16),
                 jax.ShapeDtypeStruct((batch, M_WIDTH, seq_len), F32),
                 jax.ShapeDtypeStruct((batch, M_GATE_COLS, seq_len), F32)]
    return pl.pallas_call(
        functools.partial(_pre_kernel, rope=rope),
        grid=(t // tm,), in_specs=in_specs, out_specs=out_specs, out_shape=out_shape,
        compiler_params=_params("arbitrary"), name="pre_rope" if rope else "pre",
    )(*args)


def _attn_kernel(*refs, cached):
    if cached:
        q_ref, k_ref, v_ref, ck_ref, cv_ref, o_ref = refs
        k_all = _bf(jnp.concatenate([ck_ref[...], k_ref[...]], axis=0))
        v_all = _bf(jnp.concatenate([cv_ref[...], v_ref[...]], axis=0))
    else:
        q_ref, k_ref, v_ref, o_ref = refs
        k_all = _bf(k_ref[...])
        v_all = _bf(v_ref[...])
    scale = ATT_HD ** -0.5 * LOG2_E
    outs = []
    for kvh in range(ATT_KV_HEADS):
        lo = kvh * ATT_HD
        kb = k_all[:, lo:lo + ATT_HD]
        vb = v_all[:, lo:lo + ATT_HD]
        for g in range(ATT_GROUP):
            c0 = (kvh * ATT_GROUP + g) * ATT_HD
            qb = _bf(q_ref[:, c0:c0 + ATT_HD] * scale)
            s = lax.dot_general(qb, kb, (((1,), (1,)), ((), ())), preferred_element_type=F32)
            m = jnp.max(s, axis=-1, keepdims=True)
            e = jnp.exp2(s - m)
            den = jnp.sum(e, axis=-1, keepdims=True)
            outs.append(_dot(_bf(e), vb) / den)
    o_ref[...] = _bf(jnp.concatenate(outs, axis=-1))


def _attn_call(qn, kn, proj, layer, cache_k, cache_v, batch, lq):
    cached = cache_k is not None
    tq = min(ATT_TQ, lq)
    nq = lq // tq
    in_specs = [pl.BlockSpec((tq, ATT_WIDTH), lambda b, i: (b * nq + i, 0)),
                pl.BlockSpec((lq, ATT_KV_WIDTH), lambda b, i: (b, 0)),
                pl.BlockSpec((lq, ATT_KV_WIDTH), lambda b, i: (b, AUX_V // ATT_KV_WIDTH))]
    args = [qn, kn, proj]
    if cached:
        past = cache_k.shape[2]
        cache_spec = pl.BlockSpec((None, None, past, ATT_KV_WIDTH), lambda b, i: (b, layer, 0, 0))
        in_specs += [cache_spec, cache_spec]
        args += [cache_k, cache_v]
    return pl.pallas_call(
        functools.partial(_attn_kernel, cached=cached),
        grid=(batch, nq), in_specs=in_specs,
        out_specs=pl.BlockSpec((tq, ATT_WIDTH), lambda b, i: (b * nq + i, 0)),
        out_shape=jax.ShapeDtypeStruct((batch * lq, ATT_WIDTH), BF16),
        compiler_params=_params("arbitrary", "arbitrary"), name="attn_cached" if cached else "attn",
    )(*args)


def _s5_kernel(*refs, batch, seq_len, zero_init):
    if zero_init:
        (u_ref, bmat_ref, cre_ref, cim_ref, ar_ref, ai_ref, d_ref, gw_ref, gb_ref,
         out_ref, fr_ref, fi_ref, utb_ref, ytb_ref, xs_ref) = refs
        ir_ref = ii_ref = None
    else:
        (u_ref, bmat_ref, cre_ref, cim_ref, ar_ref, ai_ref, d_ref, gw_ref, gb_ref, ir_ref, ii_ref,
         out_ref, fr_ref, fi_ref, utb_ref, ytb_ref, xs_ref) = refs
    rows = S5_ROWS
    tc = rows // batch
    n_chunks = seq_len // tc
    n_sub = batch // SUBLANES
    halves = S5_WIDTH // LANES

    def aligned(x, m):
        return x if isinstance(x, int) else pl.multiple_of(x, m)

    for b in range(batch):
        for hf in range(halves):
            utb_ref[hf, pl.ds(b, seq_len, stride=batch), :] = (
                u_ref[b * seq_len:(b + 1) * seq_len, hf * LANES:(hf + 1) * LANES])
    for hf in range(halves):
        ytb_ref[hf] = utb_ref[hf] * d_ref[:, hf * LANES:(hf + 1) * LANES]

    def chunk_of(d, i):
        return i if d == 0 else n_chunks - 1 - i

    def stage_in(i, slot):
        for d in range(N_DIR):
            r0 = aligned(chunk_of(d, i) * rows, rows)
            u_c = jnp.concatenate([utb_ref[hf, pl.ds(r0, rows), :] for hf in range(halves)], axis=-1)
            xs_ref[slot, d] = _dot(_bf(u_c), bmat_ref[d])

    def stage_scan(slot, carry):
        new_carry = []
        for d in range(N_DIR):
            a_re = ar_ref[d]
            a_im = ai_ref[d]
            per_sub = []
            for sub in range(n_sub):
                s_re, s_im = carry[d][sub]
                for t in range(tc):
                    r = (t if d == 0 else tc - 1 - t) * batch + sub * SUBLANES
                    x_re = xs_ref[slot, d, r:r + SUBLANES, :S5_FLAT]
                    x_im = xs_ref[slot, d, r:r + SUBLANES, S5_FLAT:]
                    s_re, s_im = (a_re * s_re - a_im * s_im + x_re, a_re * s_im + a_im * s_re + x_im)
                    xs_ref[slot, d, r:r + SUBLANES, :S5_FLAT] = s_re
                    xs_ref[slot, d, r:r + SUBLANES, S5_FLAT:] = s_im
                per_sub.append((s_re, s_im))
            new_carry.append(tuple(per_sub))
        return tuple(new_carry)

    def stage_out(i, slot):
        for d in range(N_DIR):
            r0 = aligned(chunk_of(d, i) * rows, rows)
            y = (_dot(_bf(xs_ref[slot, d, :, :S5_FLAT]), cre_ref[d])
                 - _dot(_bf(xs_ref[slot, d, :, S5_FLAT:]), cim_ref[d]))
            for hf in range(halves):
                ytb_ref[hf, pl.ds(r0, rows), :] += y[:, hf * LANES:(hf + 1) * LANES]

    def start_state(ref, d, sub):
        if zero_init:
            return jnp.zeros((SUBLANES, S5_FLAT), F32)
        return ref[d, sub * SUBLANES:(sub + 1) * SUBLANES, :]

    carry = tuple(tuple((start_state(ir_ref, d, sub), start_state(ii_ref, d, sub)) for sub in range(n_sub))
                  for d in range(N_DIR))
    def step(i, phase, carry):
        stage_in(i + 1, (phase + 1) % S5_SLOTS)
        carry = stage_scan(phase, carry)
        stage_out(i - 1, (phase - 1) % S5_SLOTS)
        return carry

    stage_in(0, 0)
    stage_in(1, 1)
    carry = stage_scan(0, carry)
    n_steady = n_chunks - 2
    n_peeled = n_steady % S5_SLOTS
    for i in range(1, 1 + n_peeled):
        carry = step(i, i % S5_SLOTS, carry)
    first = 1 + n_peeled

    def body(g, carry):
        for k in range(S5_SLOTS):
            carry = step(first + g * S5_SLOTS + k, (first + k) % S5_SLOTS, carry)
        return carry

    carry = lax.fori_loop(0, n_steady // S5_SLOTS, body, carry)
    last = n_chunks - 1
    carry = stage_scan(last % S5_SLOTS, carry)
    stage_out(last - 1, (last - 1) % S5_SLOTS)
    stage_out(last, last % S5_SLOTS)
    for d in range(N_DIR):
        for sub in range(n_sub):
            fr_ref[d, sub * SUBLANES:(sub + 1) * SUBLANES, :] = carry[d][sub][0]
            fi_ref[d, sub * SUBLANES:(sub + 1) * SUBLANES, :] = carry[d][sub][1]

    for b in range(batch):
        z = jax.nn.gelu(jnp.concatenate(
            [ytb_ref[hf, pl.ds(b, seq_len, stride=batch), :] for hf in range(halves)], axis=-1))
        gate = _sigmoid(_dot(_bf(z), gw_ref[...]) + gb_ref[...])
        out_ref[b * seq_len:(b + 1) * seq_len, :] = _bf(z * gate)


def _s5_call(proj, layer, pp, init_re, init_im, batch, seq_len):
    rows = seq_len * batch
    single = pl.Buffered(1)
    zero_init = init_re is None
    names = ['s5_bmat', 's5_cre', 's5_cim', 's5_ab_re', 's5_ab_im', 's5_d', 's5_glu_w', 's5_glu_b']
    args = [pp[n] for n in names] + ([] if zero_init else [init_re, init_im])
    whole = lambda a: _layered(layer, a.shape[1:], lambda i: (0,) * (a.ndim - 1), pipeline_mode=single)
    in_specs = ([pl.BlockSpec((rows, S5_WIDTH), lambda i: (0, AUX_S5 // S5_WIDTH), pipeline_mode=single)]
                + [whole(a) for a in args])
    fin_spec = pl.BlockSpec((N_DIR, batch, S5_FLAT), lambda i: (0, 0, 0))
    out_specs = [pl.BlockSpec((rows, S5_WIDTH), lambda i: (0, 0), pipeline_mode=single), fin_spec, fin_spec]
    out_shape = (jax.ShapeDtypeStruct((rows, S5_WIDTH), BF16),
                 jax.ShapeDtypeStruct((N_DIR, batch, S5_FLAT), F32),
                 jax.ShapeDtypeStruct((N_DIR, batch, S5_FLAT), F32))
    return pl.pallas_call(
        functools.partial(_s5_kernel, batch=batch, seq_len=seq_len, zero_init=zero_init),
        grid=(1,), in_specs=in_specs, out_specs=out_specs, out_shape=out_shape,
        scratch_shapes=[pltpu.VMEM((S5_WIDTH // LANES, rows, LANES), F32),
                        pltpu.VMEM((S5_WIDTH // LANES, rows, LANES), F32),
                        pltpu.VMEM((S5_SLOTS, N_DIR, S5_ROWS, 2 * S5_FLAT), F32)],
        compiler_params=_params("arbitrary"), name="s5",
    )(proj, *args)


def _mlstm_kernel(*refs, seq_len, zero_init):
    if zero_init:
        (q_ref, k_ref, vt_ref, mg_ref, mgt_ref, gbr_ref, gbc_ref, tril_ref, triu_ref, ones_ref,
         ht_ref, cn_ref, m_ref) = refs
    else:
        (q_ref, k_ref, vt_ref, mg_ref, mgt_ref, gbr_ref, gbc_ref, tril_ref, triu_ref, ones_ref,
         c0_ref, n0_ref, m0_ref, ht_ref, cn_ref, m_ref) = refs
    _mlstm_body(q_ref, k_ref, vt_ref, mg_ref, mgt_ref, gbr_ref, gbc_ref, tril_ref, triu_ref, ones_ref,
                None if zero_init else (c0_ref, n0_ref, m0_ref), ht_ref, cn_ref, m_ref, seq_len)


def _mlstm_body(q_ref, k_ref, vt_ref, mg_ref, mgt_ref, gbr_ref, gbc_ref, tril_ref, triu_ref, ones_ref,
                init_refs, ht_ref, cn_ref, m_ref, seq_len):
    tile = M_TILE
    n_chunks = seq_len // tile
    bg = q_ref.shape[0]
    if init_refs is None:
        cn_ref[...] = jnp.zeros_like(cn_ref)
        m_ref[...] = jnp.zeros_like(m_ref)
    else:
        c0_ref, n0_ref, m0_ref = init_refs
        cn_ref[:, :, :M_HD, :] = c0_ref[...]
        cn_ref[:, :, M_HD:, :] = n0_ref[...]
        m_ref[...] = m0_ref[...]
    s_ids = lax.broadcasted_iota(jnp.int32, (tile, tile), 0)
    t_ids = lax.broadcasted_iota(jnp.int32, (tile, tile), 1)
    first_row = lax.broadcasted_iota(jnp.int32, (CN_ROWS - M_HD, tile), 0) == 0
    neg_inf = jnp.float32(-jnp.inf)
    ones = ones_ref[...]

    def direction(d, c):
        t0 = pl.multiple_of(c * tile, tile)
        col_mat = tril_ref[...] if d == 0 else triu_ref[...]
        row_mat = triu_ref[...] if d == 0 else tril_ref[...]
        valid = (s_ids <= t_ids) if d == 0 else (s_ids >= t_ids)
        gt = mgt_ref[:, :, pl.ds(t0, tile)] + gbc_ref[...]
        lft_hi, lft_lo = _split(_log_sigmoid(gt).reshape(bg * M_GATE_COLS, tile))
        b_rows = (_dot(lft_hi, row_mat) + _dot(lft_lo, row_mat)).reshape(bg, M_GATE_COLS, tile)
        totals = (_dot(lft_hi, ones) + _dot(lft_lo, ones)).reshape(bg, M_GATE_COLS, tile)
        gc = mg_ref[:, pl.ds(t0, tile), :] + gbr_ref[...]
        lfc_hi, lfc_lo = _split(_log_sigmoid(gc))
        col_b = jnp.broadcast_to(col_mat, (bg, tile, tile))
        b_cols = (jnp.einsum('bts,bsl->btl', col_b, lfc_hi, preferred_element_type=F32)
                  + jnp.einsum('bts,bsl->btl', col_b, lfc_lo, preferred_element_type=F32))
        for h in range(M_HEADS):
            j = d * M_HEADS + h
            fj = M_CHAINS + j
            rows = slice(h * M_HD, (h + 1) * M_HD)
            q_c = q_ref[:, pl.ds(t0, tile), rows]
            k_c = k_ref[:, pl.ds(t0, tile), rows]
            vt_c = vt_ref[:, rows, pl.ds(t0, tile)]
            b_t = b_rows[:, fj:fj + 1, :]
            li_t = gt[:, j:j + 1, :]
            tot = totals[:, fj:fj + 1, :]
            m_prev = m_ref[:, j]
            cn_prev = cn_ref[:, j]

            c_col = gc[:, :, j:j + 1] - b_cols[:, :, fj:fj + 1]
            cm = jnp.where(valid, c_col, neg_inf)
            inter = b_t + m_prev
            m_t = jnp.maximum(inter, b_t + jnp.max(cm, axis=1, keepdims=True))
            w_t = jnp.exp(cm + (b_t - m_t))
            s_inter = jnp.exp(inter - m_t)
            sc_t = jnp.einsum('bse,bte->bst', k_c, q_c, preferred_element_type=F32) * w_t
            num = jnp.einsum('bds,bst->bdt', _bf(vt_c), _bf(sc_t), preferred_element_type=F32)
            ext = jnp.einsum('bre,bte->brt', _bf(cn_prev), q_c, preferred_element_type=F32)
            num = num + s_inter * ext[:, :M_HD, :]
            den = jnp.sum(sc_t, axis=1, keepdims=True) + s_inter * ext[:, M_HD:M_HD + 1, :]
            ht_ref[d, :, rows, pl.ds(t0, tile)] = num / jnp.maximum(jnp.abs(den), jnp.exp(-m_t))

            g_row = tot - b_t + li_t
            m_new = jnp.maximum(tot + m_prev, jnp.max(g_row, axis=-1, keepdims=True))
            wk = jnp.exp(g_row - m_new)
            decay = jnp.exp(tot + m_prev - m_new)
            vw = jnp.concatenate([vt_c * wk, jnp.where(first_row, wk, 0.0)], axis=1)
            cn_ref[:, j] = (decay[:, :, :M_HD] * cn_prev
                            + jnp.einsum('brs,bse->bre', _bf(vw), k_c, preferred_element_type=F32))
            m_ref[:, j] = m_new

    def body(i, _):
        direction(0, i)
        direction(1, n_chunks - 1 - i)
        return 0

    lax.fori_loop(0, n_chunks, body, 0)


def _mlstm_call(mqk3, vt, mg3, mgt, layer, pp, init, batch, seq_len):
    bg = min(batch // 2, M_BG * M_BG_SEQ // seq_len)
    const2 = lambda g: (0, 0)
    lead3 = lambda g: (g, 0, 0)
    lead4 = lambda g: (g, 0, 0, 0)
    in_specs = [pl.BlockSpec((bg, seq_len, M_WIDTH), lead3),
                pl.BlockSpec((bg, seq_len, M_WIDTH), lambda g: (g, 0, 1)),
                pl.BlockSpec((bg, M_WIDTH, seq_len), lead3),
                pl.BlockSpec((bg, seq_len, LANES), lead3),
                pl.BlockSpec((bg, M_GATE_COLS, seq_len), lead3),
                _layered(layer, (1, LANES), const2),
                _layered(layer, (M_GATE_COLS, 1), const2),
                pl.BlockSpec((M_TILE, M_TILE), const2),
                pl.BlockSpec((M_TILE, M_TILE), const2),
                pl.BlockSpec((M_TILE, M_TILE), const2)]
    args = [mqk3, mqk3, vt, mg3, mgt, pp['m_gb_row'], pp['m_gb_col'], pp['tril'], pp['triu'], pp['ones']]
    if init is not None:
        state_spec = lambda *tail: pl.BlockSpec((bg, None, M_CHAINS) + tail, lambda g: (g, layer, 0, 0, 0))
        in_specs += [state_spec(M_HD, M_HD), state_spec(CN_ROWS - M_HD, M_HD), state_spec(1, LANES)]
        args += list(init)
    out_specs = [pl.BlockSpec((N_DIR, bg, M_WIDTH, seq_len), lambda g: (0, g, 0, 0)),
                 pl.BlockSpec((bg, M_CHAINS, CN_ROWS, M_HD), lead4),
                 pl.BlockSpec((bg, M_CHAINS, 1, LANES), lead4)]
    out_shape = [jax.ShapeDtypeStruct((N_DIR, batch, M_WIDTH, seq_len), F32),
                 jax.ShapeDtypeStruct((batch, M_CHAINS, CN_ROWS, M_HD), F32),
                 jax.ShapeDtypeStruct((batch, M_CHAINS, 1, LANES), F32)]
    return pl.pallas_call(
        functools.partial(_mlstm_kernel, seq_len=seq_len, zero_init=init is None),
        grid=(batch // bg,), in_specs=in_specs, out_specs=out_specs, out_shape=out_shape,
        compiler_params=_params("arbitrary"), name="mlstm",
    )(*args)


def _tail_kernel(x_ref, att_ref, s5_ref, ht_ref, mo_ref, mod_ref, g2_ref, ng_ref, blkm_ref, wo_ref,
                 wg_ref, wu_ref, cw_ref, cb_ref, wd_ref, o_ref, h2_ref, gated_ref, *, seq_len):
    tm = x_ref.shape[0]
    mh = jnp.concatenate([(ht_ref[0, s] + ht_ref[1, s]).T for s in range(ht_ref.shape[1])], axis=0)
    ml = mh * lax.rsqrt(_seg_mean_sq(mh, blkm_ref[...]) + EPS) * ng_ref[...] * _sigmoid(mo_ref[...])
    mixers = jnp.concatenate([att_ref[...], s5_ref[...], _bf(ml)], axis=-1)
    for c0 in range(0, D_MODEL, TAIL_OUT_CHUNK):
        cols = slice(c0, c0 + TAIL_OUT_CHUNK)
        o_ref[:, cols] = x_ref[:, cols] + mod_ref[2:3, cols] * _dot(mixers, wo_ref[:, cols])
    x1 = o_ref[...]
    ms = jnp.mean(x1 * x1, axis=-1, keepdims=True)
    xn = x1 * lax.rsqrt(ms + EPS) * g2_ref[...]
    h2_ref[...] = _bf(xn * (1.0 + mod_ref[4:5, :]) + mod_ref[3:4, :])

    half = TAIL_HALF
    for r0 in range(0, tm, half):
        h2 = h2_ref[r0:r0 + half, :]
        p0 = max(r0 - BF16_ROWS, 0)
        n0 = min(r0 + half, tm - BF16_ROWS)
        h2_ext = jnp.concatenate([h2, h2_ref[p0:p0 + BF16_ROWS, :], h2_ref[n0:n0 + BF16_ROWS, :]], axis=0)
        for c0 in range(0, D_FF, FF_CHUNK):
            width = min(FF_CHUNK, D_FF - c0)
            cols = slice(c0, c0 + width)
            row = lax.broadcasted_iota(jnp.int32, (half, width), 0)
            half_start = row == 0
            half_end = row == half - 1
            pos = (r0 + row) % seq_len
            seq_start = pos == 0
            seq_end = pos == seq_len - 1
            a_ext = _dot(h2_ext, wg_ref[:, cols])
            a = a_ext[:half]
            before = a_ext[half + BF16_ROWS - 1:half + BF16_ROWS, :]
            after = a_ext[half + BF16_ROWS:half + BF16_ROWS + 1, :]
            a_prev = jnp.where(half_start, before, pltpu.roll(a, 1, axis=0))
            a_next = jnp.where(half_end, after, pltpu.roll(a, half - 1, axis=0))
            a_prev = jnp.where(seq_start, 0.0, a_prev)
            a_next = jnp.where(seq_end, 0.0, a_next)
            ac = a_prev * cw_ref[0:1, cols] + a * cw_ref[1:2, cols] + a_next * cw_ref[2:3, cols] + cb_ref[:, cols]
            up = _dot(h2, wu_ref[:, cols])
            gated_ref[:, cols] = _bf(ac * _sigmoid(ac) * up)
        o_ref[r0:r0 + half, :] = o_ref[r0:r0 + half, :] + mod_ref[5:6, :] * _dot(gated_ref[...], wd_ref[...])


def _tail_call(x2, att, s5o, ht, proj, layer, pp, seq_len, per_batch_mod):
    t = x2.shape[0]
    tm = TAIL_TM
    seqs_per_tile = tm // seq_len
    if per_batch_mod:
        mod_row = lambda i: 1 + i * seqs_per_tile
    else:
        mod_row = lambda i: 0
    row = lambda i: (i, 0)
    const = lambda i: (0, 0)
    single = pl.Buffered(1)
    whole = lambda shape: _layered(layer, shape, const, pipeline_mode=single)
    in_specs = [pl.BlockSpec((tm, D_MODEL), row),
                pl.BlockSpec((tm, ATT_WIDTH), row),
                pl.BlockSpec((tm, S5_WIDTH), row),
                pl.BlockSpec((N_DIR, seqs_per_tile, M_WIDTH, seq_len), lambda i: (0, i, 0, 0)),
                pl.BlockSpec((tm, M_WIDTH), lambda i: (i, AUX_MO // M_WIDTH)),
                _mod_spec(layer, mod_row),
                _layered(layer, (1, D_MODEL), const),
                _layered(layer, (1, M_WIDTH), const),
                pl.BlockSpec((M_WIDTH, M_WIDTH), const),
                whole((D_MODEL, D_MODEL)),
                whole((D_MODEL, D_FF)), whole((D_MODEL, D_FF)), whole((3, D_FF)), whole((1, D_FF)),
                whole((D_FF, D_MODEL))]
    return pl.pallas_call(
        functools.partial(_tail_kernel, seq_len=seq_len),
        grid=(t // tm,), in_specs=in_specs,
        out_specs=pl.BlockSpec((tm, D_MODEL), row),
        out_shape=jax.ShapeDtypeStruct((t, D_MODEL), F32),
        scratch_shapes=[pltpu.VMEM((tm, D_MODEL), BF16), pltpu.VMEM((TAIL_HALF, D_FF), BF16)],
        compiler_params=_params("arbitrary"), name="tail",
    )(x2, att, s5o, ht, proj, pp['mod'], pp['g2'], pp['m_ng'], pp['blkm'], pp['w_out'],
      pp['ffn_w_gate'], pp['ffn_w_up'], pp['ffn_conv_w'], pp['ffn_conv_b'], pp['ffn_w_down'])


def _trunk_layer(x2, batch, seq_len, layer, pp, ctx, rope_tabs):
    latent = ctx is not None
    proj, mg, qn, kn, mqk, vt, mgt = _pre_call(x2, layer, pp, rope_tabs if latent else None, seq_len, latent)
    cache_k, cache_v, init_re, init_im, m_init = ctx if latent else (None,) * 5

    att = _attn_call(qn, kn, proj, layer, cache_k, cache_v, batch, seq_len)
    s5o, fin_re, fin_im = _s5_call(proj, layer, pp, init_re, init_im, batch, seq_len)
    ht, cn_f, m_f = _mlstm_call(mqk.reshape(batch, seq_len, 2 * M_WIDTH), vt, mg.reshape(batch, seq_len, LANES), mgt,
                                layer, pp, m_init, batch, seq_len)
    x_out = _tail_call(x2, att, s5o, ht, proj, layer, pp, seq_len, latent)
    if latent:
        return x_out, None
    states = (kn.reshape(batch, seq_len, ATT_KV_HEADS, ATT_HD),
              proj[:, AUX_V:].reshape(batch, seq_len, ATT_KV_HEADS, ATT_HD),
              fin_re.transpose(1, 0, 2).reshape(batch, N_DIR, S5_GROUPS, S5_STATE),
              fin_im.transpose(1, 0, 2).reshape(batch, N_DIR, S5_GROUPS, S5_STATE),
              cn_f[:, :, :M_HD, :].reshape(batch, N_DIR, M_HEADS, M_HD, M_HD),
              cn_f[:, :, M_HD, :].reshape(batch, N_DIR, M_HEADS, M_HD),
              m_f[:, :, 0, 0].reshape(batch, N_DIR, M_HEADS))
    return x_out, states


def _head_block(width, head_dim):
    ids = np.arange(width) // head_dim
    return jnp.asarray((ids[:, None] == ids[None, :]).astype(np.float32) / head_dim, dtype=BF16)


def kernel(x_prompt, x_sample, c, cache_attn_k, cache_attn_v, state_s5_re, state_s5_im, state_mlstm_C, state_mlstm_n, state_mlstm_m, c_ctx, ada_w, ada_b, norm1_g, norm2_g, w_in, q_norm_g, k_norm_g, s5_a_re, s5_a_im, s5_log_dt, s5_b_re, s5_b_im, s5_c_re, s5_c_im, s5_d, s5_glu_w, s5_glu_b, m_gate_b, m_norm_g, w_out, ffn_w_gate, ffn_w_up, ffn_conv_w, ffn_conv_b, ffn_w_down):
    batch, seq = x_prompt.shape[0], x_prompt.shape[1]
    dec_batch, dec_seq = x_sample.shape[0], x_sample.shape[1]

    n_mod_rows = 2 * SUBLANES
    cvec = jnp.zeros((n_mod_rows, D_MODEL), F32).at[0].set(c_ctx).at[1:1 + dec_batch].set(c)
    mod_all = _ada_call(cvec, ada_w, ada_b).reshape(DEPTH, n_mod_rows, 6, D_MODEL)

    ab_re, ab_im, bb_re, bb_im = _s5_disc_call(s5_a_re, s5_a_im, s5_log_dt, s5_b_re, s5_b_im)
    ab_re = ab_re.reshape(DEPTH, N_DIR, 1, S5_FLAT)
    ab_im = ab_im.reshape(DEPTH, N_DIR, 1, S5_FLAT)
    bb_re = bb_re.reshape(DEPTH, N_DIR, S5_GROUPS, S5_STATE, S5_CH)
    bb_im = bb_im.reshape(DEPTH, N_DIR, S5_GROUPS, S5_STATE, S5_CH)

    rope_tabs = _rope_tables(dec_seq)
    tri = np.tril(np.ones((M_TILE, M_TILE), np.float32))
    tril = jnp.asarray(tri, dtype=BF16)
    triu = jnp.asarray(tri.T, dtype=BF16)
    ones = jnp.ones((M_TILE, M_TILE), BF16)
    blkq = _head_block(ATT_WIDTH, ATT_HD)
    blkk = _head_block(ATT_KV_WIDTH, ATT_HD)
    blkm = _head_block(M_WIDTH, M_HD)

    w_gate_cols_b = _bf(w_in[:, :, PROJ_MAIN:])
    gb = m_gate_b.reshape(DEPTH, M_GATE_COLS)
    pp = dict(
        mod=mod_all, g1=norm1_g.reshape(DEPTH, 1, D_MODEL), g2=norm2_g.reshape(DEPTH, 1, D_MODEL),
        w_in=_bf(w_in[:, :, :PROJ_MAIN]),
        w_gatecols=jnp.zeros((DEPTH, D_MODEL, LANES), BF16).at[:, :, :M_GATE_COLS].set(w_gate_cols_b),
        blkq=blkq, blkk=blkk, blkm=blkm,
        qg=jnp.tile(q_norm_g, (1, ATT_HEADS)).reshape(DEPTH, 1, ATT_WIDTH),
        kg=jnp.tile(k_norm_g, (1, ATT_KV_HEADS)).reshape(DEPTH, 1, ATT_KV_WIDTH),
        s5_bmat=_bf(jnp.concatenate([_block_diag(jnp.swapaxes(bb_re, -1, -2)),
                                     _block_diag(jnp.swapaxes(bb_im, -1, -2))], axis=-1)),
        s5_cre=_bf(_block_diag(jnp.swapaxes(s5_c_re, -1, -2))),
        s5_cim=_bf(_block_diag(jnp.swapaxes(s5_c_im, -1, -2))),
        s5_ab_re=jnp.broadcast_to(ab_re, (DEPTH, N_DIR, SUBLANES, S5_FLAT)),
        s5_ab_im=jnp.broadcast_to(ab_im, (DEPTH, N_DIR, SUBLANES, S5_FLAT)),
        s5_d=s5_d.reshape(DEPTH, 1, S5_WIDTH), s5_glu_w=_bf(s5_glu_w), s5_glu_b=s5_glu_b.reshape(DEPTH, 1, S5_WIDTH),
        m_gb_row=jnp.zeros((DEPTH, 1, LANES), F32).at[:, 0, :M_GATE_COLS].set(gb),
        m_gb_col=gb.reshape(DEPTH, M_GATE_COLS, 1),
        m_ng=jnp.tile(m_norm_g, (1, M_HEADS)).reshape(DEPTH, 1, M_WIDTH), tril=tril, triu=triu, ones=ones,
        w_out=_bf(w_out), ffn_w_gate=_bf(ffn_w_gate), ffn_w_up=_bf(ffn_w_up),
        ffn_conv_w=ffn_conv_w, ffn_conv_b=ffn_conv_b.reshape(DEPTH, 1, D_FF), ffn_w_down=_bf(ffn_w_down))

    past = cache_attn_k.shape[2]
    n_pad = CN_ROWS - M_HD
    s5_init = lambda st: st.reshape(dec_batch, DEPTH, N_DIR, S5_FLAT).transpose(1, 2, 0, 3)
    m_init = (state_mlstm_C.reshape(dec_batch, DEPTH, M_CHAINS, M_HD, M_HD),
              jnp.zeros((dec_batch, DEPTH, M_CHAINS, n_pad, M_HD), F32).at[:, :, :, 0, :].set(
                  state_mlstm_n.reshape(dec_batch, DEPTH, M_CHAINS, M_HD)),
              jnp.broadcast_to(state_mlstm_m.reshape(dec_batch, DEPTH, M_CHAINS, 1, 1),
                               (dec_batch, DEPTH, M_CHAINS, 1, LANES)))
    ctx = (cache_attn_k.reshape(dec_batch, DEPTH, past, ATT_KV_WIDTH),
           cache_attn_v.reshape(dec_batch, DEPTH, past, ATT_KV_WIDTH),
           s5_init(state_s5_re), s5_init(state_s5_im), m_init)

    xp = x_prompt.reshape(batch * seq, D_MODEL)
    xs = x_sample.reshape(dec_batch * dec_seq, D_MODEL)
    ctx_out = []
    for l in range(DEPTH):
        xp, st = _trunk_layer(xp, batch, seq, l, pp, None, None)
        ctx_out.append(st)
        xs, _ = _trunk_layer(xs, dec_batch, dec_seq, l, pp, ctx, rope_tabs)
    outs = [jnp.stack([s[i] for s in ctx_out], axis=1) for i in range(7)]
    return (xp.reshape(batch, seq, D_MODEL), xs.reshape(dec_batch, dec_seq, D_MODEL), *outs)
```

```python
import functools

import numpy as np
import jax
import jax.numpy as jnp
from jax import lax
from jax.experimental import pallas as pl
from jax.experimental.pallas import tpu as pltpu

F32 = jnp.float32
BF16 = jnp.bfloat16

D_MODEL = 1024
DEPTH = 2
GRID_W = 64
N_DIR = 2
EPS = 1e-6
ATT_HD = 64
ATT_WIDTH = 512
ATT_HEADS = 8
ATT_KV_HEADS = 2
ATT_GROUP = ATT_HEADS // ATT_KV_HEADS
ATT_KV_WIDTH = ATT_KV_HEADS * ATT_HD
ROPE_THETA = 10000.0
LOG2_E = 1.4426950408889634
S5_CH = 16
S5_STATE = 64
S5_WIDTH = 256
S5_GROUPS = 16
S5_FLAT = S5_GROUPS * S5_STATE
M_HD = 64
M_WIDTH = 256
M_HEADS = 4
M_GATE_COLS = 2 * N_DIR * M_HEADS
M_CHAINS = N_DIR * M_HEADS
D_FF = 2816
PROJ_MAIN = 2048
COL_K = ATT_WIDTH
COL_V = COL_K + ATT_KV_WIDTH
COL_S5 = COL_V + ATT_KV_WIDTH
COL_MQ = COL_S5 + S5_WIDTH
COL_MK = COL_MQ + M_WIDTH
COL_MV = COL_MK + M_WIDTH
COL_MO = COL_MV + M_WIDTH
AUX_S5 = 0
AUX_MO = AUX_S5 + S5_WIDTH
AUX_V = AUX_MO + M_WIDTH
AUX_WIDTH = AUX_V + ATT_KV_WIDTH
LANES = 128
SUBLANES = 8
BF16_ROWS = 16
VMEM_LIMIT = 56 * 1024 * 1024

ADA_TN = 1024
PRE_TM = 512
ATT_TQ = 512
S5_ROWS = 256
S5_SLOTS = 3
M_TILE = 128
M_BG = 4
M_BG_SEQ = 1024
CN_ROWS = M_HD + BF16_ROWS
TAIL_TM = 1024
TAIL_OUT_CHUNK = 256
TAIL_HALF = 512
FF_CHUNK = 512


def _bf(x):
    return x.astype(BF16)


def _dot(a, b):
    return jnp.dot(a, b, preferred_element_type=F32)


def _split(x):
    hi = _bf(x)
    lo = _bf(x - hi.astype(F32))
    return hi, lo


def _seg_mean_sq(x, blk):
    hi, lo = _split(x * x)
    return _dot(hi, blk) + _dot(lo, blk)


def _sigmoid(x):
    return 1.0 / (1.0 + jnp.exp(-x))


def _log_sigmoid(x):
    return -(jnp.maximum(-x, 0.0) + jnp.log1p(jnp.exp(-jnp.abs(x))))


def _params(*sem):
    return pltpu.CompilerParams(dimension_semantics=sem, vmem_limit_bytes=VMEM_LIMIT)


def _ada_kernel(c_ref, w_ref, b_ref, o_ref):
    c = c_ref[...]
    s = c * _sigmoid(c)
    o_ref[...] = _dot(_bf(s), _bf(w_ref[...])) + b_ref[...]


def _ada_call(cvec, ada_w, ada_b):
    rows = cvec.shape[0]
    tn = ADA_TN
    n = ada_w.shape[-1]
    return pl.pallas_call(
        _ada_kernel,
        grid=(DEPTH, n // tn),
        in_specs=[pl.BlockSpec((rows, D_MODEL), lambda l, j: (0, 0)),
                  pl.BlockSpec((None, D_MODEL, tn), lambda l, j: (l, 0, j)),
                  pl.BlockSpec((None, 1, tn), lambda l, j: (l, 0, j))],
        out_specs=pl.BlockSpec((None, rows, tn), lambda l, j: (l, 0, j)),
        out_shape=jax.ShapeDtypeStruct((DEPTH, rows, n), F32),
        compiler_params=_params("arbitrary", "arbitrary"),
        name="adaln",
    )(cvec, ada_w, ada_b.reshape(DEPTH, 1, n))


def _s5_disc_kernel(are_ref, aim_ref, ldt_ref, arex_ref, aimx_ref, bre_ref, bim_ref,
                    abr_ref, abi_ref, bbr_ref, bbi_ref):
    dt = jnp.exp(ldt_ref[...])

    def disc(a_re, a_im):
        mag = jnp.exp(dt * a_re)
        ab_re = mag * jnp.cos(dt * a_im)
        ab_im = mag * jnp.sin(dt * a_im)
        den = a_re * a_re + a_im * a_im
        nr = ab_re - 1.0
        ni = ab_im
        f_re = (nr * a_re + ni * a_im) / den
        f_im = (ni * a_re - nr * a_im) / den
        return ab_re, ab_im, f_re, f_im

    ab_re, ab_im, _, _ = disc(are_ref[...], aim_ref[...])
    abr_ref[...] = ab_re
    abi_ref[...] = ab_im
    _, _, f_re, f_im = disc(arex_ref[...], aimx_ref[...])
    b_re = bre_ref[...]
    b_im = bim_ref[...]
    bbr_ref[...] = f_re * b_re - f_im * b_im
    bbi_ref[...] = f_re * b_im + f_im * b_re


def _s5_disc_call(a_re, a_im, log_dt, b_re, b_im):
    r = DEPTH * N_DIR * S5_GROUPS
    a_re2 = a_re.reshape(r, S5_STATE)
    a_im2 = a_im.reshape(r, S5_STATE)
    wide = S5_STATE * S5_CH
    out_shape = (jax.ShapeDtypeStruct((r, S5_STATE), F32), jax.ShapeDtypeStruct((r, S5_STATE), F32),
                 jax.ShapeDtypeStruct((r, wide), F32), jax.ShapeDtypeStruct((r, wide), F32))
    return pl.pallas_call(_s5_disc_kernel, out_shape=out_shape, name="s5_disc")(
        a_re2, a_im2, log_dt.reshape(r, 1),
        jnp.repeat(a_re2, S5_CH, axis=1), jnp.repeat(a_im2, S5_CH, axis=1),
        b_re.reshape(r, wide), b_im.reshape(r, wide))


def _block_diag(blocks):
    g, r, c = blocks.shape[-3:]
    lead = blocks.shape[:-3]
    tiled = jnp.tile(blocks.reshape(lead + (g * r, c)), (1,) * len(lead) + (1, g))
    on_diagonal = (np.arange(g * r)[:, None] // r) == (np.arange(g * c)[None, :] // c)
    return jnp.where(on_diagonal, tiled, 0.0)


def _rope(x, cos, sin_signed, second):
    width = x.shape[-1]
    quarter = ATT_HD // 4
    partner = jnp.where(second, pltpu.roll(x, quarter, axis=1), pltpu.roll(x, width - quarter, axis=1))
    return x * cos + partner * sin_signed


def _pre_kernel(*refs, rope):
    if rope:
        (x_ref, mod_ref, g1_ref, w_ref, wg_ref, blkq_ref, blkk_ref, qg_ref, kg_ref,
         cosq_ref, sinq_ref, cosk_ref, sink_ref, proj_ref, mg_ref, qn_ref, kn_ref, mqk_ref, vt_ref, mgt_ref) = refs
    else:
        (x_ref, mod_ref, g1_ref, w_ref, wg_ref, blkq_ref, blkk_ref, qg_ref, kg_ref,
         proj_ref, mg_ref, qn_ref, kn_ref, mqk_ref, vt_ref, mgt_ref) = refs
    x = x_ref[...]
    ms = jnp.mean(x * x, axis=-1, keepdims=True)
    xn = x * lax.rsqrt(ms + EPS) * g1_ref[...]
    h = _bf(xn * (1.0 + mod_ref[1:2, :]) + mod_ref[0:1, :])
    def proj(c0, c1):
        return _dot(h, w_ref[:, c0:c1])

    q = proj(0, COL_K)
    kvs = proj(COL_K, COL_MQ)
    k = kvs[:, :ATT_KV_WIDTH]
    mqk = proj(COL_MQ, COL_MV)
    mvo = proj(COL_MV, PROJ_MAIN)
    proj_ref[:, AUX_S5:AUX_MO] = kvs[:, 2 * ATT_KV_WIDTH:]
    proj_ref[:, AUX_MO:AUX_V] = mvo[:, M_WIDTH:]
    proj_ref[:, AUX_V:] = kvs[:, ATT_KV_WIDTH:2 * ATT_KV_WIDTH]
    mg = _dot(h, wg_ref[...])
    mg_ref[...] = mg
    mqk_ref[:, :M_WIDTH] = _bf(mqk[:, :M_WIDTH])
    mqk_ref[:, M_WIDTH:] = _bf(mqk[:, M_WIDTH:] * (M_HD ** -0.5))
    vt_ref[...] = mvo[:, :M_WIDTH].T
    mgt_ref[...] = mg.T[:M_GATE_COLS, :]
    qn = q * lax.rsqrt(_seg_mean_sq(q, blkq_ref[...]) + EPS) * qg_ref[...]
    kn = k * lax.rsqrt(_seg_mean_sq(k, blkk_ref[...]) + EPS) * kg_ref[...]
    if rope:
        lane_q = lax.broadcasted_iota(jnp.int32, qn.shape, 1)
        lane_k = lax.broadcasted_iota(jnp.int32, kn.shape, 1)
        qn = _rope(qn, cosq_ref[...], sinq_ref[...], (lane_q & (ATT_HD // 4)) != 0)
        kn = _rope(kn, cosk_ref[...], sink_ref[...], (lane_k & (ATT_HD // 4)) != 0)
    qn_ref[...] = qn
    kn_ref[...] = kn


def _rope_tables(seq_len):
    n_rows = seq_len // GRID_W
    row = np.repeat(np.arange(n_rows), GRID_W)
    col = np.tile(np.arange(GRID_W), n_rows)
    half = ATT_HD // 2
    inv_freq = (1.0 / (np.float32(ROPE_THETA) ** (np.arange(0, half, 2, dtype=np.float32) / half))).astype(np.float32)

    def tables(pos):
        ang = pos.astype(np.float32)[:, None] * inv_freq[None, :]
        cos = np.cos(ang)
        sin = np.sin(ang)
        return np.concatenate([cos, cos], axis=-1), np.concatenate([-sin, sin], axis=-1)

    cr, sr = tables(row)
    cc, sc = tables(col)
    cos = np.concatenate([cr, cc], axis=-1)
    sin = np.concatenate([sr, sc], axis=-1)
    return tuple(jnp.asarray(np.tile(t, (1, n)), dtype=F32)
                 for t, n in ((cos, ATT_HEADS), (sin, ATT_HEADS), (cos, ATT_KV_HEADS), (sin, ATT_KV_HEADS)))


def _layered(layer, block, index_map, **kw):
    return pl.BlockSpec((None,) + tuple(block), lambda *g: (layer,) + tuple(index_map(*g)), **kw)


def _mod_spec(layer, row_of):
    return pl.BlockSpec((None, None, 6, D_MODEL), lambda *g: (layer, row_of(*g), 0, 0))


def _pre_call(x2, layer, pp, rope_tabs, seq_len, per_batch_mod):
    t = x2.shape[0]
    tm = min(PRE_TM, seq_len)
    tiles_per_seq = seq_len // tm
    rope = rope_tabs is not None
    if per_batch_mod:
        mod_row = lambda i: 1 + i // tiles_per_seq
    else:
        mod_row = lambda i: 0
    const = lambda i: (0, 0)
    row = lambda i: (i, 0)
    in_specs = [pl.BlockSpec((tm, D_MODEL), row),
                _mod_spec(layer, mod_row),
                _layered(layer, (1, D_MODEL), const),
                _layered(layer, (D_MODEL, PROJ_MAIN), const),
                _layered(layer, (D_MODEL, LANES), const),
                pl.BlockSpec((ATT_WIDTH, ATT_WIDTH), const),
                pl.BlockSpec((ATT_KV_WIDTH, ATT_KV_WIDTH), const),
                _layered(layer, (1, ATT_WIDTH), const),
                _layered(layer, (1, ATT_KV_WIDTH), const)]
    args = [x2, pp['mod'], pp['g1'], pp['w_in'], pp['w_gatecols'], pp['blkq'], pp['blkk'], pp['qg'], pp['kg']]
    if rope:
        pos_map = lambda i: (i % tiles_per_seq, 0)
        in_specs += [pl.BlockSpec((tm, ATT_WIDTH), pos_map), pl.BlockSpec((tm, ATT_WIDTH), pos_map),
                     pl.BlockSpec((tm, ATT_KV_WIDTH), pos_map), pl.BlockSpec((tm, ATT_KV_WIDTH), pos_map)]
        args += list(rope_tabs)
    time_on_lanes = lambda i: (i // tiles_per_seq, 0, i % tiles_per_seq)
    batch = t // seq_len
    out_specs = [pl.BlockSpec((tm, AUX_WIDTH), row),
                 pl.BlockSpec((tm, LANES), row),
                 pl.BlockSpec((tm, ATT_WIDTH), row),
                 pl.BlockSpec((tm, ATT_KV_WIDTH), row),
                 pl.BlockSpec((tm, 2 * M_WIDTH), row),
                 pl.BlockSpec((None, M_WIDTH, tm), time_on_lanes),
                 pl.BlockSpec((None, M_GATE_COLS, tm), time_on_lanes)]
    out_shape = [jax.ShapeDtypeStruct((t, AUX_WIDTH), F32), jax.ShapeDtypeStruct((t, LANES), F32),
                 jax.ShapeDtypeStruct((t, ATT_WIDTH), F32), jax.ShapeDtypeStruct((t, ATT_KV_WIDTH), F32),
                 jax.ShapeDtypeStruct((t, 2 * M_WIDTH), BF16),
                 jax.ShapeDtypeStruct((batch, M_WIDTH, seq_len), F32),
                 jax.ShapeDtypeStruct((batch, M_GATE_COLS, seq_len), F32)]
    return pl.pallas_call(
        functools.partial(_pre_kernel, rope=rope),
        grid=(t // tm,), in_specs=in_specs, out_specs=out_specs, out_shape=out_shape,
        compiler_params=_params("arbitrary"), name="pre_rope" if rope else "pre",
    )(*args)


def _attn_kernel(*refs, cached):
    if cached:
        q_ref, k_ref, v_ref, ck_ref, cv_ref, o_ref = refs
        k_all = _bf(jnp.concatenate([ck_ref[...], k_ref[...]], axis=0))
        v_all = _bf(jnp.concatenate([cv_ref[...], v_ref[...]], axis=0))
    else:
        q_ref, k_ref, v_ref, o_ref = refs
        k_all = _bf(k_ref[...])
        v_all = _bf(v_ref[...])
    scale = ATT_HD ** -0.5 * LOG2_E
    tq = q_ref.shape[0]
    stack = ATT_GROUP if not cached else 1
    outs = []
    for kvh in range(ATT_KV_HEADS):
        lo = kvh * ATT_HD
        kb = k_all[:, lo:lo + ATT_HD]
        vb = v_all[:, lo:lo + ATT_HD]
        for g0 in range(0, ATT_GROUP, stack):
            heads = [(kvh * ATT_GROUP + g) * ATT_HD for g in range(g0, g0 + stack)]
            qb = _bf(jnp.concatenate([q_ref[:, c0:c0 + ATT_HD] for c0 in heads], axis=0) * scale)
            s = lax.dot_general(qb, kb, (((1,), (1,)), ((), ())), preferred_element_type=F32)
            m = jnp.max(s, axis=-1, keepdims=True)
            e = jnp.exp2(s - m)
            den = jnp.sum(e, axis=-1, keepdims=True)
            o = _dot(_bf(e), vb) / den
            outs.extend(o[i * tq:(i + 1) * tq] for i in range(stack))
    o_ref[...] = _bf(jnp.concatenate(outs, axis=-1))


def _attn_call(qn, kn, proj, layer, cache_k, cache_v, batch, lq):
    cached = cache_k is not None
    tq = min(ATT_TQ, lq)
    nq = lq // tq
    in_specs = [pl.BlockSpec((tq, ATT_WIDTH), lambda b, i: (b * nq + i, 0)),
                pl.BlockSpec((lq, ATT_KV_WIDTH), lambda b, i: (b, 0)),
                pl.BlockSpec((lq, ATT_KV_WIDTH), lambda b, i: (b, AUX_V // ATT_KV_WIDTH))]
    args = [qn, kn, proj]
    if cached:
        past = cache_k.shape[2]
        cache_spec = pl.BlockSpec((None, None, past, ATT_KV_WIDTH), lambda b, i: (b, layer, 0, 0))
        in_specs += [cache_spec, cache_spec]
        args += [cache_k, cache_v]
    return pl.pallas_call(
        functools.partial(_attn_kernel, cached=cached),
        grid=(batch, nq), in_specs=in_specs,
        out_specs=pl.BlockSpec((tq, ATT_WIDTH), lambda b, i: (b * nq + i, 0)),
        out_shape=jax.ShapeDtypeStruct((batch * lq, ATT_WIDTH), BF16),
        compiler_params=_params("arbitrary", "arbitrary"), name="attn_cached" if cached else "attn",
    )(*args)


def _s5_kernel(*refs, batch, seq_len, zero_init):
    if zero_init:
        (u_ref, bmat_ref, cre_ref, cim_ref, ar_ref, ai_ref, d_ref, gw_ref, gb_ref,
         out_ref, fr_ref, fi_ref, utb_ref, ytb_ref, xs_ref) = refs
        ir_ref = ii_ref = None
    else:
        (u_ref, bmat_ref, cre_ref, cim_ref, ar_ref, ai_ref, d_ref, gw_ref, gb_ref, ir_ref, ii_ref,
         out_ref, fr_ref, fi_ref, utb_ref, ytb_ref, xs_ref) = refs
    rows = S5_ROWS
    tc = rows // batch
    n_chunks = seq_len // tc
    n_sub = batch // SUBLANES
    halves = S5_WIDTH // LANES

    def aligned(x, m):
        return x if isinstance(x, int) else pl.multiple_of(x, m)

    for b in range(batch):
        for hf in range(halves):
            utb_ref[hf, pl.ds(b, seq_len, stride=batch), :] = (
                u_ref[b * seq_len:(b + 1) * seq_len, hf * LANES:(hf + 1) * LANES])
    for hf in range(halves):
        ytb_ref[hf] = utb_ref[hf] * d_ref[:, hf * LANES:(hf + 1) * LANES]

    def chunk_of(d, i):
        return i if d == 0 else n_chunks - 1 - i

    def stage_in(i, slot):
        for d in range(N_DIR):
            r0 = aligned(chunk_of(d, i) * rows, rows)
            u_c = jnp.concatenate([utb_ref[hf, pl.ds(r0, rows), :] for hf in range(halves)], axis=-1)
            xs_ref[slot, d] = _dot(_bf(u_c), bmat_ref[d])

    def stage_scan(slot, carry):
        new_carry = []
        for d in range(N_DIR):
            a_re = ar_ref[d]
            a_im = ai_ref[d]
            per_sub = []
            for sub in range(n_sub):
                s_re, s_im = carry[d][sub]
                for t in range(tc):
                    r = (t if d == 0 else tc - 1 - t) * batch + sub * SUBLANES
                    x_re = xs_ref[slot, d, r:r + SUBLANES, :S5_FLAT]
                    x_im = xs_ref[slot, d, r:r + SUBLANES, S5_FLAT:]
                    s_re, s_im = (a_re * s_re - a_im * s_im + x_re, a_re * s_im + a_im * s_re + x_im)
                    xs_ref[slot, d, r:r + SUBLANES, :S5_FLAT] = s_re
                    xs_ref[slot, d, r:r + SUBLANES, S5_FLAT:] = s_im
                per_sub.append((s_re, s_im))
            new_carry.append(tuple(per_sub))
        return tuple(new_carry)

    def stage_out(i, slot):
        for d in range(N_DIR):
            r0 = aligned(chunk_of(d, i) * rows, rows)
            y = (_dot(_bf(xs_ref[slot, d, :, :S5_FLAT]), cre_ref[d])
                 - _dot(_bf(xs_ref[slot, d, :, S5_FLAT:]), cim_ref[d]))
            for hf in range(halves):
                ytb_ref[hf, pl.ds(r0, rows), :] += y[:, hf * LANES:(hf + 1) * LANES]

    def start_state(ref, d, sub):
        if zero_init:
            return jnp.zeros((SUBLANES, S5_FLAT), F32)
        return ref[d, sub * SUBLANES:(sub + 1) * SUBLANES, :]

    carry = tuple(tuple((start_state(ir_ref, d, sub), start_state(ii_ref, d, sub)) for sub in range(n_sub))
                  for d in range(N_DIR))
    def step(i, phase, carry):
        stage_in(i + 1, (phase + 1) % S5_SLOTS)
        carry = stage_scan(phase, carry)
        stage_out(i - 1, (phase - 1) % S5_SLOTS)
        return carry

    stage_in(0, 0)
    stage_in(1, 1)
    carry = stage_scan(0, carry)
    n_steady = n_chunks - 2
    n_peeled = n_steady % S5_SLOTS
    for i in range(1, 1 + n_peeled):
        carry = step(i, i % S5_SLOTS, carry)
    first = 1 + n_peeled

    def body(g, carry):
        for k in range(S5_SLOTS):
            carry = step(first + g * S5_SLOTS + k, (first + k) % S5_SLOTS, carry)
        return carry

    carry = lax.fori_loop(0, n_steady // S5_SLOTS, body, carry)
    last = n_chunks - 1
    carry = stage_scan(last % S5_SLOTS, carry)
    stage_out(last - 1, (last - 1) % S5_SLOTS)
    stage_out(last, last % S5_SLOTS)
    for d in range(N_DIR):
        for sub in range(n_sub):
            fr_ref[d, sub * SUBLANES:(sub + 1) * SUBLANES, :] = carry[d][sub][0]
            fi_ref[d, sub * SUBLANES:(sub + 1) * SUBLANES, :] = carry[d][sub][1]

    for b in range(batch):
        z = jax.nn.gelu(jnp.concatenate(
            [ytb_ref[hf, pl.ds(b, seq_len, stride=batch), :] for hf in range(halves)], axis=-1))
        gate = _sigmoid(_dot(_bf(z), gw_ref[...]) + gb_ref[...])
        out_ref[b * seq_len:(b + 1) * seq_len, :] = _bf(z * gate)


def _s5_call(proj, layer, pp, init_re, init_im, batch, seq_len):
    rows = seq_len * batch
    single = pl.Buffered(1)
    zero_init = init_re is None
    names = ['s5_bmat', 's5_cre', 's5_cim', 's5_ab_re', 's5_ab_im', 's5_d', 's5_glu_w', 's5_glu_b']
    args = [pp[n] for n in names] + ([] if zero_init else [init_re, init_im])
    whole = lambda a: _layered(layer, a.shape[1:], lambda i: (0,) * (a.ndim - 1), pipeline_mode=single)
    in_specs = ([pl.BlockSpec((rows, S5_WIDTH), lambda i: (0, AUX_S5 // S5_WIDTH), pipeline_mode=single)]
                + [whole(a) for a in args])
    fin_spec = pl.BlockSpec((N_DIR, batch, S5_FLAT), lambda i: (0, 0, 0))
    out_specs = [pl.BlockSpec((rows, S5_WIDTH), lambda i: (0, 0), pipeline_mode=single), fin_spec, fin_spec]
    out_shape = (jax.ShapeDtypeStruct((rows, S5_WIDTH), BF16),
                 jax.ShapeDtypeStruct((N_DIR, batch, S5_FLAT), F32),
                 jax.ShapeDtypeStruct((N_DIR, batch, S5_FLAT), F32))
    return pl.pallas_call(
        functools.partial(_s5_kernel, batch=batch, seq_len=seq_len, zero_init=zero_init),
        grid=(1,), in_specs=in_specs, out_specs=out_specs, out_shape=out_shape,
        scratch_shapes=[pltpu.VMEM((S5_WIDTH // LANES, rows, LANES), F32),
                        pltpu.VMEM((S5_WIDTH // LANES, rows, LANES), F32),
                        pltpu.VMEM((S5_SLOTS, N_DIR, S5_ROWS, 2 * S5_FLAT), F32)],
        compiler_params=_params("arbitrary"), name="s5",
    )(proj, *args)


def _mlstm_kernel(*refs, seq_len, zero_init):
    if zero_init:
        (q_ref, k_ref, vt_ref, mg_ref, mgt_ref, gbr_ref, gbc_ref, tril_ref, triu_ref, ones_ref,
         ht_ref, cn_ref, m_ref) = refs
    else:
        (q_ref, k_ref, vt_ref, mg_ref, mgt_ref, gbr_ref, gbc_ref, tril_ref, triu_ref, ones_ref,
         c0_ref, n0_ref, m0_ref, ht_ref, cn_ref, m_ref) = refs
    _mlstm_body(q_ref, k_ref, vt_ref, mg_ref, mgt_ref, gbr_ref, gbc_ref, tril_ref, triu_ref, ones_ref,
                None if zero_init else (c0_ref, n0_ref, m0_ref), ht_ref, cn_ref, m_ref, seq_len)


def _mlstm_body(q_ref, k_ref, vt_ref, mg_ref, mgt_ref, gbr_ref, gbc_ref, tril_ref, triu_ref, ones_ref,
                init_refs, ht_ref, cn_ref, m_ref, seq_len):
    tile = M_TILE
    n_chunks = seq_len // tile
    bg = q_ref.shape[0]
    if init_refs is None:
        cn_ref[...] = jnp.zeros_like(cn_ref)
        m_ref[...] = jnp.zeros_like(m_ref)
    else:
        c0_ref, n0_ref, m0_ref = init_refs
        cn_ref[:, :, :M_HD, :] = c0_ref[...]
        cn_ref[:, :, M_HD:, :] = n0_ref[...]
        m_ref[...] = m0_ref[...]
    s_ids = lax.broadcasted_iota(jnp.int32, (tile, tile), 0)
    t_ids = lax.broadcasted_iota(jnp.int32, (tile, tile), 1)
    first_row = lax.broadcasted_iota(jnp.int32, (CN_ROWS - M_HD, tile), 0) == 0
    neg_inf = jnp.float32(-jnp.inf)
    ones = ones_ref[...]

    def direction(d, c):
        t0 = pl.multiple_of(c * tile, tile)
        col_mat = tril_ref[...] if d == 0 else triu_ref[...]
        row_mat = triu_ref[...] if d == 0 else tril_ref[...]
        valid = (s_ids <= t_ids) if d == 0 else (s_ids >= t_ids)
        gt = mgt_ref[:, :, pl.ds(t0, tile)] + gbc_ref[...]
        lft_hi, lft_lo = _split(_log_sigmoid(gt).reshape(bg * M_GATE_COLS, tile))
        b_rows = (_dot(lft_hi, row_mat) + _dot(lft_lo, row_mat)).reshape(bg, M_GATE_COLS, tile)
        totals = (_dot(lft_hi, ones) + _dot(lft_lo, ones)).reshape(bg, M_GATE_COLS, tile)
        gc = mg_ref[:, pl.ds(t0, tile), :] + gbr_ref[...]
        lfc_hi, lfc_lo = _split(_log_sigmoid(gc))
        col_b = jnp.broadcast_to(col_mat, (bg, tile, tile))
        b_cols = (jnp.einsum('bts,bsl->btl', col_b, lfc_hi, preferred_element_type=F32)
                  + jnp.einsum('bts,bsl->btl', col_b, lfc_lo, preferred_element_type=F32))
        for h in range(M_HEADS):
            j = d * M_HEADS + h
            fj = M_CHAINS + j
            rows = slice(h * M_HD, (h + 1) * M_HD)
            q_c = q_ref[:, pl.ds(t0, tile), rows]
            k_c = k_ref[:, pl.ds(t0, tile), rows]
            vt_c = vt_ref[:, rows, pl.ds(t0, tile)]
            b_t = b_rows[:, fj:fj + 1, :]
            li_t = gt[:, j:j + 1, :]
            tot = totals[:, fj:fj + 1, :]
            m_prev = m_ref[:, j]
            cn_prev = cn_ref[:, j]

            c_col = gc[:, :, j:j + 1] - b_cols[:, :, fj:fj + 1]
            cm = jnp.where(valid, c_col, neg_inf)
            inter = b_t + m_prev
            m_t = jnp.maximum(inter, b_t + jnp.max(cm, axis=1, keepdims=True))
            w_t = jnp.exp(cm + (b_t - m_t))
            s_inter = jnp.exp(inter - m_t)
            sc_t = jnp.einsum('bse,bte->bst', k_c, q_c, preferred_element_type=F32) * w_t
            num = jnp.einsum('bds,bst->bdt', _bf(vt_c), _bf(sc_t), preferred_element_type=F32)
            ext = jnp.einsum('bre,bte->brt', _bf(cn_prev), q_c, preferred_element_type=F32)
            num = num + s_inter * ext[:, :M_HD, :]
            den = jnp.sum(sc_t, axis=1, keepdims=True) + s_inter * ext[:, M_HD:M_HD + 1, :]
            ht_ref[d, :, rows, pl.ds(t0, tile)] = num / jnp.maximum(jnp.abs(den), jnp.exp(-m_t))

            g_row = tot - b_t + li_t
            m_new = jnp.maximum(tot + m_prev, jnp.max(g_row, axis=-1, keepdims=True))
            wk = jnp.exp(g_row - m_new)
            decay = jnp.exp(tot + m_prev - m_new)
            vw = jnp.concatenate([vt_c * wk, jnp.where(first_row, wk, 0.0)], axis=1)
            cn_ref[:, j] = (decay[:, :, :M_HD] * cn_prev
                            + jnp.einsum('brs,bse->bre', _bf(vw), k_c, preferred_element_type=F32))
            m_ref[:, j] = m_new

    def body(i, _):
        direction(0, i)
        direction(1, n_chunks - 1 - i)
        return 0

    lax.fori_loop(0, n_chunks, body, 0)


def _mlstm_call(mqk3, vt, mg3, mgt, layer, pp, init, batch, seq_len):
    bg = min(batch // 2, M_BG * M_BG_SEQ // seq_len)
    const2 = lambda g: (0, 0)
    lead3 = lambda g: (g, 0, 0)
    lead4 = lambda g: (g, 0, 0, 0)
    in_specs = [pl.BlockSpec((bg, seq_len, M_WIDTH), lead3),
                pl.BlockSpec((bg, seq_len, M_WIDTH), lambda g: (g, 0, 1)),
                pl.BlockSpec((bg, M_WIDTH, seq_len), lead3),
                pl.BlockSpec((bg, seq_len, LANES), lead3),
                pl.BlockSpec((bg, M_GATE_COLS, seq_len), lead3),
                _layered(layer, (1, LANES), const2),
                _layered(layer, (M_GATE_COLS, 1), const2),
                pl.BlockSpec((M_TILE, M_TILE), const2),
                pl.BlockSpec((M_TILE, M_TILE), const2),
                pl.BlockSpec((M_TILE, M_TILE), const2)]
    args = [mqk3, mqk3, vt, mg3, mgt, pp['m_gb_row'], pp['m_gb_col'], pp['tril'], pp['triu'], pp['ones']]
    if init is not None:
        state_spec = lambda *tail: pl.BlockSpec((bg, None, M_CHAINS) + tail, lambda g: (g, layer, 0, 0, 0))
        in_specs += [state_spec(M_HD, M_HD), state_spec(CN_ROWS - M_HD, M_HD), state_spec(1, LANES)]
        args += list(init)
    out_specs = [pl.BlockSpec((N_DIR, bg, M_WIDTH, seq_len), lambda g: (0, g, 0, 0)),
                 pl.BlockSpec((bg, M_CHAINS, CN_ROWS, M_HD), lead4),
                 pl.BlockSpec((bg, M_CHAINS, 1, LANES), lead4)]
    out_shape = [jax.ShapeDtypeStruct((N_DIR, batch, M_WIDTH, seq_len), F32),
                 jax.ShapeDtypeStruct((batch, M_CHAINS, CN_ROWS, M_HD), F32),
                 jax.ShapeDtypeStruct((batch, M_CHAINS, 1, LANES), F32)]
    return pl.pallas_call(
        functools.partial(_mlstm_kernel, seq_len=seq_len, zero_init=init is None),
        grid=(batch // bg,), in_specs=in_specs, out_specs=out_specs, out_shape=out_shape,
        compiler_params=_params("arbitrary"), name="mlstm",
    )(*args)


def _tail_kernel(x_ref, att_ref, s5_ref, ht_ref, mo_ref, mod_ref, g2_ref, ng_ref, blkm_ref, wo_ref,
                 wg_ref, wu_ref, cw_ref, cb_ref, wd_ref, o_ref, h2_ref, gated_ref, *, seq_len):
    tm = x_ref.shape[0]
    mh = jnp.concatenate([(ht_ref[0, s] + ht_ref[1, s]).T for s in range(ht_ref.shape[1])], axis=0)
    ml = mh * lax.rsqrt(_seg_mean_sq(mh, blkm_ref[...]) + EPS) * ng_ref[...] * _sigmoid(mo_ref[...])
    mixers = jnp.concatenate([att_ref[...], s5_ref[...], _bf(ml)], axis=-1)
    for c0 in range(0, D_MODEL, TAIL_OUT_CHUNK):
        cols = slice(c0, c0 + TAIL_OUT_CHUNK)
        o_ref[:, cols] = x_ref[:, cols] + mod_ref[2:3, cols] * _dot(mixers, wo_ref[:, cols])
    x1 = o_ref[...]
    ms = jnp.mean(x1 * x1, axis=-1, keepdims=True)
    xn = x1 * lax.rsqrt(ms + EPS) * g2_ref[...]
    h2_ref[...] = _bf(xn * (1.0 + mod_ref[4:5, :]) + mod_ref[3:4, :])

    half = TAIL_HALF
    for r0 in range(0, tm, half):
        h2 = h2_ref[r0:r0 + half, :]
        p0 = max(r0 - BF16_ROWS, 0)
        n0 = min(r0 + half, tm - BF16_ROWS)
        h2_ext = jnp.concatenate([h2, h2_ref[p0:p0 + BF16_ROWS, :], h2_ref[n0:n0 + BF16_ROWS, :]], axis=0)
        for c0 in range(0, D_FF, FF_CHUNK):
            width = min(FF_CHUNK, D_FF - c0)
            cols = slice(c0, c0 + width)
            row = lax.broadcasted_iota(jnp.int32, (half, width), 0)
            half_start = row == 0
            half_end = row == half - 1
            pos = (r0 + row) % seq_len
            seq_start = pos == 0
            seq_end = pos == seq_len - 1
            a_ext = _dot(h2_ext, wg_ref[:, cols])
            a = a_ext[:half]
            before = a_ext[half + BF16_ROWS - 1:half + BF16_ROWS, :]
            after = a_ext[half + BF16_ROWS:half + BF16_ROWS + 1, :]
            a_prev = jnp.where(half_start, before, pltpu.roll(a, 1, axis=0))
            a_next = jnp.where(half_end, after, pltpu.roll(a, half - 1, axis=0))
            a_prev = jnp.where(seq_start, 0.0, a_prev)
            a_next = jnp.where(seq_end, 0.0, a_next)
            ac = a_prev * cw_ref[0:1, cols] + a * cw_ref[1:2, cols] + a_next * cw_ref[2:3, cols] + cb_ref[:, cols]
            up = _dot(h2, wu_ref[:, cols])
            gated_ref[:, cols] = _bf(ac * _sigmoid(ac) * up)
        o_ref[r0:r0 + half, :] = o_ref[r0:r0 + half, :] + mod_ref[5:6, :] * _dot(gated_ref[...], wd_ref[...])


def _tail_call(x2, att, s5o, ht, proj, layer, pp, seq_len, per_batch_mod):
    t = x2.shape[0]
    tm = TAIL_TM
    seqs_per_tile = tm // seq_len
    if per_batch_mod:
        mod_row = lambda i: 1 + i * seqs_per_tile
    else:
        mod_row = lambda i: 0
    row = lambda i: (i, 0)
    const = lambda i: (0, 0)
    single = pl.Buffered(1)
    whole = lambda shape: _layered(layer, shape, const, pipeline_mode=single)
    in_specs = [pl.BlockSpec((tm, D_MODEL), row),
                pl.BlockSpec((tm, ATT_WIDTH), row),
                pl.BlockSpec((tm, S5_WIDTH), row),
                pl.BlockSpec((N_DIR, seqs_per_tile, M_WIDTH, seq_len), lambda i: (0, i, 0, 0)),
                pl.BlockSpec((tm, M_WIDTH), lambda i: (i, AUX_MO // M_WIDTH)),
                _mod_spec(layer, mod_row),
                _layered(layer, (1, D_MODEL), const),
                _layered(layer, (1, M_WIDTH), const),
                pl.BlockSpec((M_WIDTH, M_WIDTH), const),
                whole((D_MODEL, D_MODEL)),
                whole((D_MODEL, D_FF)), whole((D_MODEL, D_FF)), whole((3, D_FF)), whole((1, D_FF)),
                whole((D_FF, D_MODEL))]
    return pl.pallas_call(
        functools.partial(_tail_kernel, seq_len=seq_len),
        grid=(t // tm,), in_specs=in_specs,
        out_specs=pl.BlockSpec((tm, D_MODEL), row),
        out_shape=jax.ShapeDtypeStruct((t, D_MODEL), F32),
        scratch_shapes=[pltpu.VMEM((tm, D_MODEL), BF16), pltpu.VMEM((TAIL_HALF, D_FF), BF16)],
        compiler_params=_params("arbitrary"), name="tail",
    )(x2, att, s5o, ht, proj, pp['mod'], pp['g2'], pp['m_ng'], pp['blkm'], pp['w_out'],
      pp['ffn_w_gate'], pp['ffn_w_up'], pp['ffn_conv_w'], pp['ffn_conv_b'], pp['ffn_w_down'])


def _trunk_layer(x2, batch, seq_len, layer, pp, ctx, rope_tabs):
    latent = ctx is not None
    proj, mg, qn, kn, mqk, vt, mgt = _pre_call(x2, layer, pp, rope_tabs if latent else None, seq_len, latent)
    cache_k, cache_v, init_re, init_im, m_init = ctx if latent else (None,) * 5

    att = _attn_call(qn, kn, proj, layer, cache_k, cache_v, batch, seq_len)
    s5o, fin_re, fin_im = _s5_call(proj, layer, pp, init_re, init_im, batch, seq_len)
    ht, cn_f, m_f = _mlstm_call(mqk.reshape(batch, seq_len, 2 * M_WIDTH), vt, mg.reshape(batch, seq_len, LANES), mgt,
                                layer, pp, m_init, batch, seq_len)
    x_out = _tail_call(x2, att, s5o, ht, proj, layer, pp, seq_len, latent)
    if latent:
        return x_out, None
    states = (kn.reshape(batch, seq_len, ATT_KV_HEADS, ATT_HD),
              proj[:, AUX_V:].reshape(batch, seq_len, ATT_KV_HEADS, ATT_HD),
              fin_re.transpose(1, 0, 2).reshape(batch, N_DIR, S5_GROUPS, S5_STATE),
              fin_im.transpose(1, 0, 2).reshape(batch, N_DIR, S5_GROUPS, S5_STATE),
              cn_f[:, :, :M_HD, :].reshape(batch, N_DIR, M_HEADS, M_HD, M_HD),
              cn_f[:, :, M_HD, :].reshape(batch, N_DIR, M_HEADS, M_HD),
              m_f[:, :, 0, 0].reshape(batch, N_DIR, M_HEADS))
    return x_out, states


def _head_block(width, head_dim):
    ids = np.arange(width) // head_dim
    return jnp.asarray((ids[:, None] == ids[None, :]).astype(np.float32) / head_dim, dtype=BF16)


def kernel(x_prompt, x_sample, c, cache_attn_k, cache_attn_v, state_s5_re, state_s5_im, state_mlstm_C, state_mlstm_n, state_mlstm_m, c_ctx, ada_w, ada_b, norm1_g, norm2_g, w_in, q_norm_g, k_norm_g, s5_a_re, s5_a_im, s5_log_dt, s5_b_re, s5_b_im, s5_c_re, s5_c_im, s5_d, s5_glu_w, s5_glu_b, m_gate_b, m_norm_g, w_out, ffn_w_gate, ffn_w_up, ffn_conv_w, ffn_conv_b, ffn_w_down):
    batch, seq = x_prompt.shape[0], x_prompt.shape[1]
    dec_batch, dec_seq = x_sample.shape[0], x_sample.shape[1]

    n_mod_rows = 2 * SUBLANES
    cvec = jnp.zeros((n_mod_rows, D_MODEL), F32).at[0].set(c_ctx).at[1:1 + dec_batch].set(c)
    mod_all = _ada_call(cvec, ada_w, ada_b).reshape(DEPTH, n_mod_rows, 6, D_MODEL)

    ab_re, ab_im, bb_re, bb_im = _s5_disc_call(s5_a_re, s5_a_im, s5_log_dt, s5_b_re, s5_b_im)
    ab_re = ab_re.reshape(DEPTH, N_DIR, 1, S5_FLAT)
    ab_im = ab_im.reshape(DEPTH, N_DIR, 1, S5_FLAT)
    bb_re = bb_re.reshape(DEPTH, N_DIR, S5_GROUPS, S5_STATE, S5_CH)
    bb_im = bb_im.reshape(DEPTH, N_DIR, S5_GROUPS, S5_STATE, S5_CH)

    rope_tabs = _rope_tables(dec_seq)
    tri = np.tril(np.ones((M_TILE, M_TILE), np.float32))
    tril = jnp.asarray(tri, dtype=BF16)
    triu = jnp.asarray(tri.T, dtype=BF16)
    ones = jnp.ones((M_TILE, M_TILE), BF16)
    blkq = _head_block(ATT_WIDTH, ATT_HD)
    blkk = _head_block(ATT_KV_WIDTH, ATT_HD)
    blkm = _head_block(M_WIDTH, M_HD)

    w_in_b = _bf(w_in)
    gb = m_gate_b.reshape(DEPTH, M_GATE_COLS)
    pp = dict(
        mod=mod_all, g1=norm1_g.reshape(DEPTH, 1, D_MODEL), g2=norm2_g.reshape(DEPTH, 1, D_MODEL),
        w_in=w_in_b,
        w_gatecols=jnp.zeros((DEPTH, D_MODEL, LANES), BF16).at[:, :, :M_GATE_COLS].set(w_in_b[:, :, PROJ_MAIN:]),
        blkq=blkq, blkk=blkk, blkm=blkm,
        qg=jnp.tile(q_norm_g, (1, ATT_HEADS)).reshape(DEPTH, 1, ATT_WIDTH),
        kg=jnp.tile(k_norm_g, (1, ATT_KV_HEADS)).reshape(DEPTH, 1, ATT_KV_WIDTH),
        s5_bmat=_bf(jnp.concatenate([_block_diag(jnp.swapaxes(bb_re, -1, -2)),
                                     _block_diag(jnp.swapaxes(bb_im, -1, -2))], axis=-1)),
        s5_cre=_bf(_block_diag(jnp.swapaxes(s5_c_re, -1, -2))),
        s5_cim=_bf(_block_diag(jnp.swapaxes(s5_c_im, -1, -2))),
        s5_ab_re=jnp.broadcast_to(ab_re, (DEPTH, N_DIR, SUBLANES, S5_FLAT)),
        s5_ab_im=jnp.broadcast_to(ab_im, (DEPTH, N_DIR, SUBLANES, S5_FLAT)),
        s5_d=s5_d.reshape(DEPTH, 1, S5_WIDTH), s5_glu_w=_bf(s5_glu_w), s5_glu_b=s5_glu_b.reshape(DEPTH, 1, S5_WIDTH),
        m_gb_row=jnp.zeros((DEPTH, 1, LANES), F32).at[:, 0, :M_GATE_COLS].set(gb),
        m_gb_col=gb.reshape(DEPTH, M_GATE_COLS, 1),
        m_ng=jnp.tile(m_norm_g, (1, M_HEADS)).reshape(DEPTH, 1, M_WIDTH), tril=tril, triu=triu, ones=ones,
        w_out=_bf(w_out), ffn_w_gate=_bf(ffn_w_gate), ffn_w_up=_bf(ffn_w_up),
        ffn_conv_w=ffn_conv_w, ffn_conv_b=ffn_conv_b.reshape(DEPTH, 1, D_FF), ffn_w_down=_bf(ffn_w_down))

    past = cache_attn_k.shape[2]
    n_pad = CN_ROWS - M_HD
    s5_init = lambda st: st.reshape(dec_batch, DEPTH, N_DIR, S5_FLAT).transpose(1, 2, 0, 3)
    m_init = (state_mlstm_C.reshape(dec_batch, DEPTH, M_CHAINS, M_HD, M_HD),
              jnp.zeros((dec_batch, DEPTH, M_CHAINS, n_pad, M_HD), F32).at[:, :, :, 0, :].set(
                  state_mlstm_n.reshape(dec_batch, DEPTH, M_CHAINS, M_HD)),
              jnp.broadcast_to(state_mlstm_m.reshape(dec_batch, DEPTH, M_CHAINS, 1, 1),
                               (dec_batch, DEPTH, M_CHAINS, 1, LANES)))
    ctx = (cache_attn_k.reshape(dec_batch, DEPTH, past, ATT_KV_WIDTH),
           cache_attn_v.reshape(dec_batch, DEPTH, past, ATT_KV_WIDTH),
           s5_init(state_s5_re), s5_init(state_s5_im), m_init)

    xp = x_prompt.reshape(batch * seq, D_MODEL)
    xs = x_sample.reshape(dec_batch * dec_seq, D_MODEL)
    ctx_out = []
    for l in range(DEPTH):
        xp, st = _trunk_layer(xp, batch, seq, l, pp, None, None)
        ctx_out.append(st)
        xs, _ = _trunk_layer(xs, dec_batch, dec_seq, l, pp, ctx, rope_tabs)
    outs = [jnp.stack([s[i] for s in ctx_out], axis=1) for i in range(7)]
    return (xp.reshape(batch, seq, D_MODEL), xs.reshape(dec_batch, dec_seq, D_MODEL), *outs)
```
